```python
import jax
import jax.numpy as jnp
from jax import lax
import numpy as np

D_MODEL = 2048
BATCH = 8
SEQ = 2048
DEPTH = 1
DEC_BATCH = 128
DEC_SEQ = 1
PAST_LEN = 2048
PAGE_SIZE = 128

N_BRANCH = 3
BRANCH_W = 1024
RET_HEADS = 4
RET_DK = 128
RET_DV = 256
RET_CHUNK = 128
RET_THETA = 10000.0
NSA_HEADS = 8
NSA_KV_HEADS = 2
HEAD_DIM = 128
CMP_BLOCK = 32
SEL_BLOCK = 64
SEL_TOPK = 4
WINDOW = 256
Q_BLOCK = 128
ROPE_THETA = 500000.0
ROPE_DIMS = HEAD_DIM // 4
N_MEM = 256
MEM_HEADS = 4
MEM_HEAD_DIM = 256
PEER_HEADS = 8
PEER_KEYS = 128
PEER_N = PEER_KEYS * PEER_KEYS
PEER_DKEY = 128
PEER_TOPK = 16
TOK_BLOCK = 128

EPS = 1e-6
NEG_INF = -1e30
FORCE = 1e4

RET_QK_W = RET_HEADS * RET_DK
RET_V_W = RET_HEADS * RET_DV
NSA_Q_W = NSA_HEADS * HEAD_DIM
NSA_KV_W = NSA_KV_HEADS * HEAD_DIM
MEM_W = MEM_HEADS * MEM_HEAD_DIM
IN_SIZES = (RET_QK_W, RET_QK_W, RET_V_W, RET_V_W, NSA_Q_W,
            NSA_KV_W, NSA_KV_W, NSA_KV_W, NSA_KV_W, NSA_KV_W, NSA_KV_W,
            NSA_HEADS * 3, MEM_W, N_BRANCH * D_MODEL)
IN_WIDTH = sum(IN_SIZES)

kernel_name = 'hybrid_retention_nsa_peer_step'


def rms_norm(x, g):
    xf = x.astype(jnp.float32)
    y = xf * lax.rsqrt(jnp.mean(xf * xf, axis=-1, keepdims=True) + EPS)
    return (y * g.astype(jnp.float32)).astype(x.dtype)


def rotary(x, pos, rot_dims, theta):
    half = rot_dims // 2
    inv = theta ** (-jnp.arange(half, dtype=jnp.float32) * 2.0 / rot_dims)
    ang = pos.astype(jnp.float32)[:, None] * inv[None, :]
    cos = jnp.cos(ang)[:, None, :]
    sin = jnp.sin(ang)[:, None, :]
    xf = x.astype(jnp.float32)
    x1 = xf[..., :half]
    x2 = xf[..., half:rot_dims]
    out = jnp.concatenate([x1 * cos - x2 * sin, x2 * cos + x1 * sin, xf[..., rot_dims:]], axis=-1)
    return out.astype(x.dtype)


def masked_softmax(s, mask):
    s = jnp.where(mask, s.astype(jnp.float32), NEG_INF)
    return jax.nn.softmax(s, axis=-1) * mask


def split_in(z):
    offs = np.cumsum(IN_SIZES)[:-1].tolist()
    return jnp.split(z, offs, axis=-1)


def retention_chunk(state, q, k, v, log_gamma):
    c = q.shape[1]
    i = jnp.arange(c, dtype=jnp.float32)
    rel = i[:, None] - i[None, :]
    decay = jnp.where(rel >= 0, jnp.exp(jnp.maximum(rel, 0.0)[None] * log_gamma[:, None, None]), 0.0)
    qf, kf, vf = q.astype(jnp.float32), k.astype(jnp.float32), v.astype(jnp.float32)
    inner = jnp.einsum('bihd,bjhd->bhij', qf, kf) * decay
    o = jnp.einsum('bhij,bjhe->bihe', inner, vf)
    cross_decay = jnp.exp((i[:, None] + 1.0) * log_gamma[None, :])[None, :, :, None]
    o = o + jnp.einsum('bihd,bhde->bihe', qf, state) * cross_decay
    k_dec = kf * jnp.exp((c - 1.0 - i)[:, None] * log_gamma[None, :])[None, :, :, None]
    new_state = jnp.exp(c * log_gamma)[None, :, None, None] * state + jnp.einsum('bjhd,bjhe->bhde', k_dec, vf)
    return new_state, o


def retention_branch(q, k, v, gate, pos, state0, gn):
    B, T = q.shape[:2]
    log_gamma = jnp.log1p(-(2.0 ** (-5.0 - jnp.arange(RET_HEADS, dtype=jnp.float32))))
    q = rotary(q, pos, RET_DK, RET_THETA)
    k = rotary(k, pos, RET_DK, RET_THETA) * (RET_DK ** -0.5)
    c = RET_CHUNK if T % RET_CHUNK == 0 else T
    nc = T // c

    def chunks(a):
        return a.reshape(B, nc, c, *a.shape[2:]).swapaxes(0, 1)

    def step(s, xs):
        return retention_chunk(s, xs[0], xs[1], xs[2], log_gamma)

    state, o = lax.scan(step, state0, (chunks(q), chunks(k), chunks(v)))
    o = o.swapaxes(0, 1).reshape(B, T, RET_HEADS, RET_DV)
    o = rms_norm(o, gn).astype(gate.dtype).reshape(B, T, RET_V_W)
    return o * jax.nn.silu(gate), state


def compress(rows, pos_emb, w):
    B, L, G, dh = rows.shape
    nb = L // CMP_BLOCK
    blocks = rows[:, :nb * CMP_BLOCK].reshape(B, nb, CMP_BLOCK, G, dh) + pos_emb[None, None, :, None, :]
    return jnp.einsum('bnlgd,lde->bnge', blocks, w)


def gather_blocks(blocks, idx):
    bt = blocks.transpose(0, 3, 1, 2, 4)
    b = jnp.arange(idx.shape[0])[:, None, None, None]
    g = jnp.arange(idx.shape[2])[None, None, :, None]
    return bt[b, g, idx]


def sel_attend(q, q_pos, idx, ksb, vsb):
    B, Q, G, Hg, dh = q.shape
    kg = gather_blocks(ksb, idx)
    vg = gather_blocks(vsb, idx)
    kpos = idx[..., None] * SEL_BLOCK + jnp.arange(SEL_BLOCK)
    mask = (kpos <= q_pos[None, :, None, None, None]).reshape(B, Q, G, 1, -1)
    s = jnp.einsum('bqghd,bqgnld->bqghnl', q, kg) * (HEAD_DIM ** -0.5)
    pr = masked_softmax(s.reshape(B, Q, G, Hg, -1), mask)
    return jnp.einsum('bqghk,bqgkd->bqghd', pr.astype(vg.dtype), vg.reshape(B, Q, G, -1, dh))


def win_attend(q, q_pos, kw, vw, kw_pos):
    s = jnp.einsum('bqghd,bkgd->bqghk', q, kw) * (HEAD_DIM ** -0.5)
    dist = q_pos[:, None] - kw_pos[None, :]
    mask = (dist >= 0) & (dist <= WINDOW) & (kw_pos[None, :] >= 0)
    pr = masked_softmax(s, mask[None, :, None, None, :])
    return jnp.einsum('bqghk,bkgd->bqghd', pr.astype(vw.dtype), vw)


def nsa_branch(q, q_pos, kc, vc, ks, vs, kw, vw, kw_pos, gates, p):
    B, Tq = q.shape[:2]
    Lk = kc.shape[1]
    hg = NSA_HEADS // NSA_KV_HEADS
    qg = q.reshape(B, Tq, NSA_KV_HEADS, hg, HEAD_DIM)
    ck = compress(kc, p['cmp_pos_k'], p['cmp_w_k'])
    cv = compress(vc, p['cmp_pos_v'], p['cmp_w_v'])
    nbc = ck.shape[1]
    s_c = jnp.einsum('bqghd,bngd->bqghn', qg, ck) * (HEAD_DIM ** -0.5)
    vis = (jnp.arange(nbc) + 1) * CMP_BLOCK - 1 <= q_pos[:, None]
    p_c = masked_softmax(s_c, vis[None, :, None, None, :])
    o_cmp = jnp.einsum('bqghn,bngd->bqghd', p_c.astype(cv.dtype), cv)
    nbs = -(-Lk // SEL_BLOCK)
    r = SEL_BLOCK // CMP_BLOCK
    imp = jnp.pad(p_c.sum(axis=3), ((0, 0), (0, 0), (0, 0), (0, nbs * r - nbc)))
    imp = imp.reshape(B, Tq, NSA_KV_HEADS, nbs, r).sum(-1)
    blk_id = jnp.arange(nbs)[None, :]
    valid = (blk_id * SEL_BLOCK <= q_pos[:, None])[None, :, None, :]
    forced = ((blk_id == 0) | (blk_id == q_pos[:, None] // SEL_BLOCK))[None, :, None, :]
    score = jnp.where(forced, FORCE, jnp.where(valid, imp, -FORCE))
    _, idx = lax.top_k(score, min(SEL_TOPK, nbs))
    pad_s = ((0, 0), (0, nbs * SEL_BLOCK - Lk), (0, 0), (0, 0))
    ksb = jnp.pad(ks, pad_s).reshape(B, nbs, SEL_BLOCK, NSA_KV_HEADS, HEAD_DIM)
    vsb = jnp.pad(vs, pad_s).reshape(B, nbs, SEL_BLOCK, NSA_KV_HEADS, HEAD_DIM)
    qb = Q_BLOCK if Tq % Q_BLOCK == 0 else Tq
    nq = Tq // qb
    lw = kw.shape[1] - (nq - 1) * qb

    def chunks(a):
        return a.reshape(B, nq, qb, *a.shape[2:]).swapaxes(0, 1)

    def block(args):
        i, q_b, pos_b, idx_b = args
        o_sel = sel_attend(q_b, pos_b, idx_b, ksb, vsb)
        kw_b = lax.dynamic_slice_in_dim(kw, i * qb, lw, axis=1)
        vw_b = lax.dynamic_slice_in_dim(vw, i * qb, lw, axis=1)
        kp_b = lax.dynamic_slice_in_dim(kw_pos, i * qb, lw, axis=0)
        o_win = win_attend(q_b, pos_b, kw_b, vw_b, kp_b)
        return o_sel, o_win

    o_sel, o_win = lax.map(block, (jnp.arange(nq), chunks(qg), q_pos.reshape(nq, qb), chunks(idx)))

    def unchunk(a):
        return a.swapaxes(0, 1).reshape(B, Tq, NSA_KV_HEADS, hg, HEAD_DIM)

    g = gates.reshape(B, Tq, NSA_KV_HEADS, hg, 3)
    o = g[..., 0:1] * o_cmp + g[..., 1:2] * unchunk(o_sel) + g[..., 2:3] * unchunk(o_win)
    return o.reshape(B, Tq, NSA_Q_W).astype(q.dtype)


def memory_kv(mem, p):
    B, M, _ = mem.shape
    mk, mv = jnp.split(rms_norm(mem, p['norm_mem']) @ p['w_mem_kv'], 2, axis=-1)
    mk = rms_norm(mk.reshape(B, M, MEM_HEADS, MEM_HEAD_DIM), p['mem_k_norm'])
    return mk, mv.reshape(B, M, MEM_HEADS, MEM_HEAD_DIM)


def mem_attend(q, mk, mv):
    s = jnp.einsum('bthd,bmhd->bhtm', q, mk).astype(jnp.float32) * (MEM_HEAD_DIM ** -0.5)
    pr = jax.nn.softmax(s, axis=-1).astype(mv.dtype)
    return jnp.einsum('bhtm,bmhd->bthd', pr, mv)


def peer(h, p):
    n = h.shape[0]
    tb = TOK_BLOCK if n % TOK_BLOCK == 0 else n

    def block(hb):
        q = (hb @ p['peer_wq']).reshape(tb, PEER_HEADS, 2, PEER_DKEY // 2)
        s = jnp.einsum('nhpk,hpek->nhpe', q, p['peer_subkeys']).astype(jnp.float32)
        v, ix = lax.top_k(s, PEER_TOPK)
        cand = (v[:, :, 0, :, None] + v[:, :, 1, None, :]).reshape(tb, PEER_HEADS, -1)
        cid = (ix[:, :, 0, :, None] * PEER_KEYS + ix[:, :, 1, None, :]).reshape(tb, PEER_HEADS, -1)
        sc, sel = lax.top_k(cand, PEER_TOPK)
        eid = jnp.take_along_axis(cid, sel, axis=-1)
        g = jax.nn.softmax(sc, axis=-1)
        u = p['peer_u'][eid]
        act = jax.nn.gelu(jnp.einsum('nd,nhkd->nhk', hb, u), approximate=False)
        w = (g * act.astype(jnp.float32)).astype(hb.dtype)
        return jnp.einsum('nhk,nhkd->nd', w, p['peer_v'][eid])

    return lax.map(block, h.reshape(n // tb, tb, -1)).reshape(n, -1)


def layer(x, pos0, mem_k, mem_v, ret_state0, past, p):
    B, T, _ = x.shape
    pos = pos0 + jnp.arange(T)
    h = rms_norm(x, p['norm_attn'])
    (rq, rk, rv, rg, nq, kc, vc, ks, vs, kw, vw, ng, mq, mg) = split_in(h @ p['w_in'])
    o_ret, ret_state = retention_branch(
        rq.reshape(B, T, RET_HEADS, RET_DK), rk.reshape(B, T, RET_HEADS, RET_DK),
        rv.reshape(B, T, RET_HEADS, RET_DV), rg, pos, ret_state0, p['ret_gn'])
    def kvh(a):
        return a.reshape(B, T, NSA_KV_HEADS, HEAD_DIM)

    def prep_k(a, g):
        return rotary(rms_norm(kvh(a), g), pos, ROPE_DIMS, ROPE_THETA)

    qn = rotary(rms_norm(nq.reshape(B, T, NSA_HEADS, HEAD_DIM), p['nsa_q_norm']), pos, ROPE_DIMS, ROPE_THETA)
    kc = prep_k(kc, p['nsa_k_norm'][0])
    ks = prep_k(ks, p['nsa_k_norm'][1])
    kw = prep_k(kw, p['nsa_k_norm'][2])
    vc, vs, vw = kvh(vc), kvh(vs), kvh(vw)
    if past is None:
        full = (kc, vc, ks, vs)
        pad = ((0, 0), (WINDOW, 0), (0, 0), (0, 0))
        wk, wv = jnp.pad(kw, pad), jnp.pad(vw, pad)
        wpos = jnp.arange(WINDOW + T) - WINDOW
        nbuf = min(WINDOW, T)
    else:
        full = tuple(jnp.concatenate([a_past, a], axis=1) for a_past, a in zip(past[:4], (kc, vc, ks, vs)))
        nbuf = past[4].shape[1]
        wk = jnp.concatenate([past[4], kw], axis=1)
        wv = jnp.concatenate([past[5], vw], axis=1)
        wpos = pos0 - nbuf + jnp.arange(nbuf + T)
    win_k_new, win_v_new = wk[:, -nbuf:], wv[:, -nbuf:]
    gates = jax.nn.sigmoid(ng.reshape(B, T, NSA_HEADS, 3))
    o_nsa = nsa_branch(qn, pos, full[0], full[1], full[2], full[3], wk, wv, wpos, gates, p)
    qm = rms_norm(mq.reshape(B, T, MEM_HEADS, MEM_HEAD_DIM), p['mem_q_norm'])
    o_mem = mem_attend(qm, mem_k, mem_v).reshape(B, T, MEM_W)
    branches = jnp.stack([o_ret.astype(x.dtype), o_nsa, o_mem.astype(x.dtype)], axis=2)
    up = jnp.einsum('btcw,cwd->btcd', branches, p['w_branch'])
    gate = jax.nn.sigmoid(mg.reshape(B, T, N_BRANCH, D_MODEL))
    x = x + jnp.einsum('btcd,btcd->btd', gate, up) @ p['w_out']
    x = x + peer(rms_norm(x, p['norm_ffn']).reshape(B * T, D_MODEL), p).reshape(B, T, D_MODEL)
    return x, (kc, vc, ks, vs, win_k_new, win_v_new, ret_state.astype(x.dtype))


def gather_pages(pool, page_table):
    g = pool[page_table]
    return g.reshape(g.shape[0], -1, *g.shape[3:])


def setup_inputs(seed: int = 0) -> dict:
    key = jax.random.key(seed)
    keys = iter(jax.random.split(key, 40))

    def nrm(shape, scale):
        return jax.random.normal(next(keys), shape, jnp.float32) * scale

    def gain(shape):
        return 1.0 + nrm(shape, 0.02)

    n_pages = PAST_LEN // PAGE_SIZE
    n_used = DEC_BATCH * n_pages
    n_pool = n_used + max(1, n_used // 4)
    page_table = jax.random.permutation(next(keys), n_pool)[:n_used].reshape(DEC_BATCH, n_pages).astype(jnp.int32)
    nbuf = min(WINDOW, PAST_LEN)
    pool = (DEPTH, n_pool, PAGE_SIZE, NSA_KV_HEADS, HEAD_DIM)
    winb = (DEPTH, DEC_BATCH, nbuf, NSA_KV_HEADS, HEAD_DIM)
    memc = (DEPTH, DEC_BATCH, N_MEM, MEM_HEADS, MEM_HEAD_DIM)
    d = D_MODEL
    return {
        'x_prompt': nrm((BATCH, SEQ, d), 1.0),
        'x_sample': nrm((DEC_BATCH, DEC_SEQ, d), 1.0),
        'cache_cmp_k': nrm(pool, 1.0),
        'cache_cmp_v': nrm(pool, 1.0),
        'cache_sel_k': nrm(pool, 1.0),
        'cache_sel_v': nrm(pool, 1.0),
        'cache_win_k': nrm(winb, 1.0),
        'cache_win_v': nrm(winb, 1.0),
        'state_ret': nrm((DEPTH, DEC_BATCH, RET_HEADS, RET_DK, RET_DV), 0.5),
        'cache_mem_k': nrm(memc, 1.0),
        'cache_mem_v': nrm(memc, 1.0),
        'page_table': page_table,
        'mem_prompt': nrm((BATCH, N_MEM, d), 1.0),
        'norm_attn': gain((DEPTH, d)),
        'w_in': nrm((DEPTH, d, IN_WIDTH), d ** -0.5),
        'ret_gn': gain((DEPTH, RET_HEADS, RET_DV)),
        'nsa_q_norm': gain((DEPTH, HEAD_DIM)),
        'nsa_k_norm': gain((DEPTH, 3, HEAD_DIM)),
        'cmp_pos_k': nrm((DEPTH, CMP_BLOCK, HEAD_DIM), 0.02),
        'cmp_pos_v': nrm((DEPTH, CMP_BLOCK, HEAD_DIM), 0.02),
        'cmp_w_k': nrm((DEPTH, CMP_BLOCK, HEAD_DIM, HEAD_DIM), (CMP_BLOCK * HEAD_DIM) ** -0.5),
        'cmp_w_v': nrm((DEPTH, CMP_BLOCK, HEAD_DIM, HEAD_DIM), (CMP_BLOCK * HEAD_DIM) ** -0.5),
        'norm_mem': gain((DEPTH, d)),
        'w_mem_kv': nrm((DEPTH, d, 2 * MEM_W), d ** -0.5),
        'mem_q_norm': gain((DEPTH, MEM_HEAD_DIM)),
        'mem_k_norm': gain((DEPTH, MEM_HEAD_DIM)),
        'w_branch': nrm((DEPTH, N_BRANCH, BRANCH_W, d), BRANCH_W ** -0.5),
        'w_out': nrm((DEPTH, d, d), d ** -0.5),
        'norm_ffn': gain((DEPTH, d)),
        'peer_wq': nrm((DEPTH, d, PEER_HEADS * PEER_DKEY), d ** -0.5),
        'peer_subkeys': nrm((DEPTH, PEER_HEADS, 2, PEER_KEYS, PEER_DKEY // 2), (PEER_DKEY // 2) ** -0.5),
        'peer_u': nrm((DEPTH, PEER_N, d), d ** -0.5),
        'peer_v': nrm((DEPTH, PEER_N, d), PEER_HEADS ** -0.5),
    }


def reference(x_prompt, x_sample, cache_cmp_k, cache_cmp_v, cache_sel_k, cache_sel_v, cache_win_k, cache_win_v,
              state_ret, cache_mem_k, cache_mem_v, page_table, mem_prompt, norm_attn, w_in, ret_gn, nsa_q_norm,
              nsa_k_norm, cmp_pos_k, cmp_pos_v, cmp_w_k, cmp_w_v, norm_mem, w_mem_kv, mem_q_norm, mem_k_norm,
              w_branch, w_out, norm_ffn, peer_wq, peer_subkeys, peer_u, peer_v):
    past_len = page_table.shape[1] * cache_cmp_k.shape[2]
    xp, xs = x_prompt, x_sample
    st_p, st_s = [], []
    for l in range(DEPTH):
        p = {'norm_attn': norm_attn[l], 'w_in': w_in[l], 'ret_gn': ret_gn[l], 'nsa_q_norm': nsa_q_norm[l],
             'nsa_k_norm': nsa_k_norm[l], 'cmp_pos_k': cmp_pos_k[l], 'cmp_pos_v': cmp_pos_v[l],
             'cmp_w_k': cmp_w_k[l], 'cmp_w_v': cmp_w_v[l], 'norm_mem': norm_mem[l], 'w_mem_kv': w_mem_kv[l],
             'mem_q_norm': mem_q_norm[l], 'mem_k_norm': mem_k_norm[l], 'w_branch': w_branch[l],
             'w_out': w_out[l], 'norm_ffn': norm_ffn[l], 'peer_wq': peer_wq[l],
             'peer_subkeys': peer_subkeys[l], 'peer_u': peer_u[l], 'peer_v': peer_v[l]}
        mem_k, mem_v = memory_kv(mem_prompt, p)
        ret0 = jnp.zeros((xp.shape[0], RET_HEADS, RET_DK, RET_DV), jnp.float32)
        xp, new_p = layer(xp, 0, mem_k, mem_v, ret0, None, p)
        st_p.append(new_p + (mem_k, mem_v))
        past = (gather_pages(cache_cmp_k[l], page_table), gather_pages(cache_cmp_v[l], page_table),
                gather_pages(cache_sel_k[l], page_table), gather_pages(cache_sel_v[l], page_table),
                cache_win_k[l], cache_win_v[l])
        xs, new_s = layer(xs, past_len, cache_mem_k[l], cache_mem_v[l], state_ret[l].astype(jnp.float32), past, p)
        st_s.append(new_s)

    def stacked(states, i):
        return jnp.stack([s[i] for s in states])

    return (xp, xs,
            stacked(st_p, 0), stacked(st_p, 1), stacked(st_p, 2), stacked(st_p, 3), stacked(st_p, 4),
            stacked(st_p, 5), stacked(st_p, 6), stacked(st_p, 7), stacked(st_p, 8),
            stacked(st_s, 0), stacked(st_s, 1), stacked(st_s, 2), stacked(st_s, 3), stacked(st_s, 4),
            stacked(st_s, 5), stacked(st_s, 6))
```

```python
import functools

import numpy as np
import jax
import jax.numpy as jnp
from jax import lax
from jax.experimental import pallas as pl
from jax.experimental.pallas import tpu as pltpu

F32 = jnp.float32
BF16 = jnp.bfloat16
I32 = jnp.int32

EPS = 1e-6
NEG_INF = -1e30
FORCE = 1e4

RET_HEADS, RET_DK, RET_DV, RET_CHUNK, RET_THETA = 4, 128, 256, 128, 10000.0
NSA_HEADS, NSA_KV_HEADS, HEAD_DIM = 8, 2, 128
HEADS_PER_GROUP = NSA_HEADS // NSA_KV_HEADS
CMP_BLOCK, SEL_BLOCK, SEL_TOPK, WINDOW, Q_BLOCK = 32, 64, 4, 256, 128
CMP_PER_SEL = SEL_BLOCK // CMP_BLOCK
ROPE_THETA, ROPE_DIMS = 500000.0, HEAD_DIM // 4
MEM_HEADS, MEM_HEAD_DIM = 4, 256
PEER_HEADS, PEER_KEYS, PEER_DKEY, PEER_TOPK = 8, 128, 128, 16
N_BRANCH = 3
LANES = 128

C_RQ, C_RK, C_RV, C_RG, C_NQ, C_MQ = 0, 512, 1024, 2048, 3072, 4096
C_KC, C_VC, C_KS, C_VS, C_KW, C_VW = 5120, 5376, 5632, 5888, 6144, 6400
C_MG = 6656
Z_TILE = 512


def _cp(*sem):
    return pltpu.CompilerParams(dimension_semantics=sem, vmem_limit_bytes=56 * 1024 * 1024)


def _nt_dot(a, b):
    return lax.dot_general(a, b, (((1,), (1,)), ((), ())), preferred_element_type=F32)


def _tn_dot(a, b):
    return lax.dot_general(a, b, (((0,), (0,)), ((), ())), preferred_element_type=F32)


def _pm_kernel(*refs, mode, residual, emit_h):
    x_ref, g_ref, w_ref = refs[:3]
    k = 3
    r_ref = None
    if residual:
        r_ref = refs[k]
        k += 1
    o_ref = refs[k]
    k += 1
    h_ref = None
    if emit_h:
        h_ref = refs[k]
        k += 1
    h_scr = refs[k]

    @pl.when(pl.program_id(1) == 0)
    def _():
        x = x_ref[...].astype(F32)
        if mode == "norm":
            h = x * lax.rsqrt(jnp.mean(x * x, axis=-1, keepdims=True) + EPS) * g_ref[...]
        elif mode == "bias":
            h = x + g_ref[...]
        else:
            h = x
        h_scr[...] = h.astype(BF16)
        if emit_h:
            h_ref[...] = h_scr[...]

    acc = jnp.dot(h_scr[...], w_ref[...], preferred_element_type=F32)
    if residual:
        acc = acc + r_ref[...]
    o_ref[...] = acc


def _prologue_matmul(x, g, w, *, mode, residual=None, emit_h=False, tm=512):
    n, kdim = x.shape
    wout = w.shape[1]
    tm = min(tm, n)
    tn = 512 if wout % 512 == 0 else (256 if wout % 256 == 0 else 128)
    assert n % tm == 0 and wout % tn == 0
    in_specs = [pl.BlockSpec((tm, kdim), lambda i, j: (i, 0)),
                pl.BlockSpec((1, kdim), lambda i, j: (0, 0)),
                pl.BlockSpec((kdim, tn), lambda i, j: (0, j))]
    args = [x, g.reshape(1, kdim).astype(F32), w]
    if residual is not None:
        in_specs.append(pl.BlockSpec((tm, tn), lambda i, j: (i, j)))
        args.append(residual)
    out_shape = [jax.ShapeDtypeStruct((n, wout), F32)]
    out_specs = [pl.BlockSpec((tm, tn), lambda i, j: (i, j))]
    if emit_h:
        out_shape.append(jax.ShapeDtypeStruct((n, kdim), BF16))
        out_specs.append(pl.BlockSpec((tm, kdim), lambda i, j: (i, 0)))
    res = pl.pallas_call(
        functools.partial(_pm_kernel, mode=mode, residual=residual is not None, emit_h=emit_h),
        grid=(n // tm, wout // tn),
        in_specs=in_specs, out_specs=out_specs, out_shape=out_shape,
        scratch_shapes=[pltpu.VMEM((tm, kdim), BF16)],
        compiler_params=_cp("arbitrary", "arbitrary"),
    )(*args)
    return res if emit_h else res[0]


def _rope_tables(pos, rot_dims, theta, dh):
    half = rot_dims // 2
    inv = theta ** (-jnp.arange(half, dtype=F32) * 2.0 / rot_dims)
    ang = pos.astype(F32)[:, None] * inv[None, :]
    cos, sin = jnp.cos(ang), jnp.sin(ang)
    t = pos.shape[0]
    one = jnp.ones((t, dh - rot_dims), F32)
    zh = jnp.zeros((t, half), F32)
    zr = jnp.zeros((t, dh - rot_dims), F32)
    c = jnp.concatenate([cos, cos, one], axis=1)
    s1 = jnp.concatenate([-sin, zh, zr], axis=1)
    s2 = jnp.concatenate([zh, sin, zr], axis=1)
    return c, s1, s2


def _rope(y, c, s1, s2, half):
    dh = y.shape[-1]
    return y * c + pltpu.roll(y, dh - half, 1) * s1 + pltpu.roll(y, half, 1) * s2


def _headnorm_kernel(*refs, nh, dh, rope_half):
    if rope_half:
        x_ref, g_ref, c_ref, s1_ref, s2_ref, o_ref = refs
    else:
        x_ref, g_ref, o_ref = refs
    for h in range(nh):
        x = x_ref[:, h * dh:(h + 1) * dh]
        y = x * lax.rsqrt(jnp.mean(x * x, axis=-1, keepdims=True) + EPS) * g_ref[...]
        if rope_half:
            y = _rope(y, c_ref[...], s1_ref[...], s2_ref[...], rope_half)
        o_ref[:, h * dh:(h + 1) * dh] = y


def _headnorm(x, col, width, gain, dh, seq, tables=None, rope_half=0):
    n = x.shape[0]
    tm = min(512, seq) if seq > 1 else n
    assert n % tm == 0 and col % width == 0
    cb = col // width
    in_specs = [pl.BlockSpec((tm, width), lambda i: (i, cb)),
                pl.BlockSpec((1, dh), lambda i: (0, 0))]
    args = [x, gain.reshape(1, dh).astype(F32)]
    if rope_half:
        if seq > 1:
            nb = seq // tm
            tspec = pl.BlockSpec((tm, dh), lambda i: (i % nb, 0))
        else:
            tspec = pl.BlockSpec((1, dh), lambda i: (0, 0))
        in_specs += [tspec] * 3
        args += list(tables)
    return pl.pallas_call(
        functools.partial(_headnorm_kernel, nh=width // dh, dh=dh, rope_half=rope_half),
        grid=(n // tm,),
        in_specs=in_specs,
        out_specs=pl.BlockSpec((tm, width), lambda i: (i, 0)),
        out_shape=jax.ShapeDtypeStruct((n, width), F32),
        compiler_params=_cp("arbitrary"),
    )(*args)


def _log_gamma():
    return np.log1p(-(np.float32(2.0) ** (-5.0 - np.arange(RET_HEADS, dtype=np.float32)))).astype(np.float32)


def _ret_finish(o, gn, gate):
    y = o * lax.rsqrt(jnp.mean(o * o, axis=-1, keepdims=True) + EPS) * gn
    return y * (gate * jax.nn.sigmoid(gate))


def _ret_prompt_kernel(lg_ref, q_ref, k_ref, v_ref, gt_ref, c_ref, s1_ref, s2_ref, st0_ref, gn_ref,
                       o_ref, st_ref):
    h = pl.program_id(1)
    c_idx = pl.program_id(2)
    cl = q_ref.shape[0]
    lg = lg_ref[h]

    @pl.when(c_idx == 0)
    def _():
        st_ref[...] = st0_ref[...]

    half = RET_DK // 2
    q = _rope(q_ref[...], c_ref[...], s1_ref[...], s2_ref[...], half)
    k = _rope(k_ref[...], c_ref[...], s1_ref[...], s2_ref[...], half) * (RET_DK ** -0.5)
    v = v_ref[...]
    ri = lax.broadcasted_iota(I32, (cl, cl), 0).astype(F32)
    ci = lax.broadcasted_iota(I32, (cl, cl), 1).astype(F32)
    rel = ri - ci
    decay = jnp.where(rel >= 0, jnp.exp(jnp.maximum(rel, 0.0) * lg), 0.0)
    qb, kb, vb = q.astype(BF16), k.astype(BF16), v.astype(BF16)
    inner = _nt_dot(qb, kb) * decay
    state = st_ref[...]
    rowi = lax.broadcasted_iota(I32, (cl, 1), 0).astype(F32)
    cross = jnp.exp((rowi + 1.0) * lg)
    o = jnp.dot(inner.astype(BF16), vb, preferred_element_type=F32)
    o = o + jnp.dot(qb, state.astype(BF16), preferred_element_type=F32) * cross
    k_dec = (k * jnp.exp((cl - 1.0 - rowi) * lg)).astype(BF16)
    st_ref[...] = jnp.exp(jnp.zeros((1, 1), F32) + cl * lg) * state + _tn_dot(k_dec, vb)
    o_ref[...] = _ret_finish(o, gn_ref[...], gt_ref[...])


def _retention_prompt(z3, tables, state0, gn):
    b, t, _ = z3.shape
    cl = RET_CHUNK
    assert t % cl == 0
    nc = t // cl
    qk_b, v_b = C_RQ // RET_DK, C_RV // RET_DV
    kk_b, g_b = C_RK // RET_DK, C_RG // RET_DV
    tspec = pl.BlockSpec((cl, RET_DK), lambda bi, h, c: (c, 0))
    o, st = pl.pallas_call(
        _ret_prompt_kernel,
        grid=(b, RET_HEADS, nc),
        in_specs=[pl.BlockSpec(memory_space=pltpu.SMEM),
                  pl.BlockSpec((None, cl, RET_DK), lambda bi, h, c: (bi, c, qk_b + h)),
                  pl.BlockSpec((None, cl, RET_DK), lambda bi, h, c: (bi, c, kk_b + h)),
                  pl.BlockSpec((None, cl, RET_DV), lambda bi, h, c: (bi, c, v_b + h)),
                  pl.BlockSpec((None, cl, RET_DV), lambda bi, h, c: (bi, c, g_b + h)),
                  tspec, tspec, tspec,
                  pl.BlockSpec((None, None, RET_DK, RET_DV), lambda bi, h, c: (bi, h, 0, 0)),
                  pl.BlockSpec((None, 1, RET_DV), lambda bi, h, c: (h, 0, 0))],
        out_specs=[pl.BlockSpec((None, cl, RET_DV), lambda bi, h, c: (bi, c, h)),
                   pl.BlockSpec((None, None, RET_DK, RET_DV), lambda bi, h, c: (bi, h, 0, 0))],
        out_shape=[jax.ShapeDtypeStruct((b, t, RET_HEADS * RET_DV), F32),
                   jax.ShapeDtypeStruct((b, RET_HEADS, RET_DK, RET_DV), F32)],
        compiler_params=_cp("arbitrary", "arbitrary", "arbitrary"),
    )(jnp.asarray(_log_gamma()), z3, z3, z3, z3, *tables, state0, gn.reshape(RET_HEADS, 1, RET_DV))
    return o, st


def _ret_decode_kernel(q_ref, k_ref, v_ref, gt_ref, c_ref, s1_ref, s2_ref, st0_ref, gn_ref, o_ref, st_ref):
    lgs = _log_gamma()
    half = RET_DK // 2
    row0 = lax.broadcasted_iota(I32, (16, 1), 0) == 0
    for h in range(RET_HEADS):
        gamma = float(np.exp(lgs[h]))
        q = _rope(q_ref[:, h * RET_DK:(h + 1) * RET_DK], c_ref[...], s1_ref[...], s2_ref[...], half)
        k = _rope(k_ref[:, h * RET_DK:(h + 1) * RET_DK], c_ref[...], s1_ref[...], s2_ref[...], half)
        k = k * (RET_DK ** -0.5)
        v = v_ref[:, h * RET_DV:(h + 1) * RET_DV]
        state = st0_ref[h].astype(F32)
        inner = jnp.sum(q * k, axis=-1, keepdims=True)
        q16 = jnp.broadcast_to(q, (16, RET_DK)).astype(BF16)
        cross = jnp.dot(q16, state.astype(BF16), preferred_element_type=F32)[0:1]
        o = inner * v + cross * gamma
        k16 = jnp.where(row0, jnp.broadcast_to(k, (16, RET_DK)), 0.0).astype(BF16)
        v16 = jnp.broadcast_to(v, (16, RET_DV)).astype(BF16)
        st_ref[h] = gamma * state + _tn_dot(k16, v16)
        o_ref[:, h * RET_DV:(h + 1) * RET_DV] = _ret_finish(
            o, gn_ref[h], gt_ref[:, h * RET_DV:(h + 1) * RET_DV])


def _retention_decode(z3, tables, state0, gn):
    b = z3.shape[0]
    wq, wv = RET_HEADS * RET_DK, RET_HEADS * RET_DV
    tspec = pl.BlockSpec((1, RET_DK), lambda bi: (0, 0))
    o, st = pl.pallas_call(
        _ret_decode_kernel,
        grid=(b,),
        in_specs=[pl.BlockSpec((None, 1, wq), lambda bi: (bi, 0, C_RQ // wq)),
                  pl.BlockSpec((None, 1, wq), lambda bi: (bi, 0, C_RK // wq)),
                  pl.BlockSpec((None, 1, wv), lambda bi: (bi, 0, C_RV // wv)),
                  pl.BlockSpec((None, 1, wv), lambda bi: (bi, 0, C_RG // wv)),
                  tspec, tspec, tspec,
                  pl.BlockSpec((None, RET_HEADS, RET_DK, RET_DV), lambda bi: (bi, 0, 0, 0)),
                  pl.BlockSpec((RET_HEADS, 1, RET_DV), lambda bi: (0, 0, 0))],
        out_specs=[pl.BlockSpec((None, 1, wv), lambda bi: (bi, 0, 0)),
                   pl.BlockSpec((None, RET_HEADS, RET_DK, RET_DV), lambda bi: (bi, 0, 0, 0))],
        out_shape=[jax.ShapeDtypeStruct((b, 1, wv), F32),
                   jax.ShapeDtypeStruct((b, RET_HEADS, RET_DK, RET_DV), F32)],
        compiler_params=_cp("arbitrary"),
    )(z3, z3, z3, z3, *tables, state0, gn.reshape(RET_HEADS, 1, RET_DV))
    return o, st


def _compress_weights(pos_emb, w):
    g, dh = NSA_KV_HEADS, HEAD_DIM
    eye = jnp.eye(g, dtype=w.dtype)
    wbig = jnp.einsum("lde,gh->lgdhe", w, eye).reshape(CMP_BLOCK * g * dh, g * dh)
    pos_row = jnp.broadcast_to(pos_emb[:, None, :], (CMP_BLOCK, g, dh)).reshape(1, CMP_BLOCK * g * dh)
    return pos_row, wbig.astype(BF16)


def _masked_softmax_parts(parts, masks):
    sm = [jnp.where(m, s, NEG_INF) for s, m in zip(parts, masks)]
    mx = functools.reduce(jnp.maximum, [jnp.max(s, axis=-1, keepdims=True) for s in sm])
    ex = [jnp.exp(s - mx) for s in sm]
    den = functools.reduce(lambda a, b2: a + b2, [jnp.sum(e, axis=-1, keepdims=True) for e in ex])
    inv = 1.0 / den
    return [jnp.where(m, e * inv, 0.0) for e, m in zip(ex, masks)]


def _select_blocks(score, nsel):
    nbs = score.shape[-1]
    lane = lax.broadcasted_iota(I32, score.shape, 1)
    sel = jnp.zeros(score.shape, jnp.bool_)
    picks = []
    for _ in range(nsel):
        m = jnp.max(score, axis=-1, keepdims=True)
        idx = jnp.min(jnp.where(score == m, lane, nbs), axis=-1, keepdims=True)
        hit = lane == idx
        sel = jnp.logical_or(sel, hit)
        score = jnp.where(hit, -jnp.inf, score)
        picks.append(idx)
    return sel, picks


def _nsa_prompt_kernel(q_ref, ck_ref, cv_ref, ks_ref, vs_ref, kw_ref, vw_ref, gt_ref, o_ref):
    i = pl.program_id(1)
    qb = q_ref.shape[0]
    t = ks_ref.shape[0]
    nbs = t // SEL_BLOCK
    hg, dh = HEADS_PER_GROUP, HEAD_DIM
    kvw = NSA_KV_HEADS * dh
    scale = dh ** -0.5
    pos1 = i * qb + lax.broadcasted_iota(I32, (qb, 1), 0)
    pos = jnp.concatenate([pos1] * hg, axis=0)
    blk = lax.broadcasted_iota(I32, (1, nbs), 1)
    vis_e = blk * SEL_BLOCK + (CMP_BLOCK - 1) <= pos
    vis_o = blk * SEL_BLOCK + (SEL_BLOCK - 1) <= pos
    valid = blk * SEL_BLOCK <= pos1
    forced = jnp.logical_or(blk == 0, blk == pos1 // SEL_BLOCK)
    key = lax.broadcasted_iota(I32, (1, t), 1)
    expand = (lax.broadcasted_iota(I32, (nbs, t), 1) // SEL_BLOCK
              == lax.broadcasted_iota(I32, (nbs, t), 0)).astype(BF16)
    causal = key <= pos
    wlen = WINDOW + qb
    wstart = pl.multiple_of(jnp.maximum(i * qb - WINDOW, 0), qb)
    wkey = wstart + lax.broadcasted_iota(I32, (1, wlen), 1)
    wdist = pos - wkey
    wmask = jnp.logical_and(wdist >= 0, wdist <= WINDOW)
    gates = gt_ref[...]

    for g in range(NSA_KV_HEADS):
        cs = slice(g * dh, (g + 1) * dh)
        q = jnp.concatenate([q_ref[:, (g * hg + hh) * dh:(g * hg + hh + 1) * dh] for hh in range(hg)],
                            axis=0).astype(BF16)
        co = slice(kvw + g * dh, kvw + (g + 1) * dh)
        ck_e, ck_o = ck_ref[:, cs].astype(BF16), ck_ref[:, co].astype(BF16)
        cv_e, cv_o = cv_ref[:, cs].astype(BF16), cv_ref[:, co].astype(BF16)
        p_e, p_o = _masked_softmax_parts([_nt_dot(q, ck_e) * scale, _nt_dot(q, ck_o) * scale], [vis_e, vis_o])
        o_cmp = (jnp.dot(p_e.astype(BF16), cv_e, preferred_element_type=F32)
                 + jnp.dot(p_o.astype(BF16), cv_o, preferred_element_type=F32))
        psum = p_e + p_o
        imp = functools.reduce(lambda a, b2: a + b2, [psum[hh * qb:(hh + 1) * qb] for hh in range(hg)])
        score = jnp.where(forced, FORCE, jnp.where(valid, imp, -FORCE))
        sel, _ = _select_blocks(score, min(SEL_TOPK, nbs))
        selk = jnp.dot(jnp.where(sel, 1.0, 0.0).astype(BF16), expand, preferred_element_type=F32)
        smask = jnp.logical_and(jnp.concatenate([selk] * hg, axis=0) > 0.5, causal)
        (p_s,) = _masked_softmax_parts([_nt_dot(q, ks_ref[:, cs].astype(BF16)) * scale], [smask])
        o_sel = jnp.dot(p_s.astype(BF16), vs_ref[:, cs].astype(BF16), preferred_element_type=F32)
        kwin = kw_ref[pl.ds(wstart, wlen), cs].astype(BF16)
        vwin = vw_ref[pl.ds(wstart, wlen), cs].astype(BF16)
        (p_w,) = _masked_softmax_parts([_nt_dot(q, kwin) * scale], [wmask])
        o_win = jnp.dot(p_w.astype(BF16), vwin, preferred_element_type=F32)
        for hh in range(hg):
            hd = g * hg + hh
            rs = slice(hh * qb, (hh + 1) * qb)
            o_ref[:, hd * dh:(hd + 1) * dh] = (gates[:, 3 * hd:3 * hd + 1] * o_cmp[rs]
                                                 + gates[:, 3 * hd + 1:3 * hd + 2] * o_sel[rs]
                                                 + gates[:, 3 * hd + 2:3 * hd + 3] * o_win[rs])


def _nsa_prompt(qn3, ck3, cv3, ksn3, z3, kwn3, gates3):
    b, t, _ = qn3.shape
    qb = Q_BLOCK
    kvw = NSA_KV_HEADS * HEAD_DIM
    assert t % qb == 0 and t >= WINDOW + qb and t % (2 * SEL_BLOCK) == 0
    nbs = ck3.shape[1]
    full = lambda cb: pl.BlockSpec((None, t, kvw), lambda bi, i: (bi, 0, cb))
    return pl.pallas_call(
        _nsa_prompt_kernel,
        grid=(b, t // qb),
        in_specs=[pl.BlockSpec((None, qb, NSA_HEADS * HEAD_DIM), lambda bi, i: (bi, i, 0)),
                  pl.BlockSpec((None, nbs, CMP_PER_SEL * kvw), lambda bi, i: (bi, 0, 0)),
                  pl.BlockSpec((None, nbs, CMP_PER_SEL * kvw), lambda bi, i: (bi, 0, 0)),
                  full(0), full(C_VS // kvw), full(0), full(C_VW // kvw),
                  pl.BlockSpec((None, qb, LANES), lambda bi, i: (bi, i, 0))],
        out_specs=pl.BlockSpec((None, qb, NSA_HEADS * HEAD_DIM), lambda bi, i: (bi, i, 0)),
        out_shape=jax.ShapeDtypeStruct((b, t, NSA_HEADS * HEAD_DIM), F32),
        compiler_params=_cp("arbitrary", "arbitrary"),
    )(qn3, ck3, cv3, ksn3, z3, kwn3, z3, gates3)


def _decode_q16(q_ref):
    dh = HEAD_DIM
    rows = [q_ref[:, h * dh:(h + 1) * dh] for h in range(NSA_HEADS)]
    return jnp.concatenate(rows + [jnp.zeros((16 - NSA_HEADS, dh), F32)], axis=0).astype(BF16)


def _nsa_decode_select_kernel(q_ref, ck_ref, cv_ref, ocmp_ref, idx_ref, *, past):
    dh, hg = HEAD_DIM, HEADS_PER_GROUP
    kvw = NSA_KV_HEADS * dh
    nhalf = ck_ref.shape[0]
    nbs = past // SEL_BLOCK + 1
    scale = dh ** -0.5
    q16 = _decode_q16(q_ref)
    row = lax.broadcasted_iota(I32, (16, 1), 0)
    blk = lax.broadcasted_iota(I32, (1, LANES), 1)
    blk_h = lax.broadcasted_iota(I32, (1, nhalf), 1)
    vis_e = blk_h * SEL_BLOCK + (CMP_BLOCK - 1) <= past
    vis_o = blk_h * SEL_BLOCK + (SEL_BLOCK - 1) <= past
    o_all = jnp.zeros((16, dh), F32)
    idx_row = jnp.zeros((1, LANES), I32)
    for g in range(NSA_KV_HEADS):
        cs = slice(g * dh, (g + 1) * dh)
        co = slice(kvw + g * dh, kvw + (g + 1) * dh)
        ck_e, ck_o = ck_ref[:, cs].astype(BF16), ck_ref[:, co].astype(BF16)
        cv_e, cv_o = cv_ref[:, cs].astype(BF16), cv_ref[:, co].astype(BF16)
        p_e, p_o = _masked_softmax_parts([_nt_dot(q16, ck_e) * scale, _nt_dot(q16, ck_o) * scale], [vis_e, vis_o])
        o_g = (jnp.dot(p_e.astype(BF16), cv_e, preferred_element_type=F32)
               + jnp.dot(p_o.astype(BF16), cv_o, preferred_element_type=F32))
        in_g = jnp.logical_and(row >= g * hg, row < (g + 1) * hg)
        o_all = jnp.where(in_g, o_g, o_all)
        imp = jnp.sum(jnp.where(in_g, p_e + p_o, 0.0), axis=0, keepdims=True)
        score = jnp.where(blk_h == 0, FORCE, imp)
        nsel = min(SEL_TOPK, nbs)
        _, past_picks = _select_blocks(score, min(nsel - 1, nhalf))
        picks = past_picks[:1] + [jnp.full((1, 1), nhalf, I32)] + past_picks[1:]
        for kk, p in enumerate(picks):
            idx_row = jnp.where(blk == g * SEL_TOPK + kk, p, idx_row)
        for kk in range(len(picks), SEL_TOPK):
            idx_row = jnp.where(blk == g * SEL_TOPK + kk, -1, idx_row)
    ocmp_ref[...] = o_all
    idx_ref[...] = idx_row


def _nsa_decode_select(qn3, ck3, cv3, past):
    b = qn3.shape[0]
    nhalf = ck3.shape[1]
    kvw = NSA_KV_HEADS * HEAD_DIM
    assert nhalf % 8 == 0 and nhalf == past // SEL_BLOCK
    return pl.pallas_call(
        functools.partial(_nsa_decode_select_kernel, past=past),
        grid=(b,),
        in_specs=[pl.BlockSpec((None, 1, NSA_HEADS * HEAD_DIM), lambda bi: (bi, 0, 0)),
                  pl.BlockSpec((None, nhalf, CMP_PER_SEL * kvw), lambda bi: (bi, 0, 0)),
                  pl.BlockSpec((None, nhalf, CMP_PER_SEL * kvw), lambda bi: (bi, 0, 0))],
        out_specs=[pl.BlockSpec((None, 16, HEAD_DIM), lambda bi: (bi, 0, 0)),
                   pl.BlockSpec((None, 1, LANES), lambda bi: (bi, 0, 0))],
        out_shape=[jax.ShapeDtypeStruct((b, 16, HEAD_DIM), F32),
                   jax.ShapeDtypeStruct((b, 1, LANES), I32)],
        compiler_params=_cp("arbitrary"),
    )(qn3, ck3, cv3)


def _nsa_decode_attend_kernel(*refs, past, nsel):
    n_blk = NSA_KV_HEADS * nsel
    rows_ref = refs[0]
    kb_refs = refs[1:1 + n_blk]
    vb_refs = refs[1 + n_blk:1 + 2 * n_blk]
    (q_ref, idx_ref, ksn_ref, vs_ref, wk_ref, wv_ref, kwn_ref, vw_ref, ocmp_ref, gt_ref, o_ref) = refs[1 + 2 * n_blk:]
    del rows_ref
    dh, hg = HEAD_DIM, HEADS_PER_GROUP
    scale = dh ** -0.5
    self_blk = past // SEL_BLOCK
    q16 = _decode_q16(q_ref)
    q16f = q16.astype(F32)
    row = lax.broadcasted_iota(I32, (16, 1), 0)
    lane = lax.broadcasted_iota(I32, (1, LANES), 1)
    idx_row = idx_ref[...]
    o_sel = jnp.zeros((16, dh), F32)
    o_win = jnp.zeros((16, dh), F32)
    nbuf = wk_ref.shape[0]
    for g in range(NSA_KV_HEADS):
        cs = slice(g * dh, (g + 1) * dh)
        in_g = jnp.logical_and(row >= g * hg, row < (g + 1) * hg)
        k_self = ksn_ref[:, cs].astype(BF16).astype(F32)
        v_self = vs_ref[:, cs].astype(BF16).astype(F32)
        s_self = jnp.sum(q16f * k_self, axis=-1, keepdims=True) * scale
        parts, masks, vals = [], [], []
        self_sel = jnp.zeros((1, 1), jnp.bool_)
        for kk in range(nsel):
            pick = jnp.max(jnp.where(lane == g * SEL_TOPK + kk, idx_row, -1), axis=-1, keepdims=True)
            in_pool = jnp.logical_and(pick >= 0, pick < self_blk)
            self_sel = jnp.logical_or(self_sel, pick == self_blk)
            kb = kb_refs[g * nsel + kk][:, cs].astype(BF16)
            parts.append(_nt_dot(q16, kb) * scale)
            masks.append(in_pool)
            vals.append(vb_refs[g * nsel + kk][:, cs].astype(BF16))
        parts.append(s_self)
        masks.append(self_sel)
        probs = _masked_softmax_parts(parts, masks)
        acc = probs[-1].astype(BF16).astype(F32) * v_self
        for p, vv in zip(probs[:-1], vals):
            acc = acc + jnp.dot(p.astype(BF16), vv, preferred_element_type=F32)
        o_sel = jnp.where(in_g, acc, o_sel)
        kw_self = kwn_ref[:, cs].astype(BF16).astype(F32)
        vw_self = vw_ref[:, cs].astype(BF16).astype(F32)
        sw_self = jnp.sum(q16f * kw_self, axis=-1, keepdims=True) * scale
        wparts = [_nt_dot(q16, wk_ref[:, cs].astype(BF16)) * scale, sw_self]
        wmasks = [lane[:, :1] >= 0, lane[:, :1] >= 0]
        pw, pws = _masked_softmax_parts(wparts, wmasks)
        accw = (jnp.dot(pw.astype(BF16), wv_ref[:, cs].astype(BF16), preferred_element_type=F32)
                + pws.astype(BF16).astype(F32) * vw_self)
        o_win = jnp.where(in_g, accw, o_win)
    gates = gt_ref[...]
    o_cmp = ocmp_ref[...]
    for hd in range(NSA_HEADS):
        o_ref[:, hd * dh:(hd + 1) * dh] = (gates[:, 3 * hd:3 * hd + 1] * o_cmp[hd:hd + 1]
                                             + gates[:, 3 * hd + 1:3 * hd + 2] * o_sel[hd:hd + 1]
                                             + gates[:, 3 * hd + 2:3 * hd + 3] * o_win[hd:hd + 1])


def _nsa_decode_attend(pool_rows, sel_k_blocks, sel_v_blocks, qn3, idx3, ksn3, z3, wk3, wv3, kwn3, ocmp, gates3,
                       past, nsel):
    b = qn3.shape[0]
    kvw = NSA_KV_HEADS * HEAD_DIM
    n_blk = NSA_KV_HEADS * nsel
    nbuf = wk3.shape[1]

    def blk_spec(tt):
        return pl.BlockSpec((None, SEL_BLOCK, kvw), lambda bi, r: (r[bi * n_blk + tt], 0, 0))

    def row_spec(cb, w=kvw):
        return pl.BlockSpec((None, 1, w), lambda bi, r: (bi, 0, cb))

    in_specs = ([blk_spec(tt) for tt in range(n_blk)] + [blk_spec(tt) for tt in range(n_blk)]
                + [pl.BlockSpec((None, 1, NSA_HEADS * HEAD_DIM), lambda bi, r: (bi, 0, 0)),
                   pl.BlockSpec((None, 1, LANES), lambda bi, r: (bi, 0, 0)),
                   row_spec(0), row_spec(C_VS // kvw),
                   pl.BlockSpec((None, nbuf, kvw), lambda bi, r: (bi, 0, 0)),
                   pl.BlockSpec((None, nbuf, kvw), lambda bi, r: (bi, 0, 0)),
                   row_spec(0), row_spec(C_VW // kvw),
                   pl.BlockSpec((None, 16, HEAD_DIM), lambda bi, r: (bi, 0, 0)),
                   pl.BlockSpec((None, 1, LANES), lambda bi, r: (bi, 0, 0))])
    return pl.pallas_call(
        functools.partial(_nsa_decode_attend_kernel, past=past, nsel=nsel),
        grid_spec=pltpu.PrefetchScalarGridSpec(
            num_scalar_prefetch=1, grid=(b,), in_specs=in_specs,
            out_specs=pl.BlockSpec((None, 1, NSA_HEADS * HEAD_DIM), lambda bi, r: (bi, 0, 0))),
        out_shape=jax.ShapeDtypeStruct((b, 1, NSA_HEADS * HEAD_DIM), F32),
        compiler_params=_cp("arbitrary"),
    )(pool_rows, *([sel_k_blocks] * n_blk), *([sel_v_blocks] * n_blk),
      qn3, idx3, ksn3, z3, wk3, wv3, kwn3, z3, ocmp, gates3)


def _mem_attn_kernel(q_ref, g_ref, mk_ref, mv_ref, o_ref):
    dh = MEM_HEAD_DIM
    tq = q_ref.shape[0]
    rows = max(tq, 16)
    for h in range(MEM_HEADS):
        cs = slice(h * dh, (h + 1) * dh)
        x = q_ref[:, cs]
        q = x * lax.rsqrt(jnp.mean(x * x, axis=-1, keepdims=True) + EPS) * g_ref[...]
        if rows != tq:
            q = jnp.broadcast_to(q, (rows, dh))
        s = _nt_dot(q.astype(BF16), mk_ref[:, cs].astype(BF16)) * (dh ** -0.5)
        m = jnp.max(s, axis=-1, keepdims=True)
        e = jnp.exp(s - m)
        p = e / jnp.sum(e, axis=-1, keepdims=True)
        o = jnp.dot(p.astype(BF16), mv_ref[:, cs].astype(BF16), preferred_element_type=F32)
        o_ref[:, cs] = o[0:tq]


def _mem_attention(z3, gq, mk3, mv3):
    b, t, _ = z3.shape
    w = MEM_HEADS * MEM_HEAD_DIM
    tq = min(t, 512)
    m = mk3.shape[1]
    assert t % tq == 0
    return pl.pallas_call(
        _mem_attn_kernel,
        grid=(b, t // tq),
        in_specs=[pl.BlockSpec((None, tq, w), lambda bi, i: (bi, i, C_MQ // w)),
                  pl.BlockSpec((1, MEM_HEAD_DIM), lambda bi, i: (0, 0)),
                  pl.BlockSpec((None, m, w), lambda bi, i: (bi, 0, 0)),
                  pl.BlockSpec((None, m, w), lambda bi, i: (bi, 0, 0))],
        out_specs=pl.BlockSpec((None, tq, w), lambda bi, i: (bi, i, 0)),
        out_shape=jax.ShapeDtypeStruct((b, t, w), F32),
        compiler_params=_cp("arbitrary", "arbitrary"),
    )(z3, gq.reshape(1, MEM_HEAD_DIM), mk3, mv3)


def _sigmoid_cols_kernel(x_ref, o_ref):
    o_ref[...] = jax.nn.sigmoid(x_ref[...])


def _nsa_gates(z, col):
    n = z.shape[0]
    tm = min(n, 1024)
    return pl.pallas_call(
        _sigmoid_cols_kernel, grid=(n // tm,),
        in_specs=[pl.BlockSpec((tm, LANES), lambda i: (i, col // LANES))],
        out_specs=pl.BlockSpec((tm, LANES), lambda i: (i, 0)),
        out_shape=jax.ShapeDtypeStruct((n, LANES), F32),
        compiler_params=_cp("arbitrary"),
    )(z)


def _merge_kernel(r_ref, n_ref, m_ref, g0_ref, g1_ref, g2_ref, w_ref, o_ref):
    acc = jnp.zeros(o_ref.shape, F32)
    for c, (b_ref, g_ref) in enumerate(((r_ref, g0_ref), (n_ref, g1_ref), (m_ref, g2_ref))):
        up = jnp.dot(b_ref[...].astype(BF16), w_ref[c], preferred_element_type=F32)
        acc = acc + jax.nn.sigmoid(g_ref[...]) * up
    o_ref[...] = acc.astype(BF16)


def _merge(o_ret, o_nsa, o_mem, z, wb, d_model):
    n, bw = o_ret.shape
    tm = min(n, 512)
    tn = 512
    gb = C_MG // tn
    nj = d_model // tn
    bspec = pl.BlockSpec((tm, bw), lambda i, j: (i, 0))
    gspec = lambda c: pl.BlockSpec((tm, tn), lambda i, j: (i, gb + c * nj + j))
    return pl.pallas_call(
        _merge_kernel, grid=(n // tm, nj),
        in_specs=[bspec, bspec, bspec, gspec(0), gspec(1), gspec(2),
                  pl.BlockSpec((N_BRANCH, bw, tn), lambda i, j: (0, 0, j))],
        out_specs=pl.BlockSpec((tm, tn), lambda i, j: (i, j)),
        out_shape=jax.ShapeDtypeStruct((n, d_model), BF16),
        compiler_params=_cp("arbitrary", "arbitrary"),
    )(o_ret, o_nsa, o_mem, z, z, z, wb)


def _topk_rows(s, kk):
    e = s.shape[0]
    ridx = lax.broadcasted_iota(I32, s.shape, 0)
    vals, idxs = [], []
    for _ in range(kk):
        m = jnp.max(s, axis=0, keepdims=True)
        i = jnp.min(jnp.where(s == m, ridx, e), axis=0, keepdims=True)
        vals.append(m)
        idxs.append(i)
        s = jnp.where(ridx == i, -jnp.inf, s)
    return jnp.concatenate(vals, axis=0), jnp.concatenate(idxs, axis=0)


def _peer_route_kernel(q_ref, sk_ref, i_ref, j_ref, g_ref):
    kk = PEER_TOPK
    tl = q_ref.shape[0]
    st = _nt_dot(sk_ref[...], q_ref[...].astype(BF16))
    v0, i0 = _topk_rows(st[0:PEER_KEYS], kk)
    v1, i1 = _topk_rows(st[PEER_KEYS:2 * PEER_KEYS], kk)
    rep = lambda a: jnp.concatenate([jnp.broadcast_to(a[r:r + 1], (kk, tl)) for r in range(kk)], axis=0)
    tile = lambda a: jnp.concatenate([a] * kk, axis=0)
    cand = rep(v0) + tile(v1)
    ci, cj = rep(i0), tile(i1)
    ridx = lax.broadcasted_iota(I32, cand.shape, 0)
    sc, si, sj = [], [], []
    for _ in range(kk):
        m = jnp.max(cand, axis=0, keepdims=True)
        r = jnp.min(jnp.where(cand == m, ridx, kk * kk), axis=0, keepdims=True)
        hit = ridx == r
        sc.append(m)
        si.append(jnp.max(jnp.where(hit, ci, -1), axis=0, keepdims=True))
        sj.append(jnp.max(jnp.where(hit, cj, -1), axis=0, keepdims=True))
        cand = jnp.where(hit, -jnp.inf, cand)
    sc = jnp.concatenate(sc, axis=0)
    e = jnp.exp(sc - jnp.max(sc, axis=0, keepdims=True))
    g_ref[...] = e / jnp.sum(e, axis=0, keepdims=True)
    i_ref[...] = jnp.concatenate(si, axis=0)
    j_ref[...] = jnp.concatenate(sj, axis=0)


def _peer_route(q, skbd):
    n = q.shape[0]
    tl = min(n, LANES)
    kk = PEER_TOPK
    slots = PEER_HEADS * kk
    ospec = pl.BlockSpec((kk, tl), lambda i, h: (h, i))
    return pl.pallas_call(
        _peer_route_kernel, grid=(n // tl, PEER_HEADS),
        in_specs=[pl.BlockSpec((tl, PEER_DKEY), lambda i, h: (i, h)),
                  pl.BlockSpec((None, 2 * PEER_KEYS, PEER_DKEY), lambda i, h: (h, 0, 0))],
        out_specs=[ospec, ospec, ospec],
        out_shape=[jax.ShapeDtypeStruct((slots, n), I32), jax.ShapeDtypeStruct((slots, n), I32),
                   jax.ShapeDtypeStruct((slots, n), F32)],
        compiler_params=_cp("arbitrary", "arbitrary"),
    )(q, skbd)


def _peer_weights_kernel(i_ref, j_ref, g_ref, o_ref, it_scr, jt_scr, gt_scr):
    tb = o_ref.shape[0]
    it_scr[...] = i_ref[...].T
    jt_scr[...] = j_ref[...].T
    gt_scr[...] = g_ref[...].T
    key = lax.broadcasted_iota(I32, (PEER_KEYS, i_ref.shape[0]), 0)

    def body(n, carry):
        irow = it_scr[pl.ds(n, 1), :]
        jrow = jt_scr[pl.ds(n, 1), :]
        grow = gt_scr[pl.ds(n, 1), :]
        a = jnp.where(key == irow, grow, 0.0).astype(BF16)
        b = jnp.where(key == jrow, 1.0, 0.0).astype(BF16)
        o_ref[n] = _nt_dot(a, b).astype(BF16)
        return carry

    lax.fori_loop(0, tb, body, 0)


def _peer_weights(it, jt, gt):
    slots, n = it.shape
    tb = min(n, LANES)
    ispec = pl.BlockSpec((slots, tb), lambda i: (0, i))
    return pl.pallas_call(
        _peer_weights_kernel, grid=(n // tb,),
        in_specs=[ispec, ispec, ispec],
        out_specs=pl.BlockSpec((tb, PEER_KEYS, PEER_KEYS), lambda i: (i, 0, 0)),
        out_shape=jax.ShapeDtypeStruct((n, PEER_KEYS, PEER_KEYS), BF16),
        scratch_shapes=[pltpu.VMEM((tb, slots), I32), pltpu.VMEM((tb, slots), I32), pltpu.VMEM((tb, slots), F32)],
        compiler_params=_cp("arbitrary"),
    )(it, jt, gt)


def _peer_dense_kernel(h_ref, g_ref, u_ref, v_ref, x_ref, o_ref, acc_ref):
    j = pl.program_id(1)

    @pl.when(j == 0)
    def _():
        acc_ref[...] = jnp.zeros_like(acc_ref)

    a = jnp.dot(h_ref[...], u_ref[...], preferred_element_type=F32)
    act = 0.5 * a * (1.0 + lax.erf(a * (2.0 ** -0.5)))
    w = (g_ref[...].astype(F32) * act).astype(BF16)
    acc_ref[...] += jnp.dot(w, v_ref[...], preferred_element_type=F32)

    @pl.when(j == pl.num_programs(1) - 1)
    def _():
        o_ref[...] = x_ref[...] + acc_ref[...]


def _peer_dense(hn, gw, ut, vv, x):
    n, d = hn.shape
    ne = ut.shape[1]
    tm = min(n, 512)
    te = 512
    return pl.pallas_call(
        _peer_dense_kernel, grid=(n // tm, ne // te),
        in_specs=[pl.BlockSpec((tm, d), lambda i, j: (i, 0)),
                  pl.BlockSpec((tm, te), lambda i, j: (i, j)),
                  pl.BlockSpec((d, te), lambda i, j: (0, j)),
                  pl.BlockSpec((te, d), lambda i, j: (j, 0)),
                  pl.BlockSpec((tm, d), lambda i, j: (i, 0))],
        out_specs=pl.BlockSpec((tm, d), lambda i, j: (i, 0)),
        out_shape=jax.ShapeDtypeStruct((n, d), F32),
        scratch_shapes=[pltpu.VMEM((tm, d), F32)],
        compiler_params=_cp("arbitrary", "arbitrary"),
    )(hn, gw, ut, vv, x)


def _prepare_weights(norm_attn, w_in, cmp_pos_k, cmp_pos_v, cmp_w_k, cmp_w_v, w_mem_kv, w_branch, w_out,
                     peer_wq, peer_subkeys, peer_u, peer_v):
    d = w_in.shape[0]
    ng_w = NSA_HEADS * 3
    o_ng = 5632
    o_mq = o_ng + ng_w
    o_mg = o_mq + MEM_HEADS * MEM_HEAD_DIM
    width = C_MG + N_BRANCH * d + LANES
    width = -(-width // Z_TILE) * Z_TILE
    w_r = jnp.concatenate([w_in[:, 0:4096], w_in[:, o_mq:o_mg], w_in[:, 4096:o_ng], w_in[:, o_mg:],
                           w_in[:, o_ng:o_mq],
                           jnp.zeros((d, width - (C_MG + N_BRANCH * d + ng_w)), w_in.dtype)], axis=1).astype(BF16)
    pos_k, wk_big = _compress_weights(cmp_pos_k, cmp_w_k)
    pos_v, wv_big = _compress_weights(cmp_pos_v, cmp_w_v)
    half = PEER_DKEY // 2
    sk = peer_subkeys.astype(BF16)
    zk = jnp.zeros((PEER_HEADS, PEER_KEYS, half), BF16)
    skbd = jnp.concatenate([jnp.concatenate([sk[:, 0], zk], axis=2),
                            jnp.concatenate([zk, sk[:, 1]], axis=2)], axis=1)
    return dict(w_in=w_r, c_ng=C_MG + N_BRANCH * d, pos_k=pos_k, wk_big=wk_big, pos_v=pos_v, wv_big=wv_big,
                w_mem_kv=w_mem_kv.astype(BF16), w_branch=w_branch.astype(BF16), w_out=w_out.astype(BF16),
                peer_wq=peer_wq.astype(BF16), skbd=skbd, ut=peer_u.astype(BF16).T, vv=peer_v.astype(BF16))


def _layer_common(x, seq, pos, norm_attn, nsa_q_norm, nsa_k_norm, pw):
    z = _prologue_matmul(x, norm_attn, pw["w_in"], mode="norm")
    nsa_tabs = _rope_tables(pos, ROPE_DIMS, ROPE_THETA, HEAD_DIM)
    half = ROPE_DIMS // 2
    kvw = NSA_KV_HEADS * HEAD_DIM
    qn = _headnorm(z, C_NQ, NSA_HEADS * HEAD_DIM, nsa_q_norm, HEAD_DIM, seq, nsa_tabs, half)
    kcn = _headnorm(z, C_KC, kvw, nsa_k_norm[0], HEAD_DIM, seq, nsa_tabs, half)
    ksn = _headnorm(z, C_KS, kvw, nsa_k_norm[1], HEAD_DIM, seq, nsa_tabs, half)
    kwn = _headnorm(z, C_KW, kvw, nsa_k_norm[2], HEAD_DIM, seq, nsa_tabs, half)
    gates = _nsa_gates(z, pw["c_ng"])
    ret_tabs = _rope_tables(pos, RET_DK, RET_THETA, RET_DK)
    return z, qn, kcn, ksn, kwn, gates, ret_tabs


def _layer_tail(x, z, o_ret, o_nsa, o_mem, norm_ffn, pw):
    d = x.shape[1]
    merged = _merge(o_ret, o_nsa, o_mem, z, pw["w_branch"], d)
    x1 = _prologue_matmul(merged, jnp.ones((d,), F32), pw["w_out"], mode="none", residual=x)
    q, hn = _prologue_matmul(x1, norm_ffn, pw["peer_wq"], mode="norm", emit_h=True)
    it, jt, gt = _peer_route(q, pw["skbd"])
    gw = _peer_weights(it, jt, gt).reshape(x.shape[0], PEER_KEYS * PEER_KEYS)
    return _peer_dense(hn, gw, pw["ut"], pw["vv"], x1)


def kernel(x_prompt, x_sample, cache_cmp_k, cache_cmp_v, cache_sel_k, cache_sel_v, cache_win_k, cache_win_v,
           state_ret, cache_mem_k, cache_mem_v, page_table, mem_prompt, norm_attn, w_in, ret_gn, nsa_q_norm,
           nsa_k_norm, cmp_pos_k, cmp_pos_v, cmp_w_k, cmp_w_v, norm_mem, w_mem_kv, mem_q_norm, mem_k_norm,
           w_branch, w_out, norm_ffn, peer_wq, peer_subkeys, peer_u, peer_v):
    depth = w_in.shape[0]
    assert depth == 1
    l = 0
    bp, t, d = x_prompt.shape
    bs, ts, _ = x_sample.shape
    assert ts == 1
    n_pool, page, g_kv, dh = cache_cmp_k.shape[1:]
    n_pages = page_table.shape[1]
    past = n_pages * page
    kvw = g_kv * dh
    memw = MEM_HEADS * MEM_HEAD_DIM
    pw = _prepare_weights(norm_attn[l], w_in[l], cmp_pos_k[l], cmp_pos_v[l], cmp_w_k[l], cmp_w_v[l], w_mem_kv[l],
                          w_branch[l], w_out[l], peer_wq[l], peer_subkeys[l], peer_u[l], peer_v[l])
    cmp_cols = CMP_BLOCK * kvw

    n_p = bp * t
    xp = x_prompt.reshape(n_p, d)
    m_tok = mem_prompt.shape[1]
    mkv = _prologue_matmul(mem_prompt.reshape(bp * m_tok, d), norm_mem[l], pw["w_mem_kv"], mode="norm")
    mk_p = _headnorm(mkv, 0, memw, mem_k_norm[l], MEM_HEAD_DIM, m_tok)
    mv_p = mkv[:, memw:]
    z, qn, kcn, ksn, kwn, gates, ret_tabs = _layer_common(xp, t, jnp.arange(t), norm_attn[l], nsa_q_norm[l],
                                                         nsa_k_norm[l], pw)
    z3 = z.reshape(bp, t, z.shape[1])
    o_ret, st_p = _retention_prompt(z3, ret_tabs, jnp.zeros((bp, RET_HEADS, RET_DK, RET_DV), F32), ret_gn[l])
    vc_p = z[:, C_VC:C_VC + kvw]
    nbc = t // CMP_BLOCK
    ck_p = _prologue_matmul(kcn.reshape(bp * nbc, cmp_cols), pw["pos_k"], pw["wk_big"], mode="bias", tm=256)
    cv_p = _prologue_matmul(vc_p.reshape(bp * nbc, cmp_cols), pw["pos_v"], pw["wv_big"], mode="bias", tm=256)
    pair = CMP_PER_SEL * kvw
    o_nsa = _nsa_prompt(qn.reshape(bp, t, -1), ck_p.reshape(bp, nbc // CMP_PER_SEL, pair),
                        cv_p.reshape(bp, nbc // CMP_PER_SEL, pair),
                        ksn.reshape(bp, t, kvw), z3, kwn.reshape(bp, t, kvw), gates.reshape(bp, t, LANES))
    o_mem = _mem_attention(z3, mem_q_norm[l], mk_p.reshape(bp, m_tok, memw), mv_p.reshape(bp, m_tok, memw))
    y_p = _layer_tail(xp, z, o_ret.reshape(n_p, -1), o_nsa.reshape(n_p, -1), o_mem.reshape(n_p, -1), norm_ffn[l], pw)

    nbuf_p = min(WINDOW, t)
    kv5 = lambda a, bb, tt: a.reshape(1, bb, tt, g_kv, dh)
    outs_p = (kv5(kcn, bp, t), kv5(vc_p, bp, t), kv5(ksn, bp, t), kv5(z[:, C_VS:C_VS + kvw], bp, t),
              kv5(kwn, bp, t)[:, :, t - nbuf_p:], kv5(z[:, C_VW:C_VW + kvw], bp, t)[:, :, t - nbuf_p:],
              st_p[None], mk_p.reshape(1, bp, m_tok, MEM_HEADS, MEM_HEAD_DIM),
              mv_p.reshape(1, bp, m_tok, MEM_HEADS, MEM_HEAD_DIM))

    xs = x_sample.reshape(bs, d)
    zs, qn_s, kcn_s, ksn_s, kwn_s, gates_s, ret_tabs_s = _layer_common(
        xs, 1, jnp.full((1,), past, I32), norm_attn[l], nsa_q_norm[l], nsa_k_norm[l], pw)
    zs3 = zs.reshape(bs, 1, zs.shape[1])
    o_ret_s, st_s = _retention_decode(zs3, ret_tabs_s, state_ret[l], ret_gn[l])
    per_page = page // CMP_BLOCK
    ckp = _prologue_matmul(cache_cmp_k[l].reshape(n_pool * per_page, cmp_cols), pw["pos_k"], pw["wk_big"],
                           mode="bias", tm=256)
    cvp = _prologue_matmul(cache_cmp_v[l].reshape(n_pool * per_page, cmp_cols), pw["pos_v"], pw["wv_big"],
                           mode="bias", tm=256)
    ck_s = ckp.reshape(n_pool, per_page, kvw)[page_table].reshape(bs, n_pages * per_page // CMP_PER_SEL, pair)
    cv_s = cvp.reshape(n_pool, per_page, kvw)[page_table].reshape(bs, n_pages * per_page // CMP_PER_SEL, pair)
    qn_s3 = qn_s.reshape(bs, 1, -1)
    ocmp_s, idx_s = _nsa_decode_select(qn_s3, ck_s, cv_s, past)
    nbs = past // SEL_BLOCK + 1
    nsel = min(SEL_TOPK, nbs)
    per_page_sel = page // SEL_BLOCK
    picks = jnp.clip(idx_s[:, 0, :NSA_KV_HEADS * SEL_TOPK].reshape(bs, NSA_KV_HEADS, SEL_TOPK)[:, :, :nsel],
                     0, nbs - 2)
    pool_rows = (jnp.take_along_axis(page_table, (picks // per_page_sel).reshape(bs, -1), axis=1) * per_page_sel
                 + (picks % per_page_sel).reshape(bs, -1)).reshape(-1).astype(I32)
    wk3 = cache_win_k[l].reshape(bs, -1, kvw)
    wv3 = cache_win_v[l].reshape(bs, -1, kvw)
    o_nsa_s = _nsa_decode_attend(
        pool_rows, cache_sel_k[l].reshape(n_pool * per_page_sel, SEL_BLOCK, kvw),
        cache_sel_v[l].reshape(n_pool * per_page_sel, SEL_BLOCK, kvw),
        qn_s3, idx_s, ksn_s.reshape(bs, 1, kvw), zs3, wk3, wv3, kwn_s.reshape(bs, 1, kvw), ocmp_s,
        gates_s.reshape(bs, 1, LANES), past, nsel)
    o_mem_s = _mem_attention(zs3, mem_q_norm[l], cache_mem_k[l].reshape(bs, -1, memw),
                             cache_mem_v[l].reshape(bs, -1, memw))
    y_s = _layer_tail(xs, zs, o_ret_s.reshape(bs, -1), o_nsa_s.reshape(bs, -1), o_mem_s.reshape(bs, -1),
                      norm_ffn[l], pw)

    nbuf_s = wk3.shape[1]
    win_k_s = jnp.concatenate([wk3, kwn_s.reshape(bs, 1, kvw)], axis=1)[:, -nbuf_s:]
    win_v_s = jnp.concatenate([wv3, zs[:, C_VW:C_VW + kvw].reshape(bs, 1, kvw)], axis=1)[:, -nbuf_s:]
    outs_s = (kv5(kcn_s, bs, 1), kv5(zs[:, C_VC:C_VC + kvw], bs, 1), kv5(ksn_s, bs, 1),
              kv5(zs[:, C_VS:C_VS + kvw], bs, 1), kv5(win_k_s, bs, nbuf_s), kv5(win_v_s, bs, nbuf_s),
              st_s[None].astype(x_sample.dtype))
    return (y_p.reshape(bp, t, d), y_s.reshape(bs, ts, d)) + outs_p + outs_s
```

```python
import functools

import numpy as np
import jax
import jax.numpy as jnp
from jax import lax
from jax.experimental import pallas as pl
from jax.experimental.pallas import tpu as pltpu

F32 = jnp.float32
BF16 = jnp.bfloat16
I32 = jnp.int32

EPS = 1e-6
NEG_INF = -1e30
FORCE = 1e4

RET_HEADS, RET_DK, RET_DV, RET_CHUNK, RET_THETA = 4, 128, 256, 128, 10000.0
NSA_HEADS, NSA_KV_HEADS, HEAD_DIM = 8, 2, 128
HEADS_PER_GROUP = NSA_HEADS // NSA_KV_HEADS
CMP_BLOCK, SEL_BLOCK, SEL_TOPK, WINDOW, Q_BLOCK = 32, 64, 4, 256, 128
CMP_PER_SEL = SEL_BLOCK // CMP_BLOCK
ROPE_THETA, ROPE_DIMS = 500000.0, HEAD_DIM // 4
MEM_HEADS, MEM_HEAD_DIM = 4, 256
PEER_HEADS, PEER_KEYS, PEER_DKEY, PEER_TOPK = 8, 128, 128, 16
N_BRANCH = 3
LANES = 128

C_RQ, C_RK, C_RV, C_RG, C_NQ, C_MQ = 0, 512, 1024, 2048, 3072, 4096
C_KC, C_VC, C_KS, C_VS, C_KW, C_VW = 5120, 5376, 5632, 5888, 6144, 6400
C_MG = 6656
Z_TILE = 512


def _cp(*sem):
    return pltpu.CompilerParams(dimension_semantics=sem, vmem_limit_bytes=56 * 1024 * 1024)


def _nt_dot(a, b):
    return lax.dot_general(a, b, (((1,), (1,)), ((), ())), preferred_element_type=F32)


def _tn_dot(a, b):
    return lax.dot_general(a, b, (((0,), (0,)), ((), ())), preferred_element_type=F32)


def _pm_kernel(*refs, mode, residual, emit_h):
    x_ref, g_ref, w_ref = refs[:3]
    k = 3
    r_ref = None
    if residual:
        r_ref = refs[k]
        k += 1
    o_ref = refs[k]
    k += 1
    h_ref = None
    if emit_h:
        h_ref = refs[k]
        k += 1
    h_scr = refs[k]

    @pl.when(pl.program_id(1) == 0)
    def _():
        x = x_ref[...].astype(F32)
        if mode == "norm":
            h = x * lax.rsqrt(jnp.mean(x * x, axis=-1, keepdims=True) + EPS) * g_ref[...]
        elif mode == "bias":
            h = x + g_ref[...]
        else:
            h = x
        h_scr[...] = h.astype(BF16)
        if emit_h:
            h_ref[...] = h_scr[...]

    acc = jnp.dot(h_scr[...], w_ref[...], preferred_element_type=F32)
    if residual:
        acc = acc + r_ref[...]
    o_ref[...] = acc


def _prologue_matmul(x, g, w, *, mode, name, residual=None, emit_h=False, tm=512):
    n, kdim = x.shape
    wout = w.shape[1]
    tm = min(tm, n)
    tn = 512 if wout % 512 == 0 else (256 if wout % 256 == 0 else 128)
    assert n % tm == 0 and wout % tn == 0
    in_specs = [pl.BlockSpec((tm, kdim), lambda i, j: (i, 0)),
                pl.BlockSpec((1, kdim), lambda i, j: (0, 0)),
                pl.BlockSpec((kdim, tn), lambda i, j: (0, j))]
    args = [x, g.reshape(1, kdim).astype(F32), w]
    if residual is not None:
        in_specs.append(pl.BlockSpec((tm, tn), lambda i, j: (i, j)))
        args.append(residual)
    out_shape = [jax.ShapeDtypeStruct((n, wout), F32)]
    out_specs = [pl.BlockSpec((tm, tn), lambda i, j: (i, j))]
    if emit_h:
        out_shape.append(jax.ShapeDtypeStruct((n, kdim), BF16))
        out_specs.append(pl.BlockSpec((tm, kdim), lambda i, j: (i, 0)))
    res = pl.pallas_call(
        functools.partial(_pm_kernel, mode=mode, residual=residual is not None, emit_h=emit_h),
        grid=(n // tm, wout // tn),
        in_specs=in_specs, out_specs=out_specs, out_shape=out_shape,
        scratch_shapes=[pltpu.VMEM((tm, kdim), BF16)],
        compiler_params=_cp("arbitrary", "arbitrary"), name=name,
    )(*args)
    return res if emit_h else res[0]


def _rope_tables(pos, rot_dims, theta, dh):
    half = rot_dims // 2
    inv = theta ** (-jnp.arange(half, dtype=F32) * 2.0 / rot_dims)
    ang = pos.astype(F32)[:, None] * inv[None, :]
    cos, sin = jnp.cos(ang), jnp.sin(ang)
    t = pos.shape[0]
    one = jnp.ones((t, dh - rot_dims), F32)
    zh = jnp.zeros((t, half), F32)
    zr = jnp.zeros((t, dh - rot_dims), F32)
    c = jnp.concatenate([cos, cos, one], axis=1)
    s1 = jnp.concatenate([-sin, zh, zr], axis=1)
    s2 = jnp.concatenate([zh, sin, zr], axis=1)
    return c, s1, s2


def _rope(y, c, s1, s2, half):
    dh = y.shape[-1]
    return y * c + pltpu.roll(y, dh - half, 1) * s1 + pltpu.roll(y, half, 1) * s2


def _headnorm_kernel(*refs, nh, dh, rope_half):
    if rope_half:
        x_ref, g_ref, c_ref, s1_ref, s2_ref, o_ref = refs
    else:
        x_ref, g_ref, o_ref = refs
    for h in range(nh):
        x = x_ref[:, h * dh:(h + 1) * dh]
        y = x * lax.rsqrt(jnp.mean(x * x, axis=-1, keepdims=True) + EPS) * g_ref[...]
        if rope_half:
            y = _rope(y, c_ref[...], s1_ref[...], s2_ref[...], rope_half)
        o_ref[:, h * dh:(h + 1) * dh] = y


def _headnorm(x, col, width, gain, dh, seq, tables=None, rope_half=0):
    n = x.shape[0]
    tm = min(512, seq) if seq > 1 else n
    assert n % tm == 0 and col % width == 0
    cb = col // width
    in_specs = [pl.BlockSpec((tm, width), lambda i: (i, cb)),
                pl.BlockSpec((1, dh), lambda i: (0, 0))]
    args = [x, gain.reshape(1, dh).astype(F32)]
    if rope_half:
        if seq > 1:
            nb = seq // tm
            tspec = pl.BlockSpec((tm, dh), lambda i: (i % nb, 0))
        else:
            tspec = pl.BlockSpec((1, dh), lambda i: (0, 0))
        in_specs += [tspec] * 3
        args += list(tables)
    return pl.pallas_call(
        functools.partial(_headnorm_kernel, nh=width // dh, dh=dh, rope_half=rope_half),
        grid=(n // tm,),
        in_specs=in_specs,
        out_specs=pl.BlockSpec((tm, width), lambda i: (i, 0)),
        out_shape=jax.ShapeDtypeStruct((n, width), F32),
        compiler_params=_cp("arbitrary"), name="head_norm",
    )(*args)


def _log_gamma():
    return np.log1p(-(np.float32(2.0) ** (-5.0 - np.arange(RET_HEADS, dtype=np.float32)))).astype(np.float32)


def _ret_finish(o, gn, gate):
    y = o * lax.rsqrt(jnp.mean(o * o, axis=-1, keepdims=True) + EPS) * gn
    return y * (gate * jax.nn.sigmoid(gate))


def _ret_prompt_kernel(lg_ref, q_ref, k_ref, v_ref, gt_ref, c_ref, s1_ref, s2_ref, st0_ref, gn_ref,
                       o_ref, st_ref):
    h = pl.program_id(1)
    c_idx = pl.program_id(2)
    cl = q_ref.shape[0]
    lg = lg_ref[h]

    @pl.when(c_idx == 0)
    def _():
        st_ref[...] = st0_ref[...]

    half = RET_DK // 2
    q = _rope(q_ref[...], c_ref[...], s1_ref[...], s2_ref[...], half)
    k = _rope(k_ref[...], c_ref[...], s1_ref[...], s2_ref[...], half) * (RET_DK ** -0.5)
    v = v_ref[...]
    ri = lax.broadcasted_iota(I32, (cl, cl), 0).astype(F32)
    ci = lax.broadcasted_iota(I32, (cl, cl), 1).astype(F32)
    rel = ri - ci
    decay = jnp.where(rel >= 0, jnp.exp(jnp.maximum(rel, 0.0) * lg), 0.0)
    qb, kb, vb = q.astype(BF16), k.astype(BF16), v.astype(BF16)
    inner = _nt_dot(qb, kb) * decay
    state = st_ref[...]
    rowi = lax.broadcasted_iota(I32, (cl, 1), 0).astype(F32)
    cross = jnp.exp((rowi + 1.0) * lg)
    o = jnp.dot(inner.astype(BF16), vb, preferred_element_type=F32)
    o = o + jnp.dot(qb, state.astype(BF16), preferred_element_type=F32) * cross
    k_dec = (k * jnp.exp((cl - 1.0 - rowi) * lg)).astype(BF16)
    st_ref[...] = jnp.exp(jnp.zeros((1, 1), F32) + cl * lg) * state + _tn_dot(k_dec, vb)
    o_ref[...] = _ret_finish(o, gn_ref[...], gt_ref[...])


def _retention_prompt(z3, tables, state0, gn):
    b, t, _ = z3.shape
    cl = RET_CHUNK
    assert t % cl == 0
    nc = t // cl
    qk_b, v_b = C_RQ // RET_DK, C_RV // RET_DV
    kk_b, g_b = C_RK // RET_DK, C_RG // RET_DV
    tspec = pl.BlockSpec((cl, RET_DK), lambda bi, h, c: (c, 0))
    o, st = pl.pallas_call(
        _ret_prompt_kernel,
        grid=(b, RET_HEADS, nc),
        in_specs=[pl.BlockSpec(memory_space=pltpu.SMEM),
                  pl.BlockSpec((None, cl, RET_DK), lambda bi, h, c: (bi, c, qk_b + h)),
                  pl.BlockSpec((None, cl, RET_DK), lambda bi, h, c: (bi, c, kk_b + h)),
                  pl.BlockSpec((None, cl, RET_DV), lambda bi, h, c: (bi, c, v_b + h)),
                  pl.BlockSpec((None, cl, RET_DV), lambda bi, h, c: (bi, c, g_b + h)),
                  tspec, tspec, tspec,
                  pl.BlockSpec((None, None, RET_DK, RET_DV), lambda bi, h, c: (bi, h, 0, 0)),
                  pl.BlockSpec((None, 1, RET_DV), lambda bi, h, c: (h, 0, 0))],
        out_specs=[pl.BlockSpec((None, cl, RET_DV), lambda bi, h, c: (bi, c, h)),
                   pl.BlockSpec((None, None, RET_DK, RET_DV), lambda bi, h, c: (bi, h, 0, 0))],
        out_shape=[jax.ShapeDtypeStruct((b, t, RET_HEADS * RET_DV), F32),
                   jax.ShapeDtypeStruct((b, RET_HEADS, RET_DK, RET_DV), F32)],
        compiler_params=_cp("arbitrary", "arbitrary", "arbitrary"), name="retention_prompt",
    )(jnp.asarray(_log_gamma()), z3, z3, z3, z3, *tables, state0, gn.reshape(RET_HEADS, 1, RET_DV))
    return o, st


def _ret_decode_kernel(q_ref, k_ref, v_ref, gt_ref, c_ref, s1_ref, s2_ref, st0_ref, gn_ref, o_ref, st_ref):
    lgs = _log_gamma()
    half = RET_DK // 2
    row0 = lax.broadcasted_iota(I32, (16, 1), 0) == 0
    for h in range(RET_HEADS):
        gamma = float(np.exp(lgs[h]))
        q = _rope(q_ref[:, h * RET_DK:(h + 1) * RET_DK], c_ref[...], s1_ref[...], s2_ref[...], half)
        k = _rope(k_ref[:, h * RET_DK:(h + 1) * RET_DK], c_ref[...], s1_ref[...], s2_ref[...], half)
        k = k * (RET_DK ** -0.5)
        v = v_ref[:, h * RET_DV:(h + 1) * RET_DV]
        state = st0_ref[h].astype(F32)
        inner = jnp.sum(q * k, axis=-1, keepdims=True)
        q16 = jnp.broadcast_to(q, (16, RET_DK)).astype(BF16)
        cross = jnp.dot(q16, state.astype(BF16), preferred_element_type=F32)[0:1]
        o = inner * v + cross * gamma
        k16 = jnp.where(row0, jnp.broadcast_to(k, (16, RET_DK)), 0.0).astype(BF16)
        v16 = jnp.broadcast_to(v, (16, RET_DV)).astype(BF16)
        st_ref[h] = gamma * state + _tn_dot(k16, v16)
        o_ref[:, h * RET_DV:(h + 1) * RET_DV] = _ret_finish(
            o, gn_ref[h], gt_ref[:, h * RET_DV:(h + 1) * RET_DV])


def _retention_decode(z3, tables, state0, gn):
    b = z3.shape[0]
    wq, wv = RET_HEADS * RET_DK, RET_HEADS * RET_DV
    tspec = pl.BlockSpec((1, RET_DK), lambda bi: (0, 0))
    o, st = pl.pallas_call(
        _ret_decode_kernel,
        grid=(b,),
        in_specs=[pl.BlockSpec((None, 1, wq), lambda bi: (bi, 0, C_RQ // wq)),
                  pl.BlockSpec((None, 1, wq), lambda bi: (bi, 0, C_RK // wq)),
                  pl.BlockSpec((None, 1, wv), lambda bi: (bi, 0, C_RV // wv)),
                  pl.BlockSpec((None, 1, wv), lambda bi: (bi, 0, C_RG // wv)),
                  tspec, tspec, tspec,
                  pl.BlockSpec((None, RET_HEADS, RET_DK, RET_DV), lambda bi: (bi, 0, 0, 0)),
                  pl.BlockSpec((RET_HEADS, 1, RET_DV), lambda bi: (0, 0, 0))],
        out_specs=[pl.BlockSpec((None, 1, wv), lambda bi: (bi, 0, 0)),
                   pl.BlockSpec((None, RET_HEADS, RET_DK, RET_DV), lambda bi: (bi, 0, 0, 0))],
        out_shape=[jax.ShapeDtypeStruct((b, 1, wv), F32),
                   jax.ShapeDtypeStruct((b, RET_HEADS, RET_DK, RET_DV), F32)],
        compiler_params=_cp("arbitrary"), name="retention_decode",
    )(z3, z3, z3, z3, *tables, state0, gn.reshape(RET_HEADS, 1, RET_DV))
    return o, st


def _compress_weights(pos_emb, w):
    g, dh = NSA_KV_HEADS, HEAD_DIM
    eye = jnp.eye(g, dtype=w.dtype)
    wbig = jnp.einsum("lde,gh->lgdhe", w, eye).reshape(CMP_BLOCK * g * dh, g * dh)
    pos_row = jnp.broadcast_to(pos_emb[:, None, :], (CMP_BLOCK, g, dh)).reshape(1, CMP_BLOCK * g * dh)
    return pos_row, wbig.astype(BF16)


def _masked_softmax_parts(parts, masks):
    sm = [jnp.where(m, s, NEG_INF) for s, m in zip(parts, masks)]
    mx = functools.reduce(jnp.maximum, [jnp.max(s, axis=-1, keepdims=True) for s in sm])
    ex = [jnp.exp(s - mx) for s in sm]
    den = functools.reduce(lambda a, b2: a + b2, [jnp.sum(e, axis=-1, keepdims=True) for e in ex])
    inv = 1.0 / den
    return [jnp.where(m, e * inv, 0.0) for e, m in zip(ex, masks)]


def _select_blocks(score, nsel):
    nbs = score.shape[-1]
    lane = lax.broadcasted_iota(I32, score.shape, 1)
    sel = jnp.zeros(score.shape, jnp.bool_)
    picks = []
    for _ in range(nsel):
        m = jnp.max(score, axis=-1, keepdims=True)
        idx = jnp.min(jnp.where(score == m, lane, nbs), axis=-1, keepdims=True)
        hit = lane == idx
        sel = jnp.logical_or(sel, hit)
        score = jnp.where(hit, -jnp.inf, score)
        picks.append(idx)
    return sel, picks


def _nsa_prompt_kernel(q_ref, ck_ref, cv_ref, ks_ref, vs_ref, kw_ref, vw_ref, gt_ref, o_ref):
    i = pl.program_id(1)
    qb = q_ref.shape[0]
    t = ks_ref.shape[0]
    nbs = t // SEL_BLOCK
    hg, dh = HEADS_PER_GROUP, HEAD_DIM
    kvw = NSA_KV_HEADS * dh
    scale = dh ** -0.5
    pos1 = i * qb + lax.broadcasted_iota(I32, (qb, 1), 0)
    pos = jnp.concatenate([pos1] * hg, axis=0)
    blk = lax.broadcasted_iota(I32, (1, nbs), 1)
    vis_e = blk * SEL_BLOCK + (CMP_BLOCK - 1) <= pos
    vis_o = blk * SEL_BLOCK + (SEL_BLOCK - 1) <= pos
    valid = blk * SEL_BLOCK <= pos1
    forced = jnp.logical_or(blk == 0, blk == pos1 // SEL_BLOCK)
    key = lax.broadcasted_iota(I32, (1, t), 1)
    expand = (lax.broadcasted_iota(I32, (nbs, t), 1) // SEL_BLOCK
              == lax.broadcasted_iota(I32, (nbs, t), 0)).astype(BF16)
    causal = key <= pos
    wlen = WINDOW + qb
    wstart = pl.multiple_of(jnp.maximum(i * qb - WINDOW, 0), qb)
    wkey = wstart + lax.broadcasted_iota(I32, (1, wlen), 1)
    wdist = pos - wkey
    wmask = jnp.logical_and(wdist >= 0, wdist <= WINDOW)
    gates = gt_ref[...]

    for g in range(NSA_KV_HEADS):
        cs = slice(g * dh, (g + 1) * dh)
        q = jnp.concatenate([q_ref[:, (g * hg + hh) * dh:(g * hg + hh + 1) * dh] for hh in range(hg)],
                            axis=0).astype(BF16)
        co = slice(kvw + g * dh, kvw + (g + 1) * dh)
        ck_e, ck_o = ck_ref[:, cs].astype(BF16), ck_ref[:, co].astype(BF16)
        cv_e, cv_o = cv_ref[:, cs].astype(BF16), cv_ref[:, co].astype(BF16)
        p_e, p_o = _masked_softmax_parts([_nt_dot(q, ck_e) * scale, _nt_dot(q, ck_o) * scale], [vis_e, vis_o])
        o_cmp = (jnp.dot(p_e.astype(BF16), cv_e, preferred_element_type=F32)
                 + jnp.dot(p_o.astype(BF16), cv_o, preferred_element_type=F32))
        psum = p_e + p_o
        imp = functools.reduce(lambda a, b2: a + b2, [psum[hh * qb:(hh + 1) * qb] for hh in range(hg)])
        score = jnp.where(forced, FORCE, jnp.where(valid, imp, -FORCE))
        sel, _ = _select_blocks(score, min(SEL_TOPK, nbs))
        selk = jnp.dot(jnp.where(sel, 1.0, 0.0).astype(BF16), expand, preferred_element_type=F32)
        smask = jnp.logical_and(jnp.concatenate([selk] * hg, axis=0) > 0.5, causal)
        (p_s,) = _masked_softmax_parts([_nt_dot(q, ks_ref[:, cs].astype(BF16)) * scale], [smask])
        o_sel = jnp.dot(p_s.astype(BF16), vs_ref[:, cs].astype(BF16), preferred_element_type=F32)
        kwin = kw_ref[pl.ds(wstart, wlen), cs].astype(BF16)
        vwin = vw_ref[pl.ds(wstart, wlen), cs].astype(BF16)
        (p_w,) = _masked_softmax_parts([_nt_dot(q, kwin) * scale], [wmask])
        o_win = jnp.dot(p_w.astype(BF16), vwin, preferred_element_type=F32)
        for hh in range(hg):
            hd = g * hg + hh
            rs = slice(hh * qb, (hh + 1) * qb)
            o_ref[:, hd * dh:(hd + 1) * dh] = (gates[:, 3 * hd:3 * hd + 1] * o_cmp[rs]
                                                 + gates[:, 3 * hd + 1:3 * hd + 2] * o_sel[rs]
                                                 + gates[:, 3 * hd + 2:3 * hd + 3] * o_win[rs])


def _nsa_prompt(qn3, ck3, cv3, ksn3, z3, kwn3, gates3):
    b, t, _ = qn3.shape
    qb = Q_BLOCK
    kvw = NSA_KV_HEADS * HEAD_DIM
    assert t % qb == 0 and t >= WINDOW + qb and t % (2 * SEL_BLOCK) == 0
    nbs = ck3.shape[1]
    full = lambda cb: pl.BlockSpec((None, t, kvw), lambda bi, i: (bi, 0, cb))
    return pl.pallas_call(
        _nsa_prompt_kernel,
        grid=(b, t // qb),
        in_specs=[pl.BlockSpec((None, qb, NSA_HEADS * HEAD_DIM), lambda bi, i: (bi, i, 0)),
                  pl.BlockSpec((None, nbs, CMP_PER_SEL * kvw), lambda bi, i: (bi, 0, 0)),
                  pl.BlockSpec((None, nbs, CMP_PER_SEL * kvw), lambda bi, i: (bi, 0, 0)),
                  full(0), full(C_VS // kvw), full(0), full(C_VW // kvw),
                  pl.BlockSpec((None, qb, LANES), lambda bi, i: (bi, i, 0))],
        out_specs=pl.BlockSpec((None, qb, NSA_HEADS * HEAD_DIM), lambda bi, i: (bi, i, 0)),
        out_shape=jax.ShapeDtypeStruct((b, t, NSA_HEADS * HEAD_DIM), F32),
        compiler_params=_cp("arbitrary", "arbitrary"), name="nsa_prompt",
    )(qn3, ck3, cv3, ksn3, z3, kwn3, z3, gates3)


def _decode_q16(q_ref):
    dh = HEAD_DIM
    rows = [q_ref[:, h * dh:(h + 1) * dh] for h in range(NSA_HEADS)]
    return jnp.concatenate(rows + [jnp.zeros((16 - NSA_HEADS, dh), F32)], axis=0).astype(BF16)


def _nsa_decode_select_kernel(q_ref, ck_ref, cv_ref, ocmp_ref, idx_ref, *, past):
    dh, hg = HEAD_DIM, HEADS_PER_GROUP
    kvw = NSA_KV_HEADS * dh
    nhalf = ck_ref.shape[0]
    nbs = past // SEL_BLOCK + 1
    scale = dh ** -0.5
    q16 = _decode_q16(q_ref)
    row = lax.broadcasted_iota(I32, (16, 1), 0)
    blk = lax.broadcasted_iota(I32, (1, LANES), 1)
    blk_h = lax.broadcasted_iota(I32, (1, nhalf), 1)
    vis_e = blk_h * SEL_BLOCK + (CMP_BLOCK - 1) <= past
    vis_o = blk_h * SEL_BLOCK + (SEL_BLOCK - 1) <= past
    o_all = jnp.zeros((16, dh), F32)
    idx_row = jnp.zeros((1, LANES), I32)
    for g in range(NSA_KV_HEADS):
        cs = slice(g * dh, (g + 1) * dh)
        co = slice(kvw + g * dh, kvw + (g + 1) * dh)
        ck_e, ck_o = ck_ref[:, cs].astype(BF16), ck_ref[:, co].astype(BF16)
        cv_e, cv_o = cv_ref[:, cs].astype(BF16), cv_ref[:, co].astype(BF16)
        p_e, p_o = _masked_softmax_parts([_nt_dot(q16, ck_e) * scale, _nt_dot(q16, ck_o) * scale], [vis_e, vis_o])
        o_g = (jnp.dot(p_e.astype(BF16), cv_e, preferred_element_type=F32)
               + jnp.dot(p_o.astype(BF16), cv_o, preferred_element_type=F32))
        in_g = jnp.logical_and(row >= g * hg, row < (g + 1) * hg)
        o_all = jnp.where(in_g, o_g, o_all)
        imp = jnp.sum(jnp.where(in_g, p_e + p_o, 0.0), axis=0, keepdims=True)
        score = jnp.where(blk_h == 0, FORCE, imp)
        nsel = min(SEL_TOPK, nbs)
        _, past_picks = _select_blocks(score, min(nsel - 1, nhalf))
        picks = past_picks[:1] + [jnp.full((1, 1), nhalf, I32)] + past_picks[1:]
        for kk, p in enumerate(picks):
            idx_row = jnp.where(blk == g * SEL_TOPK + kk, p, idx_row)
        for kk in range(len(picks), SEL_TOPK):
            idx_row = jnp.where(blk == g * SEL_TOPK + kk, -1, idx_row)
    ocmp_ref[...] = o_all
    idx_ref[...] = idx_row


def _nsa_decode_select(qn3, ck3, cv3, past):
    b = qn3.shape[0]
    nhalf = ck3.shape[1]
    kvw = NSA_KV_HEADS * HEAD_DIM
    assert nhalf % 8 == 0 and nhalf == past // SEL_BLOCK
    return pl.pallas_call(
        functools.partial(_nsa_decode_select_kernel, past=past),
        grid=(b,),
        in_specs=[pl.BlockSpec((None, 1, NSA_HEADS * HEAD_DIM), lambda bi: (bi, 0, 0)),
                  pl.BlockSpec((None, nhalf, CMP_PER_SEL * kvw), lambda bi: (bi, 0, 0)),
                  pl.BlockSpec((None, nhalf, CMP_PER_SEL * kvw), lambda bi: (bi, 0, 0))],
        out_specs=[pl.BlockSpec((None, 16, HEAD_DIM), lambda bi: (bi, 0, 0)),
                   pl.BlockSpec((None, 1, LANES), lambda bi: (bi, 0, 0))],
        out_shape=[jax.ShapeDtypeStruct((b, 16, HEAD_DIM), F32),
                   jax.ShapeDtypeStruct((b, 1, LANES), I32)],
        compiler_params=_cp("arbitrary"), name="nsa_decode_select",
    )(qn3, ck3, cv3)


def _nsa_decode_attend_kernel(*refs, past, nsel):
    n_blk = NSA_KV_HEADS * nsel
    rows_ref = refs[0]
    kb_refs = refs[1:1 + n_blk]
    vb_refs = refs[1 + n_blk:1 + 2 * n_blk]
    (q_ref, idx_ref, ksn_ref, vs_ref, wk_ref, wv_ref, kwn_ref, vw_ref, ocmp_ref, gt_ref, o_ref) = refs[1 + 2 * n_blk:]
    del rows_ref
    dh, hg = HEAD_DIM, HEADS_PER_GROUP
    scale = dh ** -0.5
    self_blk = past // SEL_BLOCK
    q16 = _decode_q16(q_ref)
    q16f = q16.astype(F32)
    row = lax.broadcasted_iota(I32, (16, 1), 0)
    lane = lax.broadcasted_iota(I32, (1, LANES), 1)
    idx_row = idx_ref[...]
    o_sel = jnp.zeros((16, dh), F32)
    o_win = jnp.zeros((16, dh), F32)
    for g in range(NSA_KV_HEADS):
        cs = slice(g * dh, (g + 1) * dh)
        in_g = jnp.logical_and(row >= g * hg, row < (g + 1) * hg)
        k_self = ksn_ref[:, cs].astype(BF16).astype(F32)
        v_self = vs_ref[:, cs].astype(BF16).astype(F32)
        s_self = jnp.sum(q16f * k_self, axis=-1, keepdims=True) * scale
        parts, masks, vals = [], [], []
        self_sel = jnp.zeros((1, 1), jnp.bool_)
        for kk in range(nsel):
            pick = jnp.max(jnp.where(lane == g * SEL_TOPK + kk, idx_row, -1), axis=-1, keepdims=True)
            in_pool = jnp.logical_and(pick >= 0, pick < self_blk)
            self_sel = jnp.logical_or(self_sel, pick == self_blk)
            kb = kb_refs[g * nsel + kk][:, g, :].astype(BF16)
            parts.append(_nt_dot(q16, kb) * scale)
            masks.append(in_pool)
            vals.append(vb_refs[g * nsel + kk][:, g, :].astype(BF16))
        parts.append(s_self)
        masks.append(self_sel)
        probs = _masked_softmax_parts(parts, masks)
        acc = probs[-1].astype(BF16).astype(F32) * v_self
        for p, vv in zip(probs[:-1], vals):
            acc = acc + jnp.dot(p.astype(BF16), vv, preferred_element_type=F32)
        o_sel = jnp.where(in_g, acc, o_sel)
        kw_self = kwn_ref[:, cs].astype(BF16).astype(F32)
        vw_self = vw_ref[:, cs].astype(BF16).astype(F32)
        sw_self = jnp.sum(q16f * kw_self, axis=-1, keepdims=True) * scale
        wparts = [_nt_dot(q16, wk_ref[:, g, :].astype(BF16)) * scale, sw_self]
        wmasks = [lane[:, :1] >= 0, lane[:, :1] >= 0]
        pw, pws = _masked_softmax_parts(wparts, wmasks)
        accw = (jnp.dot(pw.astype(BF16), wv_ref[:, g, :].astype(BF16), preferred_element_type=F32)
                + pws.astype(BF16).astype(F32) * vw_self)
        o_win = jnp.where(in_g, accw, o_win)
    gates = gt_ref[...]
    o_cmp = ocmp_ref[...]
    for hd in range(NSA_HEADS):
        o_ref[:, hd * dh:(hd + 1) * dh] = (gates[:, 3 * hd:3 * hd + 1] * o_cmp[hd:hd + 1]
                                             + gates[:, 3 * hd + 1:3 * hd + 2] * o_sel[hd:hd + 1]
                                             + gates[:, 3 * hd + 2:3 * hd + 3] * o_win[hd:hd + 1])


def _nsa_decode_attend(pool_rows, sel_k_pool, sel_v_pool, qn3, idx3, ksn3, z3, win_k, win_v, kwn3, ocmp, gates3,
                       past, nsel):
    b = qn3.shape[0]
    kvw = NSA_KV_HEADS * HEAD_DIM
    n_blk = NSA_KV_HEADS * nsel
    nbuf = win_k.shape[2]
    per_page = sel_k_pool.shape[2] // SEL_BLOCK

    def blk_spec(tt):
        return pl.BlockSpec((None, None, SEL_BLOCK, NSA_KV_HEADS, HEAD_DIM),
                            lambda bi, r: (0, r[bi * n_blk + tt] // per_page, r[bi * n_blk + tt] % per_page, 0, 0))

    def row_spec(cb, w=kvw):
        return pl.BlockSpec((None, 1, w), lambda bi, r: (bi, 0, cb))

    wspec = pl.BlockSpec((None, None, nbuf, NSA_KV_HEADS, HEAD_DIM), lambda bi, r: (0, bi, 0, 0, 0))
    in_specs = ([blk_spec(tt) for tt in range(n_blk)] + [blk_spec(tt) for tt in range(n_blk)]
                + [pl.BlockSpec((None, 1, NSA_HEADS * HEAD_DIM), lambda bi, r: (bi, 0, 0)),
                   pl.BlockSpec((None, 1, LANES), lambda bi, r: (bi, 0, 0)),
                   row_spec(0), row_spec(C_VS // kvw),
                   wspec, wspec,
                   row_spec(0), row_spec(C_VW // kvw),
                   pl.BlockSpec((None, 16, HEAD_DIM), lambda bi, r: (bi, 0, 0)),
                   pl.BlockSpec((None, 1, LANES), lambda bi, r: (bi, 0, 0))])
    return pl.pallas_call(
        functools.partial(_nsa_decode_attend_kernel, past=past, nsel=nsel),
        grid_spec=pltpu.PrefetchScalarGridSpec(
            num_scalar_prefetch=1, grid=(b,), in_specs=in_specs,
            out_specs=pl.BlockSpec((None, 1, NSA_HEADS * HEAD_DIM), lambda bi, r: (bi, 0, 0))),
        out_shape=jax.ShapeDtypeStruct((b, 1, NSA_HEADS * HEAD_DIM), F32),
        compiler_params=_cp("arbitrary"), name="nsa_decode_attend",
    )(pool_rows, *([sel_k_pool] * n_blk), *([sel_v_pool] * n_blk),
      qn3, idx3, ksn3, z3, win_k, win_v, kwn3, z3, ocmp, gates3)


def _mem_attn_kernel(q_ref, g_ref, mk_ref, mv_ref, o_ref, *, cache_layout):
    dh = MEM_HEAD_DIM
    tq = q_ref.shape[0]
    rows = max(tq, 16)
    for h in range(MEM_HEADS):
        cs = slice(h * dh, (h + 1) * dh)
        x = q_ref[:, cs]
        q = x * lax.rsqrt(jnp.mean(x * x, axis=-1, keepdims=True) + EPS) * g_ref[...]
        if rows != tq:
            q = jnp.broadcast_to(q, (rows, dh))
        mk = mk_ref[:, h, :] if cache_layout else mk_ref[:, cs]
        mv = mv_ref[:, h, :] if cache_layout else mv_ref[:, cs]
        s = _nt_dot(q.astype(BF16), mk.astype(BF16)) * (dh ** -0.5)
        m = jnp.max(s, axis=-1, keepdims=True)
        e = jnp.exp(s - m)
        p = e / jnp.sum(e, axis=-1, keepdims=True)
        o = jnp.dot(p.astype(BF16), mv.astype(BF16), preferred_element_type=F32)
        o_ref[:, cs] = o[0:tq]


def _mem_attention(z3, gq, mk, mv, cache_layout=False):
    b, t, _ = z3.shape
    w = MEM_HEADS * MEM_HEAD_DIM
    tq = min(t, 512)
    assert t % tq == 0
    if cache_layout:
        m = mk.shape[2]
        kv_spec = pl.BlockSpec((None, None, m, MEM_HEADS, MEM_HEAD_DIM), lambda bi, i: (0, bi, 0, 0, 0))
    else:
        m = mk.shape[1]
        kv_spec = pl.BlockSpec((None, m, w), lambda bi, i: (bi, 0, 0))
    return pl.pallas_call(
        functools.partial(_mem_attn_kernel, cache_layout=cache_layout),
        grid=(b, t // tq),
        in_specs=[pl.BlockSpec((None, tq, w), lambda bi, i: (bi, i, C_MQ // w)),
                  pl.BlockSpec((1, MEM_HEAD_DIM), lambda bi, i: (0, 0)),
                  kv_spec, kv_spec],
        out_specs=pl.BlockSpec((None, tq, w), lambda bi, i: (bi, i, 0)),
        out_shape=jax.ShapeDtypeStruct((b, t, w), F32),
        compiler_params=_cp("arbitrary", "arbitrary"), name="mem_attention",
    )(z3, gq.reshape(1, MEM_HEAD_DIM), mk, mv)


def _sigmoid_cols_kernel(x_ref, o_ref):
    o_ref[...] = jax.nn.sigmoid(x_ref[...])


def _nsa_gates(z, col):
    n = z.shape[0]
    tm = min(n, 1024)
    return pl.pallas_call(
        _sigmoid_cols_kernel, grid=(n // tm,),
        in_specs=[pl.BlockSpec((tm, LANES), lambda i: (i, col // LANES))],
        out_specs=pl.BlockSpec((tm, LANES), lambda i: (i, 0)),
        out_shape=jax.ShapeDtypeStruct((n, LANES), F32),
        compiler_params=_cp("arbitrary"), name="nsa_gates",
    )(z)


def _merge_kernel(r_ref, n_ref, m_ref, g0_ref, g1_ref, g2_ref, w_ref, o_ref):
    acc = jnp.zeros(o_ref.shape, F32)
    for c, (b_ref, g_ref) in enumerate(((r_ref, g0_ref), (n_ref, g1_ref), (m_ref, g2_ref))):
        up = jnp.dot(b_ref[...].astype(BF16), w_ref[c], preferred_element_type=F32)
        acc = acc + jax.nn.sigmoid(g_ref[...]) * up
    o_ref[...] = acc.astype(BF16)


def _merge(o_ret, o_nsa, o_mem, z, wb, d_model):
    n, bw = o_ret.shape
    tm = min(n, 512)
    tn = 512
    gb = C_MG // tn
    nj = d_model // tn
    bspec = pl.BlockSpec((tm, bw), lambda i, j: (i, 0))
    gspec = lambda c: pl.BlockSpec((tm, tn), lambda i, j: (i, gb + c * nj + j))
    return pl.pallas_call(
        _merge_kernel, grid=(n // tm, nj),
        in_specs=[bspec, bspec, bspec, gspec(0), gspec(1), gspec(2),
                  pl.BlockSpec((N_BRANCH, bw, tn), lambda i, j: (0, 0, j))],
        out_specs=pl.BlockSpec((tm, tn), lambda i, j: (i, j)),
        out_shape=jax.ShapeDtypeStruct((n, d_model), BF16),
        compiler_params=_cp("arbitrary", "arbitrary"), name="branch_merge",
    )(o_ret, o_nsa, o_mem, z, z, z, wb)


def _topk_rows(s, kk):
    e = s.shape[0]
    ridx = lax.broadcasted_iota(I32, s.shape, 0)
    vals, idxs = [], []
    for _ in range(kk):
        m = jnp.max(s, axis=0, keepdims=True)
        i = jnp.min(jnp.where(s == m, ridx, e), axis=0, keepdims=True)
        vals.append(m)
        idxs.append(i)
        s = jnp.where(ridx == i, -jnp.inf, s)
    return jnp.concatenate(vals, axis=0), jnp.concatenate(idxs, axis=0)


def _pair_pieces(kk):
    pieces, cur = [], []

    def flush(rows):
        pieces.append(rows + [None] * (8 - len(rows)))

    for a in range(kk):
        grp = [(a, b) for b in range(kk // (a + 1))]
        if cur and len(cur) + len(grp) > 8:
            flush(cur)
            cur = []
        cur = cur + grp
        while len(cur) >= 8:
            flush(cur[:8])
            cur = cur[8:]
    if cur:
        flush(cur)
    return pieces


def _rows_by_runs(x, ids):
    if ids[0] % 8 == 0 and ids == list(range(ids[0], ids[0] + 8)):
        return x[ids[0]:ids[0] + 8]
    r = lax.broadcasted_iota(I32, (8, 1), 0)
    runs = []
    for p, i in enumerate(ids):
        if not runs or runs[-1][1] != i:
            runs.append((p, i))
    out = jnp.broadcast_to(x[runs[-1][1]:runs[-1][1] + 1], (8, x.shape[1]))
    for (_, i), (nxt, _) in zip(reversed(runs[:-1]), reversed(runs[1:])):
        out = jnp.where(r < nxt, x[i:i + 1], out)
    return out


def _peer_route_kernel(q_ref, sk_ref, i_ref, j_ref, g_ref):
    kk = PEER_TOPK
    tl = q_ref.shape[0]
    st = _nt_dot(sk_ref[...], q_ref[...].astype(BF16))
    v0, i0 = _topk_rows(st[0:PEER_KEYS], kk)
    v1, i1 = _topk_rows(st[PEER_KEYS:2 * PEER_KEYS], kk)
    cand, ci, cj = [], [], []
    for piece in _pair_pieces(kk):
        ra = [p[0] if p else 0 for p in piece]
        rb = [p[1] if p else 0 for p in piece]
        live = lax.broadcasted_iota(I32, (8, 1), 0) < sum(p is not None for p in piece)
        cand.append(jnp.where(live, _rows_by_runs(v0, ra) + _rows_by_runs(v1, rb), -jnp.inf))
        ci.append(_rows_by_runs(i0, ra))
        cj.append(_rows_by_runs(i1, rb))
    cand, ci, cj = (jnp.concatenate(c, axis=0) for c in (cand, ci, cj))
    ridx = lax.broadcasted_iota(I32, cand.shape, 0)
    sc, si, sj = [], [], []
    for _ in range(kk):
        m = jnp.max(cand, axis=0, keepdims=True)
        r = jnp.min(jnp.where(cand == m, ridx, cand.shape[0]), axis=0, keepdims=True)
        hit = ridx == r
        sc.append(m)
        si.append(jnp.max(jnp.where(hit, ci, -1), axis=0, keepdims=True))
        sj.append(jnp.max(jnp.where(hit, cj, -1), axis=0, keepdims=True))
        cand = jnp.where(hit, -jnp.inf, cand)
    sc = jnp.concatenate(sc, axis=0)
    e = jnp.exp(sc - jnp.max(sc, axis=0, keepdims=True))
    g_ref[...] = e / jnp.sum(e, axis=0, keepdims=True)
    i_ref[...] = jnp.concatenate(si, axis=0)
    j_ref[...] = jnp.concatenate(sj, axis=0)


def _peer_route(q, skbd):
    n = q.shape[0]
    tl = min(n, LANES)
    kk = PEER_TOPK
    slots = PEER_HEADS * kk
    ospec = pl.BlockSpec((kk, tl), lambda i, h: (h, i))
    return pl.pallas_call(
        _peer_route_kernel, grid=(n // tl, PEER_HEADS),
        in_specs=[pl.BlockSpec((tl, PEER_DKEY), lambda i, h: (i, h)),
                  pl.BlockSpec((None, 2 * PEER_KEYS, PEER_DKEY), lambda i, h: (h, 0, 0))],
        out_specs=[ospec, ospec, ospec],
        out_shape=[jax.ShapeDtypeStruct((slots, n), I32), jax.ShapeDtypeStruct((slots, n), I32),
                   jax.ShapeDtypeStruct((slots, n), F32)],
        compiler_params=_cp("arbitrary", "arbitrary"), name="peer_route",
    )(q, skbd)


def _peer_weights_kernel(i_ref, j_ref, g_ref, o_ref, it_scr, jt_scr, gt_scr):
    tb = o_ref.shape[0]
    it_scr[...] = i_ref[...].T
    jt_scr[...] = j_ref[...].T
    gt_scr[...] = g_ref[...].T
    key = lax.broadcasted_iota(I32, (PEER_KEYS, i_ref.shape[0]), 0)

    def body(n, carry):
        irow = it_scr[pl.ds(n, 1), :]
        jrow = jt_scr[pl.ds(n, 1), :]
        grow = gt_scr[pl.ds(n, 1), :]
        a = jnp.where(key == irow, grow, 0.0).astype(BF16)
        b = jnp.where(key == jrow, 1.0, 0.0).astype(BF16)
        o_ref[n] = _nt_dot(a, b)
        return carry

    lax.fori_loop(0, tb, body, 0)


def _peer_weights(it, jt, gt):
    slots, n = it.shape
    tb = min(n, LANES)
    ispec = pl.BlockSpec((slots, tb), lambda i: (0, i))
    return pl.pallas_call(
        _peer_weights_kernel, grid=(n // tb,),
        in_specs=[ispec, ispec, ispec],
        out_specs=pl.BlockSpec((tb, PEER_KEYS, PEER_KEYS), lambda i: (i, 0, 0)),
        out_shape=jax.ShapeDtypeStruct((n, PEER_KEYS, PEER_KEYS), F32),
        scratch_shapes=[pltpu.VMEM((tb, slots), I32), pltpu.VMEM((tb, slots), I32), pltpu.VMEM((tb, slots), F32)],
        compiler_params=_cp("arbitrary"), name="peer_weights",
    )(it, jt, gt)


def _peer_dense_kernel(h_ref, g_ref, u_ref, v_ref, x_ref, o_ref, acc_ref):
    j = pl.program_id(1)

    @pl.when(j == 0)
    def _():
        acc_ref[...] = jnp.zeros_like(acc_ref)

    a = jnp.dot(h_ref[...], u_ref[...], preferred_element_type=F32)
    act = 0.5 * a * (1.0 + lax.erf(a * (2.0 ** -0.5)))
    w = jnp.concatenate([(g_ref[:, ii, :] * act[:, ii * PEER_KEYS:(ii + 1) * PEER_KEYS]).astype(BF16)
                         for ii in range(g_ref.shape[1])], axis=1)
    acc_ref[...] += jnp.dot(w, v_ref[...], preferred_element_type=F32)

    @pl.when(j == pl.num_programs(1) - 1)
    def _():
        o_ref[...] = x_ref[...] + acc_ref[...]


def _peer_dense(hn, gw, ut, vv, x):
    n, d = hn.shape
    ne = ut.shape[1]
    tm = min(n, 512)
    te = 1024
    return pl.pallas_call(
        _peer_dense_kernel, grid=(n // tm, ne // te),
        in_specs=[pl.BlockSpec((tm, d), lambda i, j: (i, 0)),
                  pl.BlockSpec((tm, te // PEER_KEYS, PEER_KEYS), lambda i, j: (i, j, 0)),
                  pl.BlockSpec((d, te), lambda i, j: (0, j)),
                  pl.BlockSpec((te, d), lambda i, j: (j, 0)),
                  pl.BlockSpec((tm, d), lambda i, j: (i, 0))],
        out_specs=pl.BlockSpec((tm, d), lambda i, j: (i, 0)),
        out_shape=jax.ShapeDtypeStruct((n, d), F32),
        scratch_shapes=[pltpu.VMEM((tm, d), F32)],
        compiler_params=_cp("arbitrary", "arbitrary"), name="peer_dense",
    )(hn, gw, ut, vv, x)


def _prepare_weights(norm_attn, w_in, cmp_pos_k, cmp_pos_v, cmp_w_k, cmp_w_v, w_mem_kv, w_branch, w_out,
                     peer_wq, peer_subkeys, peer_u, peer_v):
    d = w_in.shape[0]
    ng_w = NSA_HEADS * 3
    o_ng = 5632
    o_mq = o_ng + ng_w
    o_mg = o_mq + MEM_HEADS * MEM_HEAD_DIM
    width = C_MG + N_BRANCH * d + LANES
    width = -(-width // Z_TILE) * Z_TILE
    w_r = jnp.concatenate([w_in[:, 0:4096], w_in[:, o_mq:o_mg], w_in[:, 4096:o_ng], w_in[:, o_mg:],
                           w_in[:, o_ng:o_mq],
                           jnp.zeros((d, width - (C_MG + N_BRANCH * d + ng_w)), w_in.dtype)], axis=1).astype(BF16)
    pos_k, wk_big = _compress_weights(cmp_pos_k, cmp_w_k)
    pos_v, wv_big = _compress_weights(cmp_pos_v, cmp_w_v)
    half = PEER_DKEY // 2
    sk = peer_subkeys.astype(BF16)
    zk = jnp.zeros((PEER_HEADS, PEER_KEYS, half), BF16)
    skbd = jnp.concatenate([jnp.concatenate([sk[:, 0], zk], axis=2),
                            jnp.concatenate([zk, sk[:, 1]], axis=2)], axis=1)
    return dict(w_in=w_r, c_ng=C_MG + N_BRANCH * d, pos_k=pos_k, wk_big=wk_big, pos_v=pos_v, wv_big=wv_big,
                w_mem_kv=w_mem_kv.astype(BF16), w_branch=w_branch.astype(BF16), w_out=w_out.astype(BF16),
                peer_wq=peer_wq.astype(BF16), skbd=skbd, ut=peer_u.astype(BF16).T, vv=peer_v.astype(BF16))


def _layer_common(x, seq, pos, norm_attn, nsa_q_norm, nsa_k_norm, pw):
    z = _prologue_matmul(x, norm_attn, pw["w_in"], mode="norm", name="in_proj", tm=1024)
    nsa_tabs = _rope_tables(pos, ROPE_DIMS, ROPE_THETA, HEAD_DIM)
    half = ROPE_DIMS // 2
    kvw = NSA_KV_HEADS * HEAD_DIM
    qn = _headnorm(z, C_NQ, NSA_HEADS * HEAD_DIM, nsa_q_norm, HEAD_DIM, seq, nsa_tabs, half)
    kcn = _headnorm(z, C_KC, kvw, nsa_k_norm[0], HEAD_DIM, seq, nsa_tabs, half)
    ksn = _headnorm(z, C_KS, kvw, nsa_k_norm[1], HEAD_DIM, seq, nsa_tabs, half)
    kwn = _headnorm(z, C_KW, kvw, nsa_k_norm[2], HEAD_DIM, seq, nsa_tabs, half)
    gates = _nsa_gates(z, pw["c_ng"])
    ret_tabs = _rope_tables(pos, RET_DK, RET_THETA, RET_DK)
    return z, qn, kcn, ksn, kwn, gates, ret_tabs


def _layer_tail(x, z, o_ret, o_nsa, o_mem, norm_ffn, pw):
    d = x.shape[1]
    merged = _merge(o_ret, o_nsa, o_mem, z, pw["w_branch"], d)
    x1 = _prologue_matmul(merged, jnp.ones((d,), F32), pw["w_out"], mode="none", residual=x, name="out_proj")
    q, hn = _prologue_matmul(x1, norm_ffn, pw["peer_wq"], mode="norm", emit_h=True, name="peer_query")
    it, jt, gt = _peer_route(q, pw["skbd"])
    return _peer_dense(hn, _peer_weights(it, jt, gt), pw["ut"], pw["vv"], x1)


def kernel(x_prompt, x_sample, cache_cmp_k, cache_cmp_v, cache_sel_k, cache_sel_v, cache_win_k, cache_win_v,
           state_ret, cache_mem_k, cache_mem_v, page_table, mem_prompt, norm_attn, w_in, ret_gn, nsa_q_norm,
           nsa_k_norm, cmp_pos_k, cmp_pos_v, cmp_w_k, cmp_w_v, norm_mem, w_mem_kv, mem_q_norm, mem_k_norm,
           w_branch, w_out, norm_ffn, peer_wq, peer_subkeys, peer_u, peer_v):
    depth = w_in.shape[0]
    assert depth == 1
    l = 0
    bp, t, d = x_prompt.shape
    bs, ts, _ = x_sample.shape
    assert ts == 1
    n_pool, page, g_kv, dh = cache_cmp_k.shape[1:]
    n_pages = page_table.shape[1]
    past = n_pages * page
    kvw = g_kv * dh
    memw = MEM_HEADS * MEM_HEAD_DIM
    pw = _prepare_weights(norm_attn[l], w_in[l], cmp_pos_k[l], cmp_pos_v[l], cmp_w_k[l], cmp_w_v[l], w_mem_kv[l],
                          w_branch[l], w_out[l], peer_wq[l], peer_subkeys[l], peer_u[l], peer_v[l])
    cmp_cols = CMP_BLOCK * kvw

    n_p = bp * t
    xp = x_prompt.reshape(n_p, d)
    m_tok = mem_prompt.shape[1]
    mkv = _prologue_matmul(mem_prompt.reshape(bp * m_tok, d), norm_mem[l], pw["w_mem_kv"], mode="norm",
                           name="mem_kv_proj")
    mk_p = _headnorm(mkv, 0, memw, mem_k_norm[l], MEM_HEAD_DIM, m_tok)
    mv_p = mkv[:, memw:]
    z, qn, kcn, ksn, kwn, gates, ret_tabs = _layer_common(xp, t, jnp.arange(t), norm_attn[l], nsa_q_norm[l],
                                                         nsa_k_norm[l], pw)
    z3 = z.reshape(bp, t, z.shape[1])
    o_ret, st_p = _retention_prompt(z3, ret_tabs, jnp.zeros((bp, RET_HEADS, RET_DK, RET_DV), F32), ret_gn[l])
    vc_p = z[:, C_VC:C_VC + kvw]
    nbc = t // CMP_BLOCK
    ck_p = _prologue_matmul(kcn.reshape(bp * nbc, cmp_cols), pw["pos_k"], pw["wk_big"], mode="bias", tm=256,
                            name="compress_k")
    cv_p = _prologue_matmul(vc_p.reshape(bp * nbc, cmp_cols), pw["pos_v"], pw["wv_big"], mode="bias", tm=256,
                            name="compress_v")
    pair = CMP_PER_SEL * kvw
    o_nsa = _nsa_prompt(qn.reshape(bp, t, -1), ck_p.reshape(bp, nbc // CMP_PER_SEL, pair),
                        cv_p.reshape(bp, nbc // CMP_PER_SEL, pair),
                        ksn.reshape(bp, t, kvw), z3, kwn.reshape(bp, t, kvw), gates.reshape(bp, t, LANES))
    o_mem = _mem_attention(z3, mem_q_norm[l], mk_p.reshape(bp, m_tok, memw), mv_p.reshape(bp, m_tok, memw))
    y_p = _layer_tail(xp, z, o_ret.reshape(n_p, -1), o_nsa.reshape(n_p, -1), o_mem.reshape(n_p, -1), norm_ffn[l], pw)

    nbuf_p = min(WINDOW, t)
    kv5 = lambda a, bb, tt: a.reshape(1, bb, tt, g_kv, dh)
    outs_p = (kv5(kcn, bp, t), kv5(vc_p, bp, t), kv5(ksn, bp, t), kv5(z[:, C_VS:C_VS + kvw], bp, t),
              kv5(kwn, bp, t)[:, :, t - nbuf_p:], kv5(z[:, C_VW:C_VW + kvw], bp, t)[:, :, t - nbuf_p:],
              st_p[None], mk_p.reshape(1, bp, m_tok, MEM_HEADS, MEM_HEAD_DIM),
              mv_p.reshape(1, bp, m_tok, MEM_HEADS, MEM_HEAD_DIM))

    xs = x_sample.reshape(bs, d)
    zs, qn_s, kcn_s, ksn_s, kwn_s, gates_s, ret_tabs_s = _layer_common(
        xs, 1, jnp.full((1,), past, I32), norm_attn[l], nsa_q_norm[l], nsa_k_norm[l], pw)
    zs3 = zs.reshape(bs, 1, zs.shape[1])
    o_ret_s, st_s = _retention_decode(zs3, ret_tabs_s, state_ret[l], ret_gn[l])
    per_page = page // CMP_BLOCK
    ckp = _prologue_matmul(cache_cmp_k[l].reshape(n_pool * per_page, cmp_cols), pw["pos_k"], pw["wk_big"],
                           mode="bias", tm=256, name="compress_pool_k")
    cvp = _prologue_matmul(cache_cmp_v[l].reshape(n_pool * per_page, cmp_cols), pw["pos_v"], pw["wv_big"],
                           mode="bias", tm=256, name="compress_pool_v")
    ck_s = ckp.reshape(n_pool, per_page, kvw)[page_table].reshape(bs, n_pages * per_page // CMP_PER_SEL, pair)
    cv_s = cvp.reshape(n_pool, per_page, kvw)[page_table].reshape(bs, n_pages * per_page // CMP_PER_SEL, pair)
    qn_s3 = qn_s.reshape(bs, 1, -1)
    ocmp_s, idx_s = _nsa_decode_select(qn_s3, ck_s, cv_s, past)
    nbs = past // SEL_BLOCK + 1
    nsel = min(SEL_TOPK, nbs)
    per_page_sel = page // SEL_BLOCK
    picks = jnp.clip(idx_s[:, 0, :NSA_KV_HEADS * SEL_TOPK].reshape(bs, NSA_KV_HEADS, SEL_TOPK)[:, :, :nsel],
                     0, nbs - 2)
    pool_rows = (jnp.take_along_axis(page_table, (picks // per_page_sel).reshape(bs, -1), axis=1) * per_page_sel
                 + (picks % per_page_sel).reshape(bs, -1)).reshape(-1).astype(I32)
    o_nsa_s = _nsa_decode_attend(
        pool_rows, cache_sel_k, cache_sel_v, qn_s3, idx_s, ksn_s.reshape(bs, 1, kvw), zs3, cache_win_k, cache_win_v,
        kwn_s.reshape(bs, 1, kvw), ocmp_s, gates_s.reshape(bs, 1, LANES), past, nsel)
    o_mem_s = _mem_attention(zs3, mem_q_norm[l], cache_mem_k, cache_mem_v, cache_layout=True)
    y_s = _layer_tail(xs, zs, o_ret_s.reshape(bs, -1), o_nsa_s.reshape(bs, -1), o_mem_s.reshape(bs, -1),
                      norm_ffn[l], pw)

    nbuf_s = cache_win_k.shape[2]
    win_k_s = jnp.concatenate([cache_win_k[l], kv5(kwn_s, bs, 1)[0]], axis=1)[None, :, -nbuf_s:]
    win_v_s = jnp.concatenate([cache_win_v[l], kv5(zs[:, C_VW:C_VW + kvw], bs, 1)[0]], axis=1)[None, :, -nbuf_s:]
    outs_s = (kv5(kcn_s, bs, 1), kv5(zs[:, C_VC:C_VC + kvw], bs, 1), kv5(ksn_s, bs, 1),
              kv5(zs[:, C_VS:C_VS + kvw], bs, 1), win_k_s, win_v_s, st_s[None].astype(x_sample.dtype))
    return (y_p.reshape(bp, t, d), y_s.reshape(bs, ts, d)) + outs_p + outs_s
```

```python
import functools

import numpy as np
import jax
import jax.numpy as jnp
from jax import lax
from jax.experimental import pallas as pl
from jax.experimental.pallas import tpu as pltpu

F32 = jnp.float32
BF16 = jnp.bfloat16
I32 = jnp.int32

EPS = 1e-6
NEG_INF = -1e30
FORCE = 1e4

RET_HEADS, RET_DK, RET_DV, RET_CHUNK, RET_THETA = 4, 128, 256, 128, 10000.0
NSA_HEADS, NSA_KV_HEADS, HEAD_DIM = 8, 2, 128
HEADS_PER_GROUP = NSA_HEADS // NSA_KV_HEADS
CMP_BLOCK, SEL_BLOCK, SEL_TOPK, WINDOW, Q_BLOCK = 32, 64, 4, 256, 128
CMP_PER_SEL = SEL_BLOCK // CMP_BLOCK
ROPE_THETA, ROPE_DIMS = 500000.0, HEAD_DIM // 4
MEM_HEADS, MEM_HEAD_DIM = 4, 256
PEER_HEADS, PEER_KEYS, PEER_DKEY, PEER_TOPK = 8, 128, 128, 16
N_BRANCH = 3
LANES = 128

C_RQ, C_RK, C_RV, C_RG, C_NQ, C_MQ = 0, 512, 1024, 2048, 3072, 4096
C_KC, C_VC, C_KS, C_VS, C_KW, C_VW = 5120, 5376, 5632, 5888, 6144, 6400
C_MG = 6656
Z_TILE = 512


def _cp(*sem):
    return pltpu.CompilerParams(dimension_semantics=sem, vmem_limit_bytes=56 * 1024 * 1024)


def _nt_dot(a, b):
    return lax.dot_general(a, b, (((1,), (1,)), ((), ())), preferred_element_type=F32)


def _tn_dot(a, b):
    return lax.dot_general(a, b, (((0,), (0,)), ((), ())), preferred_element_type=F32)


def _pm_kernel(*refs, mode, residual, emit_h):
    x_ref, g_ref, w_ref = refs[:3]
    k = 3
    r_ref = None
    if residual:
        r_ref = refs[k]
        k += 1
    o_ref = refs[k]
    k += 1
    h_ref = None
    if emit_h:
        h_ref = refs[k]
        k += 1
    h_scr = refs[k]

    @pl.when(pl.program_id(1) == 0)
    def _():
        x = x_ref[...].astype(F32)
        if mode == "norm":
            h = x * lax.rsqrt(jnp.mean(x * x, axis=-1, keepdims=True) + EPS) * g_ref[...]
        elif mode == "bias":
            h = x + g_ref[...]
        else:
            h = x
        h_scr[...] = h.astype(BF16)
        if emit_h:
            h_ref[...] = h_scr[...]

    acc = jnp.dot(h_scr[...], w_ref[...], preferred_element_type=F32)
    if residual:
        acc = acc + r_ref[...]
    o_ref[...] = acc


def _prologue_matmul(x, g, w, *, mode, name, residual=None, emit_h=False, tm=512):
    n, kdim = x.shape
    wout = w.shape[1]
    tm = min(tm, n)
    tn = 512 if wout % 512 == 0 else (256 if wout % 256 == 0 else 128)
    assert n % tm == 0 and wout % tn == 0
    in_specs = [pl.BlockSpec((tm, kdim), lambda i, j: (i, 0)),
                pl.BlockSpec((1, kdim), lambda i, j: (0, 0)),
                pl.BlockSpec((kdim, tn), lambda i, j: (0, j))]
    args = [x, g.reshape(1, kdim).astype(F32), w]
    if residual is not None:
        in_specs.append(pl.BlockSpec((tm, tn), lambda i, j: (i, j)))
        args.append(residual)
    out_shape = [jax.ShapeDtypeStruct((n, wout), F32)]
    out_specs = [pl.BlockSpec((tm, tn), lambda i, j: (i, j))]
    if emit_h:
        out_shape.append(jax.ShapeDtypeStruct((n, kdim), BF16))
        out_specs.append(pl.BlockSpec((tm, kdim), lambda i, j: (i, 0)))
    res = pl.pallas_call(
        functools.partial(_pm_kernel, mode=mode, residual=residual is not None, emit_h=emit_h),
        grid=(n // tm, wout // tn),
        in_specs=in_specs, out_specs=out_specs, out_shape=out_shape,
        scratch_shapes=[pltpu.VMEM((tm, kdim), BF16)],
        compiler_params=_cp("arbitrary", "arbitrary"), name=name,
    )(*args)
    return res if emit_h else res[0]


def _rope_tables(pos, rot_dims, theta, dh):
    half = rot_dims // 2
    inv = theta ** (-jnp.arange(half, dtype=F32) * 2.0 / rot_dims)
    ang = pos.astype(F32)[:, None] * inv[None, :]
    cos, sin = jnp.cos(ang), jnp.sin(ang)
    t = pos.shape[0]
    one = jnp.ones((t, dh - rot_dims), F32)
    zh = jnp.zeros((t, half), F32)
    zr = jnp.zeros((t, dh - rot_dims), F32)
    c = jnp.concatenate([cos, cos, one], axis=1)
    s1 = jnp.concatenate([-sin, zh, zr], axis=1)
    s2 = jnp.concatenate([zh, sin, zr], axis=1)
    return c, s1, s2


def _rope(y, c, s1, s2, half):
    dh = y.shape[-1]
    return y * c + pltpu.roll(y, dh - half, 1) * s1 + pltpu.roll(y, half, 1) * s2


def _headnorm_kernel(*refs, nh, dh, rope_half):
    if rope_half:
        x_ref, g_ref, c_ref, s1_ref, s2_ref, o_ref = refs
    else:
        x_ref, g_ref, o_ref = refs
    for h in range(nh):
        x = x_ref[:, h * dh:(h + 1) * dh]
        y = x * lax.rsqrt(jnp.mean(x * x, axis=-1, keepdims=True) + EPS) * g_ref[...]
        if rope_half:
            y = _rope(y, c_ref[...], s1_ref[...], s2_ref[...], rope_half)
        o_ref[:, h * dh:(h + 1) * dh] = y


def _headnorm(x, col, width, gain, dh, seq, tables=None, rope_half=0):
    n = x.shape[0]
    tm = min(512, seq) if seq > 1 else n
    assert n % tm == 0 and col % width == 0
    cb = col // width
    in_specs = [pl.BlockSpec((tm, width), lambda i: (i, cb)),
                pl.BlockSpec((1, dh), lambda i: (0, 0))]
    args = [x, gain.reshape(1, dh).astype(F32)]
    if rope_half:
        if seq > 1:
            nb = seq // tm
            tspec = pl.BlockSpec((tm, dh), lambda i: (i % nb, 0))
        else:
            tspec = pl.BlockSpec((1, dh), lambda i: (0, 0))
        in_specs += [tspec] * 3
        args += list(tables)
    return pl.pallas_call(
        functools.partial(_headnorm_kernel, nh=width // dh, dh=dh, rope_half=rope_half),
        grid=(n // tm,),
        in_specs=in_specs,
        out_specs=pl.BlockSpec((tm, width), lambda i: (i, 0)),
        out_shape=jax.ShapeDtypeStruct((n, width), F32),
        compiler_params=_cp("arbitrary"), name="head_norm",
    )(*args)


def _log_gamma():
    return np.log1p(-(np.float32(2.0) ** (-5.0 - np.arange(RET_HEADS, dtype=np.float32)))).astype(np.float32)


def _ret_finish(o, gn, gate):
    y = o * lax.rsqrt(jnp.mean(o * o, axis=-1, keepdims=True) + EPS) * gn
    return y * (gate * jax.nn.sigmoid(gate))


def _ret_prompt_kernel(lg_ref, q_ref, k_ref, v_ref, gt_ref, c_ref, s1_ref, s2_ref, st0_ref, gn_ref,
                       o_ref, st_ref):
    h = pl.program_id(1)
    c_idx = pl.program_id(2)
    cl = q_ref.shape[0]
    lg = lg_ref[h]

    @pl.when(c_idx == 0)
    def _():
        st_ref[...] = st0_ref[...]

    half = RET_DK // 2
    q = _rope(q_ref[...], c_ref[...], s1_ref[...], s2_ref[...], half)
    k = _rope(k_ref[...], c_ref[...], s1_ref[...], s2_ref[...], half) * (RET_DK ** -0.5)
    v = v_ref[...]
    ri = lax.broadcasted_iota(I32, (cl, cl), 0).astype(F32)
    ci = lax.broadcasted_iota(I32, (cl, cl), 1).astype(F32)
    rel = ri - ci
    decay = jnp.where(rel >= 0, jnp.exp(jnp.maximum(rel, 0.0) * lg), 0.0)
    qb, kb, vb = q.astype(BF16), k.astype(BF16), v.astype(BF16)
    inner = _nt_dot(qb, kb) * decay
    state = st_ref[...]
    rowi = lax.broadcasted_iota(I32, (cl, 1), 0).astype(F32)
    cross = jnp.exp((rowi + 1.0) * lg)
    o = jnp.dot(inner.astype(BF16), vb, preferred_element_type=F32)
    o = o + jnp.dot(qb, state.astype(BF16), preferred_element_type=F32) * cross
    k_dec = (k * jnp.exp((cl - 1.0 - rowi) * lg)).astype(BF16)
    st_ref[...] = jnp.exp(jnp.zeros((1, 1), F32) + cl * lg) * state + _tn_dot(k_dec, vb)
    o_ref[...] = _ret_finish(o, gn_ref[...], gt_ref[...])


def _retention_prompt(z3, tables, state0, gn):
    b, t, _ = z3.shape
    cl = RET_CHUNK
    assert t % cl == 0
    nc = t // cl
    qk_b, v_b = C_RQ // RET_DK, C_RV // RET_DV
    kk_b, g_b = C_RK // RET_DK, C_RG // RET_DV
    tspec = pl.BlockSpec((cl, RET_DK), lambda bi, h, c: (c, 0))
    o, st = pl.pallas_call(
        _ret_prompt_kernel,
        grid=(b, RET_HEADS, nc),
        in_specs=[pl.BlockSpec(memory_space=pltpu.SMEM),
                  pl.BlockSpec((None, cl, RET_DK), lambda bi, h, c: (bi, c, qk_b + h)),
                  pl.BlockSpec((None, cl, RET_DK), lambda bi, h, c: (bi, c, kk_b + h)),
                  pl.BlockSpec((None, cl, RET_DV), lambda bi, h, c: (bi, c, v_b + h)),
                  pl.BlockSpec((None, cl, RET_DV), lambda bi, h, c: (bi, c, g_b + h)),
                  tspec, tspec, tspec,
                  pl.BlockSpec((None, None, RET_DK, RET_DV), lambda bi, h, c: (bi, h, 0, 0)),
                  pl.BlockSpec((None, 1, RET_DV), lambda bi, h, c: (h, 0, 0))],
        out_specs=[pl.BlockSpec((None, cl, RET_DV), lambda bi, h, c: (bi, c, h)),
                   pl.BlockSpec((None, None, RET_DK, RET_DV), lambda bi, h, c: (bi, h, 0, 0))],
        out_shape=[jax.ShapeDtypeStruct((b, t, RET_HEADS * RET_DV), F32),
                   jax.ShapeDtypeStruct((b, RET_HEADS, RET_DK, RET_DV), F32)],
        compiler_params=_cp("arbitrary", "arbitrary", "arbitrary"), name="retention_prompt",
    )(jnp.asarray(_log_gamma()), z3, z3, z3, z3, *tables, state0, gn.reshape(RET_HEADS, 1, RET_DV))
    return o, st


def _ret_decode_kernel(q_ref, k_ref, v_ref, gt_ref, c_ref, s1_ref, s2_ref, st0_ref, gn_ref, o_ref, st_ref):
    lgs = _log_gamma()
    half = RET_DK // 2
    row0 = lax.broadcasted_iota(I32, (16, 1), 0) == 0
    for h in range(RET_HEADS):
        gamma = float(np.exp(lgs[h]))
        q = _rope(q_ref[:, h * RET_DK:(h + 1) * RET_DK], c_ref[...], s1_ref[...], s2_ref[...], half)
        k = _rope(k_ref[:, h * RET_DK:(h + 1) * RET_DK], c_ref[...], s1_ref[...], s2_ref[...], half)
        k = k * (RET_DK ** -0.5)
        v = v_ref[:, h * RET_DV:(h + 1) * RET_DV]
        state = st0_ref[h].astype(F32)
        inner = jnp.sum(q * k, axis=-1, keepdims=True)
        q16 = jnp.broadcast_to(q, (16, RET_DK)).astype(BF16)
        cross = jnp.dot(q16, state.astype(BF16), preferred_element_type=F32)[0:1]
        o = inner * v + cross * gamma
        k16 = jnp.where(row0, jnp.broadcast_to(k, (16, RET_DK)), 0.0).astype(BF16)
        v16 = jnp.broadcast_to(v, (16, RET_DV)).astype(BF16)
        st_ref[h] = gamma * state + _tn_dot(k16, v16)
        o_ref[:, h * RET_DV:(h + 1) * RET_DV] = _ret_finish(
            o, gn_ref[h], gt_ref[:, h * RET_DV:(h + 1) * RET_DV])


def _retention_decode(z3, tables, state0, gn):
    b = z3.shape[0]
    wq, wv = RET_HEADS * RET_DK, RET_HEADS * RET_DV
    tspec = pl.BlockSpec((1, RET_DK), lambda bi: (0, 0))
    o, st = pl.pallas_call(
        _ret_decode_kernel,
        grid=(b,),
        in_specs=[pl.BlockSpec((None, 1, wq), lambda bi: (bi, 0, C_RQ // wq)),
                  pl.BlockSpec((None, 1, wq), lambda bi: (bi, 0, C_RK // wq)),
                  pl.BlockSpec((None, 1, wv), lambda bi: (bi, 0, C_RV // wv)),
                  pl.BlockSpec((None, 1, wv), lambda bi: (bi, 0, C_RG // wv)),
                  tspec, tspec, tspec,
                  pl.BlockSpec((None, RET_HEADS, RET_DK, RET_DV), lambda bi: (bi, 0, 0, 0)),
                  pl.BlockSpec((RET_HEADS, 1, RET_DV), lambda bi: (0, 0, 0))],
        out_specs=[pl.BlockSpec((None, 1, wv), lambda bi: (bi, 0, 0)),
                   pl.BlockSpec((None, RET_HEADS, RET_DK, RET_DV), lambda bi: (bi, 0, 0, 0))],
        out_shape=[jax.ShapeDtypeStruct((b, 1, wv), F32),
                   jax.ShapeDtypeStruct((b, RET_HEADS, RET_DK, RET_DV), F32)],
        compiler_params=_cp("arbitrary"), name="retention_decode",
    )(z3, z3, z3, z3, *tables, state0, gn.reshape(RET_HEADS, 1, RET_DV))
    return o, st


def _compress_weights(pos_emb, w):
    g, dh = NSA_KV_HEADS, HEAD_DIM
    eye = jnp.eye(g, dtype=w.dtype)
    wbig = jnp.einsum("lde,gh->lgdhe", w, eye).reshape(CMP_BLOCK * g * dh, g * dh)
    pos_row = jnp.broadcast_to(pos_emb[:, None, :], (CMP_BLOCK, g, dh)).reshape(1, CMP_BLOCK * g * dh)
    return pos_row, wbig.astype(BF16)


def _masked_softmax_parts(parts, masks):
    sm = [jnp.where(m, s, NEG_INF) for s, m in zip(parts, masks)]
    mx = functools.reduce(jnp.maximum, [jnp.max(s, axis=-1, keepdims=True) for s in sm])
    ex = [jnp.exp(s - mx) for s in sm]
    den = functools.reduce(lambda a, b2: a + b2, [jnp.sum(e, axis=-1, keepdims=True) for e in ex])
    inv = 1.0 / den
    return [jnp.where(m, e * inv, 0.0) for e, m in zip(ex, masks)]


def _select_blocks(score, nsel):
    nbs = score.shape[-1]
    lane = lax.broadcasted_iota(I32, score.shape, 1)
    sel = jnp.zeros(score.shape, jnp.bool_)
    picks = []
    for _ in range(nsel):
        m = jnp.max(score, axis=-1, keepdims=True)
        idx = jnp.min(jnp.where(score == m, lane, nbs), axis=-1, keepdims=True)
        hit = lane == idx
        sel = jnp.logical_or(sel, hit)
        score = jnp.where(hit, -jnp.inf, score)
        picks.append(idx)
    return sel, picks


def _nsa_prompt_kernel(q_ref, ck_ref, cv_ref, ks_ref, vs_ref, kw_ref, vw_ref, gt_ref, o_ref, osel_ref):
    i = pl.program_id(1)
    qb = q_ref.shape[0]
    t = ks_ref.shape[0]
    nbs = t // SEL_BLOCK
    hg, dh = HEADS_PER_GROUP, HEAD_DIM
    kvw = NSA_KV_HEADS * dh
    scale = dh ** -0.5
    pos1 = i * qb + lax.broadcasted_iota(I32, (qb, 1), 0)
    pos = jnp.concatenate([pos1] * hg, axis=0)
    blk = lax.broadcasted_iota(I32, (1, nbs), 1)
    vis_e = blk * SEL_BLOCK + (CMP_BLOCK - 1) <= pos
    vis_o = blk * SEL_BLOCK + (SEL_BLOCK - 1) <= pos
    valid = blk * SEL_BLOCK <= pos1
    forced = jnp.logical_or(blk == 0, blk == pos1 // SEL_BLOCK)
    n_case = 4 if t % (4 * qb) == 0 else 1
    span = t // n_case
    case = (i * qb) // span

    def selected_attend(limit, q, sel_f, cs):
        key = lax.broadcasted_iota(I32, (1, limit), 1)
        expand = (lax.broadcasted_iota(I32, (nbs, limit), 1) // SEL_BLOCK
                  == lax.broadcasted_iota(I32, (nbs, limit), 0)).astype(BF16)
        selk = jnp.dot(sel_f, expand, preferred_element_type=F32)
        smask = jnp.logical_and(jnp.concatenate([selk] * hg, axis=0) > 0.5, key <= pos)
        (p_s,) = _masked_softmax_parts([_nt_dot(q, ks_ref[0:limit, cs].astype(BF16)) * scale], [smask])
        osel_ref[...] = jnp.dot(p_s.astype(BF16), vs_ref[0:limit, cs].astype(BF16), preferred_element_type=F32)

    wlen = WINDOW + qb
    wstart = pl.multiple_of(jnp.maximum(i * qb - WINDOW, 0), qb)
    wkey = wstart + lax.broadcasted_iota(I32, (1, wlen), 1)
    wdist = pos - wkey
    wmask = jnp.logical_and(wdist >= 0, wdist <= WINDOW)
    gates = gt_ref[...]

    for g in range(NSA_KV_HEADS):
        cs = slice(g * dh, (g + 1) * dh)
        q = jnp.concatenate([q_ref[:, (g * hg + hh) * dh:(g * hg + hh + 1) * dh] for hh in range(hg)],
                            axis=0).astype(BF16)
        co = slice(kvw + g * dh, kvw + (g + 1) * dh)
        ck_e, ck_o = ck_ref[:, cs].astype(BF16), ck_ref[:, co].astype(BF16)
        cv_e, cv_o = cv_ref[:, cs].astype(BF16), cv_ref[:, co].astype(BF16)
        p_e, p_o = _masked_softmax_parts([_nt_dot(q, ck_e) * scale, _nt_dot(q, ck_o) * scale], [vis_e, vis_o])
        o_cmp = (jnp.dot(p_e.astype(BF16), cv_e, preferred_element_type=F32)
                 + jnp.dot(p_o.astype(BF16), cv_o, preferred_element_type=F32))
        psum = p_e + p_o
        imp = functools.reduce(lambda a, b2: a + b2, [psum[hh * qb:(hh + 1) * qb] for hh in range(hg)])
        score = jnp.where(forced, FORCE, jnp.where(valid, imp, -FORCE))
        sel, _ = _select_blocks(score, min(SEL_TOPK, nbs))
        sel_f = jnp.where(sel, 1.0, 0.0).astype(BF16)
        for c in range(n_case):
            pl.when(case == c)(functools.partial(selected_attend, (c + 1) * span, q, sel_f, cs))
        o_sel = osel_ref[...]
        kwin = kw_ref[pl.ds(wstart, wlen), cs].astype(BF16)
        vwin = vw_ref[pl.ds(wstart, wlen), cs].astype(BF16)
        (p_w,) = _masked_softmax_parts([_nt_dot(q, kwin) * scale], [wmask])
        o_win = jnp.dot(p_w.astype(BF16), vwin, preferred_element_type=F32)
        for hh in range(hg):
            hd = g * hg + hh
            rs = slice(hh * qb, (hh + 1) * qb)
            o_ref[:, hd * dh:(hd + 1) * dh] = (gates[:, 3 * hd:3 * hd + 1] * o_cmp[rs]
                                                 + gates[:, 3 * hd + 1:3 * hd + 2] * o_sel[rs]
                                                 + gates[:, 3 * hd + 2:3 * hd + 3] * o_win[rs])


def _nsa_prompt(qn3, ck3, cv3, ksn3, z3, kwn3, gates3):
    b, t, _ = qn3.shape
    qb = Q_BLOCK
    kvw = NSA_KV_HEADS * HEAD_DIM
    assert t % qb == 0 and t >= WINDOW + qb and t % (2 * SEL_BLOCK) == 0
    nbs = ck3.shape[1]
    full = lambda cb: pl.BlockSpec((None, t, kvw), lambda bi, i: (bi, 0, cb))
    return pl.pallas_call(
        _nsa_prompt_kernel,
        grid=(b, t // qb),
        in_specs=[pl.BlockSpec((None, qb, NSA_HEADS * HEAD_DIM), lambda bi, i: (bi, i, 0)),
                  pl.BlockSpec((None, nbs, CMP_PER_SEL * kvw), lambda bi, i: (bi, 0, 0)),
                  pl.BlockSpec((None, nbs, CMP_PER_SEL * kvw), lambda bi, i: (bi, 0, 0)),
                  full(0), full(C_VS // kvw), full(0), full(C_VW // kvw),
                  pl.BlockSpec((None, qb, LANES), lambda bi, i: (bi, i, 0))],
        out_specs=pl.BlockSpec((None, qb, NSA_HEADS * HEAD_DIM), lambda bi, i: (bi, i, 0)),
        out_shape=jax.ShapeDtypeStruct((b, t, NSA_HEADS * HEAD_DIM), F32),
        scratch_shapes=[pltpu.VMEM((HEADS_PER_GROUP * qb, HEAD_DIM), F32)],
        compiler_params=_cp("arbitrary", "arbitrary"), name="nsa_prompt",
    )(qn3, ck3, cv3, ksn3, z3, kwn3, z3, gates3)


def _decode_q16(q_ref):
    dh = HEAD_DIM
    rows = [q_ref[:, h * dh:(h + 1) * dh] for h in range(NSA_HEADS)]
    return jnp.concatenate(rows + [jnp.zeros((16 - NSA_HEADS, dh), F32)], axis=0).astype(BF16)


def _nsa_decode_select_kernel(q_ref, ck_ref, cv_ref, ocmp_ref, idx_ref, *, past):
    dh, hg = HEAD_DIM, HEADS_PER_GROUP
    kvw = NSA_KV_HEADS * dh
    nhalf = ck_ref.shape[0]
    nbs = past // SEL_BLOCK + 1
    scale = dh ** -0.5
    q16 = _decode_q16(q_ref)
    row = lax.broadcasted_iota(I32, (16, 1), 0)
    blk = lax.broadcasted_iota(I32, (1, LANES), 1)
    blk_h = lax.broadcasted_iota(I32, (1, nhalf), 1)
    vis_e = blk_h * SEL_BLOCK + (CMP_BLOCK - 1) <= past
    vis_o = blk_h * SEL_BLOCK + (SEL_BLOCK - 1) <= past
    o_all = jnp.zeros((16, dh), F32)
    idx_row = jnp.zeros((1, LANES), I32)
    for g in range(NSA_KV_HEADS):
        cs = slice(g * dh, (g + 1) * dh)
        co = slice(kvw + g * dh, kvw + (g + 1) * dh)
        ck_e, ck_o = ck_ref[:, cs].astype(BF16), ck_ref[:, co].astype(BF16)
        cv_e, cv_o = cv_ref[:, cs].astype(BF16), cv_ref[:, co].astype(BF16)
        p_e, p_o = _masked_softmax_parts([_nt_dot(q16, ck_e) * scale, _nt_dot(q16, ck_o) * scale], [vis_e, vis_o])
        o_g = (jnp.dot(p_e.astype(BF16), cv_e, preferred_element_type=F32)
               + jnp.dot(p_o.astype(BF16), cv_o, preferred_element_type=F32))
        in_g = jnp.logical_and(row >= g * hg, row < (g + 1) * hg)
        o_all = jnp.where(in_g, o_g, o_all)
        imp = jnp.sum(jnp.where(in_g, p_e + p_o, 0.0), axis=0, keepdims=True)
        score = jnp.where(blk_h == 0, FORCE, imp)
        nsel = min(SEL_TOPK, nbs)
        _, past_picks = _select_blocks(score, min(nsel - 1, nhalf))
        picks = past_picks[:1] + [jnp.full((1, 1), nhalf, I32)] + past_picks[1:]
        for kk, p in enumerate(picks):
            idx_row = jnp.where(blk == g * SEL_TOPK + kk, p, idx_row)
        for kk in range(len(picks), SEL_TOPK):
            idx_row = jnp.where(blk == g * SEL_TOPK + kk, -1, idx_row)
    ocmp_ref[...] = o_all
    idx_ref[...] = idx_row


def _nsa_decode_select(qn3, ck3, cv3, past):
    b = qn3.shape[0]
    nhalf = ck3.shape[1]
    kvw = NSA_KV_HEADS * HEAD_DIM
    assert nhalf % 8 == 0 and nhalf == past // SEL_BLOCK
    return pl.pallas_call(
        functools.partial(_nsa_decode_select_kernel, past=past),
        grid=(b,),
        in_specs=[pl.BlockSpec((None, 1, NSA_HEADS * HEAD_DIM), lambda bi: (bi, 0, 0)),
                  pl.BlockSpec((None, nhalf, CMP_PER_SEL * kvw), lambda bi: (bi, 0, 0)),
                  pl.BlockSpec((None, nhalf, CMP_PER_SEL * kvw), lambda bi: (bi, 0, 0))],
        out_specs=[pl.BlockSpec((None, 16, HEAD_DIM), lambda bi: (bi, 0, 0)),
                   pl.BlockSpec((None, 1, LANES), lambda bi: (bi, 0, 0))],
        out_shape=[jax.ShapeDtypeStruct((b, 16, HEAD_DIM), F32),
                   jax.ShapeDtypeStruct((b, 1, LANES), I32)],
        compiler_params=_cp("arbitrary"), name="nsa_decode_select",
    )(qn3, ck3, cv3)


def _nsa_decode_attend_kernel(*refs, past, nsel):
    n_blk = NSA_KV_HEADS * nsel
    rows_ref = refs[0]
    kb_refs = refs[1:1 + n_blk]
    vb_refs = refs[1 + n_blk:1 + 2 * n_blk]
    (q_ref, idx_ref, ksn_ref, vs_ref, wk_ref, wv_ref, kwn_ref, vw_ref, ocmp_ref, gt_ref, o_ref) = refs[1 + 2 * n_blk:]
    del rows_ref
    dh, hg = HEAD_DIM, HEADS_PER_GROUP
    scale = dh ** -0.5
    self_blk = past // SEL_BLOCK
    q16 = _decode_q16(q_ref)
    q16f = q16.astype(F32)
    row = lax.broadcasted_iota(I32, (16, 1), 0)
    lane = lax.broadcasted_iota(I32, (1, LANES), 1)
    idx_row = idx_ref[...]
    o_sel = jnp.zeros((16, dh), F32)
    o_win = jnp.zeros((16, dh), F32)
    for g in range(NSA_KV_HEADS):
        cs = slice(g * dh, (g + 1) * dh)
        in_g = jnp.logical_and(row >= g * hg, row < (g + 1) * hg)
        k_self = ksn_ref[:, cs].astype(BF16).astype(F32)
        v_self = vs_ref[:, cs].astype(BF16).astype(F32)
        s_self = jnp.sum(q16f * k_self, axis=-1, keepdims=True) * scale
        parts, masks, vals = [], [], []
        self_sel = jnp.zeros((1, 1), jnp.bool_)
        for kk in range(nsel):
            pick = jnp.max(jnp.where(lane == g * SEL_TOPK + kk, idx_row, -1), axis=-1, keepdims=True)
            in_pool = jnp.logical_and(pick >= 0, pick < self_blk)
            self_sel = jnp.logical_or(self_sel, pick == self_blk)
            kb = kb_refs[g * nsel + kk][:, g, :].astype(BF16)
            parts.append(_nt_dot(q16, kb) * scale)
            masks.append(in_pool)
            vals.append(vb_refs[g * nsel + kk][:, g, :].astype(BF16))
        parts.append(s_self)
        masks.append(self_sel)
        probs = _masked_softmax_parts(parts, masks)
        acc = probs[-1].astype(BF16).astype(F32) * v_self
        for p, vv in zip(probs[:-1], vals):
            acc = acc + jnp.dot(p.astype(BF16), vv, preferred_element_type=F32)
        o_sel = jnp.where(in_g, acc, o_sel)
        kw_self = kwn_ref[:, cs].astype(BF16).astype(F32)
        vw_self = vw_ref[:, cs].astype(BF16).astype(F32)
        sw_self = jnp.sum(q16f * kw_self, axis=-1, keepdims=True) * scale
        wparts = [_nt_dot(q16, wk_ref[:, g, :].astype(BF16)) * scale, sw_self]
        wmasks = [lane[:, :1] >= 0, lane[:, :1] >= 0]
        pw, pws = _masked_softmax_parts(wparts, wmasks)
        accw = (jnp.dot(pw.astype(BF16), wv_ref[:, g, :].astype(BF16), preferred_element_type=F32)
                + pws.astype(BF16).astype(F32) * vw_self)
        o_win = jnp.where(in_g, accw, o_win)
    gates = gt_ref[...]
    o_cmp = ocmp_ref[...]
    for hd in range(NSA_HEADS):
        o_ref[:, hd * dh:(hd + 1) * dh] = (gates[:, 3 * hd:3 * hd + 1] * o_cmp[hd:hd + 1]
                                             + gates[:, 3 * hd + 1:3 * hd + 2] * o_sel[hd:hd + 1]
                                             + gates[:, 3 * hd + 2:3 * hd + 3] * o_win[hd:hd + 1])


def _nsa_decode_attend(pool_rows, sel_k_pool, sel_v_pool, qn3, idx3, ksn3, z3, win_k, win_v, kwn3, ocmp, gates3,
                       past, nsel):
    b = qn3.shape[0]
    kvw = NSA_KV_HEADS * HEAD_DIM
    n_blk = NSA_KV_HEADS * nsel
    nbuf = win_k.shape[2]
    per_page = sel_k_pool.shape[2] // SEL_BLOCK

    def blk_spec(tt):
        return pl.BlockSpec((None, None, SEL_BLOCK, NSA_KV_HEADS, HEAD_DIM),
                            lambda bi, r: (0, r[bi * n_blk + tt] // per_page, r[bi * n_blk + tt] % per_page, 0, 0))

    def row_spec(cb, w=kvw):
        return pl.BlockSpec((None, 1, w), lambda bi, r: (bi, 0, cb))

    wspec = pl.BlockSpec((None, None, nbuf, NSA_KV_HEADS, HEAD_DIM), lambda bi, r: (0, bi, 0, 0, 0))
    in_specs = ([blk_spec(tt) for tt in range(n_blk)] + [blk_spec(tt) for tt in range(n_blk)]
                + [pl.BlockSpec((None, 1, NSA_HEADS * HEAD_DIM), lambda bi, r: (bi, 0, 0)),
                   pl.BlockSpec((None, 1, LANES), lambda bi, r: (bi, 0, 0)),
                   row_spec(0), row_spec(C_VS // kvw),
                   wspec, wspec,
                   row_spec(0), row_spec(C_VW // kvw),
                   pl.BlockSpec((None, 16, HEAD_DIM), lambda bi, r: (bi, 0, 0)),
                   pl.BlockSpec((None, 1, LANES), lambda bi, r: (bi, 0, 0))])
    return pl.pallas_call(
        functools.partial(_nsa_decode_attend_kernel, past=past, nsel=nsel),
        grid_spec=pltpu.PrefetchScalarGridSpec(
            num_scalar_prefetch=1, grid=(b,), in_specs=in_specs,
            out_specs=pl.BlockSpec((None, 1, NSA_HEADS * HEAD_DIM), lambda bi, r: (bi, 0, 0))),
        out_shape=jax.ShapeDtypeStruct((b, 1, NSA_HEADS * HEAD_DIM), F32),
        compiler_params=_cp("arbitrary"), name="nsa_decode_attend",
    )(pool_rows, *([sel_k_pool] * n_blk), *([sel_v_pool] * n_blk),
      qn3, idx3, ksn3, z3, win_k, win_v, kwn3, z3, ocmp, gates3)


def _mem_attn_kernel(q_ref, g_ref, mk_ref, mv_ref, o_ref, *, cache_layout):
    dh = MEM_HEAD_DIM
    tq = q_ref.shape[0]
    rows = max(tq, 16)
    for h in range(MEM_HEADS):
        cs = slice(h * dh, (h + 1) * dh)
        x = q_ref[:, cs]
        q = x * lax.rsqrt(jnp.mean(x * x, axis=-1, keepdims=True) + EPS) * g_ref[...]
        if rows != tq:
            q = jnp.broadcast_to(q, (rows, dh))
        mk = mk_ref[:, h, :] if cache_layout else mk_ref[:, cs]
        mv = mv_ref[:, h, :] if cache_layout else mv_ref[:, cs]
        s = _nt_dot(q.astype(BF16), mk.astype(BF16)) * (dh ** -0.5)
        m = jnp.max(s, axis=-1, keepdims=True)
        e = jnp.exp(s - m)
        p = e / jnp.sum(e, axis=-1, keepdims=True)
        o = jnp.dot(p.astype(BF16), mv.astype(BF16), preferred_element_type=F32)
        o_ref[:, cs] = o[0:tq]


def _mem_attention(z3, gq, mk, mv, cache_layout=False):
    b, t, _ = z3.shape
    w = MEM_HEADS * MEM_HEAD_DIM
    tq = min(t, 512)
    assert t % tq == 0
    if cache_layout:
        m = mk.shape[2]
        kv_spec = pl.BlockSpec((None, None, m, MEM_HEADS, MEM_HEAD_DIM), lambda bi, i: (0, bi, 0, 0, 0))
    else:
        m = mk.shape[1]
        kv_spec = pl.BlockSpec((None, m, w), lambda bi, i: (bi, 0, 0))
    return pl.pallas_call(
        functools.partial(_mem_attn_kernel, cache_layout=cache_layout),
        grid=(b, t // tq),
        in_specs=[pl.BlockSpec((None, tq, w), lambda bi, i: (bi, i, C_MQ // w)),
                  pl.BlockSpec((1, MEM_HEAD_DIM), lambda bi, i: (0, 0)),
                  kv_spec, kv_spec],
        out_specs=pl.BlockSpec((None, tq, w), lambda bi, i: (bi, i, 0)),
        out_shape=jax.ShapeDtypeStruct((b, t, w), F32),
        compiler_params=_cp("arbitrary", "arbitrary"), name="mem_attention",
    )(z3, gq.reshape(1, MEM_HEAD_DIM), mk, mv)


def _sigmoid_cols_kernel(x_ref, o_ref):
    o_ref[...] = jax.nn.sigmoid(x_ref[...])


def _nsa_gates(z, col):
    n = z.shape[0]
    tm = min(n, 1024)
    return pl.pallas_call(
        _sigmoid_cols_kernel, grid=(n // tm,),
        in_specs=[pl.BlockSpec((tm, LANES), lambda i: (i, col // LANES))],
        out_specs=pl.BlockSpec((tm, LANES), lambda i: (i, 0)),
        out_shape=jax.ShapeDtypeStruct((n, LANES), F32),
        compiler_params=_cp("arbitrary"), name="nsa_gates",
    )(z)


def _merge_kernel(r_ref, n_ref, m_ref, g0_ref, g1_ref, g2_ref, w_ref, o_ref):
    acc = jnp.zeros(o_ref.shape, F32)
    for c, (b_ref, g_ref) in enumerate(((r_ref, g0_ref), (n_ref, g1_ref), (m_ref, g2_ref))):
        up = jnp.dot(b_ref[...].astype(BF16), w_ref[c], preferred_element_type=F32)
        acc = acc + jax.nn.sigmoid(g_ref[...]) * up
    o_ref[...] = acc.astype(BF16)


def _merge(o_ret, o_nsa, o_mem, z, wb, d_model):
    n, bw = o_ret.shape
    tm = min(n, 512)
    tn = 512
    gb = C_MG // tn
    nj = d_model // tn
    bspec = pl.BlockSpec((tm, bw), lambda i, j: (i, 0))
    gspec = lambda c: pl.BlockSpec((tm, tn), lambda i, j: (i, gb + c * nj + j))
    return pl.pallas_call(
        _merge_kernel, grid=(n // tm, nj),
        in_specs=[bspec, bspec, bspec, gspec(0), gspec(1), gspec(2),
                  pl.BlockSpec((N_BRANCH, bw, tn), lambda i, j: (0, 0, j))],
        out_specs=pl.BlockSpec((tm, tn), lambda i, j: (i, j)),
        out_shape=jax.ShapeDtypeStruct((n, d_model), BF16),
        compiler_params=_cp("arbitrary", "arbitrary"), name="branch_merge",
    )(o_ret, o_nsa, o_mem, z, z, z, wb)


def _topk_rows(s, kk):
    e = s.shape[0]
    ridx = lax.broadcasted_iota(I32, s.shape, 0)
    vals, idxs = [], []
    for _ in range(kk):
        m = jnp.max(s, axis=0, keepdims=True)
        i = jnp.min(jnp.where(s == m, ridx, e), axis=0, keepdims=True)
        vals.append(m)
        idxs.append(i)
        s = jnp.where(ridx == i, -jnp.inf, s)
    return jnp.concatenate(vals, axis=0), jnp.concatenate(idxs, axis=0)


def _pair_pieces(kk):
    pieces, cur = [], []

    def flush(rows):
        pieces.append(rows + [None] * (8 - len(rows)))

    for a in range(kk):
        grp = [(a, b) for b in range(kk // (a + 1))]
        if cur and len(cur) + len(grp) > 8:
            flush(cur)
            cur = []
        cur = cur + grp
        while len(cur) >= 8:
            flush(cur[:8])
            cur = cur[8:]
    if cur:
        flush(cur)
    return pieces


def _rows_by_runs(x, ids):
    if ids[0] % 8 == 0 and ids == list(range(ids[0], ids[0] + 8)):
        return x[ids[0]:ids[0] + 8]
    r = lax.broadcasted_iota(I32, (8, 1), 0)
    runs = []
    for p, i in enumerate(ids):
        if not runs or runs[-1][1] != i:
            runs.append((p, i))
    out = jnp.broadcast_to(x[runs[-1][1]:runs[-1][1] + 1], (8, x.shape[1]))
    for (_, i), (nxt, _) in zip(reversed(runs[:-1]), reversed(runs[1:])):
        out = jnp.where(r < nxt, x[i:i + 1], out)
    return out


def _peer_route_kernel(q_ref, sk_ref, i_ref, j_ref, g_ref):
    kk = PEER_TOPK
    tl = q_ref.shape[0]
    st = _nt_dot(sk_ref[...], q_ref[...].astype(BF16))
    v0, i0 = _topk_rows(st[0:PEER_KEYS], kk)
    v1, i1 = _topk_rows(st[PEER_KEYS:2 * PEER_KEYS], kk)
    cand, ci, cj = [], [], []
    for piece in _pair_pieces(kk):
        ra = [p[0] if p else 0 for p in piece]
        rb = [p[1] if p else 0 for p in piece]
        live = lax.broadcasted_iota(I32, (8, 1), 0) < sum(p is not None for p in piece)
        cand.append(jnp.where(live, _rows_by_runs(v0, ra) + _rows_by_runs(v1, rb), -jnp.inf))
        ci.append(_rows_by_runs(i0, ra))
        cj.append(_rows_by_runs(i1, rb))
    cand, ci, cj = (jnp.concatenate(c, axis=0) for c in (cand, ci, cj))
    ridx = lax.broadcasted_iota(I32, cand.shape, 0)
    sc, si, sj = [], [], []
    for _ in range(kk):
        m = jnp.max(cand, axis=0, keepdims=True)
        r = jnp.min(jnp.where(cand == m, ridx, cand.shape[0]), axis=0, keepdims=True)
        hit = ridx == r
        sc.append(m)
        si.append(jnp.max(jnp.where(hit, ci, -1), axis=0, keepdims=True))
        sj.append(jnp.max(jnp.where(hit, cj, -1), axis=0, keepdims=True))
        cand = jnp.where(hit, -jnp.inf, cand)
    sc = jnp.concatenate(sc, axis=0)
    e = jnp.exp(sc - jnp.max(sc, axis=0, keepdims=True))
    g_ref[...] = e / jnp.sum(e, axis=0, keepdims=True)
    i_ref[...] = jnp.concatenate(si, axis=0)
    j_ref[...] = jnp.concatenate(sj, axis=0)


def _peer_route(q, skbd):
    n = q.shape[0]
    tl = min(n, LANES)
    kk = PEER_TOPK
    slots = PEER_HEADS * kk
    ospec = pl.BlockSpec((kk, tl), lambda i, h: (h, i))
    return pl.pallas_call(
        _peer_route_kernel, grid=(n // tl, PEER_HEADS),
        in_specs=[pl.BlockSpec((tl, PEER_DKEY), lambda i, h: (i, h)),
                  pl.BlockSpec((None, 2 * PEER_KEYS, PEER_DKEY), lambda i, h: (h, 0, 0))],
        out_specs=[ospec, ospec, ospec],
        out_shape=[jax.ShapeDtypeStruct((slots, n), I32), jax.ShapeDtypeStruct((slots, n), I32),
                   jax.ShapeDtypeStruct((slots, n), F32)],
        compiler_params=_cp("arbitrary", "arbitrary"), name="peer_route",
    )(q, skbd)


TOKEN_GROUP = 8


def _peer_weights_kernel(i_ref, j_ref, g_ref, o_ref, it_scr, jt_scr, gt_scr):
    it_scr[...] = i_ref[...].T
    jt_scr[...] = j_ref[...].T
    gt_scr[...] = g_ref[...].T
    key = lax.broadcasted_iota(I32, (PEER_KEYS, i_ref.shape[0]), 0)

    def body(nb, carry):
        for t in range(TOKEN_GROUP):
            n = nb * TOKEN_GROUP + t
            irow = it_scr[pl.ds(n, 1), :]
            jrow = jt_scr[pl.ds(n, 1), :]
            grow = gt_scr[pl.ds(n, 1), :]
            a = jnp.where(key == irow, grow, 0.0).astype(BF16)
            b = jnp.where(key == jrow, 1.0, 0.0).astype(BF16)
            o_ref[nb, pl.ds(t, PEER_KEYS, stride=TOKEN_GROUP), :] = _nt_dot(a, b)
        return carry

    lax.fori_loop(0, o_ref.shape[0], body, 0)


def _peer_weights(it, jt, gt):
    slots, n = it.shape
    tb = min(n, LANES)
    assert tb % TOKEN_GROUP == 0
    ispec = pl.BlockSpec((slots, tb), lambda i: (0, i))
    out = pl.pallas_call(
        _peer_weights_kernel, grid=(n // tb,),
        in_specs=[ispec, ispec, ispec],
        out_specs=pl.BlockSpec((tb // TOKEN_GROUP, PEER_KEYS * TOKEN_GROUP, PEER_KEYS), lambda i: (i, 0, 0)),
        out_shape=jax.ShapeDtypeStruct((n // TOKEN_GROUP, PEER_KEYS * TOKEN_GROUP, PEER_KEYS), F32),
        scratch_shapes=[pltpu.VMEM((tb, slots), I32), pltpu.VMEM((tb, slots), I32), pltpu.VMEM((tb, slots), F32)],
        compiler_params=_cp("arbitrary"), name="peer_weights",
    )(it, jt, gt)
    return out.reshape(n // TOKEN_GROUP, PEER_KEYS, TOKEN_GROUP, PEER_KEYS)


def _peer_dense_kernel(h_ref, g_ref, u_ref, v_ref, x_ref, o_ref, acc_ref):
    j = pl.program_id(1)

    @pl.when(j == 0)
    def _():
        acc_ref[...] = jnp.zeros_like(acc_ref)

    a = jnp.dot(h_ref[...], u_ref[...], preferred_element_type=F32)
    act = 0.5 * a * (1.0 + lax.erf(a * (2.0 ** -0.5)))
    tm = h_ref.shape[0]
    w = jnp.concatenate([(g_ref[:, ii].reshape(tm, PEER_KEYS)
                          * act[:, ii * PEER_KEYS:(ii + 1) * PEER_KEYS]).astype(BF16)
                         for ii in range(g_ref.shape[1])], axis=1)
    acc_ref[...] += jnp.dot(w, v_ref[...], preferred_element_type=F32)

    @pl.when(j == pl.num_programs(1) - 1)
    def _():
        o_ref[...] = x_ref[...] + acc_ref[...]


def _peer_dense(hn, gw, ut, vv, x):
    n, d = hn.shape
    ne = ut.shape[1]
    tm = min(n, 512)
    te = 512
    return pl.pallas_call(
        _peer_dense_kernel, grid=(n // tm, ne // te),
        in_specs=[pl.BlockSpec((tm, d), lambda i, j: (i, 0)),
                  pl.BlockSpec((tm // TOKEN_GROUP, te // PEER_KEYS, TOKEN_GROUP, PEER_KEYS),
                               lambda i, j: (i, j, 0, 0)),
                  pl.BlockSpec((d, te), lambda i, j: (0, j)),
                  pl.BlockSpec((te, d), lambda i, j: (j, 0)),
                  pl.BlockSpec((tm, d), lambda i, j: (i, 0))],
        out_specs=pl.BlockSpec((tm, d), lambda i, j: (i, 0)),
        out_shape=jax.ShapeDtypeStruct((n, d), F32),
        scratch_shapes=[pltpu.VMEM((tm, d), F32)],
        compiler_params=_cp("arbitrary", "arbitrary"), name="peer_dense",
    )(hn, gw, ut, vv, x)


def _prepare_weights(norm_attn, w_in, cmp_pos_k, cmp_pos_v, cmp_w_k, cmp_w_v, w_mem_kv, w_branch, w_out,
                     peer_wq, peer_subkeys, peer_u, peer_v):
    d = w_in.shape[0]
    ng_w = NSA_HEADS * 3
    o_ng = 5632
    o_mq = o_ng + ng_w
    o_mg = o_mq + MEM_HEADS * MEM_HEAD_DIM
    width = C_MG + N_BRANCH * d + LANES
    width = -(-width // Z_TILE) * Z_TILE
    w_r = jnp.concatenate([w_in[:, 0:4096], w_in[:, o_mq:o_mg], w_in[:, 4096:o_ng], w_in[:, o_mg:],
                           w_in[:, o_ng:o_mq],
                           jnp.zeros((d, width - (C_MG + N_BRANCH * d + ng_w)), w_in.dtype)], axis=1).astype(BF16)
    pos_k, wk_big = _compress_weights(cmp_pos_k, cmp_w_k)
    pos_v, wv_big = _compress_weights(cmp_pos_v, cmp_w_v)
    half = PEER_DKEY // 2
    sk = peer_subkeys.astype(BF16)
    zk = jnp.zeros((PEER_HEADS, PEER_KEYS, half), BF16)
    skbd = jnp.concatenate([jnp.concatenate([sk[:, 0], zk], axis=2),
                            jnp.concatenate([zk, sk[:, 1]], axis=2)], axis=1)
    return dict(w_in=w_r, c_ng=C_MG + N_BRANCH * d, pos_k=pos_k, wk_big=wk_big, pos_v=pos_v, wv_big=wv_big,
                w_mem_kv=w_mem_kv.astype(BF16), w_branch=w_branch.astype(BF16), w_out=w_out.astype(BF16),
                peer_wq=peer_wq.astype(BF16), skbd=skbd, ut=peer_u.astype(BF16).T, vv=peer_v.astype(BF16))


def _layer_common(x, seq, pos, norm_attn, nsa_q_norm, nsa_k_norm, pw):
    z = _prologue_matmul(x, norm_attn, pw["w_in"], mode="norm", name="in_proj", tm=1024)
    nsa_tabs = _rope_tables(pos, ROPE_DIMS, ROPE_THETA, HEAD_DIM)
    half = ROPE_DIMS // 2
    kvw = NSA_KV_HEADS * HEAD_DIM
    qn = _headnorm(z, C_NQ, NSA_HEADS * HEAD_DIM, nsa_q_norm, HEAD_DIM, seq, nsa_tabs, half)
    kcn = _headnorm(z, C_KC, kvw, nsa_k_norm[0], HEAD_DIM, seq, nsa_tabs, half)
    ksn = _headnorm(z, C_KS, kvw, nsa_k_norm[1], HEAD_DIM, seq, nsa_tabs, half)
    kwn = _headnorm(z, C_KW, kvw, nsa_k_norm[2], HEAD_DIM, seq, nsa_tabs, half)
    gates = _nsa_gates(z, pw["c_ng"])
    ret_tabs = _rope_tables(pos, RET_DK, RET_THETA, RET_DK)
    return z, qn, kcn, ksn, kwn, gates, ret_tabs


def _layer_tail(x, z, o_ret, o_nsa, o_mem, norm_ffn, pw):
    d = x.shape[1]
    merged = _merge(o_ret, o_nsa, o_mem, z, pw["w_branch"], d)
    x1 = _prologue_matmul(merged, jnp.ones((d,), F32), pw["w_out"], mode="none", residual=x, name="out_proj")
    q, hn = _prologue_matmul(x1, norm_ffn, pw["peer_wq"], mode="norm", emit_h=True, name="peer_query")
    it, jt, gt = _peer_route(q, pw["skbd"])
    return _peer_dense(hn, _peer_weights(it, jt, gt), pw["ut"], pw["vv"], x1)


def kernel(x_prompt, x_sample, cache_cmp_k, cache_cmp_v, cache_sel_k, cache_sel_v, cache_win_k, cache_win_v,
           state_ret, cache_mem_k, cache_mem_v, page_table, mem_prompt, norm_attn, w_in, ret_gn, nsa_q_norm,
           nsa_k_norm, cmp_pos_k, cmp_pos_v, cmp_w_k, cmp_w_v, norm_mem, w_mem_kv, mem_q_norm, mem_k_norm,
           w_branch, w_out, norm_ffn, peer_wq, peer_subkeys, peer_u, peer_v):
    depth = w_in.shape[0]
    assert depth == 1
    l = 0
    bp, t, d = x_prompt.shape
    bs, ts, _ = x_sample.shape
    assert ts == 1
    n_pool, page, g_kv, dh = cache_cmp_k.shape[1:]
    n_pages = page_table.shape[1]
    past = n_pages * page
    kvw = g_kv * dh
    memw = MEM_HEADS * MEM_HEAD_DIM
    pw = _prepare_weights(norm_attn[l], w_in[l], cmp_pos_k[l], cmp_pos_v[l], cmp_w_k[l], cmp_w_v[l], w_mem_kv[l],
                          w_branch[l], w_out[l], peer_wq[l], peer_subkeys[l], peer_u[l], peer_v[l])
    cmp_cols = CMP_BLOCK * kvw

    n_p = bp * t
    xp = x_prompt.reshape(n_p, d)
    m_tok = mem_prompt.shape[1]
    mkv = _prologue_matmul(mem_prompt.reshape(bp * m_tok, d), norm_mem[l], pw["w_mem_kv"], mode="norm",
                           name="mem_kv_proj")
    mk_p = _headnorm(mkv, 0, memw, mem_k_norm[l], MEM_HEAD_DIM, m_tok)
    mv_p = mkv[:, memw:]
    z, qn, kcn, ksn, kwn, gates, ret_tabs = _layer_common(xp, t, jnp.arange(t), norm_attn[l], nsa_q_norm[l],
                                                         nsa_k_norm[l], pw)
    z3 = z.reshape(bp, t, z.shape[1])
    o_ret, st_p = _retention_prompt(z3, ret_tabs, jnp.zeros((bp, RET_HEADS, RET_DK, RET_DV), F32), ret_gn[l])
    vc_p = z[:, C_VC:C_VC + kvw]
    nbc = t // CMP_BLOCK
    ck_p = _prologue_matmul(kcn.reshape(bp * nbc, cmp_cols), pw["pos_k"], pw["wk_big"], mode="bias", tm=256,
                            name="compress_k")
    cv_p = _prologue_matmul(vc_p.reshape(bp * nbc, cmp_cols), pw["pos_v"], pw["wv_big"], mode="bias", tm=256,
                            name="compress_v")
    pair = CMP_PER_SEL * kvw
    o_nsa = _nsa_prompt(qn.reshape(bp, t, -1), ck_p.reshape(bp, nbc // CMP_PER_SEL, pair),
                        cv_p.reshape(bp, nbc // CMP_PER_SEL, pair),
                        ksn.reshape(bp, t, kvw), z3, kwn.reshape(bp, t, kvw), gates.reshape(bp, t, LANES))
    o_mem = _mem_attention(z3, mem_q_norm[l], mk_p.reshape(bp, m_tok, memw), mv_p.reshape(bp, m_tok, memw))
    y_p = _layer_tail(xp, z, o_ret.reshape(n_p, -1), o_nsa.reshape(n_p, -1), o_mem.reshape(n_p, -1), norm_ffn[l], pw)

    nbuf_p = min(WINDOW, t)
    kv5 = lambda a, bb, tt: a.reshape(1, bb, tt, g_kv, dh)
    outs_p = (kv5(kcn, bp, t), kv5(vc_p, bp, t), kv5(ksn, bp, t), kv5(z[:, C_VS:C_VS + kvw], bp, t),
              kv5(kwn, bp, t)[:, :, t - nbuf_p:], kv5(z[:, C_VW:C_VW + kvw], bp, t)[:, :, t - nbuf_p:],
              st_p[None], mk_p.reshape(1, bp, m_tok, MEM_HEADS, MEM_HEAD_DIM),
              mv_p.reshape(1, bp, m_tok, MEM_HEADS, MEM_HEAD_DIM))

    xs = x_sample.reshape(bs, d)
    zs, qn_s, kcn_s, ksn_s, kwn_s, gates_s, ret_tabs_s = _layer_common(
        xs, 1, jnp.full((1,), past, I32), norm_attn[l], nsa_q_norm[l], nsa_k_norm[l], pw)
    zs3 = zs.reshape(bs, 1, zs.shape[1])
    o_ret_s, st_s = _retention_decode(zs3, ret_tabs_s, state_ret[l], ret_gn[l])
    per_page = page // CMP_BLOCK
    ckp = _prologue_matmul(cache_cmp_k[l].reshape(n_pool * per_page, cmp_cols), pw["pos_k"], pw["wk_big"],
                           mode="bias", tm=256, name="compress_pool_k")
    cvp = _prologue_matmul(cache_cmp_v[l].reshape(n_pool * per_page, cmp_cols), pw["pos_v"], pw["wv_big"],
                           mode="bias", tm=256, name="compress_pool_v")
    ck_s = ckp.reshape(n_pool, per_page, kvw)[page_table].reshape(bs, n_pages * per_page // CMP_PER_SEL, pair)
    cv_s = cvp.reshape(n_pool, per_page, kvw)[page_table].reshape(bs, n_pages * per_page // CMP_PER_SEL, pair)
    qn_s3 = qn_s.reshape(bs, 1, -1)
    ocmp_s, idx_s = _nsa_decode_select(qn_s3, ck_s, cv_s, past)
    nbs = past // SEL_BLOCK + 1
    nsel = min(SEL_TOPK, nbs)
    per_page_sel = page // SEL_BLOCK
    picks = jnp.clip(idx_s[:, 0, :NSA_KV_HEADS * SEL_TOPK].reshape(bs, NSA_KV_HEADS, SEL_TOPK)[:, :, :nsel],
                     0, nbs - 2)
    pool_rows = (jnp.take_along_axis(page_table, (picks // per_page_sel).reshape(bs, -1), axis=1) * per_page_sel
                 + (picks % per_page_sel).reshape(bs, -1)).reshape(-1).astype(I32)
    o_nsa_s = _nsa_decode_attend(
        pool_rows, cache_sel_k, cache_sel_v, qn_s3, idx_s, ksn_s.reshape(bs, 1, kvw), zs3, cache_win_k, cache_win_v,
        kwn_s.reshape(bs, 1, kvw), ocmp_s, gates_s.reshape(bs, 1, LANES), past, nsel)
    o_mem_s = _mem_attention(zs3, mem_q_norm[l], cache_mem_k, cache_mem_v, cache_layout=True)
    y_s = _layer_tail(xs, zs, o_ret_s.reshape(bs, -1), o_nsa_s.reshape(bs, -1), o_mem_s.reshape(bs, -1),
                      norm_ffn[l], pw)

    nbuf_s = cache_win_k.shape[2]
    win_k_s = jnp.concatenate([cache_win_k[l], kv5(kwn_s, bs, 1)[0]], axis=1)[None, :, -nbuf_s:]
    win_v_s = jnp.concatenate([cache_win_v[l], kv5(zs[:, C_VW:C_VW + kvw], bs, 1)[0]], axis=1)[None, :, -nbuf_s:]
    outs_s = (kv5(kcn_s, bs, 1), kv5(zs[:, C_VC:C_VC + kvw], bs, 1), kv5(ksn_s, bs, 1),
              kv5(zs[:, C_VS:C_VS + kvw], bs, 1), win_k_s, win_v_s, st_s[None].astype(x_sample.dtype))
    return (y_p.reshape(bp, t, d), y_s.reshape(bs, ts, d)) + outs_p + outs_s
```

```python
import functools

import numpy as np
import jax
import jax.numpy as jnp
from jax import lax
from jax.experimental import pallas as pl
from jax.experimental.pallas import tpu as pltpu

F32 = jnp.float32
BF16 = jnp.bfloat16
I32 = jnp.int32

EPS = 1e-6
NEG_INF = -1e30
FORCE = 1e4

RET_HEADS, RET_DK, RET_DV, RET_CHUNK, RET_THETA = 4, 128, 256, 128, 10000.0
NSA_HEADS, NSA_KV_HEADS, HEAD_DIM = 8, 2, 128
HEADS_PER_GROUP = NSA_HEADS // NSA_KV_HEADS
CMP_BLOCK, SEL_BLOCK, SEL_TOPK, WINDOW, Q_BLOCK = 32, 64, 4, 256, 128
CMP_PER_SEL = SEL_BLOCK // CMP_BLOCK
ROPE_THETA, ROPE_DIMS = 500000.0, HEAD_DIM // 4
MEM_HEADS, MEM_HEAD_DIM = 4, 256
PEER_HEADS, PEER_KEYS, PEER_DKEY, PEER_TOPK = 8, 128, 128, 16
N_BRANCH = 3
LANES = 128

C_RQ, C_RK, C_RV, C_RG, C_NQ, C_MQ = 0, 512, 1024, 2048, 3072, 4096
C_KC, C_VC, C_KS, C_VS, C_KW, C_VW = 5120, 5376, 5632, 5888, 6144, 6400
C_MG = 6656
Z_TILE = 512


def _cp(*sem):
    return pltpu.CompilerParams(dimension_semantics=sem, vmem_limit_bytes=56 * 1024 * 1024)


def _nt_dot(a, b):
    return lax.dot_general(a, b, (((1,), (1,)), ((), ())), preferred_element_type=F32)


def _tn_dot(a, b):
    return lax.dot_general(a, b, (((0,), (0,)), ((), ())), preferred_element_type=F32)


def _pm_kernel(*refs, mode, residual, emit_h):
    x_ref, g_ref, w_ref = refs[:3]
    k = 3
    r_ref = None
    if residual:
        r_ref = refs[k]
        k += 1
    o_ref = refs[k]
    k += 1
    h_ref = None
    if emit_h:
        h_ref = refs[k]
        k += 1
    h_scr = refs[k]

    @pl.when(pl.program_id(1) == 0)
    def _():
        x = x_ref[...].astype(F32)
        if mode == "norm":
            h = x * lax.rsqrt(jnp.mean(x * x, axis=-1, keepdims=True) + EPS) * g_ref[...]
        elif mode == "bias":
            h = x + g_ref[...]
        else:
            h = x
        h_scr[...] = h.astype(BF16)
        if emit_h:
            h_ref[...] = h_scr[...]

    acc = jnp.dot(h_scr[...], w_ref[...], preferred_element_type=F32)
    if residual:
        acc = acc + r_ref[...]
    o_ref[...] = acc


def _prologue_matmul(x, g, w, *, mode, name, residual=None, emit_h=False, tm=512):
    n, kdim = x.shape
    wout = w.shape[1]
    tm = min(tm, n)
    tn = 512 if wout % 512 == 0 else (256 if wout % 256 == 0 else 128)
    assert n % tm == 0 and wout % tn == 0
    in_specs = [pl.BlockSpec((tm, kdim), lambda i, j: (i, 0)),
                pl.BlockSpec((1, kdim), lambda i, j: (0, 0)),
                pl.BlockSpec((kdim, tn), lambda i, j: (0, j))]
    args = [x, g.reshape(1, kdim).astype(F32), w]
    if residual is not None:
        in_specs.append(pl.BlockSpec((tm, tn), lambda i, j: (i, j)))
        args.append(residual)
    out_shape = [jax.ShapeDtypeStruct((n, wout), F32)]
    out_specs = [pl.BlockSpec((tm, tn), lambda i, j: (i, j))]
    if emit_h:
        out_shape.append(jax.ShapeDtypeStruct((n, kdim), BF16))
        out_specs.append(pl.BlockSpec((tm, kdim), lambda i, j: (i, 0)))
    res = pl.pallas_call(
        functools.partial(_pm_kernel, mode=mode, residual=residual is not None, emit_h=emit_h),
        grid=(n // tm, wout // tn),
        in_specs=in_specs, out_specs=out_specs, out_shape=out_shape,
        scratch_shapes=[pltpu.VMEM((tm, kdim), BF16)],
        compiler_params=_cp("arbitrary", "arbitrary"), name=name,
    )(*args)
    return res if emit_h else res[0]


def _rope_tables(pos, rot_dims, theta, dh):
    half = rot_dims // 2
    inv = theta ** (-jnp.arange(half, dtype=F32) * 2.0 / rot_dims)
    ang = pos.astype(F32)[:, None] * inv[None, :]
    cos, sin = jnp.cos(ang), jnp.sin(ang)
    t = pos.shape[0]
    one = jnp.ones((t, dh - rot_dims), F32)
    zh = jnp.zeros((t, half), F32)
    zr = jnp.zeros((t, dh - rot_dims), F32)
    c = jnp.concatenate([cos, cos, one], axis=1)
    s1 = jnp.concatenate([-sin, zh, zr], axis=1)
    s2 = jnp.concatenate([zh, sin, zr], axis=1)
    return c, s1, s2


def _rope(y, c, s1, s2, half):
    dh = y.shape[-1]
    return y * c + pltpu.roll(y, dh - half, 1) * s1 + pltpu.roll(y, half, 1) * s2


def _headnorm_kernel(*refs, nh, dh, rope_half):
    if rope_half:
        x_ref, g_ref, c_ref, s1_ref, s2_ref, o_ref = refs
    else:
        x_ref, g_ref, o_ref = refs
    for h in range(nh):
        x = x_ref[:, h * dh:(h + 1) * dh]
        y = x * lax.rsqrt(jnp.mean(x * x, axis=-1, keepdims=True) + EPS) * g_ref[...]
        if rope_half:
            y = _rope(y, c_ref[...], s1_ref[...], s2_ref[...], rope_half)
        o_ref[:, h * dh:(h + 1) * dh] = y


def _headnorm(x, col, width, gain, dh, seq, tables=None, rope_half=0):
    n = x.shape[0]
    tm = min(512, seq) if seq > 1 else n
    assert n % tm == 0 and col % width == 0
    cb = col // width
    in_specs = [pl.BlockSpec((tm, width), lambda i: (i, cb)),
                pl.BlockSpec((1, dh), lambda i: (0, 0))]
    args = [x, gain.reshape(1, dh).astype(F32)]
    if rope_half:
        if seq > 1:
            nb = seq // tm
            tspec = pl.BlockSpec((tm, dh), lambda i: (i % nb, 0))
        else:
            tspec = pl.BlockSpec((1, dh), lambda i: (0, 0))
        in_specs += [tspec] * 3
        args += list(tables)
    return pl.pallas_call(
        functools.partial(_headnorm_kernel, nh=width // dh, dh=dh, rope_half=rope_half),
        grid=(n // tm,),
        in_specs=in_specs,
        out_specs=pl.BlockSpec((tm, width), lambda i: (i, 0)),
        out_shape=jax.ShapeDtypeStruct((n, width), F32),
        compiler_params=_cp("arbitrary"), name="head_norm",
    )(*args)


def _log_gamma():
    return np.log1p(-(np.float32(2.0) ** (-5.0 - np.arange(RET_HEADS, dtype=np.float32)))).astype(np.float32)


def _ret_finish(o, gn, gate):
    y = o * lax.rsqrt(jnp.mean(o * o, axis=-1, keepdims=True) + EPS) * gn
    return y * (gate * jax.nn.sigmoid(gate))


def _ret_prompt_kernel(lg_ref, q_ref, k_ref, v_ref, gt_ref, c_ref, s1_ref, s2_ref, st0_ref, gn_ref,
                       o_ref, st_ref):
    h = pl.program_id(1)
    c_idx = pl.program_id(2)
    cl = q_ref.shape[0]
    lg = lg_ref[h]

    @pl.when(c_idx == 0)
    def _():
        st_ref[...] = st0_ref[...]

    half = RET_DK // 2
    q = _rope(q_ref[...], c_ref[...], s1_ref[...], s2_ref[...], half)
    k = _rope(k_ref[...], c_ref[...], s1_ref[...], s2_ref[...], half) * (RET_DK ** -0.5)
    v = v_ref[...]
    ri = lax.broadcasted_iota(I32, (cl, cl), 0).astype(F32)
    ci = lax.broadcasted_iota(I32, (cl, cl), 1).astype(F32)
    rel = ri - ci
    decay = jnp.where(rel >= 0, jnp.exp(jnp.maximum(rel, 0.0) * lg), 0.0)
    qb, kb, vb = q.astype(BF16), k.astype(BF16), v.astype(BF16)
    inner = _nt_dot(qb, kb) * decay
    state = st_ref[...]
    rowi = lax.broadcasted_iota(I32, (cl, 1), 0).astype(F32)
    cross = jnp.exp((rowi + 1.0) * lg)
    o = jnp.dot(inner.astype(BF16), vb, preferred_element_type=F32)
    o = o + jnp.dot(qb, state.astype(BF16), preferred_element_type=F32) * cross
    k_dec = (k * jnp.exp((cl - 1.0 - rowi) * lg)).astype(BF16)
    st_ref[...] = jnp.exp(jnp.zeros((1, 1), F32) + cl * lg) * state + _tn_dot(k_dec, vb)
    o_ref[...] = _ret_finish(o, gn_ref[...], gt_ref[...])


def _retention_prompt(z3, tables, state0, gn):
    b, t, _ = z3.shape
    cl = RET_CHUNK
    assert t % cl == 0
    nc = t // cl
    qk_b, v_b = C_RQ // RET_DK, C_RV // RET_DV
    kk_b, g_b = C_RK // RET_DK, C_RG // RET_DV
    tspec = pl.BlockSpec((cl, RET_DK), lambda bi, h, c: (c, 0))
    o, st = pl.pallas_call(
        _ret_prompt_kernel,
        grid=(b, RET_HEADS, nc),
        in_specs=[pl.BlockSpec(memory_space=pltpu.SMEM),
                  pl.BlockSpec((None, cl, RET_DK), lambda bi, h, c: (bi, c, qk_b + h)),
                  pl.BlockSpec((None, cl, RET_DK), lambda bi, h, c: (bi, c, kk_b + h)),
                  pl.BlockSpec((None, cl, RET_DV), lambda bi, h, c: (bi, c, v_b + h)),
                  pl.BlockSpec((None, cl, RET_DV), lambda bi, h, c: (bi, c, g_b + h)),
                  tspec, tspec, tspec,
                  pl.BlockSpec((None, None, RET_DK, RET_DV), lambda bi, h, c: (bi, h, 0, 0)),
                  pl.BlockSpec((None, 1, RET_DV), lambda bi, h, c: (h, 0, 0))],
        out_specs=[pl.BlockSpec((None, cl, RET_DV), lambda bi, h, c: (bi, c, h)),
                   pl.BlockSpec((None, None, RET_DK, RET_DV), lambda bi, h, c: (bi, h, 0, 0))],
        out_shape=[jax.ShapeDtypeStruct((b, t, RET_HEADS * RET_DV), F32),
                   jax.ShapeDtypeStruct((b, RET_HEADS, RET_DK, RET_DV), F32)],
        compiler_params=_cp("arbitrary", "arbitrary", "arbitrary"), name="retention_prompt",
    )(jnp.asarray(_log_gamma()), z3, z3, z3, z3, *tables, state0, gn.reshape(RET_HEADS, 1, RET_DV))
    return o, st


def _ret_decode_kernel(q_ref, k_ref, v_ref, gt_ref, c_ref, s1_ref, s2_ref, st0_ref, gn_ref, o_ref, st_ref):
    lgs = _log_gamma()
    half = RET_DK // 2
    row0 = lax.broadcasted_iota(I32, (16, 1), 0) == 0
    for h in range(RET_HEADS):
        gamma = float(np.exp(lgs[h]))
        q = _rope(q_ref[:, h * RET_DK:(h + 1) * RET_DK], c_ref[...], s1_ref[...], s2_ref[...], half)
        k = _rope(k_ref[:, h * RET_DK:(h + 1) * RET_DK], c_ref[...], s1_ref[...], s2_ref[...], half)
        k = k * (RET_DK ** -0.5)
        v = v_ref[:, h * RET_DV:(h + 1) * RET_DV]
        state = st0_ref[h].astype(F32)
        inner = jnp.sum(q * k, axis=-1, keepdims=True)
        q16 = jnp.broadcast_to(q, (16, RET_DK)).astype(BF16)
        cross = jnp.dot(q16, state.astype(BF16), preferred_element_type=F32)[0:1]
        o = inner * v + cross * gamma
        k16 = jnp.where(row0, jnp.broadcast_to(k, (16, RET_DK)), 0.0).astype(BF16)
        v16 = jnp.broadcast_to(v, (16, RET_DV)).astype(BF16)
        st_ref[h] = gamma * state + _tn_dot(k16, v16)
        o_ref[:, h * RET_DV:(h + 1) * RET_DV] = _ret_finish(
            o, gn_ref[h], gt_ref[:, h * RET_DV:(h + 1) * RET_DV])


def _retention_decode(z3, tables, state0, gn):
    b = z3.shape[0]
    wq, wv = RET_HEADS * RET_DK, RET_HEADS * RET_DV
    tspec = pl.BlockSpec((1, RET_DK), lambda bi: (0, 0))
    o, st = pl.pallas_call(
        _ret_decode_kernel,
        grid=(b,),
        in_specs=[pl.BlockSpec((None, 1, wq), lambda bi: (bi, 0, C_RQ // wq)),
                  pl.BlockSpec((None, 1, wq), lambda bi: (bi, 0, C_RK // wq)),
                  pl.BlockSpec((None, 1, wv), lambda bi: (bi, 0, C_RV // wv)),
                  pl.BlockSpec((None, 1, wv), lambda bi: (bi, 0, C_RG // wv)),
                  tspec, tspec, tspec,
                  pl.BlockSpec((None, RET_HEADS, RET_DK, RET_DV), lambda bi: (bi, 0, 0, 0)),
                  pl.BlockSpec((RET_HEADS, 1, RET_DV), lambda bi: (0, 0, 0))],
        out_specs=[pl.BlockSpec((None, 1, wv), lambda bi: (bi, 0, 0)),
                   pl.BlockSpec((None, RET_HEADS, RET_DK, RET_DV), lambda bi: (bi, 0, 0, 0))],
        out_shape=[jax.ShapeDtypeStruct((b, 1, wv), F32),
                   jax.ShapeDtypeStruct((b, RET_HEADS, RET_DK, RET_DV), F32)],
        compiler_params=_cp("arbitrary"), name="retention_decode",
    )(z3, z3, z3, z3, *tables, state0, gn.reshape(RET_HEADS, 1, RET_DV))
    return o, st


def _compress_weights(pos_emb, w):
    g, dh = NSA_KV_HEADS, HEAD_DIM
    eye = jnp.eye(g, dtype=w.dtype)
    wbig = jnp.einsum("lde,gh->lgdhe", w, eye).reshape(CMP_BLOCK * g * dh, g * dh)
    pos_row = jnp.broadcast_to(pos_emb[:, None, :], (CMP_BLOCK, g, dh)).reshape(1, CMP_BLOCK * g * dh)
    return pos_row, wbig.astype(BF16)


def _compress_pool_kernel(x_ref, pos_ref, w_ref, o_ref):
    p, page, groups, dh = x_ref.shape
    per_page = page // CMP_BLOCK
    for g in range(groups):
        acc = jnp.zeros((per_page * p, dh), F32)
        for l in range(CMP_BLOCK):
            xl = jnp.concatenate([x_ref[:, n * CMP_BLOCK + l, g, :] for n in range(per_page)], axis=0)
            xl = (xl + pos_ref[l:l + 1, :]).astype(BF16)
            acc = acc + jnp.dot(xl, w_ref[l], preferred_element_type=F32)
        o_ref[:, :, g * dh:(g + 1) * dh] = acc.reshape(per_page, p, dh)


def _compress_pool(pool, pos_emb, w, name):
    _, n_pool, page, groups, dh = pool.shape
    per_page = page // CMP_BLOCK
    p = max(c for c in range(8, 65, 8) if n_pool % c == 0)
    return pl.pallas_call(
        _compress_pool_kernel, grid=(n_pool // p,),
        in_specs=[pl.BlockSpec((None, p, page, groups, dh), lambda i: (0, i, 0, 0, 0)),
                  pl.BlockSpec((CMP_BLOCK, dh), lambda i: (0, 0)),
                  pl.BlockSpec((CMP_BLOCK, dh, dh), lambda i: (0, 0, 0))],
        out_specs=pl.BlockSpec((per_page, p, groups * dh), lambda i: (0, i, 0)),
        out_shape=jax.ShapeDtypeStruct((per_page, n_pool, groups * dh), F32),
        compiler_params=_cp("arbitrary"), name=name,
    )(pool, pos_emb, w.astype(BF16))


def _masked_softmax_parts(parts, masks):
    sm = [jnp.where(m, s, NEG_INF) for s, m in zip(parts, masks)]
    mx = functools.reduce(jnp.maximum, [jnp.max(s, axis=-1, keepdims=True) for s in sm])
    ex = [jnp.exp(s - mx) for s in sm]
    den = functools.reduce(lambda a, b2: a + b2, [jnp.sum(e, axis=-1, keepdims=True) for e in ex])
    inv = 1.0 / den
    return [jnp.where(m, e * inv, 0.0) for e, m in zip(ex, masks)]


def _select_blocks(score, nsel):
    nbs = score.shape[-1]
    lane = lax.broadcasted_iota(I32, score.shape, 1)
    sel = jnp.zeros(score.shape, jnp.bool_)
    picks = []
    for _ in range(nsel):
        m = jnp.max(score, axis=-1, keepdims=True)
        idx = jnp.min(jnp.where(score == m, lane, nbs), axis=-1, keepdims=True)
        hit = lane == idx
        sel = jnp.logical_or(sel, hit)
        score = jnp.where(hit, -jnp.inf, score)
        picks.append(idx)
    return sel, picks


def _nsa_prompt_kernel(q_ref, ck_ref, cv_ref, ks_ref, vs_ref, kw_ref, vw_ref, gt_ref, o_ref, osel_ref):
    i = pl.program_id(1)
    qb = q_ref.shape[0]
    t = ks_ref.shape[0]
    nbs = t // SEL_BLOCK
    hg, dh = HEADS_PER_GROUP, HEAD_DIM
    kvw = NSA_KV_HEADS * dh
    scale = dh ** -0.5
    pos1 = i * qb + lax.broadcasted_iota(I32, (qb, 1), 0)
    pos = jnp.concatenate([pos1] * hg, axis=0)
    blk = lax.broadcasted_iota(I32, (1, nbs), 1)
    vis_e = blk * SEL_BLOCK + (CMP_BLOCK - 1) <= pos
    vis_o = blk * SEL_BLOCK + (SEL_BLOCK - 1) <= pos
    valid = blk * SEL_BLOCK <= pos1
    forced = jnp.logical_or(blk == 0, blk == pos1 // SEL_BLOCK)
    n_case = 4 if t % (4 * qb) == 0 else 1
    span = t // n_case
    case = (i * qb) // span

    def selected_attend(limit, q, sel_f, cs):
        key = lax.broadcasted_iota(I32, (1, limit), 1)
        expand = (lax.broadcasted_iota(I32, (nbs, limit), 1) // SEL_BLOCK
                  == lax.broadcasted_iota(I32, (nbs, limit), 0)).astype(BF16)
        selk = jnp.dot(sel_f, expand, preferred_element_type=F32)
        smask = jnp.logical_and(jnp.concatenate([selk] * hg, axis=0) > 0.5, key <= pos)
        (p_s,) = _masked_softmax_parts([_nt_dot(q, ks_ref[0:limit, cs].astype(BF16)) * scale], [smask])
        osel_ref[...] = jnp.dot(p_s.astype(BF16), vs_ref[0:limit, cs].astype(BF16), preferred_element_type=F32)

    wlen = WINDOW + qb
    wstart = pl.multiple_of(jnp.maximum(i * qb - WINDOW, 0), qb)
    wkey = wstart + lax.broadcasted_iota(I32, (1, wlen), 1)
    wdist = pos - wkey
    wmask = jnp.logical_and(wdist >= 0, wdist <= WINDOW)
    gates = gt_ref[...]

    for g in range(NSA_KV_HEADS):
        cs = slice(g * dh, (g + 1) * dh)
        q = jnp.concatenate([q_ref[:, (g * hg + hh) * dh:(g * hg + hh + 1) * dh] for hh in range(hg)],
                            axis=0).astype(BF16)
        co = slice(kvw + g * dh, kvw + (g + 1) * dh)
        ck_e, ck_o = ck_ref[:, cs].astype(BF16), ck_ref[:, co].astype(BF16)
        cv_e, cv_o = cv_ref[:, cs].astype(BF16), cv_ref[:, co].astype(BF16)
        p_e, p_o = _masked_softmax_parts([_nt_dot(q, ck_e) * scale, _nt_dot(q, ck_o) * scale], [vis_e, vis_o])
        o_cmp = (jnp.dot(p_e.astype(BF16), cv_e, preferred_element_type=F32)
                 + jnp.dot(p_o.astype(BF16), cv_o, preferred_element_type=F32))
        psum = p_e + p_o
        imp = functools.reduce(lambda a, b2: a + b2, [psum[hh * qb:(hh + 1) * qb] for hh in range(hg)])
        score = jnp.where(forced, FORCE, jnp.where(valid, imp, -FORCE))
        sel, _ = _select_blocks(score, min(SEL_TOPK, nbs))
        sel_f = jnp.where(sel, 1.0, 0.0).astype(BF16)
        for c in range(n_case):
            pl.when(case == c)(functools.partial(selected_attend, (c + 1) * span, q, sel_f, cs))
        o_sel = osel_ref[...]
        kwin = kw_ref[pl.ds(wstart, wlen), cs].astype(BF16)
        vwin = vw_ref[pl.ds(wstart, wlen), cs].astype(BF16)
        (p_w,) = _masked_softmax_parts([_nt_dot(q, kwin) * scale], [wmask])
        o_win = jnp.dot(p_w.astype(BF16), vwin, preferred_element_type=F32)
        for hh in range(hg):
            hd = g * hg + hh
            rs = slice(hh * qb, (hh + 1) * qb)
            o_ref[:, hd * dh:(hd + 1) * dh] = (gates[:, 3 * hd:3 * hd + 1] * o_cmp[rs]
                                                 + gates[:, 3 * hd + 1:3 * hd + 2] * o_sel[rs]
                                                 + gates[:, 3 * hd + 2:3 * hd + 3] * o_win[rs])


def _nsa_prompt(qn3, ck3, cv3, ksn3, z3, kwn3, gates3):
    b, t, _ = qn3.shape
    qb = Q_BLOCK
    kvw = NSA_KV_HEADS * HEAD_DIM
    assert t % qb == 0 and t >= WINDOW + qb and t % (2 * SEL_BLOCK) == 0
    nbs = ck3.shape[1]
    full = lambda cb: pl.BlockSpec((None, t, kvw), lambda bi, i: (bi, 0, cb))
    return pl.pallas_call(
        _nsa_prompt_kernel,
        grid=(b, t // qb),
        in_specs=[pl.BlockSpec((None, qb, NSA_HEADS * HEAD_DIM), lambda bi, i: (bi, i, 0)),
                  pl.BlockSpec((None, nbs, CMP_PER_SEL * kvw), lambda bi, i: (bi, 0, 0)),
                  pl.BlockSpec((None, nbs, CMP_PER_SEL * kvw), lambda bi, i: (bi, 0, 0)),
                  full(0), full(C_VS // kvw), full(0), full(C_VW // kvw),
                  pl.BlockSpec((None, qb, LANES), lambda bi, i: (bi, i, 0))],
        out_specs=pl.BlockSpec((None, qb, NSA_HEADS * HEAD_DIM), lambda bi, i: (bi, i, 0)),
        out_shape=jax.ShapeDtypeStruct((b, t, NSA_HEADS * HEAD_DIM), F32),
        scratch_shapes=[pltpu.VMEM((HEADS_PER_GROUP * qb, HEAD_DIM), F32)],
        compiler_params=_cp("arbitrary", "arbitrary"), name="nsa_prompt",
    )(qn3, ck3, cv3, ksn3, z3, kwn3, z3, gates3)


def _decode_q16(q_ref):
    dh = HEAD_DIM
    rows = [q_ref[:, h * dh:(h + 1) * dh] for h in range(NSA_HEADS)]
    return jnp.concatenate(rows + [jnp.zeros((16 - NSA_HEADS, dh), F32)], axis=0).astype(BF16)


def _nsa_decode_select_kernel(q_ref, ck_ref, cv_ref, ocmp_ref, idx_ref, *, past):
    dh, hg = HEAD_DIM, HEADS_PER_GROUP
    kvw = NSA_KV_HEADS * dh
    nhalf = ck_ref.shape[0]
    nbs = past // SEL_BLOCK + 1
    scale = dh ** -0.5
    q16 = _decode_q16(q_ref)
    row = lax.broadcasted_iota(I32, (16, 1), 0)
    blk = lax.broadcasted_iota(I32, (1, LANES), 1)
    blk_h = lax.broadcasted_iota(I32, (1, nhalf), 1)
    vis_e = blk_h * SEL_BLOCK + (CMP_BLOCK - 1) <= past
    vis_o = blk_h * SEL_BLOCK + (SEL_BLOCK - 1) <= past
    o_all = jnp.zeros((16, dh), F32)
    idx_row = jnp.zeros((1, LANES), I32)
    for g in range(NSA_KV_HEADS):
        cs = slice(g * dh, (g + 1) * dh)
        co = slice(kvw + g * dh, kvw + (g + 1) * dh)
        ck_e, ck_o = ck_ref[:, cs].astype(BF16), ck_ref[:, co].astype(BF16)
        cv_e, cv_o = cv_ref[:, cs].astype(BF16), cv_ref[:, co].astype(BF16)
        p_e, p_o = _masked_softmax_parts([_nt_dot(q16, ck_e) * scale, _nt_dot(q16, ck_o) * scale], [vis_e, vis_o])
        o_g = (jnp.dot(p_e.astype(BF16), cv_e, preferred_element_type=F32)
               + jnp.dot(p_o.astype(BF16), cv_o, preferred_element_type=F32))
        in_g = jnp.logical_and(row >= g * hg, row < (g + 1) * hg)
        o_all = jnp.where(in_g, o_g, o_all)
        imp = jnp.sum(jnp.where(in_g, p_e + p_o, 0.0), axis=0, keepdims=True)
        score = jnp.where(blk_h == 0, FORCE, imp)
        nsel = min(SEL_TOPK, nbs)
        _, past_picks = _select_blocks(score, min(nsel - 1, nhalf))
        picks = past_picks[:1] + [jnp.full((1, 1), nhalf, I32)] + past_picks[1:]
        for kk, p in enumerate(picks):
            idx_row = jnp.where(blk == g * SEL_TOPK + kk, p, idx_row)
        for kk in range(len(picks), SEL_TOPK):
            idx_row = jnp.where(blk == g * SEL_TOPK + kk, -1, idx_row)
    ocmp_ref[...] = o_all
    idx_ref[...] = idx_row


def _nsa_decode_select(qn3, ck3, cv3, past):
    b = qn3.shape[0]
    nhalf = ck3.shape[1]
    kvw = NSA_KV_HEADS * HEAD_DIM
    assert nhalf % 8 == 0 and nhalf == past // SEL_BLOCK
    return pl.pallas_call(
        functools.partial(_nsa_decode_select_kernel, past=past),
        grid=(b,),
        in_specs=[pl.BlockSpec((None, 1, NSA_HEADS * HEAD_DIM), lambda bi: (bi, 0, 0)),
                  pl.BlockSpec((None, nhalf, CMP_PER_SEL * kvw), lambda bi: (bi, 0, 0)),
                  pl.BlockSpec((None, nhalf, CMP_PER_SEL * kvw), lambda bi: (bi, 0, 0))],
        out_specs=[pl.BlockSpec((None, 16, HEAD_DIM), lambda bi: (bi, 0, 0)),
                   pl.BlockSpec((None, 1, LANES), lambda bi: (bi, 0, 0))],
        out_shape=[jax.ShapeDtypeStruct((b, 16, HEAD_DIM), F32),
                   jax.ShapeDtypeStruct((b, 1, LANES), I32)],
        compiler_params=_cp("arbitrary"), name="nsa_decode_select",
    )(qn3, ck3, cv3)


def _nsa_decode_attend_kernel(*refs, past, nsel):
    n_blk = NSA_KV_HEADS * nsel
    rows_ref = refs[0]
    kb_refs = refs[1:1 + n_blk]
    vb_refs = refs[1 + n_blk:1 + 2 * n_blk]
    (q_ref, idx_ref, ksn_ref, vs_ref, wk_ref, wv_ref, kwn_ref, vw_ref, ocmp_ref, gt_ref, o_ref) = refs[1 + 2 * n_blk:]
    del rows_ref
    dh, hg = HEAD_DIM, HEADS_PER_GROUP
    scale = dh ** -0.5
    self_blk = past // SEL_BLOCK
    q16 = _decode_q16(q_ref)
    q16f = q16.astype(F32)
    row = lax.broadcasted_iota(I32, (16, 1), 0)
    lane = lax.broadcasted_iota(I32, (1, LANES), 1)
    idx_row = idx_ref[...]
    o_sel = jnp.zeros((16, dh), F32)
    o_win = jnp.zeros((16, dh), F32)
    for g in range(NSA_KV_HEADS):
        cs = slice(g * dh, (g + 1) * dh)
        in_g = jnp.logical_and(row >= g * hg, row < (g + 1) * hg)
        k_self = ksn_ref[:, cs].astype(BF16).astype(F32)
        v_self = vs_ref[:, cs].astype(BF16).astype(F32)
        s_self = jnp.sum(q16f * k_self, axis=-1, keepdims=True) * scale
        parts, masks, vals = [], [], []
        self_sel = jnp.zeros((1, 1), jnp.bool_)
        for kk in range(nsel):
            pick = jnp.max(jnp.where(lane == g * SEL_TOPK + kk, idx_row, -1), axis=-1, keepdims=True)
            in_pool = jnp.logical_and(pick >= 0, pick < self_blk)
            self_sel = jnp.logical_or(self_sel, pick == self_blk)
            kb = kb_refs[g * nsel + kk][:, g, :].astype(BF16)
            parts.append(_nt_dot(q16, kb) * scale)
            masks.append(in_pool)
            vals.append(vb_refs[g * nsel + kk][:, g, :].astype(BF16))
        parts.append(s_self)
        masks.append(self_sel)
        probs = _masked_softmax_parts(parts, masks)
        acc = probs[-1].astype(BF16).astype(F32) * v_self
        for p, vv in zip(probs[:-1], vals):
            acc = acc + jnp.dot(p.astype(BF16), vv, preferred_element_type=F32)
        o_sel = jnp.where(in_g, acc, o_sel)
        kw_self = kwn_ref[:, cs].astype(BF16).astype(F32)
        vw_self = vw_ref[:, cs].astype(BF16).astype(F32)
        sw_self = jnp.sum(q16f * kw_self, axis=-1, keepdims=True) * scale
        wparts = [_nt_dot(q16, wk_ref[:, g, :].astype(BF16)) * scale, sw_self]
        wmasks = [lane[:, :1] >= 0, lane[:, :1] >= 0]
        pw, pws = _masked_softmax_parts(wparts, wmasks)
        accw = (jnp.dot(pw.astype(BF16), wv_ref[:, g, :].astype(BF16), preferred_element_type=F32)
                + pws.astype(BF16).astype(F32) * vw_self)
        o_win = jnp.where(in_g, accw, o_win)
    gates = gt_ref[...]
    o_cmp = ocmp_ref[...]
    for hd in range(NSA_HEADS):
        o_ref[:, hd * dh:(hd + 1) * dh] = (gates[:, 3 * hd:3 * hd + 1] * o_cmp[hd:hd + 1]
                                             + gates[:, 3 * hd + 1:3 * hd + 2] * o_sel[hd:hd + 1]
                                             + gates[:, 3 * hd + 2:3 * hd + 3] * o_win[hd:hd + 1])


def _nsa_decode_attend(pool_rows, sel_k_pool, sel_v_pool, qn3, idx3, ksn3, z3, win_k, win_v, kwn3, ocmp, gates3,
                       past, nsel):
    b = qn3.shape[0]
    kvw = NSA_KV_HEADS * HEAD_DIM
    n_blk = NSA_KV_HEADS * nsel
    nbuf = win_k.shape[2]
    per_page = sel_k_pool.shape[2] // SEL_BLOCK

    def blk_spec(tt):
        return pl.BlockSpec((None, None, SEL_BLOCK, NSA_KV_HEADS, HEAD_DIM),
                            lambda bi, r: (0, r[bi * n_blk + tt] // per_page, r[bi * n_blk + tt] % per_page, 0, 0))

    def row_spec(cb, w=kvw):
        return pl.BlockSpec((None, 1, w), lambda bi, r: (bi, 0, cb))

    wspec = pl.BlockSpec((None, None, nbuf, NSA_KV_HEADS, HEAD_DIM), lambda bi, r: (0, bi, 0, 0, 0))
    in_specs = ([blk_spec(tt) for tt in range(n_blk)] + [blk_spec(tt) for tt in range(n_blk)]
                + [pl.BlockSpec((None, 1, NSA_HEADS * HEAD_DIM), lambda bi, r: (bi, 0, 0)),
                   pl.BlockSpec((None, 1, LANES), lambda bi, r: (bi, 0, 0)),
                   row_spec(0), row_spec(C_VS // kvw),
                   wspec, wspec,
                   row_spec(0), row_spec(C_VW // kvw),
                   pl.BlockSpec((None, 16, HEAD_DIM), lambda bi, r: (bi, 0, 0)),
                   pl.BlockSpec((None, 1, LANES), lambda bi, r: (bi, 0, 0))])
    return pl.pallas_call(
        functools.partial(_nsa_decode_attend_kernel, past=past, nsel=nsel),
        grid_spec=pltpu.PrefetchScalarGridSpec(
            num_scalar_prefetch=1, grid=(b,), in_specs=in_specs,
            out_specs=pl.BlockSpec((None, 1, NSA_HEADS * HEAD_DIM), lambda bi, r: (bi, 0, 0))),
        out_shape=jax.ShapeDtypeStruct((b, 1, NSA_HEADS * HEAD_DIM), F32),
        compiler_params=_cp("arbitrary"), name="nsa_decode_attend",
    )(pool_rows, *([sel_k_pool] * n_blk), *([sel_v_pool] * n_blk),
      qn3, idx3, ksn3, z3, win_k, win_v, kwn3, z3, ocmp, gates3)


def _mem_attn_kernel(*refs, cache_layout):
    dh = MEM_HEAD_DIM
    if cache_layout:
        q_ref, g_ref, mk0_ref, mk1_ref, mv0_ref, mv1_ref, o_ref = refs
    else:
        q_ref, g_ref, mk_ref, mv_ref, o_ref = refs
    tq = q_ref.shape[0]
    rows = max(tq, 16)
    for h in range(MEM_HEADS):
        cs = slice(h * dh, (h + 1) * dh)
        x = q_ref[:, cs]
        q = x * lax.rsqrt(jnp.mean(x * x, axis=-1, keepdims=True) + EPS) * g_ref[...]
        if rows != tq:
            q = jnp.broadcast_to(q, (rows, dh))
        if cache_layout:
            mk = jnp.concatenate([mk0_ref[:, h, :], mk1_ref[:, h, :]], axis=1)
            mv = jnp.concatenate([mv0_ref[:, h, :], mv1_ref[:, h, :]], axis=1)
        else:
            mk, mv = mk_ref[:, cs], mv_ref[:, cs]
        s = _nt_dot(q.astype(BF16), mk.astype(BF16)) * (dh ** -0.5)
        m = jnp.max(s, axis=-1, keepdims=True)
        e = jnp.exp(s - m)
        p = e / jnp.sum(e, axis=-1, keepdims=True)
        o = jnp.dot(p.astype(BF16), mv.astype(BF16), preferred_element_type=F32)
        o_ref[:, cs] = o[0:tq]


def _mem_attention(z3, gq, mk, mv, cache_layout=False):
    b, t, _ = z3.shape
    w = MEM_HEADS * MEM_HEAD_DIM
    tq = min(t, 512)
    assert t % tq == 0
    if cache_layout:
        m = mk.shape[2]
        assert MEM_HEAD_DIM == 2 * LANES
        half = lambda c: pl.BlockSpec((None, None, m, MEM_HEADS, LANES), lambda bi, i: (0, bi, 0, 0, c))
        kv_specs, kv_args = [half(0), half(1), half(0), half(1)], [mk, mk, mv, mv]
    else:
        m = mk.shape[1]
        kv_specs, kv_args = [pl.BlockSpec((None, m, w), lambda bi, i: (bi, 0, 0))] * 2, [mk, mv]
    return pl.pallas_call(
        functools.partial(_mem_attn_kernel, cache_layout=cache_layout),
        grid=(b, t // tq),
        in_specs=[pl.BlockSpec((None, tq, w), lambda bi, i: (bi, i, C_MQ // w)),
                  pl.BlockSpec((1, MEM_HEAD_DIM), lambda bi, i: (0, 0))] + kv_specs,
        out_specs=pl.BlockSpec((None, tq, w), lambda bi, i: (bi, i, 0)),
        out_shape=jax.ShapeDtypeStruct((b, t, w), F32),
        compiler_params=_cp("arbitrary", "arbitrary"), name="mem_attention",
    )(z3, gq.reshape(1, MEM_HEAD_DIM), *kv_args)


def _sigmoid_cols_kernel(x_ref, o_ref):
    o_ref[...] = jax.nn.sigmoid(x_ref[...])


def _nsa_gates(z, col):
    n = z.shape[0]
    tm = min(n, 1024)
    return pl.pallas_call(
        _sigmoid_cols_kernel, grid=(n // tm,),
        in_specs=[pl.BlockSpec((tm, LANES), lambda i: (i, col // LANES))],
        out_specs=pl.BlockSpec((tm, LANES), lambda i: (i, 0)),
        out_shape=jax.ShapeDtypeStruct((n, LANES), F32),
        compiler_params=_cp("arbitrary"), name="nsa_gates",
    )(z)


def _merge_kernel(r_ref, n_ref, m_ref, g0_ref, g1_ref, g2_ref, w_ref, o_ref):
    acc = jnp.zeros(o_ref.shape, F32)
    for c, (b_ref, g_ref) in enumerate(((r_ref, g0_ref), (n_ref, g1_ref), (m_ref, g2_ref))):
        up = jnp.dot(b_ref[...].astype(BF16), w_ref[c], preferred_element_type=F32)
        acc = acc + jax.nn.sigmoid(g_ref[...]) * up
    o_ref[...] = acc.astype(BF16)


def _merge(o_ret, o_nsa, o_mem, z, wb, d_model):
    n, bw = o_ret.shape
    tm = min(n, 512)
    tn = 512
    gb = C_MG // tn
    nj = d_model // tn
    bspec = pl.BlockSpec((tm, bw), lambda i, j: (i, 0))
    gspec = lambda c: pl.BlockSpec((tm, tn), lambda i, j: (i, gb + c * nj + j))
    return pl.pallas_call(
        _merge_kernel, grid=(n // tm, nj),
        in_specs=[bspec, bspec, bspec, gspec(0), gspec(1), gspec(2),
                  pl.BlockSpec((N_BRANCH, bw, tn), lambda i, j: (0, 0, j))],
        out_specs=pl.BlockSpec((tm, tn), lambda i, j: (i, j)),
        out_shape=jax.ShapeDtypeStruct((n, d_model), BF16),
        compiler_params=_cp("arbitrary", "arbitrary"), name="branch_merge",
    )(o_ret, o_nsa, o_mem, z, z, z, wb)


def _topk_rows(s, kk):
    e = s.shape[0]
    ridx = lax.broadcasted_iota(I32, s.shape, 0)
    vals, idxs = [], []
    for _ in range(kk):
        m = jnp.max(s, axis=0, keepdims=True)
        i = jnp.min(jnp.where(s == m, ridx, e), axis=0, keepdims=True)
        vals.append(m)
        idxs.append(i)
        s = jnp.where(ridx == i, -jnp.inf, s)
    return jnp.concatenate(vals, axis=0), jnp.concatenate(idxs, axis=0)


def _pair_pieces(kk):
    pieces, cur = [], []

    def flush(rows):
        pieces.append(rows + [None] * (8 - len(rows)))

    for a in range(kk):
        grp = [(a, b) for b in range(kk // (a + 1))]
        if cur and len(cur) + len(grp) > 8:
            flush(cur)
            cur = []
        cur = cur + grp
        while len(cur) >= 8:
            flush(cur[:8])
            cur = cur[8:]
    if cur:
        flush(cur)
    return pieces


def _rows_by_runs(x, ids):
    if ids[0] % 8 == 0 and ids == list(range(ids[0], ids[0] + 8)):
        return x[ids[0]:ids[0] + 8]
    r = lax.broadcasted_iota(I32, (8, 1), 0)
    runs = []
    for p, i in enumerate(ids):
        if not runs or runs[-1][1] != i:
            runs.append((p, i))
    out = jnp.broadcast_to(x[runs[-1][1]:runs[-1][1] + 1], (8, x.shape[1]))
    for (_, i), (nxt, _) in zip(reversed(runs[:-1]), reversed(runs[1:])):
        out = jnp.where(r < nxt, x[i:i + 1], out)
    return out


def _peer_route_kernel(q_ref, sk_ref, i_ref, j_ref, g_ref):
    kk = PEER_TOPK
    tl = q_ref.shape[0]
    st = _nt_dot(sk_ref[...], q_ref[...].astype(BF16))
    v0, i0 = _topk_rows(st[0:PEER_KEYS], kk)
    v1, i1 = _topk_rows(st[PEER_KEYS:2 * PEER_KEYS], kk)
    cand, ci, cj = [], [], []
    for piece in _pair_pieces(kk):
        ra = [p[0] if p else 0 for p in piece]
        rb = [p[1] if p else 0 for p in piece]
        live = lax.broadcasted_iota(I32, (8, 1), 0) < sum(p is not None for p in piece)
        cand.append(jnp.where(live, _rows_by_runs(v0, ra) + _rows_by_runs(v1, rb), -jnp.inf))
        ci.append(_rows_by_runs(i0, ra))
        cj.append(_rows_by_runs(i1, rb))
    cand, ci, cj = (jnp.concatenate(c, axis=0) for c in (cand, ci, cj))
    ridx = lax.broadcasted_iota(I32, cand.shape, 0)
    sc, si, sj = [], [], []
    for _ in range(kk):
        m = jnp.max(cand, axis=0, keepdims=True)
        r = jnp.min(jnp.where(cand == m, ridx, cand.shape[0]), axis=0, keepdims=True)
        hit = ridx == r
        sc.append(m)
        si.append(jnp.max(jnp.where(hit, ci, -1), axis=0, keepdims=True))
        sj.append(jnp.max(jnp.where(hit, cj, -1), axis=0, keepdims=True))
        cand = jnp.where(hit, -jnp.inf, cand)
    sc = jnp.concatenate(sc, axis=0)
    e = jnp.exp(sc - jnp.max(sc, axis=0, keepdims=True))
    g_ref[...] = e / jnp.sum(e, axis=0, keepdims=True)
    i_ref[...] = jnp.concatenate(si, axis=0)
    j_ref[...] = jnp.concatenate(sj, axis=0)


def _peer_route(q, skbd):
    n = q.shape[0]
    tl = min(n, LANES)
    kk = PEER_TOPK
    slots = PEER_HEADS * kk
    ospec = pl.BlockSpec((kk, tl), lambda i, h: (h, i))
    return pl.pallas_call(
        _peer_route_kernel, grid=(n // tl, PEER_HEADS),
        in_specs=[pl.BlockSpec((tl, PEER_DKEY), lambda i, h: (i, h)),
                  pl.BlockSpec((None, 2 * PEER_KEYS, PEER_DKEY), lambda i, h: (h, 0, 0))],
        out_specs=[ospec, ospec, ospec],
        out_shape=[jax.ShapeDtypeStruct((slots, n), I32), jax.ShapeDtypeStruct((slots, n), I32),
                   jax.ShapeDtypeStruct((slots, n), F32)],
        compiler_params=_cp("arbitrary", "arbitrary"), name="peer_route",
    )(q, skbd)


TOKEN_GROUP = 8


def _peer_weights_kernel(i_ref, j_ref, g_ref, o_ref, it_scr, jt_scr, gt_scr):
    it_scr[...] = i_ref[...].T
    jt_scr[...] = j_ref[...].T
    gt_scr[...] = g_ref[...].T
    key = lax.broadcasted_iota(I32, (PEER_KEYS, i_ref.shape[0]), 0)

    def body(nb, carry):
        for t in range(TOKEN_GROUP):
            n = nb * TOKEN_GROUP + t
            irow = it_scr[pl.ds(n, 1), :]
            jrow = jt_scr[pl.ds(n, 1), :]
            grow = gt_scr[pl.ds(n, 1), :]
            a = jnp.where(key == irow, grow, 0.0).astype(BF16)
            b = jnp.where(key == jrow, 1.0, 0.0).astype(BF16)
            o_ref[nb, pl.ds(t, PEER_KEYS, stride=TOKEN_GROUP), :] = _nt_dot(a, b)
        return carry

    lax.fori_loop(0, o_ref.shape[0], body, 0)


def _peer_weights(it, jt, gt):
    slots, n = it.shape
    tb = min(n, LANES)
    assert tb % TOKEN_GROUP == 0
    ispec = pl.BlockSpec((slots, tb), lambda i: (0, i))
    out = pl.pallas_call(
        _peer_weights_kernel, grid=(n // tb,),
        in_specs=[ispec, ispec, ispec],
        out_specs=pl.BlockSpec((tb // TOKEN_GROUP, PEER_KEYS * TOKEN_GROUP, PEER_KEYS), lambda i: (i, 0, 0)),
        out_shape=jax.ShapeDtypeStruct((n // TOKEN_GROUP, PEER_KEYS * TOKEN_GROUP, PEER_KEYS), F32),
        scratch_shapes=[pltpu.VMEM((tb, slots), I32), pltpu.VMEM((tb, slots), I32), pltpu.VMEM((tb, slots), F32)],
        compiler_params=_cp("arbitrary"), name="peer_weights",
    )(it, jt, gt)
    return out.reshape(n // TOKEN_GROUP, PEER_KEYS, TOKEN_GROUP, PEER_KEYS)


def _peer_dense_kernel(h_ref, g_ref, u_ref, v_ref, x_ref, o_ref, acc_ref):
    j = pl.program_id(1)

    @pl.when(j == 0)
    def _():
        acc_ref[...] = jnp.zeros_like(acc_ref)

    a = jnp.dot(h_ref[...], u_ref[...], preferred_element_type=F32)
    act = 0.5 * a * (1.0 + lax.erf(a * (2.0 ** -0.5)))
    tm = h_ref.shape[0]
    w = jnp.concatenate([(g_ref[:, ii].reshape(tm, PEER_KEYS)
                          * act[:, ii * PEER_KEYS:(ii + 1) * PEER_KEYS]).astype(BF16)
                         for ii in range(g_ref.shape[1])], axis=1)
    acc_ref[...] += jnp.dot(w, v_ref[...], preferred_element_type=F32)

    @pl.when(j == pl.num_programs(1) - 1)
    def _():
        o_ref[...] = x_ref[...] + acc_ref[...]


def _peer_dense(hn, gw, ut, vv, x):
    n, d = hn.shape
    ne = ut.shape[1]
    tm = min(n, 512)
    te = 512
    return pl.pallas_call(
        _peer_dense_kernel, grid=(n // tm, ne // te),
        in_specs=[pl.BlockSpec((tm, d), lambda i, j: (i, 0)),
                  pl.BlockSpec((tm // TOKEN_GROUP, te // PEER_KEYS, TOKEN_GROUP, PEER_KEYS),
                               lambda i, j: (i, j, 0, 0)),
                  pl.BlockSpec((d, te), lambda i, j: (0, j)),
                  pl.BlockSpec((te, d), lambda i, j: (j, 0)),
                  pl.BlockSpec((tm, d), lambda i, j: (i, 0))],
        out_specs=pl.BlockSpec((tm, d), lambda i, j: (i, 0)),
        out_shape=jax.ShapeDtypeStruct((n, d), F32),
        scratch_shapes=[pltpu.VMEM((tm, d), F32)],
        compiler_params=_cp("arbitrary", "arbitrary"), name="peer_dense",
    )(hn, gw, ut, vv, x)


def _prepare_weights(norm_attn, w_in, cmp_pos_k, cmp_pos_v, cmp_w_k, cmp_w_v, w_mem_kv, w_branch, w_out,
                     peer_wq, peer_subkeys, peer_u, peer_v):
    d = w_in.shape[0]
    ng_w = NSA_HEADS * 3
    o_ng = 5632
    o_mq = o_ng + ng_w
    o_mg = o_mq + MEM_HEADS * MEM_HEAD_DIM
    width = C_MG + N_BRANCH * d + LANES
    width = -(-width // Z_TILE) * Z_TILE
    w_r = jnp.concatenate([w_in[:, 0:4096], w_in[:, o_mq:o_mg], w_in[:, 4096:o_ng], w_in[:, o_mg:],
                           w_in[:, o_ng:o_mq],
                           jnp.zeros((d, width - (C_MG + N_BRANCH * d + ng_w)), w_in.dtype)], axis=1).astype(BF16)
    pos_k, wk_big = _compress_weights(cmp_pos_k, cmp_w_k)
    pos_v, wv_big = _compress_weights(cmp_pos_v, cmp_w_v)
    half = PEER_DKEY // 2
    sk = peer_subkeys.astype(BF16)
    zk = jnp.zeros((PEER_HEADS, PEER_KEYS, half), BF16)
    skbd = jnp.concatenate([jnp.concatenate([sk[:, 0], zk], axis=2),
                            jnp.concatenate([zk, sk[:, 1]], axis=2)], axis=1)
    return dict(w_in=w_r, c_ng=C_MG + N_BRANCH * d, pos_k=pos_k, wk_big=wk_big, pos_v=pos_v, wv_big=wv_big,
                w_mem_kv=w_mem_kv.astype(BF16), w_branch=w_branch.astype(BF16), w_out=w_out.astype(BF16),
                peer_wq=peer_wq.astype(BF16), skbd=skbd, ut=peer_u.astype(BF16).T, vv=peer_v.astype(BF16))


def _layer_common(x, seq, pos, norm_attn, nsa_q_norm, nsa_k_norm, pw):
    z = _prologue_matmul(x, norm_attn, pw["w_in"], mode="norm", name="in_proj", tm=1024)
    nsa_tabs = _rope_tables(pos, ROPE_DIMS, ROPE_THETA, HEAD_DIM)
    half = ROPE_DIMS // 2
    kvw = NSA_KV_HEADS * HEAD_DIM
    qn = _headnorm(z, C_NQ, NSA_HEADS * HEAD_DIM, nsa_q_norm, HEAD_DIM, seq, nsa_tabs, half)
    kcn = _headnorm(z, C_KC, kvw, nsa_k_norm[0], HEAD_DIM, seq, nsa_tabs, half)
    ksn = _headnorm(z, C_KS, kvw, nsa_k_norm[1], HEAD_DIM, seq, nsa_tabs, half)
    kwn = _headnorm(z, C_KW, kvw, nsa_k_norm[2], HEAD_DIM, seq, nsa_tabs, half)
    gates = _nsa_gates(z, pw["c_ng"])
    ret_tabs = _rope_tables(pos, RET_DK, RET_THETA, RET_DK)
    return z, qn, kcn, ksn, kwn, gates, ret_tabs


def _layer_tail(x, z, o_ret, o_nsa, o_mem, norm_ffn, pw):
    d = x.shape[1]
    merged = _merge(o_ret, o_nsa, o_mem, z, pw["w_branch"], d)
    x1 = _prologue_matmul(merged, jnp.ones((d,), F32), pw["w_out"], mode="none", residual=x, name="out_proj")
    q, hn = _prologue_matmul(x1, norm_ffn, pw["peer_wq"], mode="norm", emit_h=True, name="peer_query")
    it, jt, gt = _peer_route(q, pw["skbd"])
    return _peer_dense(hn, _peer_weights(it, jt, gt), pw["ut"], pw["vv"], x1)


def kernel(x_prompt, x_sample, cache_cmp_k, cache_cmp_v, cache_sel_k, cache_sel_v, cache_win_k, cache_win_v,
           state_ret, cache_mem_k, cache_mem_v, page_table, mem_prompt, norm_attn, w_in, ret_gn, nsa_q_norm,
           nsa_k_norm, cmp_pos_k, cmp_pos_v, cmp_w_k, cmp_w_v, norm_mem, w_mem_kv, mem_q_norm, mem_k_norm,
           w_branch, w_out, norm_ffn, peer_wq, peer_subkeys, peer_u, peer_v):
    depth = w_in.shape[0]
    assert depth == 1
    l = 0
    bp, t, d = x_prompt.shape
    bs, ts, _ = x_sample.shape
    assert ts == 1
    n_pool, page, g_kv, dh = cache_cmp_k.shape[1:]
    n_pages = page_table.shape[1]
    past = n_pages * page
    kvw = g_kv * dh
    memw = MEM_HEADS * MEM_HEAD_DIM
    pw = _prepare_weights(norm_attn[l], w_in[l], cmp_pos_k[l], cmp_pos_v[l], cmp_w_k[l], cmp_w_v[l], w_mem_kv[l],
                          w_branch[l], w_out[l], peer_wq[l], peer_subkeys[l], peer_u[l], peer_v[l])
    cmp_cols = CMP_BLOCK * kvw

    n_p = bp * t
    xp = x_prompt.reshape(n_p, d)
    m_tok = mem_prompt.shape[1]
    mkv = _prologue_matmul(mem_prompt.reshape(bp * m_tok, d), norm_mem[l], pw["w_mem_kv"], mode="norm",
                           name="mem_kv_proj")
    mk_p = _headnorm(mkv, 0, memw, mem_k_norm[l], MEM_HEAD_DIM, m_tok)
    mv_p = mkv[:, memw:]
    z, qn, kcn, ksn, kwn, gates, ret_tabs = _layer_common(xp, t, jnp.arange(t), norm_attn[l], nsa_q_norm[l],
                                                         nsa_k_norm[l], pw)
    z3 = z.reshape(bp, t, z.shape[1])
    o_ret, st_p = _retention_prompt(z3, ret_tabs, jnp.zeros((bp, RET_HEADS, RET_DK, RET_DV), F32), ret_gn[l])
    vc_p = z[:, C_VC:C_VC + kvw]
    nbc = t // CMP_BLOCK
    ck_p = _prologue_matmul(kcn.reshape(bp * nbc, cmp_cols), pw["pos_k"], pw["wk_big"], mode="bias", tm=256,
                            name="compress_k")
    cv_p = _prologue_matmul(vc_p.reshape(bp * nbc, cmp_cols), pw["pos_v"], pw["wv_big"], mode="bias", tm=256,
                            name="compress_v")
    pair = CMP_PER_SEL * kvw
    o_nsa = _nsa_prompt(qn.reshape(bp, t, -1), ck_p.reshape(bp, nbc // CMP_PER_SEL, pair),
                        cv_p.reshape(bp, nbc // CMP_PER_SEL, pair),
                        ksn.reshape(bp, t, kvw), z3, kwn.reshape(bp, t, kvw), gates.reshape(bp, t, LANES))
    o_mem = _mem_attention(z3, mem_q_norm[l], mk_p.reshape(bp, m_tok, memw), mv_p.reshape(bp, m_tok, memw))
    y_p = _layer_tail(xp, z, o_ret.reshape(n_p, -1), o_nsa.reshape(n_p, -1), o_mem.reshape(n_p, -1), norm_ffn[l], pw)

    nbuf_p = min(WINDOW, t)
    kv5 = lambda a, bb, tt: a.reshape(1, bb, tt, g_kv, dh)
    outs_p = (kv5(kcn, bp, t), kv5(vc_p, bp, t), kv5(ksn, bp, t), kv5(z[:, C_VS:C_VS + kvw], bp, t),
              kv5(kwn, bp, t)[:, :, t - nbuf_p:], kv5(z[:, C_VW:C_VW + kvw], bp, t)[:, :, t - nbuf_p:],
              st_p[None], mk_p.reshape(1, bp, m_tok, MEM_HEADS, MEM_HEAD_DIM),
              mv_p.reshape(1, bp, m_tok, MEM_HEADS, MEM_HEAD_DIM))

    xs = x_sample.reshape(bs, d)
    zs, qn_s, kcn_s, ksn_s, kwn_s, gates_s, ret_tabs_s = _layer_common(
        xs, 1, jnp.full((1,), past, I32), norm_attn[l], nsa_q_norm[l], nsa_k_norm[l], pw)
    zs3 = zs.reshape(bs, 1, zs.shape[1])
    o_ret_s, st_s = _retention_decode(zs3, ret_tabs_s, state_ret[l], ret_gn[l])
    per_page = page // CMP_BLOCK
    ckp = _compress_pool(cache_cmp_k, cmp_pos_k[l], cmp_w_k[l], "compress_pool_k")
    cvp = _compress_pool(cache_cmp_v, cmp_pos_v[l], cmp_w_v[l], "compress_pool_v")
    pages_of = lambda a: a[:, page_table].transpose(1, 2, 0, 3).reshape(bs, n_pages * per_page // CMP_PER_SEL, pair)
    ck_s, cv_s = pages_of(ckp), pages_of(cvp)
    qn_s3 = qn_s.reshape(bs, 1, -1)
    ocmp_s, idx_s = _nsa_decode_select(qn_s3, ck_s, cv_s, past)
    nbs = past // SEL_BLOCK + 1
    nsel = min(SEL_TOPK, nbs)
    per_page_sel = page // SEL_BLOCK
    picks = jnp.clip(idx_s[:, 0, :NSA_KV_HEADS * SEL_TOPK].reshape(bs, NSA_KV_HEADS, SEL_TOPK)[:, :, :nsel],
                     0, nbs - 2)
    pool_rows = (jnp.take_along_axis(page_table, (picks // per_page_sel).reshape(bs, -1), axis=1) * per_page_sel
                 + (picks % per_page_sel).reshape(bs, -1)).reshape(-1).astype(I32)
    o_nsa_s = _nsa_decode_attend(
        pool_rows, cache_sel_k, cache_sel_v, qn_s3, idx_s, ksn_s.reshape(bs, 1, kvw), zs3, cache_win_k, cache_win_v,
        kwn_s.reshape(bs, 1, kvw), ocmp_s, gates_s.reshape(bs, 1, LANES), past, nsel)
    o_mem_s = _mem_attention(zs3, mem_q_norm[l], cache_mem_k, cache_mem_v, cache_layout=True)
    y_s = _layer_tail(xs, zs, o_ret_s.reshape(bs, -1), o_nsa_s.reshape(bs, -1), o_mem_s.reshape(bs, -1),
                      norm_ffn[l], pw)

    nbuf_s = cache_win_k.shape[2]
    win_k_s = jnp.concatenate([cache_win_k[l], kv5(kwn_s, bs, 1)[0]], axis=1)[None, :, -nbuf_s:]
    win_v_s = jnp.concatenate([cache_win_v[l], kv5(zs[:, C_VW:C_VW + kvw], bs, 1)[0]], axis=1)[None, :, -nbuf_s:]
    outs_s = (kv5(kcn_s, bs, 1), kv5(zs[:, C_VC:C_VC + kvw], bs, 1), kv5(ksn_s, bs, 1),
              kv5(zs[:, C_VS:C_VS + kvw], bs, 1), win_k_s, win_v_s, st_s[None].astype(x_sample.dtype))
    return (y_p.reshape(bp, t, d), y_s.reshape(bs, ts, d)) + outs_p + outs_s
```

```python
import functools

import numpy as np
import jax
import jax.numpy as jnp
from jax import lax
from jax.experimental import pallas as pl
from jax.experimental.pallas import tpu as pltpu

F32 = jnp.float32
BF16 = jnp.bfloat16
I32 = jnp.int32

EPS = 1e-6
NEG_INF = -1e30
FORCE = 1e4

RET_HEADS, RET_DK, RET_DV, RET_CHUNK, RET_THETA = 4, 128, 256, 128, 10000.0
NSA_HEADS, NSA_KV_HEADS, HEAD_DIM = 8, 2, 128
HEADS_PER_GROUP = NSA_HEADS // NSA_KV_HEADS
CMP_BLOCK, SEL_BLOCK, SEL_TOPK, WINDOW, Q_BLOCK = 32, 64, 4, 256, 128
CMP_PER_SEL = SEL_BLOCK // CMP_BLOCK
ROPE_THETA, ROPE_DIMS = 500000.0, HEAD_DIM // 4
MEM_HEADS, MEM_HEAD_DIM = 4, 256
PEER_HEADS, PEER_KEYS, PEER_DKEY, PEER_TOPK = 8, 128, 128, 16
N_BRANCH = 3
LANES = 128

C_RQ, C_RK, C_RV, C_RG, C_NQ, C_MQ = 0, 512, 1024, 2048, 3072, 4096
C_KC, C_VC, C_KS, C_VS, C_KW, C_VW = 5120, 5376, 5632, 5888, 6144, 6400
C_MG = 6656
Z_TILE = 512


def _cp(*sem):
    return pltpu.CompilerParams(dimension_semantics=sem, vmem_limit_bytes=56 * 1024 * 1024)


def _nt_dot(a, b):
    return lax.dot_general(a, b, (((1,), (1,)), ((), ())), preferred_element_type=F32)


def _tn_dot(a, b):
    return lax.dot_general(a, b, (((0,), (0,)), ((), ())), preferred_element_type=F32)


def _pm_kernel(*refs, mode, residual, emit_h):
    x_ref, g_ref, w_ref = refs[:3]
    k = 3
    r_ref = None
    if residual:
        r_ref = refs[k]
        k += 1
    o_ref = refs[k]
    k += 1
    h_ref = None
    if emit_h:
        h_ref = refs[k]
        k += 1
    h_scr = refs[k]

    @pl.when(pl.program_id(1) == 0)
    def _():
        x = x_ref[...].astype(F32)
        if mode == "norm":
            h = x * lax.rsqrt(jnp.mean(x * x, axis=-1, keepdims=True) + EPS) * g_ref[...]
        elif mode == "bias":
            h = x + g_ref[...]
        else:
            h = x
        h_scr[...] = h.astype(BF16)
        if emit_h:
            h_ref[...] = h_scr[...]

    acc = jnp.dot(h_scr[...], w_ref[...], preferred_element_type=F32)
    if residual:
        acc = acc + r_ref[...]
    o_ref[...] = acc


def _prologue_matmul(x, g, w, *, mode, name, residual=None, emit_h=False, tm=512):
    n, kdim = x.shape
    wout = w.shape[1]
    tm = min(tm, n)
    tn = 512 if wout % 512 == 0 else (256 if wout % 256 == 0 else 128)
    assert n % tm == 0 and wout % tn == 0
    in_specs = [pl.BlockSpec((tm, kdim), lambda i, j: (i, 0)),
                pl.BlockSpec((1, kdim), lambda i, j: (0, 0)),
                pl.BlockSpec((kdim, tn), lambda i, j: (0, j))]
    args = [x, g.reshape(1, kdim).astype(F32), w]
    if residual is not None:
        in_specs.append(pl.BlockSpec((tm, tn), lambda i, j: (i, j)))
        args.append(residual)
    out_shape = [jax.ShapeDtypeStruct((n, wout), F32)]
    out_specs = [pl.BlockSpec((tm, tn), lambda i, j: (i, j))]
    if emit_h:
        out_shape.append(jax.ShapeDtypeStruct((n, kdim), BF16))
        out_specs.append(pl.BlockSpec((tm, kdim), lambda i, j: (i, 0)))
    res = pl.pallas_call(
        functools.partial(_pm_kernel, mode=mode, residual=residual is not None, emit_h=emit_h),
        grid=(n // tm, wout // tn),
        in_specs=in_specs, out_specs=out_specs, out_shape=out_shape,
        scratch_shapes=[pltpu.VMEM((tm, kdim), BF16)],
        compiler_params=_cp("arbitrary", "arbitrary"), name=name,
    )(*args)
    return res if emit_h else res[0]


def _rope_tables(pos, rot_dims, theta, dh):
    half = rot_dims // 2
    inv = theta ** (-jnp.arange(half, dtype=F32) * 2.0 / rot_dims)
    ang = pos.astype(F32)[:, None] * inv[None, :]
    cos, sin = jnp.cos(ang), jnp.sin(ang)
    t = pos.shape[0]
    one = jnp.ones((t, dh - rot_dims), F32)
    zh = jnp.zeros((t, half), F32)
    zr = jnp.zeros((t, dh - rot_dims), F32)
    c = jnp.concatenate([cos, cos, one], axis=1)
    s1 = jnp.concatenate([-sin, zh, zr], axis=1)
    s2 = jnp.concatenate([zh, sin, zr], axis=1)
    return c, s1, s2


def _rope(y, c, s1, s2, half):
    dh = y.shape[-1]
    return y * c + pltpu.roll(y, dh - half, 1) * s1 + pltpu.roll(y, half, 1) * s2


def _headnorm_kernel(*refs, nh, dh, rope_half):
    if rope_half:
        x_ref, g_ref, c_ref, s1_ref, s2_ref, o_ref = refs
    else:
        x_ref, g_ref, o_ref = refs
    for h in range(nh):
        x = x_ref[:, h * dh:(h + 1) * dh]
        y = x * lax.rsqrt(jnp.mean(x * x, axis=-1, keepdims=True) + EPS) * g_ref[...]
        if rope_half:
            y = _rope(y, c_ref[...], s1_ref[...], s2_ref[...], rope_half)
        o_ref[:, h * dh:(h + 1) * dh] = y


def _headnorm(x, col, width, gain, dh, seq, tables=None, rope_half=0):
    n = x.shape[0]
    tm = min(512, seq) if seq > 1 else n
    assert n % tm == 0 and col % width == 0
    cb = col // width
    in_specs = [pl.BlockSpec((tm, width), lambda i: (i, cb)),
                pl.BlockSpec((1, dh), lambda i: (0, 0))]
    args = [x, gain.reshape(1, dh).astype(F32)]
    if rope_half:
        if seq > 1:
            nb = seq // tm
            tspec = pl.BlockSpec((tm, dh), lambda i: (i % nb, 0))
        else:
            tspec = pl.BlockSpec((1, dh), lambda i: (0, 0))
        in_specs += [tspec] * 3
        args += list(tables)
    return pl.pallas_call(
        functools.partial(_headnorm_kernel, nh=width // dh, dh=dh, rope_half=rope_half),
        grid=(n // tm,),
        in_specs=in_specs,
        out_specs=pl.BlockSpec((tm, width), lambda i: (i, 0)),
        out_shape=jax.ShapeDtypeStruct((n, width), F32),
        compiler_params=_cp("arbitrary"), name="head_norm",
    )(*args)


def _log_gamma():
    return np.log1p(-(np.float32(2.0) ** (-5.0 - np.arange(RET_HEADS, dtype=np.float32)))).astype(np.float32)


def _ret_finish(o, gn, gate):
    y = o * lax.rsqrt(jnp.mean(o * o, axis=-1, keepdims=True) + EPS) * gn
    return y * (gate * jax.nn.sigmoid(gate))


def _ret_prompt_kernel(q_ref, k_ref, v_ref, gt_ref, c_ref, s1_ref, s2_ref, st0_ref, gn_ref, o_ref, st_ref):
    cl = q_ref.shape[0]

    @pl.when(pl.program_id(1) == 0)
    def _():
        st_ref[...] = st0_ref[...]

    half = RET_DK // 2
    ri = lax.broadcasted_iota(I32, (cl, cl), 0).astype(F32)
    ci = lax.broadcasted_iota(I32, (cl, cl), 1).astype(F32)
    rel = ri - ci
    rowi = lax.broadcasted_iota(I32, (cl, 1), 0).astype(F32)
    for h, lg in enumerate(_log_gamma().tolist()):
        ks, vs = slice(h * RET_DK, (h + 1) * RET_DK), slice(h * RET_DV, (h + 1) * RET_DV)
        q = _rope(q_ref[:, ks], c_ref[...], s1_ref[...], s2_ref[...], half)
        k = _rope(k_ref[:, ks], c_ref[...], s1_ref[...], s2_ref[...], half) * (RET_DK ** -0.5)
        decay = jnp.where(rel >= 0, jnp.exp(jnp.maximum(rel, 0.0) * lg), 0.0)
        qb, kb, vb = q.astype(BF16), k.astype(BF16), v_ref[:, vs].astype(BF16)
        inner = _nt_dot(qb, kb) * decay
        state = st_ref[h]
        cross = jnp.exp((rowi + 1.0) * lg)
        o = jnp.dot(inner.astype(BF16), vb, preferred_element_type=F32)
        o = o + jnp.dot(qb, state.astype(BF16), preferred_element_type=F32) * cross
        k_dec = (k * jnp.exp((cl - 1.0 - rowi) * lg)).astype(BF16)
        st_ref[h] = float(np.exp(np.float32(cl) * np.float32(lg))) * state + _tn_dot(k_dec, vb)
        o_ref[:, vs] = _ret_finish(o, gn_ref[h], gt_ref[:, vs])


def _retention_prompt(z3, tables, state0, gn):
    b, t, _ = z3.shape
    cl = RET_CHUNK
    assert t % cl == 0
    wq, wv = RET_HEADS * RET_DK, RET_HEADS * RET_DV
    tspec = pl.BlockSpec((cl, RET_DK), lambda bi, c: (c, 0))
    st_spec = pl.BlockSpec((None, RET_HEADS, RET_DK, RET_DV), lambda bi, c: (bi, 0, 0, 0))
    o, st = pl.pallas_call(
        _ret_prompt_kernel,
        grid=(b, t // cl),
        in_specs=[pl.BlockSpec((None, cl, wq), lambda bi, c: (bi, c, C_RQ // wq)),
                  pl.BlockSpec((None, cl, wq), lambda bi, c: (bi, c, C_RK // wq)),
                  pl.BlockSpec((None, cl, wv), lambda bi, c: (bi, c, C_RV // wv)),
                  pl.BlockSpec((None, cl, wv), lambda bi, c: (bi, c, C_RG // wv)),
                  tspec, tspec, tspec, st_spec,
                  pl.BlockSpec((RET_HEADS, 1, RET_DV), lambda bi, c: (0, 0, 0))],
        out_specs=[pl.BlockSpec((None, cl, wv), lambda bi, c: (bi, c, 0)), st_spec],
        out_shape=[jax.ShapeDtypeStruct((b, t, wv), F32),
                   jax.ShapeDtypeStruct((b, RET_HEADS, RET_DK, RET_DV), F32)],
        compiler_params=_cp("arbitrary", "arbitrary"), name="retention_prompt",
    )(z3, z3, z3, z3, *tables, state0, gn.reshape(RET_HEADS, 1, RET_DV))
    return o, st


def _ret_decode_kernel(q_ref, k_ref, v_ref, gt_ref, c_ref, s1_ref, s2_ref, st0_ref, gn_ref, o_ref, st_ref):
    lgs = _log_gamma()
    half = RET_DK // 2
    row0 = lax.broadcasted_iota(I32, (16, 1), 0) == 0
    for h in range(RET_HEADS):
        gamma = float(np.exp(lgs[h]))
        q = _rope(q_ref[:, h * RET_DK:(h + 1) * RET_DK], c_ref[...], s1_ref[...], s2_ref[...], half)
        k = _rope(k_ref[:, h * RET_DK:(h + 1) * RET_DK], c_ref[...], s1_ref[...], s2_ref[...], half)
        k = k * (RET_DK ** -0.5)
        v = v_ref[:, h * RET_DV:(h + 1) * RET_DV]
        state = st0_ref[h].astype(F32)
        inner = jnp.sum(q * k, axis=-1, keepdims=True)
        q16 = jnp.broadcast_to(q, (16, RET_DK)).astype(BF16)
        cross = jnp.dot(q16, state.astype(BF16), preferred_element_type=F32)[0:1]
        o = inner * v + cross * gamma
        k16 = jnp.where(row0, jnp.broadcast_to(k, (16, RET_DK)), 0.0).astype(BF16)
        v16 = jnp.broadcast_to(v, (16, RET_DV)).astype(BF16)
        st_ref[h] = gamma * state + _tn_dot(k16, v16)
        o_ref[:, h * RET_DV:(h + 1) * RET_DV] = _ret_finish(
            o, gn_ref[h], gt_ref[:, h * RET_DV:(h + 1) * RET_DV])


def _retention_decode(z3, tables, state0, gn):
    b = z3.shape[0]
    wq, wv = RET_HEADS * RET_DK, RET_HEADS * RET_DV
    tspec = pl.BlockSpec((1, RET_DK), lambda bi: (0, 0))
    o, st = pl.pallas_call(
        _ret_decode_kernel,
        grid=(b,),
        in_specs=[pl.BlockSpec((None, 1, wq), lambda bi: (bi, 0, C_RQ // wq)),
                  pl.BlockSpec((None, 1, wq), lambda bi: (bi, 0, C_RK // wq)),
                  pl.BlockSpec((None, 1, wv), lambda bi: (bi, 0, C_RV // wv)),
                  pl.BlockSpec((None, 1, wv), lambda bi: (bi, 0, C_RG // wv)),
                  tspec, tspec, tspec,
                  pl.BlockSpec((None, RET_HEADS, RET_DK, RET_DV), lambda bi: (bi, 0, 0, 0)),
                  pl.BlockSpec((RET_HEADS, 1, RET_DV), lambda bi: (0, 0, 0))],
        out_specs=[pl.BlockSpec((None, 1, wv), lambda bi: (bi, 0, 0)),
                   pl.BlockSpec((None, RET_HEADS, RET_DK, RET_DV), lambda bi: (bi, 0, 0, 0))],
        out_shape=[jax.ShapeDtypeStruct((b, 1, wv), F32),
                   jax.ShapeDtypeStruct((b, RET_HEADS, RET_DK, RET_DV), F32)],
        compiler_params=_cp("arbitrary"), name="retention_decode",
    )(z3, z3, z3, z3, *tables, state0, gn.reshape(RET_HEADS, 1, RET_DV))
    return o, st


def _compress_weights(pos_emb, w):
    g, dh = NSA_KV_HEADS, HEAD_DIM
    eye = jnp.eye(g, dtype=w.dtype)
    wbig = jnp.einsum("lde,gh->lgdhe", w, eye).reshape(CMP_BLOCK * g * dh, g * dh)
    pos_row = jnp.broadcast_to(pos_emb[:, None, :], (CMP_BLOCK, g, dh)).reshape(1, CMP_BLOCK * g * dh)
    return pos_row, wbig.astype(BF16)


def _compress_pool_kernel(x_ref, pos_ref, w_ref, o_ref):
    p, page, groups, dh = x_ref.shape
    per_page = page // CMP_BLOCK
    for g in range(groups):
        acc = jnp.zeros((per_page * p, dh), F32)
        for l in range(CMP_BLOCK):
            xl = jnp.concatenate([x_ref[:, n * CMP_BLOCK + l, g, :] for n in range(per_page)], axis=0)
            xl = (xl + pos_ref[l:l + 1, :]).astype(BF16)
            acc = acc + jnp.dot(xl, w_ref[l], preferred_element_type=F32)
        o_ref[:, :, g * dh:(g + 1) * dh] = acc.reshape(per_page, p, dh)


def _compress_pool(pool, pos_emb, w, name):
    _, n_pool, page, groups, dh = pool.shape
    per_page = page // CMP_BLOCK
    p = max(c for c in range(8, 65, 8) if n_pool % c == 0)
    return pl.pallas_call(
        _compress_pool_kernel, grid=(n_pool // p,),
        in_specs=[pl.BlockSpec((None, p, page, groups, dh), lambda i: (0, i, 0, 0, 0)),
                  pl.BlockSpec((CMP_BLOCK, dh), lambda i: (0, 0)),
                  pl.BlockSpec((CMP_BLOCK, dh, dh), lambda i: (0, 0, 0))],
        out_specs=pl.BlockSpec((per_page, p, groups * dh), lambda i: (0, i, 0)),
        out_shape=jax.ShapeDtypeStruct((per_page, n_pool, groups * dh), F32),
        compiler_params=_cp("arbitrary"), name=name,
    )(pool, pos_emb, w.astype(BF16))


def _masked_softmax_parts(parts, masks):
    sm = [jnp.where(m, s, NEG_INF) for s, m in zip(parts, masks)]
    mx = functools.reduce(jnp.maximum, [jnp.max(s, axis=-1, keepdims=True) for s in sm])
    ex = [jnp.exp(s - mx) for s in sm]
    den = functools.reduce(lambda a, b2: a + b2, [jnp.sum(e, axis=-1, keepdims=True) for e in ex])
    inv = 1.0 / den
    return [jnp.where(m, e * inv, 0.0) for e, m in zip(ex, masks)]


def _select_blocks(score, nsel):
    nbs = score.shape[-1]
    lane = lax.broadcasted_iota(I32, score.shape, 1)
    sel = jnp.zeros(score.shape, jnp.bool_)
    picks = []
    for _ in range(nsel):
        m = jnp.max(score, axis=-1, keepdims=True)
        idx = jnp.min(jnp.where(score == m, lane, nbs), axis=-1, keepdims=True)
        hit = lane == idx
        sel = jnp.logical_or(sel, hit)
        score = jnp.where(hit, -jnp.inf, score)
        picks.append(idx)
    return sel, picks


def _nsa_prompt_kernel(q_ref, ck_ref, cv_ref, ks_ref, vs_ref, kw_ref, vw_ref, gt_ref, o_ref, osel_ref):
    i = pl.program_id(1)
    qb = q_ref.shape[0]
    t = ks_ref.shape[0]
    nbs = t // SEL_BLOCK
    hg, dh = HEADS_PER_GROUP, HEAD_DIM
    kvw = NSA_KV_HEADS * dh
    scale = dh ** -0.5
    pos1 = i * qb + lax.broadcasted_iota(I32, (qb, 1), 0)
    pos = jnp.concatenate([pos1] * hg, axis=0)
    blk = lax.broadcasted_iota(I32, (1, nbs), 1)
    vis_e = blk * SEL_BLOCK + (CMP_BLOCK - 1) <= pos
    vis_o = blk * SEL_BLOCK + (SEL_BLOCK - 1) <= pos
    valid = blk * SEL_BLOCK <= pos1
    forced = jnp.logical_or(blk == 0, blk == pos1 // SEL_BLOCK)
    n_case = 4 if t % (4 * qb) == 0 else 1
    span = t // n_case
    case = (i * qb) // span

    def selected_attend(limit, q, sel_f, cs):
        key = lax.broadcasted_iota(I32, (1, limit), 1)
        expand = (lax.broadcasted_iota(I32, (nbs, limit), 1) // SEL_BLOCK
                  == lax.broadcasted_iota(I32, (nbs, limit), 0)).astype(BF16)
        selk = jnp.dot(sel_f, expand, preferred_element_type=F32)
        smask = jnp.logical_and(jnp.concatenate([selk] * hg, axis=0) > 0.5, key <= pos)
        (p_s,) = _masked_softmax_parts([_nt_dot(q, ks_ref[0:limit, cs].astype(BF16)) * scale], [smask])
        osel_ref[...] = jnp.dot(p_s.astype(BF16), vs_ref[0:limit, cs].astype(BF16), preferred_element_type=F32)

    wlen = WINDOW + qb
    wstart = pl.multiple_of(jnp.maximum(i * qb - WINDOW, 0), qb)
    wkey = wstart + lax.broadcasted_iota(I32, (1, wlen), 1)
    wdist = pos - wkey
    wmask = jnp.logical_and(wdist >= 0, wdist <= WINDOW)
    gates = gt_ref[...]

    for g in range(NSA_KV_HEADS):
        cs = slice(g * dh, (g + 1) * dh)
        q = jnp.concatenate([q_ref[:, (g * hg + hh) * dh:(g * hg + hh + 1) * dh] for hh in range(hg)],
                            axis=0).astype(BF16)
        co = slice(kvw + g * dh, kvw + (g + 1) * dh)
        ck_e, ck_o = ck_ref[:, cs].astype(BF16), ck_ref[:, co].astype(BF16)
        cv_e, cv_o = cv_ref[:, cs].astype(BF16), cv_ref[:, co].astype(BF16)
        p_e, p_o = _masked_softmax_parts([_nt_dot(q, ck_e) * scale, _nt_dot(q, ck_o) * scale], [vis_e, vis_o])
        o_cmp = (jnp.dot(p_e.astype(BF16), cv_e, preferred_element_type=F32)
                 + jnp.dot(p_o.astype(BF16), cv_o, preferred_element_type=F32))
        psum = p_e + p_o
        imp = functools.reduce(lambda a, b2: a + b2, [psum[hh * qb:(hh + 1) * qb] for hh in range(hg)])
        score = jnp.where(forced, FORCE, jnp.where(valid, imp, -FORCE))
        sel, _ = _select_blocks(score, min(SEL_TOPK, nbs))
        sel_f = jnp.where(sel, 1.0, 0.0).astype(BF16)
        for c in range(n_case):
            pl.when(case == c)(functools.partial(selected_attend, (c + 1) * span, q, sel_f, cs))
        o_sel = osel_ref[...]
        kwin = kw_ref[pl.ds(wstart, wlen), cs].astype(BF16)
        vwin = vw_ref[pl.ds(wstart, wlen), cs].astype(BF16)
        (p_w,) = _masked_softmax_parts([_nt_dot(q, kwin) * scale], [wmask])
        o_win = jnp.dot(p_w.astype(BF16), vwin, preferred_element_type=F32)
        for hh in range(hg):
            hd = g * hg + hh
            rs = slice(hh * qb, (hh + 1) * qb)
            o_ref[:, hd * dh:(hd + 1) * dh] = (gates[:, 3 * hd:3 * hd + 1] * o_cmp[rs]
                                                 + gates[:, 3 * hd + 1:3 * hd + 2] * o_sel[rs]
                                                 + gates[:, 3 * hd + 2:3 * hd + 3] * o_win[rs])


def _nsa_prompt(qn3, ck3, cv3, ksn3, z3, kwn3, gates3):
    b, t, _ = qn3.shape
    qb = Q_BLOCK
    kvw = NSA_KV_HEADS * HEAD_DIM
    assert t % qb == 0 and t >= WINDOW + qb and t % (2 * SEL_BLOCK) == 0
    nbs = ck3.shape[1]
    full = lambda cb: pl.BlockSpec((None, t, kvw), lambda bi, i: (bi, 0, cb))
    return pl.pallas_call(
        _nsa_prompt_kernel,
        grid=(b, t // qb),
        in_specs=[pl.BlockSpec((None, qb, NSA_HEADS * HEAD_DIM), lambda bi, i: (bi, i, 0)),
                  pl.BlockSpec((None, nbs, CMP_PER_SEL * kvw), lambda bi, i: (bi, 0, 0)),
                  pl.BlockSpec((None, nbs, CMP_PER_SEL * kvw), lambda bi, i: (bi, 0, 0)),
                  full(0), full(C_VS // kvw), full(0), full(C_VW // kvw),
                  pl.BlockSpec((None, qb, LANES), lambda bi, i: (bi, i, 0))],
        out_specs=pl.BlockSpec((None, qb, NSA_HEADS * HEAD_DIM), lambda bi, i: (bi, i, 0)),
        out_shape=jax.ShapeDtypeStruct((b, t, NSA_HEADS * HEAD_DIM), F32),
        scratch_shapes=[pltpu.VMEM((HEADS_PER_GROUP * qb, HEAD_DIM), F32)],
        compiler_params=_cp("arbitrary", "arbitrary"), name="nsa_prompt",
    )(qn3, ck3, cv3, ksn3, z3, kwn3, z3, gates3)


def _decode_q16(q_ref):
    dh = HEAD_DIM
    rows = [q_ref[:, h * dh:(h + 1) * dh] for h in range(NSA_HEADS)]
    return jnp.concatenate(rows + [jnp.zeros((16 - NSA_HEADS, dh), F32)], axis=0).astype(BF16)


def _nsa_decode_select_kernel(q_ref, ck_ref, cv_ref, ocmp_ref, idx_ref, *, past):
    dh, hg = HEAD_DIM, HEADS_PER_GROUP
    kvw = NSA_KV_HEADS * dh
    nhalf = ck_ref.shape[0]
    nbs = past // SEL_BLOCK + 1
    scale = dh ** -0.5
    q16 = _decode_q16(q_ref)
    row = lax.broadcasted_iota(I32, (16, 1), 0)
    blk = lax.broadcasted_iota(I32, (1, LANES), 1)
    blk_h = lax.broadcasted_iota(I32, (1, nhalf), 1)
    vis_e = blk_h * SEL_BLOCK + (CMP_BLOCK - 1) <= past
    vis_o = blk_h * SEL_BLOCK + (SEL_BLOCK - 1) <= past
    o_all = jnp.zeros((16, dh), F32)
    idx_row = jnp.zeros((1, LANES), I32)
    for g in range(NSA_KV_HEADS):
        cs = slice(g * dh, (g + 1) * dh)
        co = slice(kvw + g * dh, kvw + (g + 1) * dh)
        ck_e, ck_o = ck_ref[:, cs].astype(BF16), ck_ref[:, co].astype(BF16)
        cv_e, cv_o = cv_ref[:, cs].astype(BF16), cv_ref[:, co].astype(BF16)
        p_e, p_o = _masked_softmax_parts([_nt_dot(q16, ck_e) * scale, _nt_dot(q16, ck_o) * scale], [vis_e, vis_o])
        o_g = (jnp.dot(p_e.astype(BF16), cv_e, preferred_element_type=F32)
               + jnp.dot(p_o.astype(BF16), cv_o, preferred_element_type=F32))
        in_g = jnp.logical_and(row >= g * hg, row < (g + 1) * hg)
        o_all = jnp.where(in_g, o_g, o_all)
        imp = jnp.sum(jnp.where(in_g, p_e + p_o, 0.0), axis=0, keepdims=True)
        score = jnp.where(blk_h == 0, FORCE, imp)
        nsel = min(SEL_TOPK, nbs)
        _, past_picks = _select_blocks(score, min(nsel - 1, nhalf))
        picks = past_picks[:1] + [jnp.full((1, 1), nhalf, I32)] + past_picks[1:]
        for kk, p in enumerate(picks):
            idx_row = jnp.where(blk == g * SEL_TOPK + kk, p, idx_row)
        for kk in range(len(picks), SEL_TOPK):
            idx_row = jnp.where(blk == g * SEL_TOPK + kk, -1, idx_row)
    ocmp_ref[...] = o_all
    idx_ref[...] = idx_row


def _nsa_decode_select(qn3, ck3, cv3, past):
    b = qn3.shape[0]
    nhalf = ck3.shape[1]
    kvw = NSA_KV_HEADS * HEAD_DIM
    assert nhalf % 8 == 0 and nhalf == past // SEL_BLOCK
    return pl.pallas_call(
        functools.partial(_nsa_decode_select_kernel, past=past),
        grid=(b,),
        in_specs=[pl.BlockSpec((None, 1, NSA_HEADS * HEAD_DIM), lambda bi: (bi, 0, 0)),
                  pl.BlockSpec((None, nhalf, CMP_PER_SEL * kvw), lambda bi: (bi, 0, 0)),
                  pl.BlockSpec((None, nhalf, CMP_PER_SEL * kvw), lambda bi: (bi, 0, 0))],
        out_specs=[pl.BlockSpec((None, 16, HEAD_DIM), lambda bi: (bi, 0, 0)),
                   pl.BlockSpec((None, 1, LANES), lambda bi: (bi, 0, 0))],
        out_shape=[jax.ShapeDtypeStruct((b, 16, HEAD_DIM), F32),
                   jax.ShapeDtypeStruct((b, 1, LANES), I32)],
        compiler_params=_cp("arbitrary"), name="nsa_decode_select",
    )(qn3, ck3, cv3)


def _nsa_decode_attend_kernel(*refs, past, nsel):
    n_blk = NSA_KV_HEADS * nsel
    rows_ref = refs[0]
    kb_refs = refs[1:1 + n_blk]
    vb_refs = refs[1 + n_blk:1 + 2 * n_blk]
    (q_ref, idx_ref, ksn_ref, vs_ref, wk_ref, wv_ref, kwn_ref, vw_ref, ocmp_ref, gt_ref, o_ref) = refs[1 + 2 * n_blk:]
    del rows_ref
    dh, hg = HEAD_DIM, HEADS_PER_GROUP
    scale = dh ** -0.5
    self_blk = past // SEL_BLOCK
    q16 = _decode_q16(q_ref)
    q16f = q16.astype(F32)
    row = lax.broadcasted_iota(I32, (16, 1), 0)
    lane = lax.broadcasted_iota(I32, (1, LANES), 1)
    idx_row = idx_ref[...]
    o_sel = jnp.zeros((16, dh), F32)
    o_win = jnp.zeros((16, dh), F32)
    for g in range(NSA_KV_HEADS):
        cs = slice(g * dh, (g + 1) * dh)
        in_g = jnp.logical_and(row >= g * hg, row < (g + 1) * hg)
        k_self = ksn_ref[:, cs].astype(BF16).astype(F32)
        v_self = vs_ref[:, cs].astype(BF16).astype(F32)
        s_self = jnp.sum(q16f * k_self, axis=-1, keepdims=True) * scale
        parts, masks, vals = [], [], []
        self_sel = jnp.zeros((1, 1), jnp.bool_)
        for kk in range(nsel):
            pick = jnp.max(jnp.where(lane == g * SEL_TOPK + kk, idx_row, -1), axis=-1, keepdims=True)
            in_pool = jnp.logical_and(pick >= 0, pick < self_blk)
            self_sel = jnp.logical_or(self_sel, pick == self_blk)
            kb = kb_refs[g * nsel + kk][:, g, :].astype(BF16)
            parts.append(_nt_dot(q16, kb) * scale)
            masks.append(in_pool)
            vals.append(vb_refs[g * nsel + kk][:, g, :].astype(BF16))
        parts.append(s_self)
        masks.append(self_sel)
        probs = _masked_softmax_parts(parts, masks)
        acc = probs[-1].astype(BF16).astype(F32) * v_self
        for p, vv in zip(probs[:-1], vals):
            acc = acc + jnp.dot(p.astype(BF16), vv, preferred_element_type=F32)
        o_sel = jnp.where(in_g, acc, o_sel)
        kw_self = kwn_ref[:, cs].astype(BF16).astype(F32)
        vw_self = vw_ref[:, cs].astype(BF16).astype(F32)
        sw_self = jnp.sum(q16f * kw_self, axis=-1, keepdims=True) * scale
        wparts = [_nt_dot(q16, wk_ref[:, g, :].astype(BF16)) * scale, sw_self]
        wmasks = [lane[:, :1] >= 0, lane[:, :1] >= 0]
        pw, pws = _masked_softmax_parts(wparts, wmasks)
        accw = (jnp.dot(pw.astype(BF16), wv_ref[:, g, :].astype(BF16), preferred_element_type=F32)
                + pws.astype(BF16).astype(F32) * vw_self)
        o_win = jnp.where(in_g, accw, o_win)
    gates = gt_ref[...]
    o_cmp = ocmp_ref[...]
    for hd in range(NSA_HEADS):
        o_ref[:, hd * dh:(hd + 1) * dh] = (gates[:, 3 * hd:3 * hd + 1] * o_cmp[hd:hd + 1]
                                             + gates[:, 3 * hd + 1:3 * hd + 2] * o_sel[hd:hd + 1]
                                             + gates[:, 3 * hd + 2:3 * hd + 3] * o_win[hd:hd + 1])


def _nsa_decode_attend(pool_rows, sel_k_pool, sel_v_pool, qn3, idx3, ksn3, z3, win_k, win_v, kwn3, ocmp, gates3,
                       past, nsel):
    b = qn3.shape[0]
    kvw = NSA_KV_HEADS * HEAD_DIM
    n_blk = NSA_KV_HEADS * nsel
    nbuf = win_k.shape[2]
    per_page = sel_k_pool.shape[2] // SEL_BLOCK

    def blk_spec(tt):
        return pl.BlockSpec((None, None, SEL_BLOCK, NSA_KV_HEADS, HEAD_DIM),
                            lambda bi, r: (0, r[bi * n_blk + tt] // per_page, r[bi * n_blk + tt] % per_page, 0, 0))

    def row_spec(cb, w=kvw):
        return pl.BlockSpec((None, 1, w), lambda bi, r: (bi, 0, cb))

    wspec = pl.BlockSpec((None, None, nbuf, NSA_KV_HEADS, HEAD_DIM), lambda bi, r: (0, bi, 0, 0, 0))
    in_specs = ([blk_spec(tt) for tt in range(n_blk)] + [blk_spec(tt) for tt in range(n_blk)]
                + [pl.BlockSpec((None, 1, NSA_HEADS * HEAD_DIM), lambda bi, r: (bi, 0, 0)),
                   pl.BlockSpec((None, 1, LANES), lambda bi, r: (bi, 0, 0)),
                   row_spec(0), row_spec(C_VS // kvw),
                   wspec, wspec,
                   row_spec(0), row_spec(C_VW // kvw),
                   pl.BlockSpec((None, 16, HEAD_DIM), lambda bi, r: (bi, 0, 0)),
                   pl.BlockSpec((None, 1, LANES), lambda bi, r: (bi, 0, 0))])
    return pl.pallas_call(
        functools.partial(_nsa_decode_attend_kernel, past=past, nsel=nsel),
        grid_spec=pltpu.PrefetchScalarGridSpec(
            num_scalar_prefetch=1, grid=(b,), in_specs=in_specs,
            out_specs=pl.BlockSpec((None, 1, NSA_HEADS * HEAD_DIM), lambda bi, r: (bi, 0, 0))),
        out_shape=jax.ShapeDtypeStruct((b, 1, NSA_HEADS * HEAD_DIM), F32),
        compiler_params=_cp("arbitrary"), name="nsa_decode_attend",
    )(pool_rows, *([sel_k_pool] * n_blk), *([sel_v_pool] * n_blk),
      qn3, idx3, ksn3, z3, win_k, win_v, kwn3, z3, ocmp, gates3)


def _mem_attn_kernel(*refs, cache_layout):
    dh = MEM_HEAD_DIM
    if cache_layout:
        q_ref, g_ref, mk0_ref, mk1_ref, mv0_ref, mv1_ref, o_ref = refs
    else:
        q_ref, g_ref, mk_ref, mv_ref, o_ref = refs
    tq = q_ref.shape[0]
    rows = max(tq, 16)
    for h in range(MEM_HEADS):
        cs = slice(h * dh, (h + 1) * dh)
        x = q_ref[:, cs]
        q = x * lax.rsqrt(jnp.mean(x * x, axis=-1, keepdims=True) + EPS) * g_ref[...]
        if rows != tq:
            q = jnp.broadcast_to(q, (rows, dh))
        if cache_layout:
            mk = jnp.concatenate([mk0_ref[:, h, :], mk1_ref[:, h, :]], axis=1)
            mv = jnp.concatenate([mv0_ref[:, h, :], mv1_ref[:, h, :]], axis=1)
        else:
            mk, mv = mk_ref[:, cs], mv_ref[:, cs]
        s = _nt_dot(q.astype(BF16), mk.astype(BF16)) * (dh ** -0.5)
        m = jnp.max(s, axis=-1, keepdims=True)
        e = jnp.exp(s - m)
        p = e / jnp.sum(e, axis=-1, keepdims=True)
        o = jnp.dot(p.astype(BF16), mv.astype(BF16), preferred_element_type=F32)
        o_ref[:, cs] = o[0:tq]


def _mem_attention(z3, gq, mk, mv, cache_layout=False):
    b, t, _ = z3.shape
    w = MEM_HEADS * MEM_HEAD_DIM
    tq = min(t, 512)
    assert t % tq == 0
    if cache_layout:
        m = mk.shape[2]
        assert MEM_HEAD_DIM == 2 * LANES
        half = lambda c: pl.BlockSpec((None, None, m, MEM_HEADS, LANES), lambda bi, i: (0, bi, 0, 0, c))
        kv_specs, kv_args = [half(0), half(1), half(0), half(1)], [mk, mk, mv, mv]
    else:
        m = mk.shape[1]
        kv_specs, kv_args = [pl.BlockSpec((None, m, w), lambda bi, i: (bi, 0, 0))] * 2, [mk, mv]
    return pl.pallas_call(
        functools.partial(_mem_attn_kernel, cache_layout=cache_layout),
        grid=(b, t // tq),
        in_specs=[pl.BlockSpec((None, tq, w), lambda bi, i: (bi, i, C_MQ // w)),
                  pl.BlockSpec((1, MEM_HEAD_DIM), lambda bi, i: (0, 0))] + kv_specs,
        out_specs=pl.BlockSpec((None, tq, w), lambda bi, i: (bi, i, 0)),
        out_shape=jax.ShapeDtypeStruct((b, t, w), F32),
        compiler_params=_cp("arbitrary", "arbitrary"), name="mem_attention",
    )(z3, gq.reshape(1, MEM_HEAD_DIM), *kv_args)


def _sigmoid_cols_kernel(x_ref, o_ref):
    o_ref[...] = jax.nn.sigmoid(x_ref[...])


def _nsa_gates(z, col):
    n = z.shape[0]
    tm = min(n, 1024)
    return pl.pallas_call(
        _sigmoid_cols_kernel, grid=(n // tm,),
        in_specs=[pl.BlockSpec((tm, LANES), lambda i: (i, col // LANES))],
        out_specs=pl.BlockSpec((tm, LANES), lambda i: (i, 0)),
        out_shape=jax.ShapeDtypeStruct((n, LANES), F32),
        compiler_params=_cp("arbitrary"), name="nsa_gates",
    )(z)


def _merge_kernel(r_ref, n_ref, m_ref, g0_ref, g1_ref, g2_ref, w_ref, o_ref):
    acc = jnp.zeros(o_ref.shape, F32)
    for c, (b_ref, g_ref) in enumerate(((r_ref, g0_ref), (n_ref, g1_ref), (m_ref, g2_ref))):
        up = jnp.dot(b_ref[...].astype(BF16), w_ref[c], preferred_element_type=F32)
        acc = acc + jax.nn.sigmoid(g_ref[...]) * up
    o_ref[...] = acc.astype(BF16)


def _merge(o_ret, o_nsa, o_mem, z, wb, d_model):
    n, bw = o_ret.shape
    tm = min(n, 1024)
    tn = 512
    gb = C_MG // tn
    nj = d_model // tn
    bspec = pl.BlockSpec((tm, bw), lambda i, j: (i, 0))
    gspec = lambda c: pl.BlockSpec((tm, tn), lambda i, j: (i, gb + c * nj + j))
    return pl.pallas_call(
        _merge_kernel, grid=(n // tm, nj),
        in_specs=[bspec, bspec, bspec, gspec(0), gspec(1), gspec(2),
                  pl.BlockSpec((N_BRANCH, bw, tn), lambda i, j: (0, 0, j))],
        out_specs=pl.BlockSpec((tm, tn), lambda i, j: (i, j)),
        out_shape=jax.ShapeDtypeStruct((n, d_model), BF16),
        compiler_params=_cp("arbitrary", "arbitrary"), name="branch_merge",
    )(o_ret, o_nsa, o_mem, z, z, z, wb)


def _topk_rows(s, kk):
    e = s.shape[0]
    ridx = lax.broadcasted_iota(I32, s.shape, 0).astype(F32)
    vals, idxs = [], []
    for _ in range(kk):
        m = jnp.max(s, axis=0, keepdims=True)
        i = jnp.min(jnp.where(s == m, ridx, float(e)), axis=0, keepdims=True)
        vals.append(m)
        idxs.append(i)
        s = jnp.where(ridx == i, -jnp.inf, s)
    return jnp.concatenate(vals, axis=0), jnp.concatenate(idxs, axis=0)


def _pair_pieces(kk):
    pieces, cur = [], []

    def flush(rows):
        pieces.append(rows + [None] * (8 - len(rows)))

    for a in range(kk):
        grp = [(a, b) for b in range(kk // (a + 1))]
        if cur and len(cur) + len(grp) > 8:
            flush(cur)
            cur = []
        cur = cur + grp
        while len(cur) >= 8:
            flush(cur[:8])
            cur = cur[8:]
    if cur:
        flush(cur)
    return pieces


def _rows_by_runs(x, ids):
    if ids[0] % 8 == 0 and ids == list(range(ids[0], ids[0] + 8)):
        return x[ids[0]:ids[0] + 8]
    r = lax.broadcasted_iota(I32, (8, 1), 0)
    runs = []
    for p, i in enumerate(ids):
        if not runs or runs[-1][1] != i:
            runs.append((p, i))
    out = jnp.broadcast_to(x[runs[-1][1]:runs[-1][1] + 1], (8, x.shape[1]))
    for (_, i), (nxt, _) in zip(reversed(runs[:-1]), reversed(runs[1:])):
        out = jnp.where(r < nxt, x[i:i + 1], out)
    return out


def _peer_route_kernel(q_ref, sk_ref, i_ref, j_ref, g_ref):
    kk = PEER_TOPK
    tl = q_ref.shape[0]
    st = _nt_dot(sk_ref[...], q_ref[...].astype(BF16))
    v0, i0 = _topk_rows(st[0:PEER_KEYS], kk)
    v1, i1 = _topk_rows(st[PEER_KEYS:2 * PEER_KEYS], kk)
    cand, ci, cj = [], [], []
    for piece in _pair_pieces(kk):
        ra = [p[0] if p else 0 for p in piece]
        rb = [p[1] if p else 0 for p in piece]
        live = lax.broadcasted_iota(I32, (8, 1), 0) < sum(p is not None for p in piece)
        cand.append(jnp.where(live, _rows_by_runs(v0, ra) + _rows_by_runs(v1, rb), -jnp.inf))
        ci.append(_rows_by_runs(i0, ra))
        cj.append(_rows_by_runs(i1, rb))
    cand, ci, cj = (jnp.concatenate(c, axis=0) for c in (cand, ci, cj))
    ridx = lax.broadcasted_iota(I32, cand.shape, 0).astype(F32)
    sc, si, sj = [], [], []
    for _ in range(kk):
        m = jnp.max(cand, axis=0, keepdims=True)
        r = jnp.min(jnp.where(cand == m, ridx, float(cand.shape[0])), axis=0, keepdims=True)
        hit = ridx == r
        sc.append(m)
        si.append(jnp.max(jnp.where(hit, ci, -1.0), axis=0, keepdims=True))
        sj.append(jnp.max(jnp.where(hit, cj, -1.0), axis=0, keepdims=True))
        cand = jnp.where(hit, -jnp.inf, cand)
    sc = jnp.concatenate(sc, axis=0)
    e = jnp.exp(sc - jnp.max(sc, axis=0, keepdims=True))
    g_ref[...] = e / jnp.sum(e, axis=0, keepdims=True)
    i_ref[...] = jnp.concatenate(si, axis=0).astype(I32)
    j_ref[...] = jnp.concatenate(sj, axis=0).astype(I32)


def _peer_route(q, skbd):
    n = q.shape[0]
    tl = min(n, LANES)
    kk = PEER_TOPK
    slots = PEER_HEADS * kk
    ospec = pl.BlockSpec((kk, tl), lambda i, h: (h, i))
    return pl.pallas_call(
        _peer_route_kernel, grid=(n // tl, PEER_HEADS),
        in_specs=[pl.BlockSpec((tl, PEER_DKEY), lambda i, h: (i, h)),
                  pl.BlockSpec((None, 2 * PEER_KEYS, PEER_DKEY), lambda i, h: (h, 0, 0))],
        out_specs=[ospec, ospec, ospec],
        out_shape=[jax.ShapeDtypeStruct((slots, n), I32), jax.ShapeDtypeStruct((slots, n), I32),
                   jax.ShapeDtypeStruct((slots, n), F32)],
        compiler_params=_cp("arbitrary", "arbitrary"), name="peer_route",
    )(q, skbd)


TOKEN_GROUP = 8


def _peer_weights_kernel(i_ref, j_ref, g_ref, o_ref, it_scr, jt_scr, gt_scr):
    it_scr[...] = i_ref[...].T
    jt_scr[...] = j_ref[...].T
    gt_scr[...] = g_ref[...].T
    key = lax.broadcasted_iota(I32, (PEER_KEYS, i_ref.shape[0]), 0)

    def body(nb, carry):
        for t in range(TOKEN_GROUP):
            n = nb * TOKEN_GROUP + t
            irow = it_scr[pl.ds(n, 1), :]
            jrow = jt_scr[pl.ds(n, 1), :]
            grow = gt_scr[pl.ds(n, 1), :]
            a = jnp.where(key == irow, grow, 0.0).astype(BF16)
            b = jnp.where(key == jrow, 1.0, 0.0).astype(BF16)
            o_ref[nb, pl.ds(t, PEER_KEYS, stride=TOKEN_GROUP), :] = _nt_dot(a, b)
        return carry

    lax.fori_loop(0, o_ref.shape[0], body, 0)


def _peer_weights(it, jt, gt):
    slots, n = it.shape
    tb = min(n, LANES)
    assert tb % TOKEN_GROUP == 0
    ispec = pl.BlockSpec((slots, tb), lambda i: (0, i))
    out = pl.pallas_call(
        _peer_weights_kernel, grid=(n // tb,),
        in_specs=[ispec, ispec, ispec],
        out_specs=pl.BlockSpec((tb // TOKEN_GROUP, PEER_KEYS * TOKEN_GROUP, PEER_KEYS), lambda i: (i, 0, 0)),
        out_shape=jax.ShapeDtypeStruct((n // TOKEN_GROUP, PEER_KEYS * TOKEN_GROUP, PEER_KEYS), F32),
        scratch_shapes=[pltpu.VMEM((tb, slots), I32), pltpu.VMEM((tb, slots), I32), pltpu.VMEM((tb, slots), F32)],
        compiler_params=_cp("arbitrary"), name="peer_weights",
    )(it, jt, gt)
    return out.reshape(n // TOKEN_GROUP, PEER_KEYS, TOKEN_GROUP, PEER_KEYS)


def _peer_dense_kernel(h_ref, g_ref, u_ref, v_ref, x_ref, o_ref):
    @pl.when(pl.program_id(1) == 0)
    def _():
        o_ref[...] = x_ref[...]

    a = jnp.dot(h_ref[...], u_ref[...], preferred_element_type=F32)
    act = 0.5 * a * (1.0 + lax.erf(a * (2.0 ** -0.5)))
    tm = h_ref.shape[0]
    w = jnp.concatenate([(g_ref[:, ii].reshape(tm, PEER_KEYS)
                          * act[:, ii * PEER_KEYS:(ii + 1) * PEER_KEYS]).astype(BF16)
                         for ii in range(g_ref.shape[1])], axis=1)
    o_ref[...] += jnp.dot(w, v_ref[...], preferred_element_type=F32)


def _peer_dense(hn, gw, ut, vv, x):
    n, d = hn.shape
    ne = ut.shape[1]
    tm = min(n, 512)
    te = 1024
    return pl.pallas_call(
        _peer_dense_kernel, grid=(n // tm, ne // te),
        in_specs=[pl.BlockSpec((tm, d), lambda i, j: (i, 0)),
                  pl.BlockSpec((tm // TOKEN_GROUP, te // PEER_KEYS, TOKEN_GROUP, PEER_KEYS),
                               lambda i, j: (i, j, 0, 0)),
                  pl.BlockSpec((d, te), lambda i, j: (0, j)),
                  pl.BlockSpec((te, d), lambda i, j: (j, 0)),
                  pl.BlockSpec((tm, d), lambda i, j: (i, 0))],
        out_specs=pl.BlockSpec((tm, d), lambda i, j: (i, 0)),
        out_shape=jax.ShapeDtypeStruct((n, d), F32),
        compiler_params=_cp("arbitrary", "arbitrary"), name="peer_dense",
    )(hn, gw, ut, vv, x)


def _prepare_weights(norm_attn, w_in, cmp_pos_k, cmp_pos_v, cmp_w_k, cmp_w_v, w_mem_kv, w_branch, w_out,
                     peer_wq, peer_subkeys, peer_u, peer_v):
    d = w_in.shape[0]
    ng_w = NSA_HEADS * 3
    o_ng = 5632
    o_mq = o_ng + ng_w
    o_mg = o_mq + MEM_HEADS * MEM_HEAD_DIM
    width = C_MG + N_BRANCH * d + LANES
    width = -(-width // Z_TILE) * Z_TILE
    w_r = jnp.concatenate([w_in[:, 0:4096], w_in[:, o_mq:o_mg], w_in[:, 4096:o_ng], w_in[:, o_mg:],
                           w_in[:, o_ng:o_mq],
                           jnp.zeros((d, width - (C_MG + N_BRANCH * d + ng_w)), w_in.dtype)], axis=1).astype(BF16)
    pos_k, wk_big = _compress_weights(cmp_pos_k, cmp_w_k)
    pos_v, wv_big = _compress_weights(cmp_pos_v, cmp_w_v)
    half = PEER_DKEY // 2
    sk = peer_subkeys.astype(BF16)
    zk = jnp.zeros((PEER_HEADS, PEER_KEYS, half), BF16)
    skbd = jnp.concatenate([jnp.concatenate([sk[:, 0], zk], axis=2),
                            jnp.concatenate([zk, sk[:, 1]], axis=2)], axis=1)
    return dict(w_in=w_r, c_ng=C_MG + N_BRANCH * d, pos_k=pos_k, wk_big=wk_big, pos_v=pos_v, wv_big=wv_big,
                w_mem_kv=w_mem_kv.astype(BF16), w_branch=w_branch.astype(BF16), w_out=w_out.astype(BF16),
                peer_wq=peer_wq.astype(BF16), skbd=skbd, ut=peer_u.astype(BF16).T, vv=peer_v.astype(BF16))


def _layer_common(x, seq, pos, norm_attn, nsa_q_norm, nsa_k_norm, pw):
    z = _prologue_matmul(x, norm_attn, pw["w_in"], mode="norm", name="in_proj", tm=1024)
    nsa_tabs = _rope_tables(pos, ROPE_DIMS, ROPE_THETA, HEAD_DIM)
    half = ROPE_DIMS // 2
    kvw = NSA_KV_HEADS * HEAD_DIM
    qn = _headnorm(z, C_NQ, NSA_HEADS * HEAD_DIM, nsa_q_norm, HEAD_DIM, seq, nsa_tabs, half)
    kcn = _headnorm(z, C_KC, kvw, nsa_k_norm[0], HEAD_DIM, seq, nsa_tabs, half)
    ksn = _headnorm(z, C_KS, kvw, nsa_k_norm[1], HEAD_DIM, seq, nsa_tabs, half)
    kwn = _headnorm(z, C_KW, kvw, nsa_k_norm[2], HEAD_DIM, seq, nsa_tabs, half)
    gates = _nsa_gates(z, pw["c_ng"])
    ret_tabs = _rope_tables(pos, RET_DK, RET_THETA, RET_DK)
    return z, qn, kcn, ksn, kwn, gates, ret_tabs


def _layer_tail(x, z, o_ret, o_nsa, o_mem, norm_ffn, pw):
    d = x.shape[1]
    merged = _merge(o_ret, o_nsa, o_mem, z, pw["w_branch"], d)
    x1 = _prologue_matmul(merged, jnp.ones((d,), F32), pw["w_out"], mode="none", residual=x, name="out_proj")
    q, hn = _prologue_matmul(x1, norm_ffn, pw["peer_wq"], mode="norm", emit_h=True, name="peer_query")
    it, jt, gt = _peer_route(q, pw["skbd"])
    return _peer_dense(hn, _peer_weights(it, jt, gt), pw["ut"], pw["vv"], x1)


def kernel(x_prompt, x_sample, cache_cmp_k, cache_cmp_v, cache_sel_k, cache_sel_v, cache_win_k, cache_win_v,
           state_ret, cache_mem_k, cache_mem_v, page_table, mem_prompt, norm_attn, w_in, ret_gn, nsa_q_norm,
           nsa_k_norm, cmp_pos_k, cmp_pos_v, cmp_w_k, cmp_w_v, norm_mem, w_mem_kv, mem_q_norm, mem_k_norm,
           w_branch, w_out, norm_ffn, peer_wq, peer_subkeys, peer_u, peer_v):
    depth = w_in.shape[0]
    assert depth == 1
    l = 0
    bp, t, d = x_prompt.shape
    bs, ts, _ = x_sample.shape
    assert ts == 1
    n_pool, page, g_kv, dh = cache_cmp_k.shape[1:]
    n_pages = page_table.shape[1]
    past = n_pages * page
    kvw = g_kv * dh
    memw = MEM_HEADS * MEM_HEAD_DIM
    pw = _prepare_weights(norm_attn[l], w_in[l], cmp_pos_k[l], cmp_pos_v[l], cmp_w_k[l], cmp_w_v[l], w_mem_kv[l],
                          w_branch[l], w_out[l], peer_wq[l], peer_subkeys[l], peer_u[l], peer_v[l])
    cmp_cols = CMP_BLOCK * kvw

    n_p = bp * t
    xp = x_prompt.reshape(n_p, d)
    m_tok = mem_prompt.shape[1]
    mkv = _prologue_matmul(mem_prompt.reshape(bp * m_tok, d), norm_mem[l], pw["w_mem_kv"], mode="norm",
                           name="mem_kv_proj")
    mk_p = _headnorm(mkv, 0, memw, mem_k_norm[l], MEM_HEAD_DIM, m_tok)
    mv_p = mkv[:, memw:]
    z, qn, kcn, ksn, kwn, gates, ret_tabs = _layer_common(xp, t, jnp.arange(t), norm_attn[l], nsa_q_norm[l],
                                                         nsa_k_norm[l], pw)
    z3 = z.reshape(bp, t, z.shape[1])
    o_ret, st_p = _retention_prompt(z3, ret_tabs, jnp.zeros((bp, RET_HEADS, RET_DK, RET_DV), F32), ret_gn[l])
    vc_p = z[:, C_VC:C_VC + kvw]
    nbc = t // CMP_BLOCK
    ck_p = _prologue_matmul(kcn.reshape(bp * nbc, cmp_cols), pw["pos_k"], pw["wk_big"], mode="bias", tm=256,
                            name="compress_k")
    cv_p = _prologue_matmul(vc_p.reshape(bp * nbc, cmp_cols), pw["pos_v"], pw["wv_big"], mode="bias", tm=256,
                            name="compress_v")
    pair = CMP_PER_SEL * kvw
    o_nsa = _nsa_prompt(qn.reshape(bp, t, -1), ck_p.reshape(bp, nbc // CMP_PER_SEL, pair),
                        cv_p.reshape(bp, nbc // CMP_PER_SEL, pair),
                        ksn.reshape(bp, t, kvw), z3, kwn.reshape(bp, t, kvw), gates.reshape(bp, t, LANES))
    o_mem = _mem_attention(z3, mem_q_norm[l], mk_p.reshape(bp, m_tok, memw), mv_p.reshape(bp, m_tok, memw))
    y_p = _layer_tail(xp, z, o_ret.reshape(n_p, -1), o_nsa.reshape(n_p, -1), o_mem.reshape(n_p, -1), norm_ffn[l], pw)

    nbuf_p = min(WINDOW, t)
    kv5 = lambda a, bb, tt: a.reshape(1, bb, tt, g_kv, dh)
    outs_p = (kv5(kcn, bp, t), kv5(vc_p, bp, t), kv5(ksn, bp, t), kv5(z[:, C_VS:C_VS + kvw], bp, t),
              kv5(kwn, bp, t)[:, :, t - nbuf_p:], kv5(z[:, C_VW:C_VW + kvw], bp, t)[:, :, t - nbuf_p:],
              st_p[None], mk_p.reshape(1, bp, m_tok, MEM_HEADS, MEM_HEAD_DIM),
              mv_p.reshape(1, bp, m_tok, MEM_HEADS, MEM_HEAD_DIM))

    xs = x_sample.reshape(bs, d)
    zs, qn_s, kcn_s, ksn_s, kwn_s, gates_s, ret_tabs_s = _layer_common(
        xs, 1, jnp.full((1,), past, I32), norm_attn[l], nsa_q_norm[l], nsa_k_norm[l], pw)
    zs3 = zs.reshape(bs, 1, zs.shape[1])
    o_ret_s, st_s = _retention_decode(zs3, ret_tabs_s, state_ret[l], ret_gn[l])
    per_page = page // CMP_BLOCK
    ckp = _compress_pool(cache_cmp_k, cmp_pos_k[l], cmp_w_k[l], "compress_pool_k")
    cvp = _compress_pool(cache_cmp_v, cmp_pos_v[l], cmp_w_v[l], "compress_pool_v")
    pages_of = lambda a: a[:, page_table].transpose(1, 2, 0, 3).reshape(bs, n_pages * per_page // CMP_PER_SEL, pair)
    ck_s, cv_s = pages_of(ckp), pages_of(cvp)
    qn_s3 = qn_s.reshape(bs, 1, -1)
    ocmp_s, idx_s = _nsa_decode_select(qn_s3, ck_s, cv_s, past)
    nbs = past // SEL_BLOCK + 1
    nsel = min(SEL_TOPK, nbs)
    per_page_sel = page // SEL_BLOCK
    picks = jnp.clip(idx_s[:, 0, :NSA_KV_HEADS * SEL_TOPK].reshape(bs, NSA_KV_HEADS, SEL_TOPK)[:, :, :nsel],
                     0, nbs - 2)
    pool_rows = (jnp.take_along_axis(page_table, (picks // per_page_sel).reshape(bs, -1), axis=1) * per_page_sel
                 + (picks % per_page_sel).reshape(bs, -1)).reshape(-1).astype(I32)
    o_nsa_s = _nsa_decode_attend(
        pool_rows, cache_sel_k, cache_sel_v, qn_s3, idx_s, ksn_s.reshape(bs, 1, kvw), zs3, cache_win_k, cache_win_v,
        kwn_s.reshape(bs, 1, kvw), ocmp_s, gates_s.reshape(bs, 1, LANES), past, nsel)
    o_mem_s = _mem_attention(zs3, mem_q_norm[l], cache_mem_k, cache_mem_v, cache_layout=True)
    y_s = _layer_tail(xs, zs, o_ret_s.reshape(bs, -1), o_nsa_s.reshape(bs, -1), o_mem_s.reshape(bs, -1),
                      norm_ffn[l], pw)

    nbuf_s = cache_win_k.shape[2]
    win_k_s = jnp.concatenate([cache_win_k[l], kv5(kwn_s, bs, 1)[0]], axis=1)[None, :, -nbuf_s:]
    win_v_s = jnp.concatenate([cache_win_v[l], kv5(zs[:, C_VW:C_VW + kvw], bs, 1)[0]], axis=1)[None, :, -nbuf_s:]
    outs_s = (kv5(kcn_s, bs, 1), kv5(zs[:, C_VC:C_VC + kvw], bs, 1), kv5(ksn_s, bs, 1),
              kv5(zs[:, C_VS:C_VS + kvw], bs, 1), win_k_s, win_v_s, st_s[None].astype(x_sample.dtype))
    return (y_p.reshape(bp, t, d), y_s.reshape(bs, ts, d)) + outs_p + outs_s
```

```python
import functools

import numpy as np
import jax
import jax.numpy as jnp
from jax import lax
from jax.experimental import pallas as pl
from jax.experimental.pallas import tpu as pltpu

F32 = jnp.float32
BF16 = jnp.bfloat16
I32 = jnp.int32

EPS = 1e-6
NEG_INF = -1e30
FORCE = 1e4

RET_HEADS, RET_DK, RET_DV, RET_CHUNK, RET_THETA = 4, 128, 256, 128, 10000.0
NSA_HEADS, NSA_KV_HEADS, HEAD_DIM = 8, 2, 128
HEADS_PER_GROUP = NSA_HEADS // NSA_KV_HEADS
CMP_BLOCK, SEL_BLOCK, SEL_TOPK, WINDOW, Q_BLOCK = 32, 64, 4, 256, 128
CMP_PER_SEL = SEL_BLOCK // CMP_BLOCK
ROPE_THETA, ROPE_DIMS = 500000.0, HEAD_DIM // 4
MEM_HEADS, MEM_HEAD_DIM = 4, 256
PEER_HEADS, PEER_KEYS, PEER_DKEY, PEER_TOPK = 8, 128, 128, 16
N_BRANCH = 3
LANES = 128

C_RQ, C_RK, C_RV, C_RG, C_NQ, C_MQ = 0, 512, 1024, 2048, 3072, 4096
C_KC, C_VC, C_KS, C_VS, C_KW, C_VW = 5120, 5376, 5632, 5888, 6144, 6400
C_MG = 6656
Z_TILE = 512


def _cp(*sem):
    return pltpu.CompilerParams(dimension_semantics=sem, vmem_limit_bytes=56 * 1024 * 1024)


def _nt_dot(a, b):
    return lax.dot_general(a, b, (((1,), (1,)), ((), ())), preferred_element_type=F32)


def _tn_dot(a, b):
    return lax.dot_general(a, b, (((0,), (0,)), ((), ())), preferred_element_type=F32)


def _pm_kernel(*refs, mode, residual, emit_h):
    x_ref, g_ref, w_ref = refs[:3]
    k = 3
    r_ref = None
    if residual:
        r_ref = refs[k]
        k += 1
    o_ref = refs[k]
    k += 1
    h_ref = None
    if emit_h:
        h_ref = refs[k]
        k += 1
    h_scr = refs[k]

    @pl.when(pl.program_id(1) == 0)
    def _():
        x = x_ref[...].astype(F32)
        if mode == "norm":
            h = x * lax.rsqrt(jnp.mean(x * x, axis=-1, keepdims=True) + EPS) * g_ref[...]
        elif mode == "bias":
            h = x + g_ref[...]
        else:
            h = x
        h_scr[...] = h.astype(BF16)
        if emit_h:
            h_ref[...] = h_scr[...]

    acc = jnp.dot(h_scr[...], w_ref[...], preferred_element_type=F32)
    if residual:
        acc = acc + r_ref[...]
    o_ref[...] = acc


def _prologue_matmul(x, g, w, *, mode, name, residual=None, emit_h=False, tm=512):
    n, kdim = x.shape
    wout = w.shape[1]
    tm = min(tm, n)
    tn = 512 if wout % 512 == 0 else (256 if wout % 256 == 0 else 128)
    assert n % tm == 0 and wout % tn == 0
    in_specs = [pl.BlockSpec((tm, kdim), lambda i, j: (i, 0)),
                pl.BlockSpec((1, kdim), lambda i, j: (0, 0)),
                pl.BlockSpec((kdim, tn), lambda i, j: (0, j))]
    args = [x, g.reshape(1, kdim).astype(F32), w]
    if residual is not None:
        in_specs.append(pl.BlockSpec((tm, tn), lambda i, j: (i, j)))
        args.append(residual)
    out_shape = [jax.ShapeDtypeStruct((n, wout), F32)]
    out_specs = [pl.BlockSpec((tm, tn), lambda i, j: (i, j))]
    if emit_h:
        out_shape.append(jax.ShapeDtypeStruct((n, kdim), BF16))
        out_specs.append(pl.BlockSpec((tm, kdim), lambda i, j: (i, 0)))
    res = pl.pallas_call(
        functools.partial(_pm_kernel, mode=mode, residual=residual is not None, emit_h=emit_h),
        grid=(n // tm, wout // tn),
        in_specs=in_specs, out_specs=out_specs, out_shape=out_shape,
        scratch_shapes=[pltpu.VMEM((tm, kdim), BF16)],
        compiler_params=_cp("arbitrary", "arbitrary"), name=name,
    )(*args)
    return res if emit_h else res[0]


def _rope_tables(pos, rot_dims, theta, dh):
    half = rot_dims // 2
    inv = theta ** (-jnp.arange(half, dtype=F32) * 2.0 / rot_dims)
    ang = pos.astype(F32)[:, None] * inv[None, :]
    cos, sin = jnp.cos(ang), jnp.sin(ang)
    t = pos.shape[0]
    one = jnp.ones((t, dh - rot_dims), F32)
    zh = jnp.zeros((t, half), F32)
    zr = jnp.zeros((t, dh - rot_dims), F32)
    c = jnp.concatenate([cos, cos, one], axis=1)
    s1 = jnp.concatenate([-sin, zh, zr], axis=1)
    s2 = jnp.concatenate([zh, sin, zr], axis=1)
    return c, s1, s2


def _rope(y, c, s1, s2, half):
    dh = y.shape[-1]
    return y * c + pltpu.roll(y, dh - half, 1) * s1 + pltpu.roll(y, half, 1) * s2


def _headnorm_kernel(*refs, nh, dh, rope_half):
    if rope_half:
        x_ref, g_ref, c_ref, s1_ref, s2_ref, o_ref = refs
    else:
        x_ref, g_ref, o_ref = refs
    for h in range(nh):
        x = x_ref[:, h * dh:(h + 1) * dh]
        y = x * lax.rsqrt(jnp.mean(x * x, axis=-1, keepdims=True) + EPS) * g_ref[...]
        if rope_half:
            y = _rope(y, c_ref[...], s1_ref[...], s2_ref[...], rope_half)
        o_ref[:, h * dh:(h + 1) * dh] = y


def _headnorm(x, col, width, gain, dh, seq, tables=None, rope_half=0):
    n = x.shape[0]
    tm = min(512, seq) if seq > 1 else n
    assert n % tm == 0 and col % width == 0
    cb = col // width
    in_specs = [pl.BlockSpec((tm, width), lambda i: (i, cb)),
                pl.BlockSpec((1, dh), lambda i: (0, 0))]
    args = [x, gain.reshape(1, dh).astype(F32)]
    if rope_half:
        if seq > 1:
            nb = seq // tm
            tspec = pl.BlockSpec((tm, dh), lambda i: (i % nb, 0))
        else:
            tspec = pl.BlockSpec((1, dh), lambda i: (0, 0))
        in_specs += [tspec] * 3
        args += list(tables)
    return pl.pallas_call(
        functools.partial(_headnorm_kernel, nh=width // dh, dh=dh, rope_half=rope_half),
        grid=(n // tm,),
        in_specs=in_specs,
        out_specs=pl.BlockSpec((tm, width), lambda i: (i, 0)),
        out_shape=jax.ShapeDtypeStruct((n, width), F32),
        compiler_params=_cp("arbitrary"), name="head_norm",
    )(*args)


def _log_gamma():
    return np.log1p(-(np.float32(2.0) ** (-5.0 - np.arange(RET_HEADS, dtype=np.float32)))).astype(np.float32)


def _ret_finish(o, gn, gate):
    y = o * lax.rsqrt(jnp.mean(o * o, axis=-1, keepdims=True) + EPS) * gn
    return y * (gate * jax.nn.sigmoid(gate))


def _ret_prompt_kernel(q_ref, k_ref, v_ref, gt_ref, c_ref, s1_ref, s2_ref, st0_ref, gn_ref, o_ref, st_ref):
    cl = q_ref.shape[0]

    @pl.when(pl.program_id(1) == 0)
    def _():
        st_ref[...] = st0_ref[...]

    half = RET_DK // 2
    ri = lax.broadcasted_iota(I32, (cl, cl), 0).astype(F32)
    ci = lax.broadcasted_iota(I32, (cl, cl), 1).astype(F32)
    rel = ri - ci
    rowi = lax.broadcasted_iota(I32, (cl, 1), 0).astype(F32)
    for h, lg in enumerate(_log_gamma().tolist()):
        ks, vs = slice(h * RET_DK, (h + 1) * RET_DK), slice(h * RET_DV, (h + 1) * RET_DV)
        q = _rope(q_ref[:, ks], c_ref[...], s1_ref[...], s2_ref[...], half)
        k = _rope(k_ref[:, ks], c_ref[...], s1_ref[...], s2_ref[...], half) * (RET_DK ** -0.5)
        decay = jnp.where(rel >= 0, jnp.exp(jnp.maximum(rel, 0.0) * lg), 0.0)
        qb, kb, vb = q.astype(BF16), k.astype(BF16), v_ref[:, vs].astype(BF16)
        inner = _nt_dot(qb, kb) * decay
        state = st_ref[h]
        cross = jnp.exp((rowi + 1.0) * lg)
        o = jnp.dot(inner.astype(BF16), vb, preferred_element_type=F32)
        o = o + jnp.dot(qb, state.astype(BF16), preferred_element_type=F32) * cross
        k_dec = (k * jnp.exp((cl - 1.0 - rowi) * lg)).astype(BF16)
        st_ref[h] = float(np.exp(np.float32(cl) * np.float32(lg))) * state + _tn_dot(k_dec, vb)
        o_ref[:, vs] = _ret_finish(o, gn_ref[h], gt_ref[:, vs])


def _retention_prompt(z3, tables, state0, gn):
    b, t, _ = z3.shape
    cl = RET_CHUNK
    assert t % cl == 0
    wq, wv = RET_HEADS * RET_DK, RET_HEADS * RET_DV
    tspec = pl.BlockSpec((cl, RET_DK), lambda bi, c: (c, 0))
    st_spec = pl.BlockSpec((None, RET_HEADS, RET_DK, RET_DV), lambda bi, c: (bi, 0, 0, 0))
    o, st = pl.pallas_call(
        _ret_prompt_kernel,
        grid=(b, t // cl),
        in_specs=[pl.BlockSpec((None, cl, wq), lambda bi, c: (bi, c, C_RQ // wq)),
                  pl.BlockSpec((None, cl, wq), lambda bi, c: (bi, c, C_RK // wq)),
                  pl.BlockSpec((None, cl, wv), lambda bi, c: (bi, c, C_RV // wv)),
                  pl.BlockSpec((None, cl, wv), lambda bi, c: (bi, c, C_RG // wv)),
                  tspec, tspec, tspec, st_spec,
                  pl.BlockSpec((RET_HEADS, 1, RET_DV), lambda bi, c: (0, 0, 0))],
        out_specs=[pl.BlockSpec((None, cl, wv), lambda bi, c: (bi, c, 0)), st_spec],
        out_shape=[jax.ShapeDtypeStruct((b, t, wv), F32),
                   jax.ShapeDtypeStruct((b, RET_HEADS, RET_DK, RET_DV), F32)],
        compiler_params=_cp("arbitrary", "arbitrary"), name="retention_prompt",
    )(z3, z3, z3, z3, *tables, state0, gn.reshape(RET_HEADS, 1, RET_DV))
    return o, st


def _ret_decode_kernel(q_ref, k_ref, v_ref, gt_ref, c_ref, s1_ref, s2_ref, st0_ref, gn_ref, o_ref, st_ref):
    lgs = _log_gamma()
    half = RET_DK // 2
    row0 = lax.broadcasted_iota(I32, (16, 1), 0) == 0
    for h in range(RET_HEADS):
        gamma = float(np.exp(lgs[h]))
        q = _rope(q_ref[:, h * RET_DK:(h + 1) * RET_DK], c_ref[...], s1_ref[...], s2_ref[...], half)
        k = _rope(k_ref[:, h * RET_DK:(h + 1) * RET_DK], c_ref[...], s1_ref[...], s2_ref[...], half)
        k = k * (RET_DK ** -0.5)
        v = v_ref[:, h * RET_DV:(h + 1) * RET_DV]
        state = st0_ref[h].astype(F32)
        inner = jnp.sum(q * k, axis=-1, keepdims=True)
        q16 = jnp.broadcast_to(q, (16, RET_DK)).astype(BF16)
        cross = jnp.dot(q16, state.astype(BF16), preferred_element_type=F32)[0:1]
        o = inner * v + cross * gamma
        k16 = jnp.where(row0, jnp.broadcast_to(k, (16, RET_DK)), 0.0).astype(BF16)
        v16 = jnp.broadcast_to(v, (16, RET_DV)).astype(BF16)
        st_ref[h] = gamma * state + _tn_dot(k16, v16)
        o_ref[:, h * RET_DV:(h + 1) * RET_DV] = _ret_finish(
            o, gn_ref[h], gt_ref[:, h * RET_DV:(h + 1) * RET_DV])


def _retention_decode(z3, tables, state0, gn):
    b = z3.shape[0]
    wq, wv = RET_HEADS * RET_DK, RET_HEADS * RET_DV
    tspec = pl.BlockSpec((1, RET_DK), lambda bi: (0, 0))
    o, st = pl.pallas_call(
        _ret_decode_kernel,
        grid=(b,),
        in_specs=[pl.BlockSpec((None, 1, wq), lambda bi: (bi, 0, C_RQ // wq)),
                  pl.BlockSpec((None, 1, wq), lambda bi: (bi, 0, C_RK // wq)),
                  pl.BlockSpec((None, 1, wv), lambda bi: (bi, 0, C_RV // wv)),
                  pl.BlockSpec((None, 1, wv), lambda bi: (bi, 0, C_RG // wv)),
                  tspec, tspec, tspec,
                  pl.BlockSpec((None, RET_HEADS, RET_DK, RET_DV), lambda bi: (bi, 0, 0, 0)),
                  pl.BlockSpec((RET_HEADS, 1, RET_DV), lambda bi: (0, 0, 0))],
        out_specs=[pl.BlockSpec((None, 1, wv), lambda bi: (bi, 0, 0)),
                   pl.BlockSpec((None, RET_HEADS, RET_DK, RET_DV), lambda bi: (bi, 0, 0, 0))],
        out_shape=[jax.ShapeDtypeStruct((b, 1, wv), F32),
                   jax.ShapeDtypeStruct((b, RET_HEADS, RET_DK, RET_DV), F32)],
        compiler_params=_cp("arbitrary"), name="retention_decode",
    )(z3, z3, z3, z3, *tables, state0, gn.reshape(RET_HEADS, 1, RET_DV))
    return o, st


def _compress_weights(pos_emb, w):
    g, dh = NSA_KV_HEADS, HEAD_DIM
    eye = jnp.eye(g, dtype=w.dtype)
    wbig = jnp.einsum("lde,gh->lgdhe", w, eye).reshape(CMP_BLOCK * g * dh, g * dh)
    pos_row = jnp.broadcast_to(pos_emb[:, None, :], (CMP_BLOCK, g, dh)).reshape(1, CMP_BLOCK * g * dh)
    return pos_row, wbig.astype(BF16)


def _compress_pool_kernel(x_ref, pos_ref, w_ref, o_ref):
    p, page, groups, dh = x_ref.shape
    per_page = page // CMP_BLOCK
    for g in range(groups):
        acc = jnp.zeros((per_page * p, dh), F32)
        for l in range(CMP_BLOCK):
            xl = jnp.concatenate([x_ref[:, n * CMP_BLOCK + l, g, :] for n in range(per_page)], axis=0)
            xl = (xl + pos_ref[l:l + 1, :]).astype(BF16)
            acc = acc + jnp.dot(xl, w_ref[l], preferred_element_type=F32)
        o_ref[:, :, g * dh:(g + 1) * dh] = acc.reshape(per_page, p, dh)


def _compress_pool(pool, pos_emb, w, name):
    _, n_pool, page, groups, dh = pool.shape
    per_page = page // CMP_BLOCK
    p = max(c for c in range(8, 65, 8) if n_pool % c == 0)
    return pl.pallas_call(
        _compress_pool_kernel, grid=(n_pool // p,),
        in_specs=[pl.BlockSpec((None, p, page, groups, dh), lambda i: (0, i, 0, 0, 0)),
                  pl.BlockSpec((CMP_BLOCK, dh), lambda i: (0, 0)),
                  pl.BlockSpec((CMP_BLOCK, dh, dh), lambda i: (0, 0, 0))],
        out_specs=pl.BlockSpec((per_page, p, groups * dh), lambda i: (0, i, 0)),
        out_shape=jax.ShapeDtypeStruct((per_page, n_pool, groups * dh), F32),
        compiler_params=_cp("arbitrary"), name=name,
    )(pool, pos_emb, w.astype(BF16))


def _masked_softmax_parts(parts, masks):
    sm = [jnp.where(m, s, NEG_INF) for s, m in zip(parts, masks)]
    mx = functools.reduce(jnp.maximum, [jnp.max(s, axis=-1, keepdims=True) for s in sm])
    ex = [jnp.exp(s - mx) for s in sm]
    den = functools.reduce(lambda a, b2: a + b2, [jnp.sum(e, axis=-1, keepdims=True) for e in ex])
    inv = 1.0 / den
    return [jnp.where(m, e * inv, 0.0) for e, m in zip(ex, masks)]


def _select_blocks(score, nsel):
    nbs = score.shape[-1]
    lane = lax.broadcasted_iota(I32, score.shape, 1)
    sel = jnp.zeros(score.shape, jnp.bool_)
    picks = []
    for _ in range(nsel):
        m = jnp.max(score, axis=-1, keepdims=True)
        idx = jnp.min(jnp.where(score == m, lane, nbs), axis=-1, keepdims=True)
        hit = lane == idx
        sel = jnp.logical_or(sel, hit)
        score = jnp.where(hit, -jnp.inf, score)
        picks.append(idx)
    return sel, picks


def _nsa_prompt_kernel(q_ref, ck_ref, cv_ref, ks_ref, vs_ref, kw_ref, vw_ref, gt_ref, o_ref, osel_ref):
    i = pl.program_id(1)
    qb = q_ref.shape[0]
    t = ks_ref.shape[0]
    nbs = t // SEL_BLOCK
    hg, dh = HEADS_PER_GROUP, HEAD_DIM
    kvw = NSA_KV_HEADS * dh
    scale = dh ** -0.5
    pos1 = i * qb + lax.broadcasted_iota(I32, (qb, 1), 0)
    pos = jnp.concatenate([pos1] * hg, axis=0)
    blk = lax.broadcasted_iota(I32, (1, nbs), 1)
    vis_e = blk * SEL_BLOCK + (CMP_BLOCK - 1) <= pos
    vis_o = blk * SEL_BLOCK + (SEL_BLOCK - 1) <= pos
    valid = blk * SEL_BLOCK <= pos1
    forced = jnp.logical_or(blk == 0, blk == pos1 // SEL_BLOCK)
    n_case = 4 if t % (4 * qb) == 0 else 1
    span = t // n_case
    case = (i * qb) // span

    def selected_attend(limit, q, sel_f, cs):
        key = lax.broadcasted_iota(I32, (1, limit), 1)
        expand = (lax.broadcasted_iota(I32, (nbs, limit), 1) // SEL_BLOCK
                  == lax.broadcasted_iota(I32, (nbs, limit), 0)).astype(BF16)
        selk = jnp.dot(sel_f, expand, preferred_element_type=F32)
        smask = jnp.logical_and(jnp.concatenate([selk] * hg, axis=0) > 0.5, key <= pos)
        (p_s,) = _masked_softmax_parts([_nt_dot(q, ks_ref[0:limit, cs].astype(BF16)) * scale], [smask])
        osel_ref[...] = jnp.dot(p_s.astype(BF16), vs_ref[0:limit, cs].astype(BF16), preferred_element_type=F32)

    wlen = WINDOW + qb
    wstart = pl.multiple_of(jnp.maximum(i * qb - WINDOW, 0), qb)
    wkey = wstart + lax.broadcasted_iota(I32, (1, wlen), 1)
    wdist = pos - wkey
    wmask = jnp.logical_and(wdist >= 0, wdist <= WINDOW)
    gates = gt_ref[...]

    for g in range(NSA_KV_HEADS):
        cs = slice(g * dh, (g + 1) * dh)
        q = jnp.concatenate([q_ref[:, (g * hg + hh) * dh:(g * hg + hh + 1) * dh] for hh in range(hg)],
                            axis=0).astype(BF16)
        co = slice(kvw + g * dh, kvw + (g + 1) * dh)
        ck_e, ck_o = ck_ref[:, cs].astype(BF16), ck_ref[:, co].astype(BF16)
        cv_e, cv_o = cv_ref[:, cs].astype(BF16), cv_ref[:, co].astype(BF16)
        p_e, p_o = _masked_softmax_parts([_nt_dot(q, ck_e) * scale, _nt_dot(q, ck_o) * scale], [vis_e, vis_o])
        o_cmp = (jnp.dot(p_e.astype(BF16), cv_e, preferred_element_type=F32)
                 + jnp.dot(p_o.astype(BF16), cv_o, preferred_element_type=F32))
        psum = p_e + p_o
        imp = functools.reduce(lambda a, b2: a + b2, [psum[hh * qb:(hh + 1) * qb] for hh in range(hg)])
        score = jnp.where(forced, FORCE, jnp.where(valid, imp, -FORCE))
        sel, _ = _select_blocks(score, min(SEL_TOPK, nbs))
        sel_f = jnp.where(sel, 1.0, 0.0).astype(BF16)
        for c in range(n_case):
            pl.when(case == c)(functools.partial(selected_attend, (c + 1) * span, q, sel_f, cs))
        o_sel = osel_ref[...]
        kwin = kw_ref[pl.ds(wstart, wlen), cs].astype(BF16)
        vwin = vw_ref[pl.ds(wstart, wlen), cs].astype(BF16)
        (p_w,) = _masked_softmax_parts([_nt_dot(q, kwin) * scale], [wmask])
        o_win = jnp.dot(p_w.astype(BF16), vwin, preferred_element_type=F32)
        for hh in range(hg):
            hd = g * hg + hh
            rs = slice(hh * qb, (hh + 1) * qb)
            o_ref[:, hd * dh:(hd + 1) * dh] = (gates[:, 3 * hd:3 * hd + 1] * o_cmp[rs]
                                                 + gates[:, 3 * hd + 1:3 * hd + 2] * o_sel[rs]
                                                 + gates[:, 3 * hd + 2:3 * hd + 3] * o_win[rs])


def _nsa_prompt(qn3, ck3, cv3, ksn3, z3, kwn3, gates3):
    b, t, _ = qn3.shape
    qb = Q_BLOCK
    kvw = NSA_KV_HEADS * HEAD_DIM
    assert t % qb == 0 and t >= WINDOW + qb and t % (2 * SEL_BLOCK) == 0
    nbs = ck3.shape[1]
    full = lambda cb: pl.BlockSpec((None, t, kvw), lambda bi, i: (bi, 0, cb))
    return pl.pallas_call(
        _nsa_prompt_kernel,
        grid=(b, t // qb),
        in_specs=[pl.BlockSpec((None, qb, NSA_HEADS * HEAD_DIM), lambda bi, i: (bi, i, 0)),
                  pl.BlockSpec((None, nbs, CMP_PER_SEL * kvw), lambda bi, i: (bi, 0, 0)),
                  pl.BlockSpec((None, nbs, CMP_PER_SEL * kvw), lambda bi, i: (bi, 0, 0)),
                  full(0), full(C_VS // kvw), full(0), full(C_VW // kvw),
                  pl.BlockSpec((None, qb, LANES), lambda bi, i: (bi, i, 0))],
        out_specs=pl.BlockSpec((None, qb, NSA_HEADS * HEAD_DIM), lambda bi, i: (bi, i, 0)),
        out_shape=jax.ShapeDtypeStruct((b, t, NSA_HEADS * HEAD_DIM), F32),
        scratch_shapes=[pltpu.VMEM((HEADS_PER_GROUP * qb, HEAD_DIM), F32)],
        compiler_params=_cp("arbitrary", "arbitrary"), name="nsa_prompt",
    )(qn3, ck3, cv3, ksn3, z3, kwn3, z3, gates3)


def _decode_q16(q_ref):
    dh = HEAD_DIM
    rows = [q_ref[:, h * dh:(h + 1) * dh] for h in range(NSA_HEADS)]
    return jnp.concatenate(rows + [jnp.zeros((16 - NSA_HEADS, dh), F32)], axis=0).astype(BF16)


def _nsa_decode_select_kernel(q_ref, ck_ref, cv_ref, ocmp_ref, idx_ref, *, past):
    dh, hg = HEAD_DIM, HEADS_PER_GROUP
    kvw = NSA_KV_HEADS * dh
    nhalf = ck_ref.shape[0]
    nbs = past // SEL_BLOCK + 1
    scale = dh ** -0.5
    q16 = _decode_q16(q_ref)
    row = lax.broadcasted_iota(I32, (16, 1), 0)
    blk = lax.broadcasted_iota(I32, (1, LANES), 1)
    blk_h = lax.broadcasted_iota(I32, (1, nhalf), 1)
    vis_e = blk_h * SEL_BLOCK + (CMP_BLOCK - 1) <= past
    vis_o = blk_h * SEL_BLOCK + (SEL_BLOCK - 1) <= past
    o_all = jnp.zeros((16, dh), F32)
    idx_row = jnp.zeros((1, LANES), I32)
    for g in range(NSA_KV_HEADS):
        cs = slice(g * dh, (g + 1) * dh)
        co = slice(kvw + g * dh, kvw + (g + 1) * dh)
        ck_e, ck_o = ck_ref[:, cs].astype(BF16), ck_ref[:, co].astype(BF16)
        cv_e, cv_o = cv_ref[:, cs].astype(BF16), cv_ref[:, co].astype(BF16)
        p_e, p_o = _masked_softmax_parts([_nt_dot(q16, ck_e) * scale, _nt_dot(q16, ck_o) * scale], [vis_e, vis_o])
        o_g = (jnp.dot(p_e.astype(BF16), cv_e, preferred_element_type=F32)
               + jnp.dot(p_o.astype(BF16), cv_o, preferred_element_type=F32))
        in_g = jnp.logical_and(row >= g * hg, row < (g + 1) * hg)
        o_all = jnp.where(in_g, o_g, o_all)
        imp = jnp.sum(jnp.where(in_g, p_e + p_o, 0.0), axis=0, keepdims=True)
        score = jnp.where(blk_h == 0, FORCE, imp)
        nsel = min(SEL_TOPK, nbs)
        _, past_picks = _select_blocks(score, min(nsel - 1, nhalf))
        picks = past_picks[:1] + [jnp.full((1, 1), nhalf, I32)] + past_picks[1:]
        for kk, p in enumerate(picks):
            idx_row = jnp.where(blk == g * SEL_TOPK + kk, p, idx_row)
        for kk in range(len(picks), SEL_TOPK):
            idx_row = jnp.where(blk == g * SEL_TOPK + kk, -1, idx_row)
    ocmp_ref[...] = o_all
    idx_ref[...] = idx_row


def _nsa_decode_select(qn3, ck3, cv3, past):
    b = qn3.shape[0]
    nhalf = ck3.shape[1]
    kvw = NSA_KV_HEADS * HEAD_DIM
    assert nhalf % 8 == 0 and nhalf == past // SEL_BLOCK
    return pl.pallas_call(
        functools.partial(_nsa_decode_select_kernel, past=past),
        grid=(b,),
        in_specs=[pl.BlockSpec((None, 1, NSA_HEADS * HEAD_DIM), lambda bi: (bi, 0, 0)),
                  pl.BlockSpec((None, nhalf, CMP_PER_SEL * kvw), lambda bi: (bi, 0, 0)),
                  pl.BlockSpec((None, nhalf, CMP_PER_SEL * kvw), lambda bi: (bi, 0, 0))],
        out_specs=[pl.BlockSpec((None, 16, HEAD_DIM), lambda bi: (bi, 0, 0)),
                   pl.BlockSpec((None, 1, LANES), lambda bi: (bi, 0, 0))],
        out_shape=[jax.ShapeDtypeStruct((b, 16, HEAD_DIM), F32),
                   jax.ShapeDtypeStruct((b, 1, LANES), I32)],
        compiler_params=_cp("arbitrary"), name="nsa_decode_select",
    )(qn3, ck3, cv3)


def _nsa_decode_attend_kernel(*refs, past, nsel):
    n_blk = NSA_KV_HEADS * nsel
    rows_ref = refs[0]
    kb_refs = refs[1:1 + n_blk]
    vb_refs = refs[1 + n_blk:1 + 2 * n_blk]
    (q_ref, idx_ref, ksn_ref, vs_ref, wk_ref, wv_ref, kwn_ref, vw_ref, ocmp_ref, gt_ref, o_ref) = refs[1 + 2 * n_blk:]
    del rows_ref
    dh, hg = HEAD_DIM, HEADS_PER_GROUP
    scale = dh ** -0.5
    self_blk = past // SEL_BLOCK
    q16 = _decode_q16(q_ref)
    q16f = q16.astype(F32)
    row = lax.broadcasted_iota(I32, (16, 1), 0)
    lane = lax.broadcasted_iota(I32, (1, LANES), 1)
    idx_row = idx_ref[...]
    o_sel = jnp.zeros((16, dh), F32)
    o_win = jnp.zeros((16, dh), F32)
    for g in range(NSA_KV_HEADS):
        cs = slice(g * dh, (g + 1) * dh)
        in_g = jnp.logical_and(row >= g * hg, row < (g + 1) * hg)
        k_self = ksn_ref[:, cs].astype(BF16).astype(F32)
        v_self = vs_ref[:, cs].astype(BF16).astype(F32)
        s_self = jnp.sum(q16f * k_self, axis=-1, keepdims=True) * scale
        parts, masks, vals = [], [], []
        self_sel = jnp.zeros((1, 1), jnp.bool_)
        for kk in range(nsel):
            pick = jnp.max(jnp.where(lane == g * SEL_TOPK + kk, idx_row, -1), axis=-1, keepdims=True)
            in_pool = jnp.logical_and(pick >= 0, pick < self_blk)
            self_sel = jnp.logical_or(self_sel, pick == self_blk)
            kb = kb_refs[g * nsel + kk][:, g, :].astype(BF16)
            parts.append(_nt_dot(q16, kb) * scale)
            masks.append(in_pool)
            vals.append(vb_refs[g * nsel + kk][:, g, :].astype(BF16))
        parts.append(s_self)
        masks.append(self_sel)
        probs = _masked_softmax_parts(parts, masks)
        acc = probs[-1].astype(BF16).astype(F32) * v_self
        for p, vv in zip(probs[:-1], vals):
            acc = acc + jnp.dot(p.astype(BF16), vv, preferred_element_type=F32)
        o_sel = jnp.where(in_g, acc, o_sel)
        kw_self = kwn_ref[:, cs].astype(BF16).astype(F32)
        vw_self = vw_ref[:, cs].astype(BF16).astype(F32)
        sw_self = jnp.sum(q16f * kw_self, axis=-1, keepdims=True) * scale
        wparts = [_nt_dot(q16, wk_ref[:, g, :].astype(BF16)) * scale, sw_self]
        wmasks = [lane[:, :1] >= 0, lane[:, :1] >= 0]
        pw, pws = _masked_softmax_parts(wparts, wmasks)
        accw = (jnp.dot(pw.astype(BF16), wv_ref[:, g, :].astype(BF16), preferred_element_type=F32)
                + pws.astype(BF16).astype(F32) * vw_self)
        o_win = jnp.where(in_g, accw, o_win)
    gates = gt_ref[...]
    o_cmp = ocmp_ref[...]
    for hd in range(NSA_HEADS):
        o_ref[:, hd * dh:(hd + 1) * dh] = (gates[:, 3 * hd:3 * hd + 1] * o_cmp[hd:hd + 1]
                                             + gates[:, 3 * hd + 1:3 * hd + 2] * o_sel[hd:hd + 1]
                                             + gates[:, 3 * hd + 2:3 * hd + 3] * o_win[hd:hd + 1])


def _nsa_decode_attend(pool_rows, sel_k_pool, sel_v_pool, qn3, idx3, ksn3, z3, win_k, win_v, kwn3, ocmp, gates3,
                       past, nsel):
    b = qn3.shape[0]
    kvw = NSA_KV_HEADS * HEAD_DIM
    n_blk = NSA_KV_HEADS * nsel
    nbuf = win_k.shape[2]
    per_page = sel_k_pool.shape[2] // SEL_BLOCK

    def blk_spec(tt):
        return pl.BlockSpec((None, None, SEL_BLOCK, NSA_KV_HEADS, HEAD_DIM),
                            lambda bi, r: (0, r[bi * n_blk + tt] // per_page, r[bi * n_blk + tt] % per_page, 0, 0))

    def row_spec(cb, w=kvw):
        return pl.BlockSpec((None, 1, w), lambda bi, r: (bi, 0, cb))

    wspec = pl.BlockSpec((None, None, nbuf, NSA_KV_HEADS, HEAD_DIM), lambda bi, r: (0, bi, 0, 0, 0))
    in_specs = ([blk_spec(tt) for tt in range(n_blk)] + [blk_spec(tt) for tt in range(n_blk)]
                + [pl.BlockSpec((None, 1, NSA_HEADS * HEAD_DIM), lambda bi, r: (bi, 0, 0)),
                   pl.BlockSpec((None, 1, LANES), lambda bi, r: (bi, 0, 0)),
                   row_spec(0), row_spec(C_VS // kvw),
                   wspec, wspec,
                   row_spec(0), row_spec(C_VW // kvw),
                   pl.BlockSpec((None, 16, HEAD_DIM), lambda bi, r: (bi, 0, 0)),
                   pl.BlockSpec((None, 1, LANES), lambda bi, r: (bi, 0, 0))])
    return pl.pallas_call(
        functools.partial(_nsa_decode_attend_kernel, past=past, nsel=nsel),
        grid_spec=pltpu.PrefetchScalarGridSpec(
            num_scalar_prefetch=1, grid=(b,), in_specs=in_specs,
            out_specs=pl.BlockSpec((None, 1, NSA_HEADS * HEAD_DIM), lambda bi, r: (bi, 0, 0))),
        out_shape=jax.ShapeDtypeStruct((b, 1, NSA_HEADS * HEAD_DIM), F32),
        compiler_params=_cp("arbitrary"), name="nsa_decode_attend",
    )(pool_rows, *([sel_k_pool] * n_blk), *([sel_v_pool] * n_blk),
      qn3, idx3, ksn3, z3, win_k, win_v, kwn3, z3, ocmp, gates3)


def _mem_attn_kernel(*refs, cache_layout):
    dh = MEM_HEAD_DIM
    if cache_layout:
        q_ref, g_ref, mk0_ref, mk1_ref, mv0_ref, mv1_ref, o_ref = refs
    else:
        q_ref, g_ref, mk_ref, mv_ref, o_ref = refs
    tq = q_ref.shape[0]
    rows = max(tq, 16)
    for h in range(MEM_HEADS):
        cs = slice(h * dh, (h + 1) * dh)
        x = q_ref[:, cs]
        q = x * lax.rsqrt(jnp.mean(x * x, axis=-1, keepdims=True) + EPS) * g_ref[...]
        if rows != tq:
            q = jnp.broadcast_to(q, (rows, dh))
        if cache_layout:
            mk = jnp.concatenate([mk0_ref[:, h, :], mk1_ref[:, h, :]], axis=1)
            mv = jnp.concatenate([mv0_ref[:, h, :], mv1_ref[:, h, :]], axis=1)
        else:
            mk, mv = mk_ref[:, cs], mv_ref[:, cs]
        s = _nt_dot(q.astype(BF16), mk.astype(BF16)) * (dh ** -0.5)
        m = jnp.max(s, axis=-1, keepdims=True)
        e = jnp.exp(s - m)
        p = e / jnp.sum(e, axis=-1, keepdims=True)
        o = jnp.dot(p.astype(BF16), mv.astype(BF16), preferred_element_type=F32)
        o_ref[:, cs] = o[0:tq]


def _mem_attention(z3, gq, mk, mv, cache_layout=False):
    b, t, _ = z3.shape
    w = MEM_HEADS * MEM_HEAD_DIM
    tq = min(t, 512)
    assert t % tq == 0
    if cache_layout:
        m = mk.shape[2]
        assert MEM_HEAD_DIM == 2 * LANES
        half = lambda c: pl.BlockSpec((None, None, m, MEM_HEADS, LANES), lambda bi, i: (0, bi, 0, 0, c))
        kv_specs, kv_args = [half(0), half(1), half(0), half(1)], [mk, mk, mv, mv]
    else:
        m = mk.shape[1]
        kv_specs, kv_args = [pl.BlockSpec((None, m, w), lambda bi, i: (bi, 0, 0))] * 2, [mk, mv]
    return pl.pallas_call(
        functools.partial(_mem_attn_kernel, cache_layout=cache_layout),
        grid=(b, t // tq),
        in_specs=[pl.BlockSpec((None, tq, w), lambda bi, i: (bi, i, C_MQ // w)),
                  pl.BlockSpec((1, MEM_HEAD_DIM), lambda bi, i: (0, 0))] + kv_specs,
        out_specs=pl.BlockSpec((None, tq, w), lambda bi, i: (bi, i, 0)),
        out_shape=jax.ShapeDtypeStruct((b, t, w), F32),
        compiler_params=_cp("arbitrary", "arbitrary"), name="mem_attention",
    )(z3, gq.reshape(1, MEM_HEAD_DIM), *kv_args)


def _sigmoid_cols_kernel(x_ref, o_ref):
    o_ref[...] = jax.nn.sigmoid(x_ref[...])


def _nsa_gates(z, col):
    n = z.shape[0]
    tm = min(n, 1024)
    return pl.pallas_call(
        _sigmoid_cols_kernel, grid=(n // tm,),
        in_specs=[pl.BlockSpec((tm, LANES), lambda i: (i, col // LANES))],
        out_specs=pl.BlockSpec((tm, LANES), lambda i: (i, 0)),
        out_shape=jax.ShapeDtypeStruct((n, LANES), F32),
        compiler_params=_cp("arbitrary"), name="nsa_gates",
    )(z)


def _merge_kernel(r_ref, n_ref, m_ref, g0_ref, g1_ref, g2_ref, w_ref, o_ref):
    acc = jnp.zeros(o_ref.shape, F32)
    for c, (b_ref, g_ref) in enumerate(((r_ref, g0_ref), (n_ref, g1_ref), (m_ref, g2_ref))):
        up = jnp.dot(b_ref[...].astype(BF16), w_ref[c], preferred_element_type=F32)
        acc = acc + jax.nn.sigmoid(g_ref[...]) * up
    o_ref[...] = acc.astype(BF16)


def _merge(o_ret, o_nsa, o_mem, z, wb, d_model):
    n, bw = o_ret.shape
    tm = min(n, 1024)
    tn = 512
    gb = C_MG // tn
    nj = d_model // tn
    bspec = pl.BlockSpec((tm, bw), lambda i, j: (i, 0))
    gspec = lambda c: pl.BlockSpec((tm, tn), lambda i, j: (i, gb + c * nj + j))
    return pl.pallas_call(
        _merge_kernel, grid=(n // tm, nj),
        in_specs=[bspec, bspec, bspec, gspec(0), gspec(1), gspec(2),
                  pl.BlockSpec((N_BRANCH, bw, tn), lambda i, j: (0, 0, j))],
        out_specs=pl.BlockSpec((tm, tn), lambda i, j: (i, j)),
        out_shape=jax.ShapeDtypeStruct((n, d_model), BF16),
        compiler_params=_cp("arbitrary", "arbitrary"), name="branch_merge",
    )(o_ret, o_nsa, o_mem, z, z, z, wb)


def _topk_rows(s, kk):
    e, tl = s.shape
    nt = e // 8
    assert e % 8 == 0 and nt & (nt - 1) == 0
    sub = lax.broadcasted_iota(I32, (8, tl), 0).astype(F32)
    vals = [s[8 * k:8 * k + 8] for k in range(nt)]
    idxs = [sub + float(8 * k) for k in range(nt)]
    for a, b in _sorting_network(nt):
        va, vb, ia, ib = vals[a], vals[b], idxs[a], idxs[b]
        swap = jnp.logical_or(vb > va, jnp.logical_and(vb == va, ib < ia))
        vals[a], vals[b] = jnp.maximum(va, vb), jnp.minimum(va, vb)
        idxs[a], idxs[b] = jnp.where(swap, ib, ia), jnp.where(swap, ia, ib)
    out_v, out_i = [], []
    for step in range(kk):
        m = jnp.max(vals[0], axis=0, keepdims=True)
        i = jnp.min(jnp.where(vals[0] == m, idxs[0], float(e)), axis=0, keepdims=True)
        out_v.append(m)
        out_i.append(i)
        win = idxs[0] == i
        depth = min(nt, kk - step)
        for k in range(depth - 1):
            vals[k] = jnp.where(win, vals[k + 1], vals[k])
            idxs[k] = jnp.where(win, idxs[k + 1], idxs[k])
        vals[depth - 1] = jnp.where(win, -jnp.inf, vals[depth - 1])
    return jnp.concatenate(out_v, axis=0), jnp.concatenate(out_i, axis=0)


def _sorting_network(n):
    def merge(lo, hi, r):
        step = r * 2
        if step < hi - lo:
            yield from merge(lo, hi, step)
            yield from merge(lo + r, hi, step)
            yield from [(i, i + r) for i in range(lo + r, hi - r, step)]
        else:
            yield (lo, lo + r)

    def sort(lo, hi):
        if hi - lo >= 1:
            mid = lo + (hi - lo) // 2
            yield from sort(lo, mid)
            yield from sort(mid + 1, hi)
            yield from merge(lo, hi, 1)

    return list(sort(0, n - 1))


def _pair_pieces(kk):
    pieces, cur = [], []

    def flush(rows):
        pieces.append(rows + [None] * (8 - len(rows)))

    for a in range(kk):
        grp = [(a, b) for b in range(kk // (a + 1))]
        if cur and len(cur) + len(grp) > 8:
            flush(cur)
            cur = []
        cur = cur + grp
        while len(cur) >= 8:
            flush(cur[:8])
            cur = cur[8:]
    if cur:
        flush(cur)
    return pieces


def _rows_by_runs(x, ids):
    if ids[0] % 8 == 0 and ids == list(range(ids[0], ids[0] + 8)):
        return x[ids[0]:ids[0] + 8]
    r = lax.broadcasted_iota(I32, (8, 1), 0)
    runs = []
    for p, i in enumerate(ids):
        if not runs or runs[-1][1] != i:
            runs.append((p, i))
    out = jnp.broadcast_to(x[runs[-1][1]:runs[-1][1] + 1], (8, x.shape[1]))
    for (_, i), (nxt, _) in zip(reversed(runs[:-1]), reversed(runs[1:])):
        out = jnp.where(r < nxt, x[i:i + 1], out)
    return out


ROUTE_HEADS_PER_STEP = 4


def _peer_route_kernel(q_ref, sk_ref, i_ref, j_ref, g_ref):
    kk = PEER_TOPK
    for hh in range(ROUTE_HEADS_PER_STEP):
        rows = slice(hh * kk, (hh + 1) * kk)
        i_ref[rows, :], j_ref[rows, :], g_ref[rows, :] = _peer_route_head(
            q_ref[:, hh * PEER_DKEY:(hh + 1) * PEER_DKEY], sk_ref[hh])


def _peer_route_head(q, sk):
    kk = PEER_TOPK
    st = _nt_dot(sk, q.astype(BF16))
    v0, i0 = _topk_rows(st[0:PEER_KEYS], kk)
    v1, i1 = _topk_rows(st[PEER_KEYS:2 * PEER_KEYS], kk)
    cand, ci, cj = [], [], []
    for piece in _pair_pieces(kk):
        ra = [p[0] if p else 0 for p in piece]
        rb = [p[1] if p else 0 for p in piece]
        live = lax.broadcasted_iota(I32, (8, 1), 0) < sum(p is not None for p in piece)
        cand.append(jnp.where(live, _rows_by_runs(v0, ra) + _rows_by_runs(v1, rb), -jnp.inf))
        ci.append(_rows_by_runs(i0, ra))
        cj.append(_rows_by_runs(i1, rb))
    cand, ci, cj = (jnp.concatenate(c, axis=0) for c in (cand, ci, cj))
    ridx = lax.broadcasted_iota(I32, cand.shape, 0).astype(F32)
    sc, si, sj = [], [], []
    for _ in range(kk):
        m = jnp.max(cand, axis=0, keepdims=True)
        r = jnp.min(jnp.where(cand == m, ridx, float(cand.shape[0])), axis=0, keepdims=True)
        hit = ridx == r
        sc.append(m)
        si.append(jnp.max(jnp.where(hit, ci, -1.0), axis=0, keepdims=True))
        sj.append(jnp.max(jnp.where(hit, cj, -1.0), axis=0, keepdims=True))
        cand = jnp.where(hit, -jnp.inf, cand)
    sc = jnp.concatenate(sc, axis=0)
    e = jnp.exp(sc - jnp.max(sc, axis=0, keepdims=True))
    return (jnp.concatenate(si, axis=0).astype(I32), jnp.concatenate(sj, axis=0).astype(I32),
            e / jnp.sum(e, axis=0, keepdims=True))


def _peer_route(q, skbd):
    n = q.shape[0]
    tl = min(n, LANES)
    kk = PEER_TOPK
    slots = PEER_HEADS * kk
    hps = ROUTE_HEADS_PER_STEP
    ospec = pl.BlockSpec((hps * kk, tl), lambda i, h: (h, i))
    return pl.pallas_call(
        _peer_route_kernel, grid=(n // tl, PEER_HEADS // hps),
        in_specs=[pl.BlockSpec((tl, hps * PEER_DKEY), lambda i, h: (i, h)),
                  pl.BlockSpec((hps, 2 * PEER_KEYS, PEER_DKEY), lambda i, h: (h, 0, 0))],
        out_specs=[ospec, ospec, ospec],
        out_shape=[jax.ShapeDtypeStruct((slots, n), I32), jax.ShapeDtypeStruct((slots, n), I32),
                   jax.ShapeDtypeStruct((slots, n), F32)],
        compiler_params=_cp("arbitrary", "arbitrary"), name="peer_route",
    )(q, skbd)


TOKEN_GROUP = 8


def _peer_weights_kernel(i_ref, j_ref, g_ref, o_ref, it_scr, jt_scr, gt_scr):
    it_scr[...] = i_ref[...].T
    jt_scr[...] = j_ref[...].T
    gt_scr[...] = g_ref[...].T
    key = lax.broadcasted_iota(I32, (PEER_KEYS, i_ref.shape[0]), 0)

    def one_hots(n):
        irow = it_scr[pl.ds(n, 1), :]
        jrow = jt_scr[pl.ds(n, 1), :]
        grow = gt_scr[pl.ds(n, 1), :]
        a = jnp.where(key == irow, grow, 0.0).astype(BF16)
        b = jnp.where(key == jrow, 1.0, 0.0).astype(BF16)
        return a, b

    groups_per_trip = 4 if o_ref.shape[0] % 4 == 0 else 1

    def body(nb2, carry):
        results = []
        for gi in range(groups_per_trip):
            nb = nb2 * groups_per_trip + gi
            for t in range(TOKEN_GROUP):
                a, b = one_hots(nb * TOKEN_GROUP + t)
                results.append((nb, t, _nt_dot(a, b)))
        for nb, t, r in results:
            o_ref[nb, pl.ds(t, PEER_KEYS, stride=TOKEN_GROUP), :] = r
        return carry

    lax.fori_loop(0, o_ref.shape[0] // groups_per_trip, body, 0)


def _peer_weights(it, jt, gt):
    slots, n = it.shape
    tb = min(n, LANES)
    assert tb % TOKEN_GROUP == 0
    ispec = pl.BlockSpec((slots, tb), lambda i: (0, i))
    out = pl.pallas_call(
        _peer_weights_kernel, grid=(n // tb,),
        in_specs=[ispec, ispec, ispec],
        out_specs=pl.BlockSpec((tb // TOKEN_GROUP, PEER_KEYS * TOKEN_GROUP, PEER_KEYS), lambda i: (i, 0, 0)),
        out_shape=jax.ShapeDtypeStruct((n // TOKEN_GROUP, PEER_KEYS * TOKEN_GROUP, PEER_KEYS), F32),
        scratch_shapes=[pltpu.VMEM((tb, slots), I32), pltpu.VMEM((tb, slots), I32), pltpu.VMEM((tb, slots), F32)],
        compiler_params=_cp("arbitrary"), name="peer_weights",
    )(it, jt, gt)
    return out.reshape(n // TOKEN_GROUP, PEER_KEYS, TOKEN_GROUP, PEER_KEYS)


def _peer_dense_kernel(h_ref, g_ref, u_ref, v_ref, x_ref, o_ref):
    @pl.when(pl.program_id(1) == 0)
    def _():
        o_ref[...] = x_ref[...]

    a = jnp.dot(h_ref[...], u_ref[...], preferred_element_type=F32)
    act = 0.5 * a * (1.0 + lax.erf(a * (2.0 ** -0.5)))
    tm = h_ref.shape[0]
    w = jnp.concatenate([(g_ref[:, ii].reshape(tm, PEER_KEYS)
                          * act[:, ii * PEER_KEYS:(ii + 1) * PEER_KEYS]).astype(BF16)
                         for ii in range(g_ref.shape[1])], axis=1)
    o_ref[...] += jnp.dot(w, v_ref[...], preferred_element_type=F32)


def _peer_dense(hn, gw, ut, vv, x):
    n, d = hn.shape
    ne = ut.shape[1]
    tm = min(n, 512)
    te = 1024
    return pl.pallas_call(
        _peer_dense_kernel, grid=(n // tm, ne // te),
        in_specs=[pl.BlockSpec((tm, d), lambda i, j: (i, 0)),
                  pl.BlockSpec((tm // TOKEN_GROUP, te // PEER_KEYS, TOKEN_GROUP, PEER_KEYS),
                               lambda i, j: (i, j, 0, 0)),
                  pl.BlockSpec((d, te), lambda i, j: (0, j)),
                  pl.BlockSpec((te, d), lambda i, j: (j, 0)),
                  pl.BlockSpec((tm, d), lambda i, j: (i, 0))],
        out_specs=pl.BlockSpec((tm, d), lambda i, j: (i, 0)),
        out_shape=jax.ShapeDtypeStruct((n, d), F32),
        compiler_params=_cp("arbitrary", "arbitrary"), name="peer_dense",
    )(hn, gw, ut, vv, x)


def _prepare_weights(norm_attn, w_in, cmp_pos_k, cmp_pos_v, cmp_w_k, cmp_w_v, w_mem_kv, w_branch, w_out,
                     peer_wq, peer_subkeys, peer_u, peer_v):
    d = w_in.shape[0]
    ng_w = NSA_HEADS * 3
    o_ng = 5632
    o_mq = o_ng + ng_w
    o_mg = o_mq + MEM_HEADS * MEM_HEAD_DIM
    width = C_MG + N_BRANCH * d + LANES
    width = -(-width // Z_TILE) * Z_TILE
    w_r = jnp.concatenate([w_in[:, 0:4096], w_in[:, o_mq:o_mg], w_in[:, 4096:o_ng], w_in[:, o_mg:],
                           w_in[:, o_ng:o_mq],
                           jnp.zeros((d, width - (C_MG + N_BRANCH * d + ng_w)), w_in.dtype)], axis=1).astype(BF16)
    pos_k, wk_big = _compress_weights(cmp_pos_k, cmp_w_k)
    pos_v, wv_big = _compress_weights(cmp_pos_v, cmp_w_v)
    half = PEER_DKEY // 2
    sk = peer_subkeys.astype(BF16)
    zk = jnp.zeros((PEER_HEADS, PEER_KEYS, half), BF16)
    skbd = jnp.concatenate([jnp.concatenate([sk[:, 0], zk], axis=2),
                            jnp.concatenate([zk, sk[:, 1]], axis=2)], axis=1)
    return dict(w_in=w_r, c_ng=C_MG + N_BRANCH * d, pos_k=pos_k, wk_big=wk_big, pos_v=pos_v, wv_big=wv_big,
                w_mem_kv=w_mem_kv.astype(BF16), w_branch=w_branch.astype(BF16), w_out=w_out.astype(BF16),
                peer_wq=peer_wq.astype(BF16), skbd=skbd, ut=peer_u.astype(BF16).T, vv=peer_v.astype(BF16))


def _layer_common(x, seq, pos, norm_attn, nsa_q_norm, nsa_k_norm, pw):
    z = _prologue_matmul(x, norm_attn, pw["w_in"], mode="norm", name="in_proj", tm=1024)
    nsa_tabs = _rope_tables(pos, ROPE_DIMS, ROPE_THETA, HEAD_DIM)
    half = ROPE_DIMS // 2
    kvw = NSA_KV_HEADS * HEAD_DIM
    qn = _headnorm(z, C_NQ, NSA_HEADS * HEAD_DIM, nsa_q_norm, HEAD_DIM, seq, nsa_tabs, half)
    kcn = _headnorm(z, C_KC, kvw, nsa_k_norm[0], HEAD_DIM, seq, nsa_tabs, half)
    ksn = _headnorm(z, C_KS, kvw, nsa_k_norm[1], HEAD_DIM, seq, nsa_tabs, half)
    kwn = _headnorm(z, C_KW, kvw, nsa_k_norm[2], HEAD_DIM, seq, nsa_tabs, half)
    gates = _nsa_gates(z, pw["c_ng"])
    ret_tabs = _rope_tables(pos, RET_DK, RET_THETA, RET_DK)
    return z, qn, kcn, ksn, kwn, gates, ret_tabs


def _layer_tail(x, z, o_ret, o_nsa, o_mem, norm_ffn, pw):
    d = x.shape[1]
    merged = _merge(o_ret, o_nsa, o_mem, z, pw["w_branch"], d)
    x1 = _prologue_matmul(merged, jnp.ones((d,), F32), pw["w_out"], mode="none", residual=x, name="out_proj")
    q, hn = _prologue_matmul(x1, norm_ffn, pw["peer_wq"], mode="norm", emit_h=True, name="peer_query")
    it, jt, gt = _peer_route(q, pw["skbd"])
    return _peer_dense(hn, _peer_weights(it, jt, gt), pw["ut"], pw["vv"], x1)


def kernel(x_prompt, x_sample, cache_cmp_k, cache_cmp_v, cache_sel_k, cache_sel_v, cache_win_k, cache_win_v,
           state_ret, cache_mem_k, cache_mem_v, page_table, mem_prompt, norm_attn, w_in, ret_gn, nsa_q_norm,
           nsa_k_norm, cmp_pos_k, cmp_pos_v, cmp_w_k, cmp_w_v, norm_mem, w_mem_kv, mem_q_norm, mem_k_norm,
           w_branch, w_out, norm_ffn, peer_wq, peer_subkeys, peer_u, peer_v):
    depth = w_in.shape[0]
    assert depth == 1
    l = 0
    bp, t, d = x_prompt.shape
    bs, ts, _ = x_sample.shape
    assert ts == 1
    n_pool, page, g_kv, dh = cache_cmp_k.shape[1:]
    n_pages = page_table.shape[1]
    past = n_pages * page
    kvw = g_kv * dh
    memw = MEM_HEADS * MEM_HEAD_DIM
    pw = _prepare_weights(norm_attn[l], w_in[l], cmp_pos_k[l], cmp_pos_v[l], cmp_w_k[l], cmp_w_v[l], w_mem_kv[l],
                          w_branch[l], w_out[l], peer_wq[l], peer_subkeys[l], peer_u[l], peer_v[l])
    cmp_cols = CMP_BLOCK * kvw

    n_p = bp * t
    xp = x_prompt.reshape(n_p, d)
    m_tok = mem_prompt.shape[1]
    mkv = _prologue_matmul(mem_prompt.reshape(bp * m_tok, d), norm_mem[l], pw["w_mem_kv"], mode="norm",
                           name="mem_kv_proj")
    mk_p = _headnorm(mkv, 0, memw, mem_k_norm[l], MEM_HEAD_DIM, m_tok)
    mv_p = mkv[:, memw:]
    z, qn, kcn, ksn, kwn, gates, ret_tabs = _layer_common(xp, t, jnp.arange(t), norm_attn[l], nsa_q_norm[l],
                                                         nsa_k_norm[l], pw)
    z3 = z.reshape(bp, t, z.shape[1])
    o_ret, st_p = _retention_prompt(z3, ret_tabs, jnp.zeros((bp, RET_HEADS, RET_DK, RET_DV), F32), ret_gn[l])
    vc_p = z[:, C_VC:C_VC + kvw]
    nbc = t // CMP_BLOCK
    ck_p = _prologue_matmul(kcn.reshape(bp * nbc, cmp_cols), pw["pos_k"], pw["wk_big"], mode="bias", tm=256,
                            name="compress_k")
    cv_p = _prologue_matmul(vc_p.reshape(bp * nbc, cmp_cols), pw["pos_v"], pw["wv_big"], mode="bias", tm=256,
                            name="compress_v")
    pair = CMP_PER_SEL * kvw
    o_nsa = _nsa_prompt(qn.reshape(bp, t, -1), ck_p.reshape(bp, nbc // CMP_PER_SEL, pair),
                        cv_p.reshape(bp, nbc // CMP_PER_SEL, pair),
                        ksn.reshape(bp, t, kvw), z3, kwn.reshape(bp, t, kvw), gates.reshape(bp, t, LANES))
    o_mem = _mem_attention(z3, mem_q_norm[l], mk_p.reshape(bp, m_tok, memw), mv_p.reshape(bp, m_tok, memw))
    y_p = _layer_tail(xp, z, o_ret.reshape(n_p, -1), o_nsa.reshape(n_p, -1), o_mem.reshape(n_p, -1), norm_ffn[l], pw)

    nbuf_p = min(WINDOW, t)
    kv5 = lambda a, bb, tt: a.reshape(1, bb, tt, g_kv, dh)
    outs_p = (kv5(kcn, bp, t), kv5(vc_p, bp, t), kv5(ksn, bp, t), kv5(z[:, C_VS:C_VS + kvw], bp, t),
              kv5(kwn, bp, t)[:, :, t - nbuf_p:], kv5(z[:, C_VW:C_VW + kvw], bp, t)[:, :, t - nbuf_p:],
              st_p[None], mk_p.reshape(1, bp, m_tok, MEM_HEADS, MEM_HEAD_DIM),
              mv_p.reshape(1, bp, m_tok, MEM_HEADS, MEM_HEAD_DIM))

    xs = x_sample.reshape(bs, d)
    zs, qn_s, kcn_s, ksn_s, kwn_s, gates_s, ret_tabs_s = _layer_common(
        xs, 1, jnp.full((1,), past, I32), norm_attn[l], nsa_q_norm[l], nsa_k_norm[l], pw)
    zs3 = zs.reshape(bs, 1, zs.shape[1])
    o_ret_s, st_s = _retention_decode(zs3, ret_tabs_s, state_ret[l], ret_gn[l])
    per_page = page // CMP_BLOCK
    ckp = _compress_pool(cache_cmp_k, cmp_pos_k[l], cmp_w_k[l], "compress_pool_k")
    cvp = _compress_pool(cache_cmp_v, cmp_pos_v[l], cmp_w_v[l], "compress_pool_v")
    pages_of = lambda a: a[:, page_table].transpose(1, 2, 0, 3).reshape(bs, n_pages * per_page // CMP_PER_SEL, pair)
    ck_s, cv_s = pages_of(ckp), pages_of(cvp)
    qn_s3 = qn_s.reshape(bs, 1, -1)
    ocmp_s, idx_s = _nsa_decode_select(qn_s3, ck_s, cv_s, past)
    nbs = past // SEL_BLOCK + 1
    nsel = min(SEL_TOPK, nbs)
    per_page_sel = page // SEL_BLOCK
    picks = jnp.clip(idx_s[:, 0, :NSA_KV_HEADS * SEL_TOPK].reshape(bs, NSA_KV_HEADS, SEL_TOPK)[:, :, :nsel],
                     0, nbs - 2)
    pool_rows = (jnp.take_along_axis(page_table, (picks // per_page_sel).reshape(bs, -1), axis=1) * per_page_sel
                 + (picks % per_page_sel).reshape(bs, -1)).reshape(-1).astype(I32)
    o_nsa_s = _nsa_decode_attend(
        pool_rows, cache_sel_k, cache_sel_v, qn_s3, idx_s, ksn_s.reshape(bs, 1, kvw), zs3, cache_win_k, cache_win_v,
        kwn_s.reshape(bs, 1, kvw), ocmp_s, gates_s.reshape(bs, 1, LANES), past, nsel)
    o_mem_s = _mem_attention(zs3, mem_q_norm[l], cache_mem_k, cache_mem_v, cache_layout=True)
    y_s = _layer_tail(xs, zs, o_ret_s.reshape(bs, -1), o_nsa_s.reshape(bs, -1), o_mem_s.reshape(bs, -1),
                      norm_ffn[l], pw)

    nbuf_s = cache_win_k.shape[2]
    win_k_s = jnp.concatenate([cache_win_k[l], kv5(kwn_s, bs, 1)[0]], axis=1)[None, :, -nbuf_s:]
    win_v_s = jnp.concatenate([cache_win_v[l], kv5(zs[:, C_VW:C_VW + kvw], bs, 1)[0]], axis=1)[None, :, -nbuf_s:]
    outs_s = (kv5(kcn_s, bs, 1), kv5(zs[:, C_VC:C_VC + kvw], bs, 1), kv5(ksn_s, bs, 1),
              kv5(zs[:, C_VS:C_VS + kvw], bs, 1), win_k_s, win_v_s, st_s[None].astype(x_sample.dtype))
    return (y_p.reshape(bp, t, d), y_s.reshape(bs, ts, d)) + outs_p + outs_s
```

```python
import functools

import numpy as np
import jax
import jax.numpy as jnp
from jax import lax
from jax.experimental import pallas as pl
from jax.experimental.pallas import tpu as pltpu

F32 = jnp.float32
BF16 = jnp.bfloat16
I32 = jnp.int32

EPS = 1e-6
NEG_INF = -1e30
FORCE = 1e4

RET_HEADS, RET_DK, RET_DV, RET_CHUNK, RET_THETA = 4, 128, 256, 128, 10000.0
NSA_HEADS, NSA_KV_HEADS, HEAD_DIM = 8, 2, 128
HEADS_PER_GROUP = NSA_HEADS // NSA_KV_HEADS
CMP_BLOCK, SEL_BLOCK, SEL_TOPK, WINDOW, Q_BLOCK = 32, 64, 4, 256, 128
CMP_PER_SEL = SEL_BLOCK // CMP_BLOCK
ROPE_THETA, ROPE_DIMS = 500000.0, HEAD_DIM // 4
MEM_HEADS, MEM_HEAD_DIM = 4, 256
PEER_HEADS, PEER_KEYS, PEER_DKEY, PEER_TOPK = 8, 128, 128, 16
N_BRANCH = 3
LANES = 128

C_RQ, C_RK, C_RV, C_RG, C_NQ, C_MQ = 0, 512, 1024, 2048, 3072, 4096
C_KC, C_VC, C_KS, C_VS, C_KW, C_VW = 5120, 5376, 5632, 5888, 6144, 6400
C_MG = 6656
Z_TILE = 512


def _cp(*sem):
    return pltpu.CompilerParams(dimension_semantics=sem, vmem_limit_bytes=56 * 1024 * 1024)


def _nt_dot(a, b):
    return lax.dot_general(a, b, (((1,), (1,)), ((), ())), preferred_element_type=F32)


def _tn_dot(a, b):
    return lax.dot_general(a, b, (((0,), (0,)), ((), ())), preferred_element_type=F32)


def _pm_kernel(*refs, mode, residual, emit_h):
    x_ref, g_ref, w_ref = refs[:3]
    k = 3
    r_ref = None
    if residual:
        r_ref = refs[k]
        k += 1
    o_ref = refs[k]
    k += 1
    h_ref = None
    if emit_h:
        h_ref = refs[k]
        k += 1
    h_scr = refs[k]

    @pl.when(pl.program_id(1) == 0)
    def _():
        x = x_ref[...].astype(F32)
        if mode == "norm":
            h = x * lax.rsqrt(jnp.mean(x * x, axis=-1, keepdims=True) + EPS) * g_ref[...]
        elif mode == "bias":
            h = x + g_ref[...]
        else:
            h = x
        h_scr[...] = h.astype(BF16)
        if emit_h:
            h_ref[...] = h_scr[...]

    acc = jnp.dot(h_scr[...], w_ref[...], preferred_element_type=F32)
    if residual:
        acc = acc + r_ref[...]
    o_ref[...] = acc


def _prologue_matmul(x, g, w, *, mode, name, residual=None, emit_h=False, tm=512):
    n, kdim = x.shape
    wout = w.shape[1]
    tm = min(tm, n)
    tn = next(c for c in (1024, 512, 256, 128) if wout % c == 0)
    assert n % tm == 0 and wout % tn == 0
    in_specs = [pl.BlockSpec((tm, kdim), lambda i, j: (i, 0)),
                pl.BlockSpec((1, kdim), lambda i, j: (0, 0)),
                pl.BlockSpec((kdim, tn), lambda i, j: (0, j))]
    args = [x, g.reshape(1, kdim).astype(F32), w]
    if residual is not None:
        in_specs.append(pl.BlockSpec((tm, tn), lambda i, j: (i, j)))
        args.append(residual)
    out_shape = [jax.ShapeDtypeStruct((n, wout), F32)]
    out_specs = [pl.BlockSpec((tm, tn), lambda i, j: (i, j))]
    if emit_h:
        out_shape.append(jax.ShapeDtypeStruct((n, kdim), BF16))
        out_specs.append(pl.BlockSpec((tm, kdim), lambda i, j: (i, 0)))
    res = pl.pallas_call(
        functools.partial(_pm_kernel, mode=mode, residual=residual is not None, emit_h=emit_h),
        grid=(n // tm, wout // tn),
        in_specs=in_specs, out_specs=out_specs, out_shape=out_shape,
        scratch_shapes=[pltpu.VMEM((tm, kdim), BF16)],
        compiler_params=_cp("arbitrary", "arbitrary"), name=name,
    )(*args)
    return res if emit_h else res[0]


def _rope_tables(pos, rot_dims, theta, dh):
    half = rot_dims // 2
    inv = theta ** (-jnp.arange(half, dtype=F32) * 2.0 / rot_dims)
    ang = pos.astype(F32)[:, None] * inv[None, :]
    cos, sin = jnp.cos(ang), jnp.sin(ang)
    t = pos.shape[0]
    one = jnp.ones((t, dh - rot_dims), F32)
    zh = jnp.zeros((t, half), F32)
    zr = jnp.zeros((t, dh - rot_dims), F32)
    c = jnp.concatenate([cos, cos, one], axis=1)
    s1 = jnp.concatenate([-sin, zh, zr], axis=1)
    s2 = jnp.concatenate([zh, sin, zr], axis=1)
    return c, s1, s2


def _rope(y, c, s1, s2, half):
    dh = y.shape[-1]
    return y * c + pltpu.roll(y, dh - half, 1) * s1 + pltpu.roll(y, half, 1) * s2


def _headnorm_kernel(*refs, nh, dh, rope_half):
    if rope_half:
        x_ref, g_ref, c_ref, s1_ref, s2_ref, o_ref = refs
    else:
        x_ref, g_ref, o_ref = refs
    for h in range(nh):
        x = x_ref[:, h * dh:(h + 1) * dh]
        y = x * lax.rsqrt(jnp.mean(x * x, axis=-1, keepdims=True) + EPS) * g_ref[...]
        if rope_half:
            y = _rope(y, c_ref[...], s1_ref[...], s2_ref[...], rope_half)
        o_ref[:, h * dh:(h + 1) * dh] = y


def _headnorm(x, col, width, gain, dh, seq, tables=None, rope_half=0):
    n = x.shape[0]
    tm = min(512, seq) if seq > 1 else n
    assert n % tm == 0 and col % width == 0
    cb = col // width
    in_specs = [pl.BlockSpec((tm, width), lambda i: (i, cb)),
                pl.BlockSpec((1, dh), lambda i: (0, 0))]
    args = [x, gain.reshape(1, dh).astype(F32)]
    if rope_half:
        if seq > 1:
            nb = seq // tm
            tspec = pl.BlockSpec((tm, dh), lambda i: (i % nb, 0))
        else:
            tspec = pl.BlockSpec((1, dh), lambda i: (0, 0))
        in_specs += [tspec] * 3
        args += list(tables)
    return pl.pallas_call(
        functools.partial(_headnorm_kernel, nh=width // dh, dh=dh, rope_half=rope_half),
        grid=(n // tm,),
        in_specs=in_specs,
        out_specs=pl.BlockSpec((tm, width), lambda i: (i, 0)),
        out_shape=jax.ShapeDtypeStruct((n, width), F32),
        compiler_params=_cp("arbitrary"), name="head_norm",
    )(*args)


def _log_gamma():
    return np.log1p(-(np.float32(2.0) ** (-5.0 - np.arange(RET_HEADS, dtype=np.float32)))).astype(np.float32)


def _ret_finish(o, gn, gate):
    y = o * lax.rsqrt(jnp.mean(o * o, axis=-1, keepdims=True) + EPS) * gn
    return y * (gate * jax.nn.sigmoid(gate))


def _ret_prompt_kernel(q_ref, k_ref, v_ref, gt_ref, c_ref, s1_ref, s2_ref, st0_ref, gn_ref, o_ref, st_ref):
    cl = q_ref.shape[0]

    @pl.when(pl.program_id(1) == 0)
    def _():
        st_ref[...] = st0_ref[...]

    half = RET_DK // 2
    ri = lax.broadcasted_iota(I32, (cl, cl), 0).astype(F32)
    ci = lax.broadcasted_iota(I32, (cl, cl), 1).astype(F32)
    rel = ri - ci
    rowi = lax.broadcasted_iota(I32, (cl, 1), 0).astype(F32)
    for h, lg in enumerate(_log_gamma().tolist()):
        ks, vs = slice(h * RET_DK, (h + 1) * RET_DK), slice(h * RET_DV, (h + 1) * RET_DV)
        q = _rope(q_ref[:, ks], c_ref[...], s1_ref[...], s2_ref[...], half)
        k = _rope(k_ref[:, ks], c_ref[...], s1_ref[...], s2_ref[...], half) * (RET_DK ** -0.5)
        decay = jnp.where(rel >= 0, jnp.exp(jnp.maximum(rel, 0.0) * lg), 0.0)
        qb, kb, vb = q.astype(BF16), k.astype(BF16), v_ref[:, vs].astype(BF16)
        inner = _nt_dot(qb, kb) * decay
        state = st_ref[h]
        cross = jnp.exp((rowi + 1.0) * lg)
        o = jnp.dot(inner.astype(BF16), vb, preferred_element_type=F32)
        o = o + jnp.dot(qb, state.astype(BF16), preferred_element_type=F32) * cross
        k_dec = (k * jnp.exp((cl - 1.0 - rowi) * lg)).astype(BF16)
        st_ref[h] = float(np.exp(np.float32(cl) * np.float32(lg))) * state + _tn_dot(k_dec, vb)
        o_ref[:, vs] = _ret_finish(o, gn_ref[h], gt_ref[:, vs]).astype(o_ref.dtype)


def _retention_prompt(z3, tables, state0, gn):
    b, t, _ = z3.shape
    cl = RET_CHUNK
    assert t % cl == 0
    wq, wv = RET_HEADS * RET_DK, RET_HEADS * RET_DV
    tspec = pl.BlockSpec((cl, RET_DK), lambda bi, c: (c, 0))
    st_spec = pl.BlockSpec((None, RET_HEADS, RET_DK, RET_DV), lambda bi, c: (bi, 0, 0, 0))
    o, st = pl.pallas_call(
        _ret_prompt_kernel,
        grid=(b, t // cl),
        in_specs=[pl.BlockSpec((None, cl, wq), lambda bi, c: (bi, c, C_RQ // wq)),
                  pl.BlockSpec((None, cl, wq), lambda bi, c: (bi, c, C_RK // wq)),
                  pl.BlockSpec((None, cl, wv), lambda bi, c: (bi, c, C_RV // wv)),
                  pl.BlockSpec((None, cl, wv), lambda bi, c: (bi, c, C_RG // wv)),
                  tspec, tspec, tspec, st_spec,
                  pl.BlockSpec((RET_HEADS, 1, RET_DV), lambda bi, c: (0, 0, 0))],
        out_specs=[pl.BlockSpec((None, cl, wv), lambda bi, c: (bi, c, 0)), st_spec],
        out_shape=[jax.ShapeDtypeStruct((b, t, wv), BF16),
                   jax.ShapeDtypeStruct((b, RET_HEADS, RET_DK, RET_DV), F32)],
        compiler_params=_cp("arbitrary", "arbitrary"), name="retention_prompt",
    )(z3, z3, z3, z3, *tables, state0, gn.reshape(RET_HEADS, 1, RET_DV))
    return o, st


def _ret_decode_kernel(q_ref, k_ref, v_ref, gt_ref, c_ref, s1_ref, s2_ref, st0_ref, gn_ref, o_ref, st_ref):
    lgs = _log_gamma()
    half = RET_DK // 2
    row0 = lax.broadcasted_iota(I32, (16, 1), 0) == 0
    for s, h in [(s, h) for s in range(q_ref.shape[0]) for h in range(RET_HEADS)]:
        gamma = float(np.exp(lgs[h]))
        ks, vs = slice(h * RET_DK, (h + 1) * RET_DK), slice(h * RET_DV, (h + 1) * RET_DV)
        q = _rope(q_ref[s, :, ks], c_ref[...], s1_ref[...], s2_ref[...], half)
        k = _rope(k_ref[s, :, ks], c_ref[...], s1_ref[...], s2_ref[...], half) * (RET_DK ** -0.5)
        v = v_ref[s, :, vs]
        state = st0_ref[s, h].astype(F32)
        inner = jnp.sum(q * k, axis=-1, keepdims=True)
        q16 = jnp.broadcast_to(q, (16, RET_DK)).astype(BF16)
        cross = jnp.dot(q16, state.astype(BF16), preferred_element_type=F32)[0:1]
        o = inner * v + cross * gamma
        k16 = jnp.where(row0, jnp.broadcast_to(k, (16, RET_DK)), 0.0).astype(BF16)
        v16 = jnp.broadcast_to(v, (16, RET_DV)).astype(BF16)
        st_ref[s, h] = gamma * state + _tn_dot(k16, v16)
        o_ref[s, :, vs] = _ret_finish(o, gn_ref[h], gt_ref[s, :, vs]).astype(o_ref.dtype)


def _retention_decode(z3, tables, state0, gn):
    b = z3.shape[0]
    wq, wv = RET_HEADS * RET_DK, RET_HEADS * RET_DV
    tspec = pl.BlockSpec((1, RET_DK), lambda bi: (0, 0))
    sb = 4 if b % 4 == 0 else 1
    st_spec = pl.BlockSpec((sb, RET_HEADS, RET_DK, RET_DV), lambda bi: (bi, 0, 0, 0))
    o, st = pl.pallas_call(
        _ret_decode_kernel,
        grid=(b // sb,),
        in_specs=[pl.BlockSpec((sb, 1, wq), lambda bi: (bi, 0, C_RQ // wq)),
                  pl.BlockSpec((sb, 1, wq), lambda bi: (bi, 0, C_RK // wq)),
                  pl.BlockSpec((sb, 1, wv), lambda bi: (bi, 0, C_RV // wv)),
                  pl.BlockSpec((sb, 1, wv), lambda bi: (bi, 0, C_RG // wv)),
                  tspec, tspec, tspec, st_spec,
                  pl.BlockSpec((RET_HEADS, 1, RET_DV), lambda bi: (0, 0, 0))],
        out_specs=[pl.BlockSpec((sb, 1, wv), lambda bi: (bi, 0, 0)), st_spec],
        out_shape=[jax.ShapeDtypeStruct((b, 1, wv), BF16),
                   jax.ShapeDtypeStruct((b, RET_HEADS, RET_DK, RET_DV), F32)],
        compiler_params=_cp("arbitrary"), name="retention_decode",
    )(z3, z3, z3, z3, *tables, state0, gn.reshape(RET_HEADS, 1, RET_DV))
    return o, st


def _compress_weights(pos_emb, w):
    g, dh = NSA_KV_HEADS, HEAD_DIM
    eye = jnp.eye(g, dtype=w.dtype)
    wbig = jnp.einsum("lde,gh->lgdhe", w, eye).reshape(CMP_BLOCK * g * dh, g * dh)
    pos_row = jnp.broadcast_to(pos_emb[:, None, :], (CMP_BLOCK, g, dh)).reshape(1, CMP_BLOCK * g * dh)
    return pos_row, wbig.astype(BF16)


def _kv_rows(a):
    return a.reshape(-1, a.shape[-1])


def _compress_pool_kernel(x_ref, pos_ref, w_ref, o_ref, *, page, groups):
    dh = x_ref.shape[1]
    p = x_ref.shape[0] // (page * groups)
    per_page = page // CMP_BLOCK
    for g in range(groups):
        acc = jnp.zeros((per_page * p, dh), F32)
        for l in range(CMP_BLOCK):
            xl = jnp.concatenate([x_ref[pl.ds((n * CMP_BLOCK + l) * groups + g, p, stride=page * groups), :]
                                  for n in range(per_page)], axis=0)
            xl = (xl + pos_ref[l:l + 1, :]).astype(BF16)
            acc = acc + jnp.dot(xl, w_ref[l], preferred_element_type=F32)
        o_ref[:, :, g * dh:(g + 1) * dh] = acc.reshape(per_page, p, dh)


def _compress_pool(pool, pos_emb, w, name):
    _, n_pool, page, groups, dh = pool.shape
    per_page = page // CMP_BLOCK
    p = max(c for c in range(8, 65, 8) if n_pool % c == 0)
    return pl.pallas_call(
        functools.partial(_compress_pool_kernel, page=page, groups=groups), grid=(n_pool // p,),
        in_specs=[pl.BlockSpec((p * page * groups, dh), lambda i: (i, 0)),
                  pl.BlockSpec((CMP_BLOCK, dh), lambda i: (0, 0)),
                  pl.BlockSpec((CMP_BLOCK, dh, dh), lambda i: (0, 0, 0))],
        out_specs=pl.BlockSpec((per_page, p, groups * dh), lambda i: (0, i, 0)),
        out_shape=jax.ShapeDtypeStruct((per_page, n_pool, groups * dh), F32),
        compiler_params=_cp("arbitrary"), name=name,
    )(_kv_rows(pool), pos_emb, w.astype(BF16))


def _masked_softmax_parts(parts, masks):
    sm = [jnp.where(m, s, NEG_INF) for s, m in zip(parts, masks)]
    mx = functools.reduce(jnp.maximum, [jnp.max(s, axis=-1, keepdims=True) for s in sm])
    ex = [jnp.exp(s - mx) for s in sm]
    den = functools.reduce(lambda a, b2: a + b2, [jnp.sum(e, axis=-1, keepdims=True) for e in ex])
    inv = 1.0 / den
    return [jnp.where(m, e * inv, 0.0) for e, m in zip(ex, masks)]


def _select_blocks(score, nsel):
    nbs = score.shape[-1]
    lane = lax.broadcasted_iota(I32, score.shape, 1)
    sel = jnp.zeros(score.shape, jnp.bool_)
    picks = []
    for _ in range(nsel):
        m = jnp.max(score, axis=-1, keepdims=True)
        idx = jnp.min(jnp.where(score == m, lane, nbs), axis=-1, keepdims=True)
        hit = lane == idx
        sel = jnp.logical_or(sel, hit)
        score = jnp.where(hit, -jnp.inf, score)
        picks.append(idx)
    return sel, picks


def _nsa_prompt_kernel(q_ref, ck_ref, cv_ref, ks_ref, vs_ref, kw_ref, vw_ref, gt_ref, o_ref, osel_ref):
    i = pl.program_id(1)
    qb = q_ref.shape[0]
    t = ks_ref.shape[0]
    nbs = t // SEL_BLOCK
    hg, dh = HEADS_PER_GROUP, HEAD_DIM
    kvw = NSA_KV_HEADS * dh
    scale = dh ** -0.5
    pos1 = i * qb + lax.broadcasted_iota(I32, (qb, 1), 0)
    pos = jnp.concatenate([pos1] * hg, axis=0)
    blk = lax.broadcasted_iota(I32, (1, nbs), 1)
    vis_e = blk * SEL_BLOCK + (CMP_BLOCK - 1) <= pos
    vis_o = blk * SEL_BLOCK + (SEL_BLOCK - 1) <= pos
    valid = blk * SEL_BLOCK <= pos1
    forced = jnp.logical_or(blk == 0, blk == pos1 // SEL_BLOCK)
    n_case = 4 if t % (4 * qb) == 0 else 1
    span = t // n_case
    case = (i * qb) // span

    def selected_attend(limit, q, sel_f, cs):
        key = lax.broadcasted_iota(I32, (1, limit), 1)
        expand = (lax.broadcasted_iota(I32, (nbs, limit), 1) // SEL_BLOCK
                  == lax.broadcasted_iota(I32, (nbs, limit), 0)).astype(BF16)
        selk = jnp.dot(sel_f, expand, preferred_element_type=F32)
        smask = jnp.logical_and(jnp.concatenate([selk] * hg, axis=0) > 0.5, key <= pos)
        (p_s,) = _masked_softmax_parts([_nt_dot(q, ks_ref[0:limit, cs].astype(BF16)) * scale], [smask])
        osel_ref[...] = jnp.dot(p_s.astype(BF16), vs_ref[0:limit, cs].astype(BF16), preferred_element_type=F32)

    wlen = WINDOW + qb
    wstart = pl.multiple_of(jnp.maximum(i * qb - WINDOW, 0), qb)
    wkey = wstart + lax.broadcasted_iota(I32, (1, wlen), 1)
    wdist = pos - wkey
    wmask = jnp.logical_and(wdist >= 0, wdist <= WINDOW)
    gates = gt_ref[...]

    for g in range(NSA_KV_HEADS):
        cs = slice(g * dh, (g + 1) * dh)
        q = jnp.concatenate([q_ref[:, (g * hg + hh) * dh:(g * hg + hh + 1) * dh] for hh in range(hg)],
                            axis=0).astype(BF16)
        co = slice(kvw + g * dh, kvw + (g + 1) * dh)
        ck_e, ck_o = ck_ref[:, cs].astype(BF16), ck_ref[:, co].astype(BF16)
        cv_e, cv_o = cv_ref[:, cs].astype(BF16), cv_ref[:, co].astype(BF16)
        p_e, p_o = _masked_softmax_parts([_nt_dot(q, ck_e) * scale, _nt_dot(q, ck_o) * scale], [vis_e, vis_o])
        o_cmp = (jnp.dot(p_e.astype(BF16), cv_e, preferred_element_type=F32)
                 + jnp.dot(p_o.astype(BF16), cv_o, preferred_element_type=F32))
        psum = p_e + p_o
        imp = functools.reduce(lambda a, b2: a + b2, [psum[hh * qb:(hh + 1) * qb] for hh in range(hg)])
        score = jnp.where(forced, FORCE, jnp.where(valid, imp, -FORCE))
        sel, _ = _select_blocks(score, min(SEL_TOPK, nbs))
        sel_f = jnp.where(sel, 1.0, 0.0).astype(BF16)
        for c in range(n_case):
            pl.when(case == c)(functools.partial(selected_attend, (c + 1) * span, q, sel_f, cs))
        o_sel = osel_ref[...]
        kwin = kw_ref[pl.ds(wstart, wlen), cs].astype(BF16)
        vwin = vw_ref[pl.ds(wstart, wlen), cs].astype(BF16)
        (p_w,) = _masked_softmax_parts([_nt_dot(q, kwin) * scale], [wmask])
        o_win = jnp.dot(p_w.astype(BF16), vwin, preferred_element_type=F32)
        for hh in range(hg):
            hd = g * hg + hh
            rs = slice(hh * qb, (hh + 1) * qb)
            o_ref[:, hd * dh:(hd + 1) * dh] = (gates[:, 3 * hd:3 * hd + 1] * o_cmp[rs]
                                                 + gates[:, 3 * hd + 1:3 * hd + 2] * o_sel[rs]
                                                 + gates[:, 3 * hd + 2:3 * hd + 3] * o_win[rs]).astype(o_ref.dtype)


def _nsa_prompt(qn3, ck3, cv3, ksn3, z3, kwn3, gates3):
    b, t, _ = qn3.shape
    qb = Q_BLOCK
    kvw = NSA_KV_HEADS * HEAD_DIM
    assert t % qb == 0 and t >= WINDOW + qb and t % (2 * SEL_BLOCK) == 0
    nbs = ck3.shape[1]
    full = lambda cb: pl.BlockSpec((None, t, kvw), lambda bi, i: (bi, 0, cb))
    return pl.pallas_call(
        _nsa_prompt_kernel,
        grid=(b, t // qb),
        in_specs=[pl.BlockSpec((None, qb, NSA_HEADS * HEAD_DIM), lambda bi, i: (bi, i, 0)),
                  pl.BlockSpec((None, nbs, CMP_PER_SEL * kvw), lambda bi, i: (bi, 0, 0)),
                  pl.BlockSpec((None, nbs, CMP_PER_SEL * kvw), lambda bi, i: (bi, 0, 0)),
                  full(0), full(C_VS // kvw), full(0), full(C_VW // kvw),
                  pl.BlockSpec((None, qb, LANES), lambda bi, i: (bi, i, 0))],
        out_specs=pl.BlockSpec((None, qb, NSA_HEADS * HEAD_DIM), lambda bi, i: (bi, i, 0)),
        out_shape=jax.ShapeDtypeStruct((b, t, NSA_HEADS * HEAD_DIM), BF16),
        scratch_shapes=[pltpu.VMEM((HEADS_PER_GROUP * qb, HEAD_DIM), F32)],
        compiler_params=_cp("arbitrary", "arbitrary"), name="nsa_prompt",
    )(qn3, ck3, cv3, ksn3, z3, kwn3, z3, gates3)


def _decode_q16(q_ref):
    dh = HEAD_DIM
    rows = [q_ref[:, h * dh:(h + 1) * dh] for h in range(NSA_HEADS)]
    return jnp.concatenate(rows + [jnp.zeros((16 - NSA_HEADS, dh), F32)], axis=0).astype(BF16)


def _nsa_decode_select_kernel(q_ref, ck_ref, cv_ref, ocmp_ref, idx_ref, *, past):
    for s in range(q_ref.shape[0]):
        _nsa_decode_select_one(q_ref.at[s], ck_ref.at[s], cv_ref.at[s], ocmp_ref.at[s], idx_ref.at[s], past)


def _nsa_decode_select_one(q_ref, ck_ref, cv_ref, ocmp_ref, idx_ref, past):
    dh, hg = HEAD_DIM, HEADS_PER_GROUP
    kvw = NSA_KV_HEADS * dh
    nhalf = ck_ref.shape[0]
    nbs = past // SEL_BLOCK + 1
    scale = dh ** -0.5
    q16 = _decode_q16(q_ref)
    row = lax.broadcasted_iota(I32, (16, 1), 0)
    blk = lax.broadcasted_iota(I32, (1, LANES), 1)
    blk_h = lax.broadcasted_iota(I32, (1, nhalf), 1)
    vis_e = blk_h * SEL_BLOCK + (CMP_BLOCK - 1) <= past
    vis_o = blk_h * SEL_BLOCK + (SEL_BLOCK - 1) <= past
    o_all = jnp.zeros((16, dh), F32)
    idx_row = jnp.zeros((1, LANES), I32)
    for g in range(NSA_KV_HEADS):
        cs = slice(g * dh, (g + 1) * dh)
        co = slice(kvw + g * dh, kvw + (g + 1) * dh)
        ck_e, ck_o = ck_ref[:, cs].astype(BF16), ck_ref[:, co].astype(BF16)
        cv_e, cv_o = cv_ref[:, cs].astype(BF16), cv_ref[:, co].astype(BF16)
        p_e, p_o = _masked_softmax_parts([_nt_dot(q16, ck_e) * scale, _nt_dot(q16, ck_o) * scale], [vis_e, vis_o])
        o_g = (jnp.dot(p_e.astype(BF16), cv_e, preferred_element_type=F32)
               + jnp.dot(p_o.astype(BF16), cv_o, preferred_element_type=F32))
        in_g = jnp.logical_and(row >= g * hg, row < (g + 1) * hg)
        o_all = jnp.where(in_g, o_g, o_all)
        imp = jnp.sum(jnp.where(in_g, p_e + p_o, 0.0), axis=0, keepdims=True)
        score = jnp.where(blk_h == 0, FORCE, imp)
        nsel = min(SEL_TOPK, nbs)
        _, past_picks = _select_blocks(score, min(nsel - 1, nhalf))
        picks = past_picks[:1] + [jnp.full((1, 1), nhalf, I32)] + past_picks[1:]
        for kk, p in enumerate(picks):
            idx_row = jnp.where(blk == g * SEL_TOPK + kk, p, idx_row)
        for kk in range(len(picks), SEL_TOPK):
            idx_row = jnp.where(blk == g * SEL_TOPK + kk, -1, idx_row)
    ocmp_ref[...] = o_all
    idx_ref[...] = idx_row


def _nsa_decode_select(qn3, ck3, cv3, past):
    b = qn3.shape[0]
    nhalf = ck3.shape[1]
    kvw = NSA_KV_HEADS * HEAD_DIM
    assert nhalf % 8 == 0 and nhalf == past // SEL_BLOCK
    sb = 8 if b % 8 == 0 else 1
    return pl.pallas_call(
        functools.partial(_nsa_decode_select_kernel, past=past),
        grid=(b // sb,),
        in_specs=[pl.BlockSpec((sb, 1, NSA_HEADS * HEAD_DIM), lambda bi: (bi, 0, 0)),
                  pl.BlockSpec((sb, nhalf, CMP_PER_SEL * kvw), lambda bi: (bi, 0, 0)),
                  pl.BlockSpec((sb, nhalf, CMP_PER_SEL * kvw), lambda bi: (bi, 0, 0))],
        out_specs=[pl.BlockSpec((sb, 16, HEAD_DIM), lambda bi: (bi, 0, 0)),
                   pl.BlockSpec((sb, 1, LANES), lambda bi: (bi, 0, 0))],
        out_shape=[jax.ShapeDtypeStruct((b, 16, HEAD_DIM), F32),
                   jax.ShapeDtypeStruct((b, 1, LANES), I32)],
        compiler_params=_cp("arbitrary"), name="nsa_decode_select",
    )(qn3, ck3, cv3)


def _nsa_decode_attend_kernel(*refs, past, nsel):
    n_blk = NSA_KV_HEADS * nsel
    rows_ref = refs[0]
    kb_refs = refs[1:1 + n_blk]
    vb_refs = refs[1 + n_blk:1 + 2 * n_blk]
    (q_ref, idx_ref, ksn_ref, vs_ref, wk_ref, wv_ref, kwn_ref, vw_ref, ocmp_ref, gt_ref, o_ref) = refs[1 + 2 * n_blk:]
    del rows_ref
    dh, hg = HEAD_DIM, HEADS_PER_GROUP
    scale = dh ** -0.5
    self_blk = past // SEL_BLOCK
    q16 = _decode_q16(q_ref)
    q16f = q16.astype(F32)
    row = lax.broadcasted_iota(I32, (16, 1), 0)
    lane = lax.broadcasted_iota(I32, (1, LANES), 1)
    idx_row = idx_ref[...]
    o_sel = jnp.zeros((16, dh), F32)
    o_win = jnp.zeros((16, dh), F32)
    for g in range(NSA_KV_HEADS):
        cs = slice(g * dh, (g + 1) * dh)
        in_g = jnp.logical_and(row >= g * hg, row < (g + 1) * hg)
        k_self = ksn_ref[:, cs].astype(BF16).astype(F32)
        v_self = vs_ref[:, cs].astype(BF16).astype(F32)
        s_self = jnp.sum(q16f * k_self, axis=-1, keepdims=True) * scale
        parts, masks, vals = [], [], []
        self_sel = jnp.zeros((1, 1), jnp.bool_)
        for kk in range(nsel):
            pick = jnp.max(jnp.where(lane == g * SEL_TOPK + kk, idx_row, -1), axis=-1, keepdims=True)
            in_pool = jnp.logical_and(pick >= 0, pick < self_blk)
            self_sel = jnp.logical_or(self_sel, pick == self_blk)
            grp = pl.ds(g, SEL_BLOCK, stride=NSA_KV_HEADS)
            kb = kb_refs[g * nsel + kk][grp, :].astype(BF16)
            parts.append(_nt_dot(q16, kb) * scale)
            masks.append(in_pool)
            vals.append(vb_refs[g * nsel + kk][grp, :].astype(BF16))
        parts.append(s_self)
        masks.append(self_sel)
        probs = _masked_softmax_parts(parts, masks)
        acc = probs[-1].astype(BF16).astype(F32) * v_self
        for p, vv in zip(probs[:-1], vals):
            acc = acc + jnp.dot(p.astype(BF16), vv, preferred_element_type=F32)
        o_sel = jnp.where(in_g, acc, o_sel)
        kw_self = kwn_ref[:, cs].astype(BF16).astype(F32)
        vw_self = vw_ref[:, cs].astype(BF16).astype(F32)
        sw_self = jnp.sum(q16f * kw_self, axis=-1, keepdims=True) * scale
        wgrp = pl.ds(g, wk_ref.shape[0] // NSA_KV_HEADS, stride=NSA_KV_HEADS)
        wparts = [_nt_dot(q16, wk_ref[wgrp, :].astype(BF16)) * scale, sw_self]
        wmasks = [lane[:, :1] >= 0, lane[:, :1] >= 0]
        pw, pws = _masked_softmax_parts(wparts, wmasks)
        accw = (jnp.dot(pw.astype(BF16), wv_ref[wgrp, :].astype(BF16), preferred_element_type=F32)
                + pws.astype(BF16).astype(F32) * vw_self)
        o_win = jnp.where(in_g, accw, o_win)
    gates = gt_ref[...]
    o_cmp = ocmp_ref[...]
    for hd in range(NSA_HEADS):
        o_ref[:, hd * dh:(hd + 1) * dh] = (gates[:, 3 * hd:3 * hd + 1] * o_cmp[hd:hd + 1]
                                             + gates[:, 3 * hd + 1:3 * hd + 2] * o_sel[hd:hd + 1]
                                             + gates[:, 3 * hd + 2:3 * hd + 3] * o_win[hd:hd + 1]).astype(o_ref.dtype)


def _nsa_decode_attend(pool_rows, sel_k_pool, sel_v_pool, qn3, idx3, ksn3, z3, win_k, win_v, kwn3, ocmp, gates3,
                       past, nsel):
    b = qn3.shape[0]
    kvw = NSA_KV_HEADS * HEAD_DIM
    n_blk = NSA_KV_HEADS * nsel
    nbuf = win_k.shape[2]

    def blk_spec(tt):
        return pl.BlockSpec((SEL_BLOCK * NSA_KV_HEADS, HEAD_DIM), lambda bi, r: (r[bi * n_blk + tt], 0))

    def row_spec(cb, w=kvw):
        return pl.BlockSpec((None, 1, w), lambda bi, r: (bi, 0, cb))

    wspec = pl.BlockSpec((nbuf * NSA_KV_HEADS, HEAD_DIM), lambda bi, r: (bi, 0))
    sel_k_pool, sel_v_pool, win_k, win_v = (_kv_rows(a) for a in (sel_k_pool, sel_v_pool, win_k, win_v))
    in_specs = ([blk_spec(tt) for tt in range(n_blk)] + [blk_spec(tt) for tt in range(n_blk)]
                + [pl.BlockSpec((None, 1, NSA_HEADS * HEAD_DIM), lambda bi, r: (bi, 0, 0)),
                   pl.BlockSpec((None, 1, LANES), lambda bi, r: (bi, 0, 0)),
                   row_spec(0), row_spec(C_VS // kvw),
                   wspec, wspec,
                   row_spec(0), row_spec(C_VW // kvw),
                   pl.BlockSpec((None, 16, HEAD_DIM), lambda bi, r: (bi, 0, 0)),
                   pl.BlockSpec((None, 1, LANES), lambda bi, r: (bi, 0, 0))])
    return pl.pallas_call(
        functools.partial(_nsa_decode_attend_kernel, past=past, nsel=nsel),
        grid_spec=pltpu.PrefetchScalarGridSpec(
            num_scalar_prefetch=1, grid=(b,), in_specs=in_specs,
            out_specs=pl.BlockSpec((None, 1, NSA_HEADS * HEAD_DIM), lambda bi, r: (bi, 0, 0))),
        out_shape=jax.ShapeDtypeStruct((b, 1, NSA_HEADS * HEAD_DIM), BF16),
        compiler_params=_cp("arbitrary"), name="nsa_decode_attend",
    )(pool_rows, *([sel_k_pool] * n_blk), *([sel_v_pool] * n_blk),
      qn3, idx3, ksn3, z3, win_k, win_v, kwn3, z3, ocmp, gates3)


def _mem_attn_kernel(*refs, cache_layout):
    dh = MEM_HEAD_DIM
    if cache_layout:
        q_ref, g_ref, mk0_ref, mk1_ref, mv0_ref, mv1_ref, o_ref = refs
    else:
        q_ref, g_ref, mk_ref, mv_ref, o_ref = refs
    tq = q_ref.shape[0]
    rows = max(tq, 16)
    for h in range(MEM_HEADS):
        cs = slice(h * dh, (h + 1) * dh)
        x = q_ref[:, cs]
        q = x * lax.rsqrt(jnp.mean(x * x, axis=-1, keepdims=True) + EPS) * g_ref[...]
        if rows != tq:
            q = jnp.broadcast_to(q, (rows, dh))
        if cache_layout:
            mk = jnp.concatenate([mk0_ref[:, h, :], mk1_ref[:, h, :]], axis=1)
            mv = jnp.concatenate([mv0_ref[:, h, :], mv1_ref[:, h, :]], axis=1)
        else:
            mk, mv = mk_ref[:, cs], mv_ref[:, cs]
        s = _nt_dot(q.astype(BF16), mk.astype(BF16)) * (dh ** -0.5)
        m = jnp.max(s, axis=-1, keepdims=True)
        e = jnp.exp(s - m)
        p = e / jnp.sum(e, axis=-1, keepdims=True)
        o = jnp.dot(p.astype(BF16), mv.astype(BF16), preferred_element_type=F32)
        o_ref[:, cs] = o[0:tq].astype(o_ref.dtype)


def _mem_attention(z3, gq, mk, mv, cache_layout=False):
    b, t, _ = z3.shape
    w = MEM_HEADS * MEM_HEAD_DIM
    tq = min(t, 512)
    assert t % tq == 0
    if cache_layout:
        m = mk.shape[2]
        assert MEM_HEAD_DIM == 2 * LANES
        half = lambda c: pl.BlockSpec((None, None, m, MEM_HEADS, LANES), lambda bi, i: (0, bi, 0, 0, c))
        kv_specs, kv_args = [half(0), half(1), half(0), half(1)], [mk, mk, mv, mv]
    else:
        m = mk.shape[1]
        kv_specs, kv_args = [pl.BlockSpec((None, m, w), lambda bi, i: (bi, 0, 0))] * 2, [mk, mv]
    return pl.pallas_call(
        functools.partial(_mem_attn_kernel, cache_layout=cache_layout),
        grid=(b, t // tq),
        in_specs=[pl.BlockSpec((None, tq, w), lambda bi, i: (bi, i, C_MQ // w)),
                  pl.BlockSpec((1, MEM_HEAD_DIM), lambda bi, i: (0, 0))] + kv_specs,
        out_specs=pl.BlockSpec((None, tq, w), lambda bi, i: (bi, i, 0)),
        out_shape=jax.ShapeDtypeStruct((b, t, w), BF16),
        compiler_params=_cp("arbitrary", "arbitrary"), name="mem_attention",
    )(z3, gq.reshape(1, MEM_HEAD_DIM), *kv_args)


def _sigmoid_cols_kernel(x_ref, o_ref):
    o_ref[...] = jax.nn.sigmoid(x_ref[...])


def _nsa_gates(z, col):
    n = z.shape[0]
    tm = min(n, 1024)
    return pl.pallas_call(
        _sigmoid_cols_kernel, grid=(n // tm,),
        in_specs=[pl.BlockSpec((tm, LANES), lambda i: (i, col // LANES))],
        out_specs=pl.BlockSpec((tm, LANES), lambda i: (i, 0)),
        out_shape=jax.ShapeDtypeStruct((n, LANES), F32),
        compiler_params=_cp("arbitrary"), name="nsa_gates",
    )(z)


def _merge_kernel(r_ref, n_ref, m_ref, g0_ref, g1_ref, g2_ref, w_ref, o_ref):
    acc = jnp.zeros(o_ref.shape, F32)
    for c, (b_ref, g_ref) in enumerate(((r_ref, g0_ref), (n_ref, g1_ref), (m_ref, g2_ref))):
        up = jnp.dot(b_ref[...].astype(BF16), w_ref[c], preferred_element_type=F32)
        acc = acc + jax.nn.sigmoid(g_ref[...]) * up
    o_ref[...] = acc.astype(BF16)


def _merge(o_ret, o_nsa, o_mem, z, wb, d_model):
    n, bw = o_ret.shape
    tm = min(n, 1024)
    tn = 512
    gb = C_MG // tn
    nj = d_model // tn
    bspec = pl.BlockSpec((tm, bw), lambda i, j: (i, 0))
    gspec = lambda c: pl.BlockSpec((tm, tn), lambda i, j: (i, gb + c * nj + j))
    return pl.pallas_call(
        _merge_kernel, grid=(n // tm, nj),
        in_specs=[bspec, bspec, bspec, gspec(0), gspec(1), gspec(2),
                  pl.BlockSpec((N_BRANCH, bw, tn), lambda i, j: (0, 0, j))],
        out_specs=pl.BlockSpec((tm, tn), lambda i, j: (i, j)),
        out_shape=jax.ShapeDtypeStruct((n, d_model), BF16),
        compiler_params=_cp("arbitrary", "arbitrary"), name="branch_merge",
    )(o_ret, o_nsa, o_mem, z, z, z, wb)


def _topk_rows(s, kk):
    e, tl = s.shape
    nt = e // 8
    assert e % 8 == 0 and nt & (nt - 1) == 0
    sub = lax.broadcasted_iota(I32, (8, tl), 0).astype(F32)
    vals = [s[8 * k:8 * k + 8] for k in range(nt)]
    idxs = [sub + float(8 * k) for k in range(nt)]
    for a, b in _sorting_network(nt):
        va, vb, ia, ib = vals[a], vals[b], idxs[a], idxs[b]
        swap = jnp.logical_or(vb > va, jnp.logical_and(vb == va, ib < ia))
        vals[a], vals[b] = jnp.maximum(va, vb), jnp.minimum(va, vb)
        idxs[a], idxs[b] = jnp.where(swap, ib, ia), jnp.where(swap, ia, ib)
    out_v, out_i = [], []
    for step in range(kk):
        m = jnp.max(vals[0], axis=0, keepdims=True)
        i = jnp.min(jnp.where(vals[0] == m, idxs[0], float(e)), axis=0, keepdims=True)
        out_v.append(m)
        out_i.append(i)
        win = idxs[0] == i
        depth = min(nt, kk - step)
        for k in range(depth - 1):
            vals[k] = jnp.where(win, vals[k + 1], vals[k])
            idxs[k] = jnp.where(win, idxs[k + 1], idxs[k])
        vals[depth - 1] = jnp.where(win, -jnp.inf, vals[depth - 1])
    return jnp.concatenate(out_v, axis=0), jnp.concatenate(out_i, axis=0)


def _sorting_network(n):
    def merge(lo, hi, r):
        step = r * 2
        if step < hi - lo:
            yield from merge(lo, hi, step)
            yield from merge(lo + r, hi, step)
            yield from [(i, i + r) for i in range(lo + r, hi - r, step)]
        else:
            yield (lo, lo + r)

    def sort(lo, hi):
        if hi - lo >= 1:
            mid = lo + (hi - lo) // 2
            yield from sort(lo, mid)
            yield from sort(mid + 1, hi)
            yield from merge(lo, hi, 1)

    return list(sort(0, n - 1))


def _pair_pieces(kk):
    pieces, cur = [], []

    def flush(rows):
        pieces.append(rows + [None] * (8 - len(rows)))

    for a in range(kk):
        grp = [(a, b) for b in range(kk // (a + 1))]
        if cur and len(cur) + len(grp) > 8:
            flush(cur)
            cur = []
        cur = cur + grp
        while len(cur) >= 8:
            flush(cur[:8])
            cur = cur[8:]
    if cur:
        flush(cur)
    return pieces


def _rows_by_runs(x, ids):
    if ids[0] % 8 == 0 and ids == list(range(ids[0], ids[0] + 8)):
        return x[ids[0]:ids[0] + 8]
    r = lax.broadcasted_iota(I32, (8, 1), 0)
    runs = []
    for p, i in enumerate(ids):
        if not runs or runs[-1][1] != i:
            runs.append((p, i))
    out = jnp.broadcast_to(x[runs[-1][1]:runs[-1][1] + 1], (8, x.shape[1]))
    for (_, i), (nxt, _) in zip(reversed(runs[:-1]), reversed(runs[1:])):
        out = jnp.where(r < nxt, x[i:i + 1], out)
    return out


ROUTE_HEADS_PER_STEP = 4


def _peer_route_kernel(q_ref, sk_ref, i_ref, j_ref, g_ref):
    kk = PEER_TOPK
    for hh in range(ROUTE_HEADS_PER_STEP):
        rows = slice(hh * kk, (hh + 1) * kk)
        i_ref[rows, :], j_ref[rows, :], g_ref[rows, :] = _peer_route_head(
            q_ref[:, hh * PEER_DKEY:(hh + 1) * PEER_DKEY], sk_ref[hh])


def _peer_route_head(q, sk):
    kk = PEER_TOPK
    st = _nt_dot(sk, q.astype(BF16))
    v0, i0 = _topk_rows(st[0:PEER_KEYS], kk)
    v1, i1 = _topk_rows(st[PEER_KEYS:2 * PEER_KEYS], kk)
    cand, ci, cj = [], [], []
    for piece in _pair_pieces(kk):
        ra = [p[0] if p else 0 for p in piece]
        rb = [p[1] if p else 0 for p in piece]
        live = lax.broadcasted_iota(I32, (8, 1), 0) < sum(p is not None for p in piece)
        cand.append(jnp.where(live, _rows_by_runs(v0, ra) + _rows_by_runs(v1, rb), -jnp.inf))
        ci.append(_rows_by_runs(i0, ra))
        cj.append(_rows_by_runs(i1, rb))
    cand, ci, cj = (jnp.concatenate(c, axis=0) for c in (cand, ci, cj))
    ridx = lax.broadcasted_iota(I32, cand.shape, 0).astype(F32)
    sc, si, sj = [], [], []
    for _ in range(kk):
        m = jnp.max(cand, axis=0, keepdims=True)
        r = jnp.min(jnp.where(cand == m, ridx, float(cand.shape[0])), axis=0, keepdims=True)
        hit = ridx == r
        sc.append(m)
        si.append(jnp.max(jnp.where(hit, ci, -1.0), axis=0, keepdims=True))
        sj.append(jnp.max(jnp.where(hit, cj, -1.0), axis=0, keepdims=True))
        cand = jnp.where(hit, -jnp.inf, cand)
    sc = jnp.concatenate(sc, axis=0)
    e = jnp.exp(sc - jnp.max(sc, axis=0, keepdims=True))
    return (jnp.concatenate(si, axis=0).astype(I32), jnp.concatenate(sj, axis=0).astype(I32),
            e / jnp.sum(e, axis=0, keepdims=True))


def _peer_route(q, skbd):
    n = q.shape[0]
    tl = min(n, LANES)
    kk = PEER_TOPK
    slots = PEER_HEADS * kk
    hps = ROUTE_HEADS_PER_STEP
    ospec = pl.BlockSpec((hps * kk, tl), lambda i, h: (h, i))
    return pl.pallas_call(
        _peer_route_kernel, grid=(n // tl, PEER_HEADS // hps),
        in_specs=[pl.BlockSpec((tl, hps * PEER_DKEY), lambda i, h: (i, h)),
                  pl.BlockSpec((hps, 2 * PEER_KEYS, PEER_DKEY), lambda i, h: (h, 0, 0))],
        out_specs=[ospec, ospec, ospec],
        out_shape=[jax.ShapeDtypeStruct((slots, n), I32), jax.ShapeDtypeStruct((slots, n), I32),
                   jax.ShapeDtypeStruct((slots, n), F32)],
        compiler_params=_cp("arbitrary", "arbitrary"), name="peer_route",
    )(q, skbd)


TOKEN_GROUP = 8


def _peer_weights_kernel(i_ref, j_ref, g_ref, o_ref, it_scr, jt_scr, gt_scr):
    it_scr[...] = i_ref[...].T
    jt_scr[...] = j_ref[...].T
    gt_scr[...] = g_ref[...].T
    key = lax.broadcasted_iota(I32, (PEER_KEYS, i_ref.shape[0]), 0)

    def one_hots(n):
        irow = it_scr[pl.ds(n, 1), :]
        jrow = jt_scr[pl.ds(n, 1), :]
        grow = gt_scr[pl.ds(n, 1), :]
        a = jnp.where(key == irow, grow, 0.0).astype(BF16)
        b = jnp.where(key == jrow, 1.0, 0.0).astype(BF16)
        return a, b

    groups_per_trip = 4 if o_ref.shape[0] % 4 == 0 else 1

    def body(nb2, carry):
        results = []
        for gi in range(groups_per_trip):
            nb = nb2 * groups_per_trip + gi
            for t in range(TOKEN_GROUP):
                a, b = one_hots(nb * TOKEN_GROUP + t)
                results.append((nb, t, _nt_dot(a, b)))
        for nb, t, r in results:
            o_ref[nb, pl.ds(t, PEER_KEYS, stride=TOKEN_GROUP), :] = r
        return carry

    lax.fori_loop(0, o_ref.shape[0] // groups_per_trip, body, 0)


def _peer_weights(it, jt, gt):
    slots, n = it.shape
    tb = min(n, LANES)
    assert tb % TOKEN_GROUP == 0
    ispec = pl.BlockSpec((slots, tb), lambda i: (0, i))
    out = pl.pallas_call(
        _peer_weights_kernel, grid=(n // tb,),
        in_specs=[ispec, ispec, ispec],
        out_specs=pl.BlockSpec((tb // TOKEN_GROUP, PEER_KEYS * TOKEN_GROUP, PEER_KEYS), lambda i: (i, 0, 0)),
        out_shape=jax.ShapeDtypeStruct((n // TOKEN_GROUP, PEER_KEYS * TOKEN_GROUP, PEER_KEYS), F32),
        scratch_shapes=[pltpu.VMEM((tb, slots), I32), pltpu.VMEM((tb, slots), I32), pltpu.VMEM((tb, slots), F32)],
        compiler_params=_cp("arbitrary"), name="peer_weights",
    )(it, jt, gt)
    return out.reshape(n // TOKEN_GROUP, PEER_KEYS, TOKEN_GROUP, PEER_KEYS)


def _peer_dense_kernel(h_ref, g_ref, u_ref, v_ref, x_ref, o_ref):
    @pl.when(pl.program_id(1) == 0)
    def _():
        o_ref[...] = x_ref[...]

    a = jnp.dot(h_ref[...], u_ref[...], preferred_element_type=F32)
    act = 0.5 * a * (1.0 + lax.erf(a * (2.0 ** -0.5)))
    tm = h_ref.shape[0]
    w = jnp.concatenate([(g_ref[:, ii].reshape(tm, PEER_KEYS)
                          * act[:, ii * PEER_KEYS:(ii + 1) * PEER_KEYS]).astype(BF16)
                         for ii in range(g_ref.shape[1])], axis=1)
    o_ref[...] += jnp.dot(w, v_ref[...], preferred_element_type=F32)


def _peer_dense(hn, gw, ut, vv, x):
    n, d = hn.shape
    ne = ut.shape[1]
    tm = min(n, 512)
    te = 1024
    return pl.pallas_call(
        _peer_dense_kernel, grid=(n // tm, ne // te),
        in_specs=[pl.BlockSpec((tm, d), lambda i, j: (i, 0)),
                  pl.BlockSpec((tm // TOKEN_GROUP, te // PEER_KEYS, TOKEN_GROUP, PEER_KEYS),
                               lambda i, j: (i, j, 0, 0)),
                  pl.BlockSpec((d, te), lambda i, j: (0, j)),
                  pl.BlockSpec((te, d), lambda i, j: (j, 0)),
                  pl.BlockSpec((tm, d), lambda i, j: (i, 0))],
        out_specs=pl.BlockSpec((tm, d), lambda i, j: (i, 0)),
        out_shape=jax.ShapeDtypeStruct((n, d), F32),
        compiler_params=_cp("arbitrary", "arbitrary"), name="peer_dense",
    )(hn, gw, ut, vv, x)


def _prepare_weights(norm_attn, w_in, cmp_pos_k, cmp_pos_v, cmp_w_k, cmp_w_v, w_mem_kv, w_branch, w_out,
                     peer_wq, peer_subkeys, peer_u, peer_v):
    d = w_in.shape[0]
    ng_w = NSA_HEADS * 3
    o_ng = 5632
    o_mq = o_ng + ng_w
    o_mg = o_mq + MEM_HEADS * MEM_HEAD_DIM
    width = C_MG + N_BRANCH * d + LANES
    width = -(-width // Z_TILE) * Z_TILE
    w_r = jnp.concatenate([w_in[:, 0:4096], w_in[:, o_mq:o_mg], w_in[:, 4096:o_ng], w_in[:, o_mg:],
                           w_in[:, o_ng:o_mq],
                           jnp.zeros((d, width - (C_MG + N_BRANCH * d + ng_w)), w_in.dtype)], axis=1).astype(BF16)
    pos_k, wk_big = _compress_weights(cmp_pos_k, cmp_w_k)
    pos_v, wv_big = _compress_weights(cmp_pos_v, cmp_w_v)
    half = PEER_DKEY // 2
    sk = peer_subkeys.astype(BF16)
    zk = jnp.zeros((PEER_HEADS, PEER_KEYS, half), BF16)
    skbd = jnp.concatenate([jnp.concatenate([sk[:, 0], zk], axis=2),
                            jnp.concatenate([zk, sk[:, 1]], axis=2)], axis=1)
    return dict(w_in=w_r, c_ng=C_MG + N_BRANCH * d, pos_k=pos_k, wk_big=wk_big, pos_v=pos_v, wv_big=wv_big,
                w_mem_kv=w_mem_kv.astype(BF16), w_branch=w_branch.astype(BF16), w_out=w_out.astype(BF16),
                peer_wq=peer_wq.astype(BF16), skbd=skbd, ut=peer_u.astype(BF16).T, vv=peer_v.astype(BF16))


def _layer_common(x, seq, pos, norm_attn, nsa_q_norm, nsa_k_norm, pw):
    z = _prologue_matmul(x, norm_attn, pw["w_in"], mode="norm", name="in_proj", tm=1024)
    nsa_tabs = _rope_tables(pos, ROPE_DIMS, ROPE_THETA, HEAD_DIM)
    half = ROPE_DIMS // 2
    kvw = NSA_KV_HEADS * HEAD_DIM
    qn = _headnorm(z, C_NQ, NSA_HEADS * HEAD_DIM, nsa_q_norm, HEAD_DIM, seq, nsa_tabs, half)
    kcn = _headnorm(z, C_KC, kvw, nsa_k_norm[0], HEAD_DIM, seq, nsa_tabs, half)
    ksn = _headnorm(z, C_KS, kvw, nsa_k_norm[1], HEAD_DIM, seq, nsa_tabs, half)
    kwn = _headnorm(z, C_KW, kvw, nsa_k_norm[2], HEAD_DIM, seq, nsa_tabs, half)
    gates = _nsa_gates(z, pw["c_ng"])
    ret_tabs = _rope_tables(pos, RET_DK, RET_THETA, RET_DK)
    return z, qn, kcn, ksn, kwn, gates, ret_tabs


def _layer_tail(x, z, o_ret, o_nsa, o_mem, norm_ffn, pw):
    d = x.shape[1]
    merged = _merge(o_ret, o_nsa, o_mem, z, pw["w_branch"], d)
    x1 = _prologue_matmul(merged, jnp.ones((d,), F32), pw["w_out"], mode="none", residual=x, name="out_proj")
    q, hn = _prologue_matmul(x1, norm_ffn, pw["peer_wq"], mode="norm", emit_h=True, name="peer_query")
    it, jt, gt = _peer_route(q, pw["skbd"])
    return _peer_dense(hn, _peer_weights(it, jt, gt), pw["ut"], pw["vv"], x1)


def kernel(x_prompt, x_sample, cache_cmp_k, cache_cmp_v, cache_sel_k, cache_sel_v, cache_win_k, cache_win_v,
           state_ret, cache_mem_k, cache_mem_v, page_table, mem_prompt, norm_attn, w_in, ret_gn, nsa_q_norm,
           nsa_k_norm, cmp_pos_k, cmp_pos_v, cmp_w_k, cmp_w_v, norm_mem, w_mem_kv, mem_q_norm, mem_k_norm,
           w_branch, w_out, norm_ffn, peer_wq, peer_subkeys, peer_u, peer_v):
    depth = w_in.shape[0]
    assert depth == 1
    l = 0
    bp, t, d = x_prompt.shape
    bs, ts, _ = x_sample.shape
    assert ts == 1
    n_pool, page, g_kv, dh = cache_cmp_k.shape[1:]
    n_pages = page_table.shape[1]
    past = n_pages * page
    kvw = g_kv * dh
    memw = MEM_HEADS * MEM_HEAD_DIM
    pw = _prepare_weights(norm_attn[l], w_in[l], cmp_pos_k[l], cmp_pos_v[l], cmp_w_k[l], cmp_w_v[l], w_mem_kv[l],
                          w_branch[l], w_out[l], peer_wq[l], peer_subkeys[l], peer_u[l], peer_v[l])
    cmp_cols = CMP_BLOCK * kvw

    n_p = bp * t
    xp = x_prompt.reshape(n_p, d)
    m_tok = mem_prompt.shape[1]
    mkv = _prologue_matmul(mem_prompt.reshape(bp * m_tok, d), norm_mem[l], pw["w_mem_kv"], mode="norm",
                           name="mem_kv_proj")
    mk_p = _headnorm(mkv, 0, memw, mem_k_norm[l], MEM_HEAD_DIM, m_tok)
    mv_p = mkv[:, memw:]
    z, qn, kcn, ksn, kwn, gates, ret_tabs = _layer_common(xp, t, jnp.arange(t), norm_attn[l], nsa_q_norm[l],
                                                         nsa_k_norm[l], pw)
    z3 = z.reshape(bp, t, z.shape[1])
    o_ret, st_p = _retention_prompt(z3, ret_tabs, jnp.zeros((bp, RET_HEADS, RET_DK, RET_DV), F32), ret_gn[l])
    vc_p = z[:, C_VC:C_VC + kvw]
    nbc = t // CMP_BLOCK
    ck_p = _prologue_matmul(kcn.reshape(bp * nbc, cmp_cols), pw["pos_k"], pw["wk_big"], mode="bias", tm=256,
                            name="compress_k")
    cv_p = _prologue_matmul(vc_p.reshape(bp * nbc, cmp_cols), pw["pos_v"], pw["wv_big"], mode="bias", tm=256,
                            name="compress_v")
    pair = CMP_PER_SEL * kvw
    o_nsa = _nsa_prompt(qn.reshape(bp, t, -1), ck_p.reshape(bp, nbc // CMP_PER_SEL, pair),
                        cv_p.reshape(bp, nbc // CMP_PER_SEL, pair),
                        ksn.reshape(bp, t, kvw), z3, kwn.reshape(bp, t, kvw), gates.reshape(bp, t, LANES))
    o_mem = _mem_attention(z3, mem_q_norm[l], mk_p.reshape(bp, m_tok, memw), mv_p.reshape(bp, m_tok, memw))
    y_p = _layer_tail(xp, z, o_ret.reshape(n_p, -1), o_nsa.reshape(n_p, -1), o_mem.reshape(n_p, -1), norm_ffn[l], pw)

    nbuf_p = min(WINDOW, t)
    kv5 = lambda a, bb, tt: a.reshape(1, bb, tt, g_kv, dh)
    outs_p = (kv5(kcn, bp, t), kv5(vc_p, bp, t), kv5(ksn, bp, t), kv5(z[:, C_VS:C_VS + kvw], bp, t),
              kv5(kwn, bp, t)[:, :, t - nbuf_p:], kv5(z[:, C_VW:C_VW + kvw], bp, t)[:, :, t - nbuf_p:],
              st_p[None], mk_p.reshape(1, bp, m_tok, MEM_HEADS, MEM_HEAD_DIM),
              mv_p.reshape(1, bp, m_tok, MEM_HEADS, MEM_HEAD_DIM))

    xs = x_sample.reshape(bs, d)
    zs, qn_s, kcn_s, ksn_s, kwn_s, gates_s, ret_tabs_s = _layer_common(
        xs, 1, jnp.full((1,), past, I32), norm_attn[l], nsa_q_norm[l], nsa_k_norm[l], pw)
    zs3 = zs.reshape(bs, 1, zs.shape[1])
    o_ret_s, st_s = _retention_decode(zs3, ret_tabs_s, state_ret[l], ret_gn[l])
    per_page = page // CMP_BLOCK
    ckp = _compress_pool(cache_cmp_k, cmp_pos_k[l], cmp_w_k[l], "compress_pool_k")
    cvp = _compress_pool(cache_cmp_v, cmp_pos_v[l], cmp_w_v[l], "compress_pool_v")
    pages_of = lambda a: a[:, page_table].transpose(1, 2, 0, 3).reshape(bs, n_pages * per_page // CMP_PER_SEL, pair)
    ck_s, cv_s = pages_of(ckp), pages_of(cvp)
    qn_s3 = qn_s.reshape(bs, 1, -1)
    ocmp_s, idx_s = _nsa_decode_select(qn_s3, ck_s, cv_s, past)
    nbs = past // SEL_BLOCK + 1
    nsel = min(SEL_TOPK, nbs)
    per_page_sel = page // SEL_BLOCK
    picks = jnp.clip(idx_s[:, 0, :NSA_KV_HEADS * SEL_TOPK].reshape(bs, NSA_KV_HEADS, SEL_TOPK)[:, :, :nsel],
                     0, nbs - 2)
    pool_rows = (jnp.take_along_axis(page_table, (picks // per_page_sel).reshape(bs, -1), axis=1) * per_page_sel
                 + (picks % per_page_sel).reshape(bs, -1)).reshape(-1).astype(I32)
    o_nsa_s = _nsa_decode_attend(
        pool_rows, cache_sel_k, cache_sel_v, qn_s3, idx_s, ksn_s.reshape(bs, 1, kvw), zs3, cache_win_k, cache_win_v,
        kwn_s.reshape(bs, 1, kvw), ocmp_s, gates_s.reshape(bs, 1, LANES), past, nsel)
    o_mem_s = _mem_attention(zs3, mem_q_norm[l], cache_mem_k, cache_mem_v, cache_layout=True)
    y_s = _layer_tail(xs, zs, o_ret_s.reshape(bs, -1), o_nsa_s.reshape(bs, -1), o_mem_s.reshape(bs, -1),
                      norm_ffn[l], pw)

    nbuf_s = cache_win_k.shape[2]
    win_k_s = jnp.concatenate([cache_win_k[l], kv5(kwn_s, bs, 1)[0]], axis=1)[None, :, -nbuf_s:]
    win_v_s = jnp.concatenate([cache_win_v[l], kv5(zs[:, C_VW:C_VW + kvw], bs, 1)[0]], axis=1)[None, :, -nbuf_s:]
    outs_s = (kv5(kcn_s, bs, 1), kv5(zs[:, C_VC:C_VC + kvw], bs, 1), kv5(ksn_s, bs, 1),
              kv5(zs[:, C_VS:C_VS + kvw], bs, 1), win_k_s, win_v_s, st_s[None].astype(x_sample.dtype))
    return (y_p.reshape(bp, t, d), y_s.reshape(bs, ts, d)) + outs_p + outs_s
```

```python
import functools

import numpy as np
import jax
import jax.numpy as jnp
from jax import lax
from jax.experimental import pallas as pl
from jax.experimental.pallas import tpu as pltpu

F32 = jnp.float32
BF16 = jnp.bfloat16
I32 = jnp.int32

EPS = 1e-6
NEG_INF = -1e30
FORCE = 1e4

RET_HEADS, RET_DK, RET_DV, RET_CHUNK, RET_THETA = 4, 128, 256, 128, 10000.0
NSA_HEADS, NSA_KV_HEADS, HEAD_DIM = 8, 2, 128
HEADS_PER_GROUP = NSA_HEADS // NSA_KV_HEADS
CMP_BLOCK, SEL_BLOCK, SEL_TOPK, WINDOW, Q_BLOCK = 32, 64, 4, 256, 128
CMP_PER_SEL = SEL_BLOCK // CMP_BLOCK
ROPE_THETA, ROPE_DIMS = 500000.0, HEAD_DIM // 4
MEM_HEADS, MEM_HEAD_DIM = 4, 256
PEER_HEADS, PEER_KEYS, PEER_DKEY, PEER_TOPK = 8, 128, 128, 16
N_BRANCH = 3
LANES = 128

C_RQ, C_RK, C_RV, C_RG, C_NQ, C_MQ = 0, 512, 1024, 2048, 3072, 4096
C_KC, C_VC, C_KS, C_VS, C_KW, C_VW = 5120, 5376, 5632, 5888, 6144, 6400
C_MG = 6656
Z_TILE = 512


def _cp(*sem):
    return pltpu.CompilerParams(dimension_semantics=sem, vmem_limit_bytes=56 * 1024 * 1024)


def _nt_dot(a, b):
    return lax.dot_general(a, b, (((1,), (1,)), ((), ())), preferred_element_type=F32)


def _tn_dot(a, b):
    return lax.dot_general(a, b, (((0,), (0,)), ((), ())), preferred_element_type=F32)


def _pm_kernel(*refs, mode, residual, emit_h):
    x_ref, g_ref, w_ref = refs[:3]
    k = 3
    r_ref = None
    if residual:
        r_ref = refs[k]
        k += 1
    o_ref = refs[k]
    k += 1
    h_ref = None
    if emit_h:
        h_ref = refs[k]
        k += 1
    h_scr = refs[k]

    @pl.when(pl.program_id(1) == 0)
    def _():
        x = x_ref[...].astype(F32)
        if mode == "norm":
            h = x * lax.rsqrt(jnp.mean(x * x, axis=-1, keepdims=True) + EPS) * g_ref[...]
        elif mode == "bias":
            h = x + g_ref[...]
        else:
            h = x
        h_scr[...] = h.astype(BF16)
        if emit_h:
            h_ref[...] = h_scr[...]

    acc = jnp.dot(h_scr[...], w_ref[...], preferred_element_type=F32)
    if residual:
        acc = acc + r_ref[...]
    o_ref[...] = acc


def _prologue_matmul(x, g, w, *, mode, name, residual=None, emit_h=False, tm=512):
    n, kdim = x.shape
    wout = w.shape[1]
    tm = min(tm, n)
    tn = next(c for c in (1024, 512, 256, 128) if wout % c == 0)
    assert n % tm == 0 and wout % tn == 0
    in_specs = [pl.BlockSpec((tm, kdim), lambda i, j: (i, 0)),
                pl.BlockSpec((1, kdim), lambda i, j: (0, 0)),
                pl.BlockSpec((kdim, tn), lambda i, j: (0, j))]
    args = [x, g.reshape(1, kdim).astype(F32), w]
    if residual is not None:
        in_specs.append(pl.BlockSpec((tm, tn), lambda i, j: (i, j)))
        args.append(residual)
    out_shape = [jax.ShapeDtypeStruct((n, wout), F32)]
    out_specs = [pl.BlockSpec((tm, tn), lambda i, j: (i, j))]
    if emit_h:
        out_shape.append(jax.ShapeDtypeStruct((n, kdim), BF16))
        out_specs.append(pl.BlockSpec((tm, kdim), lambda i, j: (i, 0)))
    res = pl.pallas_call(
        functools.partial(_pm_kernel, mode=mode, residual=residual is not None, emit_h=emit_h),
        grid=(n // tm, wout // tn),
        in_specs=in_specs, out_specs=out_specs, out_shape=out_shape,
        scratch_shapes=[pltpu.VMEM((tm, kdim), BF16)],
        compiler_params=_cp("arbitrary", "arbitrary"), name=name,
    )(*args)
    return res if emit_h else res[0]


def _rope_tables(pos, rot_dims, theta, dh):
    half = rot_dims // 2
    inv = theta ** (-jnp.arange(half, dtype=F32) * 2.0 / rot_dims)
    ang = pos.astype(F32)[:, None] * inv[None, :]
    cos, sin = jnp.cos(ang), jnp.sin(ang)
    t = pos.shape[0]
    one = jnp.ones((t, dh - rot_dims), F32)
    zh = jnp.zeros((t, half), F32)
    zr = jnp.zeros((t, dh - rot_dims), F32)
    c = jnp.concatenate([cos, cos, one], axis=1)
    s1 = jnp.concatenate([-sin, zh, zr], axis=1)
    s2 = jnp.concatenate([zh, sin, zr], axis=1)
    return c, s1, s2


def _rope(y, c, s1, s2, half):
    dh = y.shape[-1]
    return y * c + pltpu.roll(y, dh - half, 1) * s1 + pltpu.roll(y, half, 1) * s2


def _headnorm_kernel(*refs, nh, dh, rope_half):
    if rope_half:
        x_ref, g_ref, c_ref, s1_ref, s2_ref, o_ref = refs
    else:
        x_ref, g_ref, o_ref = refs
    for h in range(nh):
        x = x_ref[:, h * dh:(h + 1) * dh]
        y = x * lax.rsqrt(jnp.mean(x * x, axis=-1, keepdims=True) + EPS) * g_ref[...]
        if rope_half:
            y = _rope(y, c_ref[...], s1_ref[...], s2_ref[...], rope_half)
        o_ref[:, h * dh:(h + 1) * dh] = y


def _headnorm(x, col, width, gain, dh, seq, tables=None, rope_half=0):
    n = x.shape[0]
    tm = min(512, seq) if seq > 1 else n
    assert n % tm == 0 and col % width == 0
    cb = col // width
    in_specs = [pl.BlockSpec((tm, width), lambda i: (i, cb)),
                pl.BlockSpec((1, dh), lambda i: (0, 0))]
    args = [x, gain.reshape(1, dh).astype(F32)]
    if rope_half:
        if seq > 1:
            nb = seq // tm
            tspec = pl.BlockSpec((tm, dh), lambda i: (i % nb, 0))
        else:
            tspec = pl.BlockSpec((1, dh), lambda i: (0, 0))
        in_specs += [tspec] * 3
        args += list(tables)
    return pl.pallas_call(
        functools.partial(_headnorm_kernel, nh=width // dh, dh=dh, rope_half=rope_half),
        grid=(n // tm,),
        in_specs=in_specs,
        out_specs=pl.BlockSpec((tm, width), lambda i: (i, 0)),
        out_shape=jax.ShapeDtypeStruct((n, width), F32),
        compiler_params=_cp("arbitrary"), name="head_norm",
    )(*args)


def _log_gamma():
    return np.log1p(-(np.float32(2.0) ** (-5.0 - np.arange(RET_HEADS, dtype=np.float32)))).astype(np.float32)


def _ret_finish(o, gn, gate):
    y = o * lax.rsqrt(jnp.mean(o * o, axis=-1, keepdims=True) + EPS) * gn
    return y * (gate * jax.nn.sigmoid(gate))


def _ret_prompt_kernel(q_ref, k_ref, v_ref, gt_ref, c_ref, s1_ref, s2_ref, st0_ref, gn_ref, o_ref, st_ref):
    cl = q_ref.shape[0]

    @pl.when(pl.program_id(1) == 0)
    def _():
        st_ref[...] = st0_ref[...]

    half = RET_DK // 2
    ri = lax.broadcasted_iota(I32, (cl, cl), 0).astype(F32)
    ci = lax.broadcasted_iota(I32, (cl, cl), 1).astype(F32)
    rel = ri - ci
    rowi = lax.broadcasted_iota(I32, (cl, 1), 0).astype(F32)
    for h, lg in enumerate(_log_gamma().tolist()):
        ks, vs = slice(h * RET_DK, (h + 1) * RET_DK), slice(h * RET_DV, (h + 1) * RET_DV)
        q = _rope(q_ref[:, ks], c_ref[...], s1_ref[...], s2_ref[...], half)
        k = _rope(k_ref[:, ks], c_ref[...], s1_ref[...], s2_ref[...], half) * (RET_DK ** -0.5)
        decay = jnp.where(rel >= 0, jnp.exp(jnp.maximum(rel, 0.0) * lg), 0.0)
        qb, kb, vb = q.astype(BF16), k.astype(BF16), v_ref[:, vs].astype(BF16)
        inner = _nt_dot(qb, kb) * decay
        state = st_ref[h]
        cross = jnp.exp((rowi + 1.0) * lg)
        o = jnp.dot(inner.astype(BF16), vb, preferred_element_type=F32)
        o = o + jnp.dot(qb, state.astype(BF16), preferred_element_type=F32) * cross
        k_dec = (k * jnp.exp((cl - 1.0 - rowi) * lg)).astype(BF16)
        st_ref[h] = float(np.exp(np.float32(cl) * np.float32(lg))) * state + _tn_dot(k_dec, vb)
        o_ref[:, vs] = _ret_finish(o, gn_ref[h], gt_ref[:, vs]).astype(o_ref.dtype)


def _retention_prompt(z3, tables, state0, gn):
    b, t, _ = z3.shape
    cl = RET_CHUNK
    assert t % cl == 0
    wq, wv = RET_HEADS * RET_DK, RET_HEADS * RET_DV
    tspec = pl.BlockSpec((cl, RET_DK), lambda bi, c: (c, 0))
    st_spec = pl.BlockSpec((None, RET_HEADS, RET_DK, RET_DV), lambda bi, c: (bi, 0, 0, 0))
    o, st = pl.pallas_call(
        _ret_prompt_kernel,
        grid=(b, t // cl),
        in_specs=[pl.BlockSpec((None, cl, wq), lambda bi, c: (bi, c, C_RQ // wq)),
                  pl.BlockSpec((None, cl, wq), lambda bi, c: (bi, c, C_RK // wq)),
                  pl.BlockSpec((None, cl, wv), lambda bi, c: (bi, c, C_RV // wv)),
                  pl.BlockSpec((None, cl, wv), lambda bi, c: (bi, c, C_RG // wv)),
                  tspec, tspec, tspec, st_spec,
                  pl.BlockSpec((RET_HEADS, 1, RET_DV), lambda bi, c: (0, 0, 0))],
        out_specs=[pl.BlockSpec((None, cl, wv), lambda bi, c: (bi, c, 0)), st_spec],
        out_shape=[jax.ShapeDtypeStruct((b, t, wv), BF16),
                   jax.ShapeDtypeStruct((b, RET_HEADS, RET_DK, RET_DV), F32)],
        compiler_params=_cp("arbitrary", "arbitrary"), name="retention_prompt",
    )(z3, z3, z3, z3, *tables, state0, gn.reshape(RET_HEADS, 1, RET_DV))
    return o, st


def _ret_decode_kernel(q_ref, k_ref, v_ref, gt_ref, c_ref, s1_ref, s2_ref, st0_ref, gn_ref, o_ref, st_ref):
    lgs = _log_gamma()
    half = RET_DK // 2
    row0 = lax.broadcasted_iota(I32, (16, 1), 0) == 0
    for s, h in [(s, h) for s in range(q_ref.shape[0]) for h in range(RET_HEADS)]:
        gamma = float(np.exp(lgs[h]))
        ks, vs = slice(h * RET_DK, (h + 1) * RET_DK), slice(h * RET_DV, (h + 1) * RET_DV)
        q = _rope(q_ref[s, :, ks], c_ref[...], s1_ref[...], s2_ref[...], half)
        k = _rope(k_ref[s, :, ks], c_ref[...], s1_ref[...], s2_ref[...], half) * (RET_DK ** -0.5)
        v = v_ref[s, :, vs]
        state = st0_ref[s, h].astype(F32)
        inner = jnp.sum(q * k, axis=-1, keepdims=True)
        q16 = jnp.broadcast_to(q, (16, RET_DK)).astype(BF16)
        cross = jnp.dot(q16, state.astype(BF16), preferred_element_type=F32)[0:1]
        o = inner * v + cross * gamma
        k16 = jnp.where(row0, jnp.broadcast_to(k, (16, RET_DK)), 0.0).astype(BF16)
        v16 = jnp.broadcast_to(v, (16, RET_DV)).astype(BF16)
        st_ref[s, h] = gamma * state + _tn_dot(k16, v16)
        o_ref[s, :, vs] = _ret_finish(o, gn_ref[h], gt_ref[s, :, vs]).astype(o_ref.dtype)


def _retention_decode(z3, tables, state0, gn):
    b = z3.shape[0]
    wq, wv = RET_HEADS * RET_DK, RET_HEADS * RET_DV
    tspec = pl.BlockSpec((1, RET_DK), lambda bi: (0, 0))
    sb = 4 if b % 4 == 0 else 1
    st_spec = pl.BlockSpec((sb, RET_HEADS, RET_DK, RET_DV), lambda bi: (bi, 0, 0, 0))
    o, st = pl.pallas_call(
        _ret_decode_kernel,
        grid=(b // sb,),
        in_specs=[pl.BlockSpec((sb, 1, wq), lambda bi: (bi, 0, C_RQ // wq)),
                  pl.BlockSpec((sb, 1, wq), lambda bi: (bi, 0, C_RK // wq)),
                  pl.BlockSpec((sb, 1, wv), lambda bi: (bi, 0, C_RV // wv)),
                  pl.BlockSpec((sb, 1, wv), lambda bi: (bi, 0, C_RG // wv)),
                  tspec, tspec, tspec, st_spec,
                  pl.BlockSpec((RET_HEADS, 1, RET_DV), lambda bi: (0, 0, 0))],
        out_specs=[pl.BlockSpec((sb, 1, wv), lambda bi: (bi, 0, 0)), st_spec],
        out_shape=[jax.ShapeDtypeStruct((b, 1, wv), BF16),
                   jax.ShapeDtypeStruct((b, RET_HEADS, RET_DK, RET_DV), F32)],
        compiler_params=_cp("arbitrary"), name="retention_decode",
    )(z3, z3, z3, z3, *tables, state0, gn.reshape(RET_HEADS, 1, RET_DV))
    return o, st


def _compress_weights(pos_emb, w):
    g, dh = NSA_KV_HEADS, HEAD_DIM
    eye = jnp.eye(g, dtype=w.dtype)
    wbig = jnp.einsum("lde,gh->lgdhe", w, eye).reshape(CMP_BLOCK * g * dh, g * dh)
    pos_row = jnp.broadcast_to(pos_emb[:, None, :], (CMP_BLOCK, g, dh)).reshape(1, CMP_BLOCK * g * dh)
    return pos_row, wbig.astype(BF16)


def _kv_rows(a):
    return a.reshape(-1, a.shape[-1])


def _compress_pool_kernel(x_ref, pos_ref, w_ref, o_ref, *, page, groups):
    dh = x_ref.shape[1]
    p = x_ref.shape[0] // (page * groups)
    per_page = page // CMP_BLOCK
    for g in range(groups):
        acc = jnp.zeros((per_page * p, dh), F32)
        for l in range(CMP_BLOCK):
            xl = jnp.concatenate([x_ref[pl.ds((n * CMP_BLOCK + l) * groups + g, p, stride=page * groups), :]
                                  for n in range(per_page)], axis=0)
            xl = (xl + pos_ref[l:l + 1, :]).astype(BF16)
            acc = acc + jnp.dot(xl, w_ref[l], preferred_element_type=F32)
        o_ref[:, :, g * dh:(g + 1) * dh] = acc.reshape(per_page, p, dh)


def _compress_pool(pool, pos_emb, w, name):
    _, n_pool, page, groups, dh = pool.shape
    per_page = page // CMP_BLOCK
    p = max(c for c in range(8, 65, 8) if n_pool % c == 0)
    return pl.pallas_call(
        functools.partial(_compress_pool_kernel, page=page, groups=groups), grid=(n_pool // p,),
        in_specs=[pl.BlockSpec((p * page * groups, dh), lambda i: (i, 0)),
                  pl.BlockSpec((CMP_BLOCK, dh), lambda i: (0, 0)),
                  pl.BlockSpec((CMP_BLOCK, dh, dh), lambda i: (0, 0, 0))],
        out_specs=pl.BlockSpec((per_page, p, groups * dh), lambda i: (0, i, 0)),
        out_shape=jax.ShapeDtypeStruct((per_page, n_pool, groups * dh), F32),
        compiler_params=_cp("arbitrary"), name=name,
    )(_kv_rows(pool), pos_emb, w.astype(BF16))


def _masked_softmax_parts(parts, masks, axis=-1):
    sm = [jnp.where(m, s, NEG_INF) for s, m in zip(parts, masks)]
    mx = functools.reduce(jnp.maximum, [jnp.max(s, axis=axis, keepdims=True) for s in sm])
    ex = [jnp.exp(s - mx) for s in sm]
    den = functools.reduce(lambda a, b2: a + b2, [jnp.sum(e, axis=axis, keepdims=True) for e in ex])
    inv = 1.0 / den
    return [jnp.where(m, e * inv, 0.0) for e, m in zip(ex, masks)]


def _select_blocks(score, nsel, axis=-1):
    axis = axis % score.ndim
    nbs = score.shape[axis]
    pos = lax.broadcasted_iota(I32, score.shape, axis)
    sel = jnp.zeros(score.shape, jnp.bool_)
    picks = []
    for _ in range(nsel):
        m = jnp.max(score, axis=axis, keepdims=True)
        idx = jnp.min(jnp.where(score == m, pos, nbs), axis=axis, keepdims=True)
        hit = pos == idx
        sel = jnp.logical_or(sel, hit)
        score = jnp.where(hit, -jnp.inf, score)
        picks.append(idx)
    return sel, picks


def _nsa_prompt_kernel(q_ref, ck_ref, cv_ref, ks_ref, vs_ref, kw_ref, vw_ref, gt_ref, o_ref, osel_ref):
    i = pl.program_id(1)
    qb = q_ref.shape[0]
    t = ks_ref.shape[0]
    nbs = t // SEL_BLOCK
    hg, dh = HEADS_PER_GROUP, HEAD_DIM
    kvw = NSA_KV_HEADS * dh
    scale = dh ** -0.5
    pos1 = i * qb + lax.broadcasted_iota(I32, (qb, 1), 0)
    pos = jnp.concatenate([pos1] * hg, axis=0)
    blk = lax.broadcasted_iota(I32, (nbs, 1), 0)
    pos_row = i * qb + lax.broadcasted_iota(I32, (1, hg * qb), 1) % qb
    pos1_row = pos_row[:, :qb]
    vis_e = blk * SEL_BLOCK + (CMP_BLOCK - 1) <= pos_row
    vis_o = blk * SEL_BLOCK + (SEL_BLOCK - 1) <= pos_row
    valid = blk * SEL_BLOCK <= pos1_row
    forced = jnp.logical_or(blk == 0, blk == pos1_row // SEL_BLOCK)
    n_case = 4 if t % (4 * qb) == 0 else 1
    span = t // n_case
    case = (i * qb) // span

    def selected_attend(limit, q, sel_f, cs):
        key = lax.broadcasted_iota(I32, (1, limit), 1)
        expand = (lax.broadcasted_iota(I32, (nbs, limit), 1) // SEL_BLOCK
                  == lax.broadcasted_iota(I32, (nbs, limit), 0)).astype(BF16)
        selk = _tn_dot(sel_f, expand)
        smask = jnp.logical_and(jnp.concatenate([selk] * hg, axis=0) > 0.5, key <= pos)
        (p_s,) = _masked_softmax_parts([_nt_dot(q, ks_ref[0:limit, cs].astype(BF16)) * scale], [smask])
        osel_ref[...] = jnp.dot(p_s.astype(BF16), vs_ref[0:limit, cs].astype(BF16), preferred_element_type=F32)

    wlen = WINDOW + qb
    wstart = pl.multiple_of(jnp.maximum(i * qb - WINDOW, 0), qb)
    wkey = wstart + lax.broadcasted_iota(I32, (1, wlen), 1)
    wdist = pos - wkey
    wmask = jnp.logical_and(wdist >= 0, wdist <= WINDOW)
    gates = gt_ref[...]

    for g in range(NSA_KV_HEADS):
        cs = slice(g * dh, (g + 1) * dh)
        q = jnp.concatenate([q_ref[:, (g * hg + hh) * dh:(g * hg + hh + 1) * dh] for hh in range(hg)],
                            axis=0).astype(BF16)
        co = slice(kvw + g * dh, kvw + (g + 1) * dh)
        ck_e, ck_o = ck_ref[:, cs].astype(BF16), ck_ref[:, co].astype(BF16)
        cv_e, cv_o = cv_ref[:, cs].astype(BF16), cv_ref[:, co].astype(BF16)
        p_e, p_o = _masked_softmax_parts([_nt_dot(ck_e, q) * scale, _nt_dot(ck_o, q) * scale], [vis_e, vis_o],
                                         axis=0)
        o_cmp = _tn_dot(p_e.astype(BF16), cv_e) + _tn_dot(p_o.astype(BF16), cv_o)
        psum = p_e + p_o
        imp = functools.reduce(lambda a, b2: a + b2, [psum[:, hh * qb:(hh + 1) * qb] for hh in range(hg)])
        score = jnp.where(forced, FORCE, jnp.where(valid, imp, -FORCE))
        sel, _ = _select_blocks(score, min(SEL_TOPK, nbs), axis=0)
        sel_f = jnp.where(sel, 1.0, 0.0).astype(BF16)
        for c in range(n_case):
            pl.when(case == c)(functools.partial(selected_attend, (c + 1) * span, q, sel_f, cs))
        o_sel = osel_ref[...]
        kwin = kw_ref[pl.ds(wstart, wlen), cs].astype(BF16)
        vwin = vw_ref[pl.ds(wstart, wlen), cs].astype(BF16)
        (p_w,) = _masked_softmax_parts([_nt_dot(q, kwin) * scale], [wmask])
        o_win = jnp.dot(p_w.astype(BF16), vwin, preferred_element_type=F32)
        for hh in range(hg):
            hd = g * hg + hh
            rs = slice(hh * qb, (hh + 1) * qb)
            o_ref[:, hd * dh:(hd + 1) * dh] = (gates[:, 3 * hd:3 * hd + 1] * o_cmp[rs]
                                                 + gates[:, 3 * hd + 1:3 * hd + 2] * o_sel[rs]
                                                 + gates[:, 3 * hd + 2:3 * hd + 3] * o_win[rs]).astype(o_ref.dtype)


def _nsa_prompt(qn3, ck3, cv3, ksn3, z3, kwn3, gates3):
    b, t, _ = qn3.shape
    qb = Q_BLOCK
    kvw = NSA_KV_HEADS * HEAD_DIM
    assert t % qb == 0 and t >= WINDOW + qb and t % (2 * SEL_BLOCK) == 0
    nbs = ck3.shape[1]
    full = lambda cb: pl.BlockSpec((None, t, kvw), lambda bi, i: (bi, 0, cb))
    return pl.pallas_call(
        _nsa_prompt_kernel,
        grid=(b, t // qb),
        in_specs=[pl.BlockSpec((None, qb, NSA_HEADS * HEAD_DIM), lambda bi, i: (bi, i, 0)),
                  pl.BlockSpec((None, nbs, CMP_PER_SEL * kvw), lambda bi, i: (bi, 0, 0)),
                  pl.BlockSpec((None, nbs, CMP_PER_SEL * kvw), lambda bi, i: (bi, 0, 0)),
                  full(0), full(C_VS // kvw), full(0), full(C_VW // kvw),
                  pl.BlockSpec((None, qb, LANES), lambda bi, i: (bi, i, 0))],
        out_specs=pl.BlockSpec((None, qb, NSA_HEADS * HEAD_DIM), lambda bi, i: (bi, i, 0)),
        out_shape=jax.ShapeDtypeStruct((b, t, NSA_HEADS * HEAD_DIM), BF16),
        scratch_shapes=[pltpu.VMEM((HEADS_PER_GROUP * qb, HEAD_DIM), F32)],
        compiler_params=_cp("arbitrary", "arbitrary"), name="nsa_prompt",
    )(qn3, ck3, cv3, ksn3, z3, kwn3, z3, gates3)


def _decode_q16(q_ref):
    dh = HEAD_DIM
    rows = [q_ref[:, h * dh:(h + 1) * dh] for h in range(NSA_HEADS)]
    return jnp.concatenate(rows + [jnp.zeros((16 - NSA_HEADS, dh), F32)], axis=0).astype(BF16)


def _nsa_decode_select_kernel(q_ref, ck_ref, cv_ref, ocmp_ref, idx_ref, *, past):
    for s in range(q_ref.shape[0]):
        _nsa_decode_select_one(q_ref.at[s], ck_ref.at[s], cv_ref.at[s], ocmp_ref.at[s], idx_ref.at[s], past)


def _nsa_decode_select_one(q_ref, ck_ref, cv_ref, ocmp_ref, idx_ref, past):
    dh, hg = HEAD_DIM, HEADS_PER_GROUP
    kvw = NSA_KV_HEADS * dh
    nhalf = ck_ref.shape[0]
    nbs = past // SEL_BLOCK + 1
    scale = dh ** -0.5
    q16 = _decode_q16(q_ref)
    row = lax.broadcasted_iota(I32, (16, 1), 0)
    blk = lax.broadcasted_iota(I32, (1, LANES), 1)
    blk_h = lax.broadcasted_iota(I32, (1, nhalf), 1)
    vis_e = blk_h * SEL_BLOCK + (CMP_BLOCK - 1) <= past
    vis_o = blk_h * SEL_BLOCK + (SEL_BLOCK - 1) <= past
    o_all = jnp.zeros((16, dh), F32)
    idx_row = jnp.zeros((1, LANES), I32)
    for g in range(NSA_KV_HEADS):
        cs = slice(g * dh, (g + 1) * dh)
        co = slice(kvw + g * dh, kvw + (g + 1) * dh)
        ck_e, ck_o = ck_ref[:, cs].astype(BF16), ck_ref[:, co].astype(BF16)
        cv_e, cv_o = cv_ref[:, cs].astype(BF16), cv_ref[:, co].astype(BF16)
        p_e, p_o = _masked_softmax_parts([_nt_dot(q16, ck_e) * scale, _nt_dot(q16, ck_o) * scale], [vis_e, vis_o])
        o_g = (jnp.dot(p_e.astype(BF16), cv_e, preferred_element_type=F32)
               + jnp.dot(p_o.astype(BF16), cv_o, preferred_element_type=F32))
        in_g = jnp.logical_and(row >= g * hg, row < (g + 1) * hg)
        o_all = jnp.where(in_g, o_g, o_all)
        imp = jnp.sum(jnp.where(in_g, p_e + p_o, 0.0), axis=0, keepdims=True)
        score = jnp.where(blk_h == 0, FORCE, imp)
        nsel = min(SEL_TOPK, nbs)
        _, past_picks = _select_blocks(score, min(nsel - 1, nhalf))
        picks = past_picks[:1] + [jnp.full((1, 1), nhalf, I32)] + past_picks[1:]
        for kk, p in enumerate(picks):
            idx_row = jnp.where(blk == g * SEL_TOPK + kk, p, idx_row)
        for kk in range(len(picks), SEL_TOPK):
            idx_row = jnp.where(blk == g * SEL_TOPK + kk, -1, idx_row)
    ocmp_ref[...] = o_all
    idx_ref[...] = idx_row


def _nsa_decode_select(qn3, ck3, cv3, past):
    b = qn3.shape[0]
    nhalf = ck3.shape[1]
    kvw = NSA_KV_HEADS * HEAD_DIM
    assert nhalf % 8 == 0 and nhalf == past // SEL_BLOCK
    sb = 8 if b % 8 == 0 else 1
    return pl.pallas_call(
        functools.partial(_nsa_decode_select_kernel, past=past),
        grid=(b // sb,),
        in_specs=[pl.BlockSpec((sb, 1, NSA_HEADS * HEAD_DIM), lambda bi: (bi, 0, 0)),
                  pl.BlockSpec((sb, nhalf, CMP_PER_SEL * kvw), lambda bi: (bi, 0, 0)),
                  pl.BlockSpec((sb, nhalf, CMP_PER_SEL * kvw), lambda bi: (bi, 0, 0))],
        out_specs=[pl.BlockSpec((sb, 16, HEAD_DIM), lambda bi: (bi, 0, 0)),
                   pl.BlockSpec((sb, 1, LANES), lambda bi: (bi, 0, 0))],
        out_shape=[jax.ShapeDtypeStruct((b, 16, HEAD_DIM), F32),
                   jax.ShapeDtypeStruct((b, 1, LANES), I32)],
        compiler_params=_cp("arbitrary"), name="nsa_decode_select",
    )(qn3, ck3, cv3)


def _nsa_decode_attend_kernel(*refs, past, nsel):
    n_blk = NSA_KV_HEADS * nsel
    rows_ref = refs[0]
    kb_refs = refs[1:1 + n_blk]
    vb_refs = refs[1 + n_blk:1 + 2 * n_blk]
    (q_ref, idx_ref, ksn_ref, vs_ref, wk_ref, wv_ref, kwn_ref, vw_ref, ocmp_ref, gt_ref, o_ref) = refs[1 + 2 * n_blk:]
    del rows_ref
    dh, hg = HEAD_DIM, HEADS_PER_GROUP
    scale = dh ** -0.5
    self_blk = past // SEL_BLOCK
    q16 = _decode_q16(q_ref)
    q16f = q16.astype(F32)
    row = lax.broadcasted_iota(I32, (16, 1), 0)
    lane = lax.broadcasted_iota(I32, (1, LANES), 1)
    idx_row = idx_ref[...]
    o_sel = jnp.zeros((16, dh), F32)
    o_win = jnp.zeros((16, dh), F32)
    for g in range(NSA_KV_HEADS):
        cs = slice(g * dh, (g + 1) * dh)
        in_g = jnp.logical_and(row >= g * hg, row < (g + 1) * hg)
        k_self = ksn_ref[:, cs].astype(BF16).astype(F32)
        v_self = vs_ref[:, cs].astype(BF16).astype(F32)
        s_self = jnp.sum(q16f * k_self, axis=-1, keepdims=True) * scale
        parts, masks, vals = [], [], []
        self_sel = jnp.zeros((1, 1), jnp.bool_)
        for kk in range(nsel):
            pick = jnp.max(jnp.where(lane == g * SEL_TOPK + kk, idx_row, -1), axis=-1, keepdims=True)
            in_pool = jnp.logical_and(pick >= 0, pick < self_blk)
            self_sel = jnp.logical_or(self_sel, pick == self_blk)
            grp = pl.ds(g, SEL_BLOCK, stride=NSA_KV_HEADS)
            kb = kb_refs[g * nsel + kk][grp, :].astype(BF16)
            parts.append(_nt_dot(q16, kb) * scale)
            masks.append(in_pool)
            vals.append(vb_refs[g * nsel + kk][grp, :].astype(BF16))
        parts.append(s_self)
        masks.append(self_sel)
        probs = _masked_softmax_parts(parts, masks)
        acc = probs[-1].astype(BF16).astype(F32) * v_self
        for p, vv in zip(probs[:-1], vals):
            acc = acc + jnp.dot(p.astype(BF16), vv, preferred_element_type=F32)
        o_sel = jnp.where(in_g, acc, o_sel)
        kw_self = kwn_ref[:, cs].astype(BF16).astype(F32)
        vw_self = vw_ref[:, cs].astype(BF16).astype(F32)
        sw_self = jnp.sum(q16f * kw_self, axis=-1, keepdims=True) * scale
        wgrp = pl.ds(g, wk_ref.shape[0] // NSA_KV_HEADS, stride=NSA_KV_HEADS)
        wparts = [_nt_dot(q16, wk_ref[wgrp, :].astype(BF16)) * scale, sw_self]
        wmasks = [lane[:, :1] >= 0, lane[:, :1] >= 0]
        pw, pws = _masked_softmax_parts(wparts, wmasks)
        accw = (jnp.dot(pw.astype(BF16), wv_ref[wgrp, :].astype(BF16), preferred_element_type=F32)
                + pws.astype(BF16).astype(F32) * vw_self)
        o_win = jnp.where(in_g, accw, o_win)
    gates = gt_ref[...]
    o_cmp = ocmp_ref[...]
    for hd in range(NSA_HEADS):
        o_ref[:, hd * dh:(hd + 1) * dh] = (gates[:, 3 * hd:3 * hd + 1] * o_cmp[hd:hd + 1]
                                             + gates[:, 3 * hd + 1:3 * hd + 2] * o_sel[hd:hd + 1]
                                             + gates[:, 3 * hd + 2:3 * hd + 3] * o_win[hd:hd + 1]).astype(o_ref.dtype)


def _nsa_decode_attend(pool_rows, sel_k_pool, sel_v_pool, qn3, idx3, ksn3, z3, win_k, win_v, kwn3, ocmp, gates3,
                       past, nsel):
    b = qn3.shape[0]
    kvw = NSA_KV_HEADS * HEAD_DIM
    n_blk = NSA_KV_HEADS * nsel
    nbuf = win_k.shape[2]

    def blk_spec(tt):
        return pl.BlockSpec((SEL_BLOCK * NSA_KV_HEADS, HEAD_DIM), lambda bi, r: (r[bi * n_blk + tt], 0))

    def row_spec(cb, w=kvw):
        return pl.BlockSpec((None, 1, w), lambda bi, r: (bi, 0, cb))

    wspec = pl.BlockSpec((nbuf * NSA_KV_HEADS, HEAD_DIM), lambda bi, r: (bi, 0))
    sel_k_pool, sel_v_pool, win_k, win_v = (_kv_rows(a) for a in (sel_k_pool, sel_v_pool, win_k, win_v))
    in_specs = ([blk_spec(tt) for tt in range(n_blk)] + [blk_spec(tt) for tt in range(n_blk)]
                + [pl.BlockSpec((None, 1, NSA_HEADS * HEAD_DIM), lambda bi, r: (bi, 0, 0)),
                   pl.BlockSpec((None, 1, LANES), lambda bi, r: (bi, 0, 0)),
                   row_spec(0), row_spec(C_VS // kvw),
                   wspec, wspec,
                   row_spec(0), row_spec(C_VW // kvw),
                   pl.BlockSpec((None, 16, HEAD_DIM), lambda bi, r: (bi, 0, 0)),
                   pl.BlockSpec((None, 1, LANES), lambda bi, r: (bi, 0, 0))])
    return pl.pallas_call(
        functools.partial(_nsa_decode_attend_kernel, past=past, nsel=nsel),
        grid_spec=pltpu.PrefetchScalarGridSpec(
            num_scalar_prefetch=1, grid=(b,), in_specs=in_specs,
            out_specs=pl.BlockSpec((None, 1, NSA_HEADS * HEAD_DIM), lambda bi, r: (bi, 0, 0))),
        out_shape=jax.ShapeDtypeStruct((b, 1, NSA_HEADS * HEAD_DIM), BF16),
        compiler_params=_cp("arbitrary"), name="nsa_decode_attend",
    )(pool_rows, *([sel_k_pool] * n_blk), *([sel_v_pool] * n_blk),
      qn3, idx3, ksn3, z3, win_k, win_v, kwn3, z3, ocmp, gates3)


def _mem_attn_kernel(*refs, cache_layout):
    dh = MEM_HEAD_DIM
    if cache_layout:
        q_ref, g_ref, mk0_ref, mk1_ref, mv0_ref, mv1_ref, o_ref = refs
    else:
        q_ref, g_ref, mk_ref, mv_ref, o_ref = refs
    tq = q_ref.shape[0]
    rows = max(tq, 16)
    for h in range(MEM_HEADS):
        cs = slice(h * dh, (h + 1) * dh)
        x = q_ref[:, cs]
        q = x * lax.rsqrt(jnp.mean(x * x, axis=-1, keepdims=True) + EPS) * g_ref[...]
        if rows != tq:
            q = jnp.broadcast_to(q, (rows, dh))
        if cache_layout:
            mk = jnp.concatenate([mk0_ref[:, h, :], mk1_ref[:, h, :]], axis=1)
            mv = jnp.concatenate([mv0_ref[:, h, :], mv1_ref[:, h, :]], axis=1)
        else:
            mk, mv = mk_ref[:, cs], mv_ref[:, cs]
        s = _nt_dot(q.astype(BF16), mk.astype(BF16)) * (dh ** -0.5)
        m = jnp.max(s, axis=-1, keepdims=True)
        e = jnp.exp(s - m)
        p = e / jnp.sum(e, axis=-1, keepdims=True)
        o = jnp.dot(p.astype(BF16), mv.astype(BF16), preferred_element_type=F32)
        o_ref[:, cs] = o[0:tq].astype(o_ref.dtype)


def _mem_attention(z3, gq, mk, mv, cache_layout=False):
    b, t, _ = z3.shape
    w = MEM_HEADS * MEM_HEAD_DIM
    tq = min(t, 512)
    assert t % tq == 0
    if cache_layout:
        m = mk.shape[2]
        assert MEM_HEAD_DIM == 2 * LANES
        half = lambda c: pl.BlockSpec((None, None, m, MEM_HEADS, LANES), lambda bi, i: (0, bi, 0, 0, c))
        kv_specs, kv_args = [half(0), half(1), half(0), half(1)], [mk, mk, mv, mv]
    else:
        m = mk.shape[1]
        kv_specs, kv_args = [pl.BlockSpec((None, m, w), lambda bi, i: (bi, 0, 0))] * 2, [mk, mv]
    return pl.pallas_call(
        functools.partial(_mem_attn_kernel, cache_layout=cache_layout),
        grid=(b, t // tq),
        in_specs=[pl.BlockSpec((None, tq, w), lambda bi, i: (bi, i, C_MQ // w)),
                  pl.BlockSpec((1, MEM_HEAD_DIM), lambda bi, i: (0, 0))] + kv_specs,
        out_specs=pl.BlockSpec((None, tq, w), lambda bi, i: (bi, i, 0)),
        out_shape=jax.ShapeDtypeStruct((b, t, w), BF16),
        compiler_params=_cp("arbitrary", "arbitrary"), name="mem_attention",
    )(z3, gq.reshape(1, MEM_HEAD_DIM), *kv_args)


def _sigmoid_cols_kernel(x_ref, o_ref):
    o_ref[...] = jax.nn.sigmoid(x_ref[...])


def _nsa_gates(z, col):
    n = z.shape[0]
    tm = min(n, 1024)
    return pl.pallas_call(
        _sigmoid_cols_kernel, grid=(n // tm,),
        in_specs=[pl.BlockSpec((tm, LANES), lambda i: (i, col // LANES))],
        out_specs=pl.BlockSpec((tm, LANES), lambda i: (i, 0)),
        out_shape=jax.ShapeDtypeStruct((n, LANES), F32),
        compiler_params=_cp("arbitrary"), name="nsa_gates",
    )(z)


def _merge_kernel(r_ref, n_ref, m_ref, g0_ref, g1_ref, g2_ref, w_ref, o_ref):
    acc = jnp.zeros(o_ref.shape, F32)
    for c, (b_ref, g_ref) in enumerate(((r_ref, g0_ref), (n_ref, g1_ref), (m_ref, g2_ref))):
        up = jnp.dot(b_ref[...].astype(BF16), w_ref[c], preferred_element_type=F32)
        acc = acc + jax.nn.sigmoid(g_ref[...]) * up
    o_ref[...] = acc.astype(BF16)


def _merge(o_ret, o_nsa, o_mem, z, wb, d_model):
    n, bw = o_ret.shape
    tm = min(n, 1024)
    tn = 512
    gb = C_MG // tn
    nj = d_model // tn
    bspec = pl.BlockSpec((tm, bw), lambda i, j: (i, 0))
    gspec = lambda c: pl.BlockSpec((tm, tn), lambda i, j: (i, gb + c * nj + j))
    return pl.pallas_call(
        _merge_kernel, grid=(n // tm, nj),
        in_specs=[bspec, bspec, bspec, gspec(0), gspec(1), gspec(2),
                  pl.BlockSpec((N_BRANCH, bw, tn), lambda i, j: (0, 0, j))],
        out_specs=pl.BlockSpec((tm, tn), lambda i, j: (i, j)),
        out_shape=jax.ShapeDtypeStruct((n, d_model), BF16),
        compiler_params=_cp("arbitrary", "arbitrary"), name="branch_merge",
    )(o_ret, o_nsa, o_mem, z, z, z, wb)


def _topk_rows(s, kk):
    e, tl = s.shape
    nt = e // 8
    assert e % 8 == 0 and nt & (nt - 1) == 0
    sub = lax.broadcasted_iota(I32, (8, tl), 0).astype(F32)
    vals = [s[8 * k:8 * k + 8] for k in range(nt)]
    idxs = [sub + float(8 * k) for k in range(nt)]
    for a, b in _sorting_network(nt):
        va, vb, ia, ib = vals[a], vals[b], idxs[a], idxs[b]
        swap = jnp.logical_or(vb > va, jnp.logical_and(vb == va, ib < ia))
        vals[a], vals[b] = jnp.maximum(va, vb), jnp.minimum(va, vb)
        idxs[a], idxs[b] = jnp.where(swap, ib, ia), jnp.where(swap, ia, ib)
    out_v, out_i = [], []
    for step in range(kk):
        m = jnp.max(vals[0], axis=0, keepdims=True)
        i = jnp.min(jnp.where(vals[0] == m, idxs[0], float(e)), axis=0, keepdims=True)
        out_v.append(m)
        out_i.append(i)
        win = idxs[0] == i
        depth = min(nt, kk - step)
        for k in range(depth - 1):
            vals[k] = jnp.where(win, vals[k + 1], vals[k])
            idxs[k] = jnp.where(win, idxs[k + 1], idxs[k])
        vals[depth - 1] = jnp.where(win, -jnp.inf, vals[depth - 1])
    return jnp.concatenate(out_v, axis=0), jnp.concatenate(out_i, axis=0)


def _sorting_network(n):
    def merge(lo, hi, r):
        step = r * 2
        if step < hi - lo:
            yield from merge(lo, hi, step)
            yield from merge(lo + r, hi, step)
            yield from [(i, i + r) for i in range(lo + r, hi - r, step)]
        else:
            yield (lo, lo + r)

    def sort(lo, hi):
        if hi - lo >= 1:
            mid = lo + (hi - lo) // 2
            yield from sort(lo, mid)
            yield from sort(mid + 1, hi)
            yield from merge(lo, hi, 1)

    return list(sort(0, n - 1))


def _pair_pieces(kk):
    pieces, cur = [], []

    def flush(rows):
        pieces.append(rows + [None] * (8 - len(rows)))

    for a in range(kk):
        grp = [(a, b) for b in range(kk // (a + 1))]
        if cur and len(cur) + len(grp) > 8:
            flush(cur)
            cur = []
        cur = cur + grp
        while len(cur) >= 8:
            flush(cur[:8])
            cur = cur[8:]
    if cur:
        flush(cur)
    return pieces


def _rows_by_runs(x, ids):
    if ids[0] % 8 == 0 and ids == list(range(ids[0], ids[0] + 8)):
        return x[ids[0]:ids[0] + 8]
    r = lax.broadcasted_iota(I32, (8, 1), 0)
    runs = []
    for p, i in enumerate(ids):
        if not runs or runs[-1][1] != i:
            runs.append((p, i))
    out = jnp.broadcast_to(x[runs[-1][1]:runs[-1][1] + 1], (8, x.shape[1]))
    for (_, i), (nxt, _) in zip(reversed(runs[:-1]), reversed(runs[1:])):
        out = jnp.where(r < nxt, x[i:i + 1], out)
    return out


ROUTE_HEADS_PER_STEP = 4


def _peer_route_kernel(q_ref, sk_ref, i_ref, j_ref, g_ref):
    kk = PEER_TOPK
    for hh in range(ROUTE_HEADS_PER_STEP):
        rows = slice(hh * kk, (hh + 1) * kk)
        i_ref[rows, :], j_ref[rows, :], g_ref[rows, :] = _peer_route_head(
            q_ref[:, hh * PEER_DKEY:(hh + 1) * PEER_DKEY], sk_ref[hh])


def _peer_route_head(q, sk):
    kk = PEER_TOPK
    st = _nt_dot(sk, q.astype(BF16))
    v0, i0 = _topk_rows(st[0:PEER_KEYS], kk)
    v1, i1 = _topk_rows(st[PEER_KEYS:2 * PEER_KEYS], kk)
    cand, ci, cj = [], [], []
    for piece in _pair_pieces(kk):
        ra = [p[0] if p else 0 for p in piece]
        rb = [p[1] if p else 0 for p in piece]
        live = lax.broadcasted_iota(I32, (8, 1), 0) < sum(p is not None for p in piece)
        cand.append(jnp.where(live, _rows_by_runs(v0, ra) + _rows_by_runs(v1, rb), -jnp.inf))
        ci.append(_rows_by_runs(i0, ra))
        cj.append(_rows_by_runs(i1, rb))
    cand, ci, cj = (jnp.concatenate(c, axis=0) for c in (cand, ci, cj))
    ridx = lax.broadcasted_iota(I32, cand.shape, 0).astype(F32)
    sc, si, sj = [], [], []
    for _ in range(kk):
        m = jnp.max(cand, axis=0, keepdims=True)
        r = jnp.min(jnp.where(cand == m, ridx, float(cand.shape[0])), axis=0, keepdims=True)
        hit = ridx == r
        sc.append(m)
        si.append(jnp.max(jnp.where(hit, ci, -1.0), axis=0, keepdims=True))
        sj.append(jnp.max(jnp.where(hit, cj, -1.0), axis=0, keepdims=True))
        cand = jnp.where(hit, -jnp.inf, cand)
    sc = jnp.concatenate(sc, axis=0)
    e = jnp.exp(sc - jnp.max(sc, axis=0, keepdims=True))
    return (jnp.concatenate(si, axis=0).astype(I32), jnp.concatenate(sj, axis=0).astype(I32),
            e / jnp.sum(e, axis=0, keepdims=True))


def _peer_route(q, skbd):
    n = q.shape[0]
    tl = min(n, LANES)
    kk = PEER_TOPK
    slots = PEER_HEADS * kk
    hps = ROUTE_HEADS_PER_STEP
    ospec = pl.BlockSpec((hps * kk, tl), lambda i, h: (h, i))
    return pl.pallas_call(
        _peer_route_kernel, grid=(n // tl, PEER_HEADS // hps),
        in_specs=[pl.BlockSpec((tl, hps * PEER_DKEY), lambda i, h: (i, h)),
                  pl.BlockSpec((hps, 2 * PEER_KEYS, PEER_DKEY), lambda i, h: (h, 0, 0))],
        out_specs=[ospec, ospec, ospec],
        out_shape=[jax.ShapeDtypeStruct((slots, n), I32), jax.ShapeDtypeStruct((slots, n), I32),
                   jax.ShapeDtypeStruct((slots, n), F32)],
        compiler_params=_cp("arbitrary", "arbitrary"), name="peer_route",
    )(q, skbd)


TOKEN_GROUP = 8


def _peer_weights_kernel(i_ref, j_ref, g_ref, o_ref, it_scr, jt_scr, gt_scr):
    it_scr[...] = i_ref[...].T
    jt_scr[...] = j_ref[...].T
    gt_scr[...] = g_ref[...].T
    key = lax.broadcasted_iota(I32, (PEER_KEYS, i_ref.shape[0]), 0)

    def one_hots(n):
        irow = it_scr[pl.ds(n, 1), :]
        jrow = jt_scr[pl.ds(n, 1), :]
        grow = gt_scr[pl.ds(n, 1), :]
        a = jnp.where(key == irow, grow, 0.0).astype(BF16)
        b = jnp.where(key == jrow, 1.0, 0.0).astype(BF16)
        return a, b

    groups_per_trip = 4 if o_ref.shape[0] % 4 == 0 else 1

    def body(nb2, carry):
        results = []
        for gi in range(groups_per_trip):
            nb = nb2 * groups_per_trip + gi
            for t in range(TOKEN_GROUP):
                a, b = one_hots(nb * TOKEN_GROUP + t)
                results.append((nb, t, _nt_dot(a, b)))
        for nb, t, r in results:
            o_ref[nb, pl.ds(t, PEER_KEYS, stride=TOKEN_GROUP), :] = r
        return carry

    lax.fori_loop(0, o_ref.shape[0] // groups_per_trip, body, 0)


def _peer_weights(it, jt, gt):
    slots, n = it.shape
    tb = min(n, LANES)
    assert tb % TOKEN_GROUP == 0
    ispec = pl.BlockSpec((slots, tb), lambda i: (0, i))
    out = pl.pallas_call(
        _peer_weights_kernel, grid=(n // tb,),
        in_specs=[ispec, ispec, ispec],
        out_specs=pl.BlockSpec((tb // TOKEN_GROUP, PEER_KEYS * TOKEN_GROUP, PEER_KEYS), lambda i: (i, 0, 0)),
        out_shape=jax.ShapeDtypeStruct((n // TOKEN_GROUP, PEER_KEYS * TOKEN_GROUP, PEER_KEYS), F32),
        scratch_shapes=[pltpu.VMEM((tb, slots), I32), pltpu.VMEM((tb, slots), I32), pltpu.VMEM((tb, slots), F32)],
        compiler_params=_cp("arbitrary"), name="peer_weights",
    )(it, jt, gt)
    return out.reshape(n // TOKEN_GROUP, PEER_KEYS, TOKEN_GROUP, PEER_KEYS)


def _peer_dense_kernel(h_ref, g_ref, u_ref, v_ref, x_ref, o_ref):
    @pl.when(pl.program_id(1) == 0)
    def _():
        o_ref[...] = x_ref[...]

    a = jnp.dot(h_ref[...], u_ref[...], preferred_element_type=F32)
    act = 0.5 * a * (1.0 + lax.erf(a * (2.0 ** -0.5)))
    tm = h_ref.shape[0]
    w = jnp.concatenate([(g_ref[:, ii].reshape(tm, PEER_KEYS)
                          * act[:, ii * PEER_KEYS:(ii + 1) * PEER_KEYS]).astype(BF16)
                         for ii in range(g_ref.shape[1])], axis=1)
    o_ref[...] += jnp.dot(w, v_ref[...], preferred_element_type=F32)


def _peer_dense(hn, gw, ut, vv, x):
    n, d = hn.shape
    ne = ut.shape[1]
    tm = min(n, 512)
    te = 1024
    return pl.pallas_call(
        _peer_dense_kernel, grid=(n // tm, ne // te),
        in_specs=[pl.BlockSpec((tm, d), lambda i, j: (i, 0)),
                  pl.BlockSpec((tm // TOKEN_GROUP, te // PEER_KEYS, TOKEN_GROUP, PEER_KEYS),
                               lambda i, j: (i, j, 0, 0)),
                  pl.BlockSpec((d, te), lambda i, j: (0, j)),
                  pl.BlockSpec((te, d), lambda i, j: (j, 0)),
                  pl.BlockSpec((tm, d), lambda i, j: (i, 0))],
        out_specs=pl.BlockSpec((tm, d), lambda i, j: (i, 0)),
        out_shape=jax.ShapeDtypeStruct((n, d), F32),
        compiler_params=_cp("arbitrary", "arbitrary"), name="peer_dense",
    )(hn, gw, ut, vv, x)


def _prepare_weights(norm_attn, w_in, cmp_pos_k, cmp_pos_v, cmp_w_k, cmp_w_v, w_mem_kv, w_branch, w_out,
                     peer_wq, peer_subkeys, peer_u, peer_v):
    d = w_in.shape[0]
    ng_w = NSA_HEADS * 3
    o_ng = 5632
    o_mq = o_ng + ng_w
    o_mg = o_mq + MEM_HEADS * MEM_HEAD_DIM
    width = C_MG + N_BRANCH * d + LANES
    width = -(-width // Z_TILE) * Z_TILE
    w_r = jnp.concatenate([w_in[:, 0:4096], w_in[:, o_mq:o_mg], w_in[:, 4096:o_ng], w_in[:, o_mg:],
                           w_in[:, o_ng:o_mq],
                           jnp.zeros((d, width - (C_MG + N_BRANCH * d + ng_w)), w_in.dtype)], axis=1).astype(BF16)
    pos_k, wk_big = _compress_weights(cmp_pos_k, cmp_w_k)
    pos_v, wv_big = _compress_weights(cmp_pos_v, cmp_w_v)
    half = PEER_DKEY // 2
    sk = peer_subkeys.astype(BF16)
    zk = jnp.zeros((PEER_HEADS, PEER_KEYS, half), BF16)
    skbd = jnp.concatenate([jnp.concatenate([sk[:, 0], zk], axis=2),
                            jnp.concatenate([zk, sk[:, 1]], axis=2)], axis=1)
    return dict(w_in=w_r, c_ng=C_MG + N_BRANCH * d, pos_k=pos_k, wk_big=wk_big, pos_v=pos_v, wv_big=wv_big,
                w_mem_kv=w_mem_kv.astype(BF16), w_branch=w_branch.astype(BF16), w_out=w_out.astype(BF16),
                peer_wq=peer_wq.astype(BF16), skbd=skbd, ut=peer_u.astype(BF16).T, vv=peer_v.astype(BF16))


def _layer_common(x, seq, pos, norm_attn, nsa_q_norm, nsa_k_norm, pw):
    z = _prologue_matmul(x, norm_attn, pw["w_in"], mode="norm", name="in_proj", tm=1024)
    nsa_tabs = _rope_tables(pos, ROPE_DIMS, ROPE_THETA, HEAD_DIM)
    half = ROPE_DIMS // 2
    kvw = NSA_KV_HEADS * HEAD_DIM
    qn = _headnorm(z, C_NQ, NSA_HEADS * HEAD_DIM, nsa_q_norm, HEAD_DIM, seq, nsa_tabs, half)
    kcn = _headnorm(z, C_KC, kvw, nsa_k_norm[0], HEAD_DIM, seq, nsa_tabs, half)
    ksn = _headnorm(z, C_KS, kvw, nsa_k_norm[1], HEAD_DIM, seq, nsa_tabs, half)
    kwn = _headnorm(z, C_KW, kvw, nsa_k_norm[2], HEAD_DIM, seq, nsa_tabs, half)
    gates = _nsa_gates(z, pw["c_ng"])
    ret_tabs = _rope_tables(pos, RET_DK, RET_THETA, RET_DK)
    return z, qn, kcn, ksn, kwn, gates, ret_tabs


def _layer_tail(x, z, o_ret, o_nsa, o_mem, norm_ffn, pw):
    d = x.shape[1]
    merged = _merge(o_ret, o_nsa, o_mem, z, pw["w_branch"], d)
    x1 = _prologue_matmul(merged, jnp.ones((d,), F32), pw["w_out"], mode="none", residual=x, name="out_proj")
    q, hn = _prologue_matmul(x1, norm_ffn, pw["peer_wq"], mode="norm", emit_h=True, name="peer_query")
    it, jt, gt = _peer_route(q, pw["skbd"])
    return _peer_dense(hn, _peer_weights(it, jt, gt), pw["ut"], pw["vv"], x1)


def kernel(x_prompt, x_sample, cache_cmp_k, cache_cmp_v, cache_sel_k, cache_sel_v, cache_win_k, cache_win_v,
           state_ret, cache_mem_k, cache_mem_v, page_table, mem_prompt, norm_attn, w_in, ret_gn, nsa_q_norm,
           nsa_k_norm, cmp_pos_k, cmp_pos_v, cmp_w_k, cmp_w_v, norm_mem, w_mem_kv, mem_q_norm, mem_k_norm,
           w_branch, w_out, norm_ffn, peer_wq, peer_subkeys, peer_u, peer_v):
    depth = w_in.shape[0]
    assert depth == 1
    l = 0
    bp, t, d = x_prompt.shape
    bs, ts, _ = x_sample.shape
    assert ts == 1
    n_pool, page, g_kv, dh = cache_cmp_k.shape[1:]
    n_pages = page_table.shape[1]
    past = n_pages * page
    kvw = g_kv * dh
    memw = MEM_HEADS * MEM_HEAD_DIM
    pw = _prepare_weights(norm_attn[l], w_in[l], cmp_pos_k[l], cmp_pos_v[l], cmp_w_k[l], cmp_w_v[l], w_mem_kv[l],
                          w_branch[l], w_out[l], peer_wq[l], peer_subkeys[l], peer_u[l], peer_v[l])
    cmp_cols = CMP_BLOCK * kvw

    n_p = bp * t
    xp = x_prompt.reshape(n_p, d)
    m_tok = mem_prompt.shape[1]
    mkv = _prologue_matmul(mem_prompt.reshape(bp * m_tok, d), norm_mem[l], pw["w_mem_kv"], mode="norm",
                           name="mem_kv_proj")
    mk_p = _headnorm(mkv, 0, memw, mem_k_norm[l], MEM_HEAD_DIM, m_tok)
    mv_p = mkv[:, memw:]
    z, qn, kcn, ksn, kwn, gates, ret_tabs = _layer_common(xp, t, jnp.arange(t), norm_attn[l], nsa_q_norm[l],
                                                         nsa_k_norm[l], pw)
    z3 = z.reshape(bp, t, z.shape[1])
    o_ret, st_p = _retention_prompt(z3, ret_tabs, jnp.zeros((bp, RET_HEADS, RET_DK, RET_DV), F32), ret_gn[l])
    vc_p = z[:, C_VC:C_VC + kvw]
    nbc = t // CMP_BLOCK
    ck_p = _prologue_matmul(kcn.reshape(bp * nbc, cmp_cols), pw["pos_k"], pw["wk_big"], mode="bias", tm=256,
                            name="compress_k")
    cv_p = _prologue_matmul(vc_p.reshape(bp * nbc, cmp_cols), pw["pos_v"], pw["wv_big"], mode="bias", tm=256,
                            name="compress_v")
    pair = CMP_PER_SEL * kvw
    o_nsa = _nsa_prompt(qn.reshape(bp, t, -1), ck_p.reshape(bp, nbc // CMP_PER_SEL, pair),
                        cv_p.reshape(bp, nbc // CMP_PER_SEL, pair),
                        ksn.reshape(bp, t, kvw), z3, kwn.reshape(bp, t, kvw), gates.reshape(bp, t, LANES))
    o_mem = _mem_attention(z3, mem_q_norm[l], mk_p.reshape(bp, m_tok, memw), mv_p.reshape(bp, m_tok, memw))
    y_p = _layer_tail(xp, z, o_ret.reshape(n_p, -1), o_nsa.reshape(n_p, -1), o_mem.reshape(n_p, -1), norm_ffn[l], pw)

    nbuf_p = min(WINDOW, t)
    kv5 = lambda a, bb, tt: a.reshape(1, bb, tt, g_kv, dh)
    outs_p = (kv5(kcn, bp, t), kv5(vc_p, bp, t), kv5(ksn, bp, t), kv5(z[:, C_VS:C_VS + kvw], bp, t),
              kv5(kwn, bp, t)[:, :, t - nbuf_p:], kv5(z[:, C_VW:C_VW + kvw], bp, t)[:, :, t - nbuf_p:],
              st_p[None], mk_p.reshape(1, bp, m_tok, MEM_HEADS, MEM_HEAD_DIM),
              mv_p.reshape(1, bp, m_tok, MEM_HEADS, MEM_HEAD_DIM))

    xs = x_sample.reshape(bs, d)
    zs, qn_s, kcn_s, ksn_s, kwn_s, gates_s, ret_tabs_s = _layer_common(
        xs, 1, jnp.full((1,), past, I32), norm_attn[l], nsa_q_norm[l], nsa_k_norm[l], pw)
    zs3 = zs.reshape(bs, 1, zs.shape[1])
    o_ret_s, st_s = _retention_decode(zs3, ret_tabs_s, state_ret[l], ret_gn[l])
    per_page = page // CMP_BLOCK
    ckp = _compress_pool(cache_cmp_k, cmp_pos_k[l], cmp_w_k[l], "compress_pool_k")
    cvp = _compress_pool(cache_cmp_v, cmp_pos_v[l], cmp_w_v[l], "compress_pool_v")
    pages_of = lambda a: a[:, page_table].transpose(1, 2, 0, 3).reshape(bs, n_pages * per_page // CMP_PER_SEL, pair)
    ck_s, cv_s = pages_of(ckp), pages_of(cvp)
    qn_s3 = qn_s.reshape(bs, 1, -1)
    ocmp_s, idx_s = _nsa_decode_select(qn_s3, ck_s, cv_s, past)
    nbs = past // SEL_BLOCK + 1
    nsel = min(SEL_TOPK, nbs)
    per_page_sel = page // SEL_BLOCK
    picks = jnp.clip(idx_s[:, 0, :NSA_KV_HEADS * SEL_TOPK].reshape(bs, NSA_KV_HEADS, SEL_TOPK)[:, :, :nsel],
                     0, nbs - 2)
    pool_rows = (jnp.take_along_axis(page_table, (picks // per_page_sel).reshape(bs, -1), axis=1) * per_page_sel
                 + (picks % per_page_sel).reshape(bs, -1)).reshape(-1).astype(I32)
    o_nsa_s = _nsa_decode_attend(
        pool_rows, cache_sel_k, cache_sel_v, qn_s3, idx_s, ksn_s.reshape(bs, 1, kvw), zs3, cache_win_k, cache_win_v,
        kwn_s.reshape(bs, 1, kvw), ocmp_s, gates_s.reshape(bs, 1, LANES), past, nsel)
    o_mem_s = _mem_attention(zs3, mem_q_norm[l], cache_mem_k, cache_mem_v, cache_layout=True)
    y_s = _layer_tail(xs, zs, o_ret_s.reshape(bs, -1), o_nsa_s.reshape(bs, -1), o_mem_s.reshape(bs, -1),
                      norm_ffn[l], pw)

    nbuf_s = cache_win_k.shape[2]
    win_k_s = jnp.concatenate([cache_win_k[l], kv5(kwn_s, bs, 1)[0]], axis=1)[None, :, -nbuf_s:]
    win_v_s = jnp.concatenate([cache_win_v[l], kv5(zs[:, C_VW:C_VW + kvw], bs, 1)[0]], axis=1)[None, :, -nbuf_s:]
    outs_s = (kv5(kcn_s, bs, 1), kv5(zs[:, C_VC:C_VC + kvw], bs, 1), kv5(ksn_s, bs, 1),
              kv5(zs[:, C_VS:C_VS + kvw], bs, 1), win_k_s, win_v_s, st_s[None].astype(x_sample.dtype))
    return (y_p.reshape(bp, t, d), y_s.reshape(bs, ts, d)) + outs_p + outs_s
```

```python
import functools

import numpy as np
import jax
import jax.numpy as jnp
from jax import lax
from jax.experimental import pallas as pl
from jax.experimental.pallas import tpu as pltpu

F32 = jnp.float32
BF16 = jnp.bfloat16
I32 = jnp.int32

EPS = 1e-6
NEG_INF = -1e30
FORCE = 1e4

RET_HEADS, RET_DK, RET_DV, RET_CHUNK, RET_THETA = 4, 128, 256, 128, 10000.0
NSA_HEADS, NSA_KV_HEADS, HEAD_DIM = 8, 2, 128
HEADS_PER_GROUP = NSA_HEADS // NSA_KV_HEADS
CMP_BLOCK, SEL_BLOCK, SEL_TOPK, WINDOW, Q_BLOCK = 32, 64, 4, 256, 128
CMP_PER_SEL = SEL_BLOCK // CMP_BLOCK
ROPE_THETA, ROPE_DIMS = 500000.0, HEAD_DIM // 4
MEM_HEADS, MEM_HEAD_DIM = 4, 256
PEER_HEADS, PEER_KEYS, PEER_DKEY, PEER_TOPK = 8, 128, 128, 16
N_BRANCH = 3
LANES = 128

C_RQ, C_RK, C_RV, C_RG, C_NQ, C_MQ = 0, 512, 1024, 2048, 3072, 4096
C_KC, C_VC, C_KS, C_VS, C_KW, C_VW = 5120, 5376, 5632, 5888, 6144, 6400
C_MG = 6656
Z_TILE = 512


def _cp(*sem):
    return pltpu.CompilerParams(dimension_semantics=sem, vmem_limit_bytes=56 * 1024 * 1024)


def _nt_dot(a, b):
    return lax.dot_general(a, b, (((1,), (1,)), ((), ())), preferred_element_type=F32)


def _tn_dot(a, b):
    return lax.dot_general(a, b, (((0,), (0,)), ((), ())), preferred_element_type=F32)


def _pm_kernel(*refs, mode, residual, emit_h):
    x_ref, g_ref, w_ref = refs[:3]
    k = 3
    r_ref = None
    if residual:
        r_ref = refs[k]
        k += 1
    o_ref = refs[k]
    k += 1
    h_ref = None
    if emit_h:
        h_ref = refs[k]
        k += 1
    h_scr = refs[k]

    @pl.when(pl.program_id(1) == 0)
    def _():
        x = x_ref[...].astype(F32)
        if mode == "norm":
            h = x * lax.rsqrt(jnp.mean(x * x, axis=-1, keepdims=True) + EPS) * g_ref[...]
        elif mode == "bias":
            h = x + g_ref[...]
        else:
            h = x
        h_scr[...] = h.astype(BF16)
        if emit_h:
            h_ref[...] = h_scr[...]

    acc = jnp.dot(h_scr[...], w_ref[...], preferred_element_type=F32)
    if residual:
        acc = acc + r_ref[...]
    o_ref[...] = acc


def _prologue_matmul(x, g, w, *, mode, name, residual=None, emit_h=False, tm=512):
    n, kdim = x.shape
    wout = w.shape[1]
    tm = min(tm, n)
    tn = next(c for c in (1024, 512, 256, 128) if wout % c == 0)
    assert n % tm == 0 and wout % tn == 0
    in_specs = [pl.BlockSpec((tm, kdim), lambda i, j: (i, 0)),
                pl.BlockSpec((1, kdim), lambda i, j: (0, 0)),
                pl.BlockSpec((kdim, tn), lambda i, j: (0, j))]
    args = [x, g.reshape(1, kdim).astype(F32), w]
    if residual is not None:
        in_specs.append(pl.BlockSpec((tm, tn), lambda i, j: (i, j)))
        args.append(residual)
    out_shape = [jax.ShapeDtypeStruct((n, wout), F32)]
    out_specs = [pl.BlockSpec((tm, tn), lambda i, j: (i, j))]
    if emit_h:
        out_shape.append(jax.ShapeDtypeStruct((n, kdim), BF16))
        out_specs.append(pl.BlockSpec((tm, kdim), lambda i, j: (i, 0)))
    res = pl.pallas_call(
        functools.partial(_pm_kernel, mode=mode, residual=residual is not None, emit_h=emit_h),
        grid=(n // tm, wout // tn),
        in_specs=in_specs, out_specs=out_specs, out_shape=out_shape,
        scratch_shapes=[pltpu.VMEM((tm, kdim), BF16)],
        compiler_params=_cp("arbitrary", "arbitrary"), name=name,
    )(*args)
    return res if emit_h else res[0]


def _rope_tables(pos, rot_dims, theta, dh):
    half = rot_dims // 2
    inv = theta ** (-jnp.arange(half, dtype=F32) * 2.0 / rot_dims)
    ang = pos.astype(F32)[:, None] * inv[None, :]
    cos, sin = jnp.cos(ang), jnp.sin(ang)
    t = pos.shape[0]
    one = jnp.ones((t, dh - rot_dims), F32)
    zh = jnp.zeros((t, half), F32)
    zr = jnp.zeros((t, dh - rot_dims), F32)
    c = jnp.concatenate([cos, cos, one], axis=1)
    s1 = jnp.concatenate([-sin, zh, zr], axis=1)
    s2 = jnp.concatenate([zh, sin, zr], axis=1)
    return c, s1, s2


def _rope(y, c, s1, s2, half):
    dh = y.shape[-1]
    return y * c + pltpu.roll(y, dh - half, 1) * s1 + pltpu.roll(y, half, 1) * s2


def _headnorm_kernel(*refs, nh, dh, rope_half):
    if rope_half:
        x_ref, g_ref, c_ref, s1_ref, s2_ref, o_ref = refs
    else:
        x_ref, g_ref, o_ref = refs
    for h in range(nh):
        x = x_ref[:, h * dh:(h + 1) * dh]
        y = x * lax.rsqrt(jnp.mean(x * x, axis=-1, keepdims=True) + EPS) * g_ref[...]
        if rope_half:
            y = _rope(y, c_ref[...], s1_ref[...], s2_ref[...], rope_half)
        o_ref[:, h * dh:(h + 1) * dh] = y


def _headnorm(x, col, width, gain, dh, seq, tables=None, rope_half=0):
    n = x.shape[0]
    tm = min(512, seq) if seq > 1 else n
    assert n % tm == 0 and col % width == 0
    cb = col // width
    in_specs = [pl.BlockSpec((tm, width), lambda i: (i, cb)),
                pl.BlockSpec((1, dh), lambda i: (0, 0))]
    args = [x, gain.reshape(1, dh).astype(F32)]
    if rope_half:
        if seq > 1:
            nb = seq // tm
            tspec = pl.BlockSpec((tm, dh), lambda i: (i % nb, 0))
        else:
            tspec = pl.BlockSpec((1, dh), lambda i: (0, 0))
        in_specs += [tspec] * 3
        args += list(tables)
    return pl.pallas_call(
        functools.partial(_headnorm_kernel, nh=width // dh, dh=dh, rope_half=rope_half),
        grid=(n // tm,),
        in_specs=in_specs,
        out_specs=pl.BlockSpec((tm, width), lambda i: (i, 0)),
        out_shape=jax.ShapeDtypeStruct((n, width), F32),
        compiler_params=_cp("arbitrary"), name="head_norm",
    )(*args)


def _log_gamma():
    return np.log1p(-(np.float32(2.0) ** (-5.0 - np.arange(RET_HEADS, dtype=np.float32)))).astype(np.float32)


def _ret_finish(o, gn, gate):
    y = o * lax.rsqrt(jnp.mean(o * o, axis=-1, keepdims=True) + EPS) * gn
    return y * (gate * jax.nn.sigmoid(gate))


def _ret_prompt_kernel(q_ref, k_ref, v_ref, gt_ref, c_ref, s1_ref, s2_ref, st0_ref, gn_ref, o_ref, st_ref):
    cl = q_ref.shape[0]

    @pl.when(pl.program_id(1) == 0)
    def _():
        st_ref[...] = st0_ref[...]

    half = RET_DK // 2
    ri = lax.broadcasted_iota(I32, (cl, cl), 0).astype(F32)
    ci = lax.broadcasted_iota(I32, (cl, cl), 1).astype(F32)
    rel = ri - ci
    rowi = lax.broadcasted_iota(I32, (cl, 1), 0).astype(F32)
    for h, lg in enumerate(_log_gamma().tolist()):
        ks, vs = slice(h * RET_DK, (h + 1) * RET_DK), slice(h * RET_DV, (h + 1) * RET_DV)
        q = _rope(q_ref[:, ks], c_ref[...], s1_ref[...], s2_ref[...], half)
        k = _rope(k_ref[:, ks], c_ref[...], s1_ref[...], s2_ref[...], half) * (RET_DK ** -0.5)
        decay = jnp.where(rel >= 0, jnp.exp(jnp.maximum(rel, 0.0) * lg), 0.0)
        qb, kb, vb = q.astype(BF16), k.astype(BF16), v_ref[:, vs].astype(BF16)
        inner = _nt_dot(qb, kb) * decay
        state = st_ref[h]
        cross = jnp.exp((rowi + 1.0) * lg)
        o = jnp.dot(inner.astype(BF16), vb, preferred_element_type=F32)
        o = o + jnp.dot(qb, state.astype(BF16), preferred_element_type=F32) * cross
        k_dec = (k * jnp.exp((cl - 1.0 - rowi) * lg)).astype(BF16)
        st_ref[h] = float(np.exp(np.float32(cl) * np.float32(lg))) * state + _tn_dot(k_dec, vb)
        o_ref[:, vs] = _ret_finish(o, gn_ref[h], gt_ref[:, vs]).astype(o_ref.dtype)


def _retention_prompt(z3, tables, state0, gn):
    b, t, _ = z3.shape
    cl = RET_CHUNK
    assert t % cl == 0
    wq, wv = RET_HEADS * RET_DK, RET_HEADS * RET_DV
    tspec = pl.BlockSpec((cl, RET_DK), lambda bi, c: (c, 0))
    st_spec = pl.BlockSpec((None, RET_HEADS, RET_DK, RET_DV), lambda bi, c: (bi, 0, 0, 0))
    o, st = pl.pallas_call(
        _ret_prompt_kernel,
        grid=(b, t // cl),
        in_specs=[pl.BlockSpec((None, cl, wq), lambda bi, c: (bi, c, C_RQ // wq)),
                  pl.BlockSpec((None, cl, wq), lambda bi, c: (bi, c, C_RK // wq)),
                  pl.BlockSpec((None, cl, wv), lambda bi, c: (bi, c, C_RV // wv)),
                  pl.BlockSpec((None, cl, wv), lambda bi, c: (bi, c, C_RG // wv)),
                  tspec, tspec, tspec, st_spec,
                  pl.BlockSpec((RET_HEADS, 1, RET_DV), lambda bi, c: (0, 0, 0))],
        out_specs=[pl.BlockSpec((None, cl, wv), lambda bi, c: (bi, c, 0)), st_spec],
        out_shape=[jax.ShapeDtypeStruct((b, t, wv), BF16),
                   jax.ShapeDtypeStruct((b, RET_HEADS, RET_DK, RET_DV), F32)],
        compiler_params=_cp("arbitrary", "arbitrary"), name="retention_prompt",
    )(z3, z3, z3, z3, *tables, state0, gn.reshape(RET_HEADS, 1, RET_DV))
    return o, st


def _ret_decode_kernel(q_ref, k_ref, v_ref, gt_ref, c_ref, s1_ref, s2_ref, st0_ref, gn_ref, o_ref, st_ref):
    lgs = _log_gamma()
    half = RET_DK // 2
    row0 = lax.broadcasted_iota(I32, (16, 1), 0) == 0
    for s, h in [(s, h) for s in range(q_ref.shape[0]) for h in range(RET_HEADS)]:
        gamma = float(np.exp(lgs[h]))
        ks, vs = slice(h * RET_DK, (h + 1) * RET_DK), slice(h * RET_DV, (h + 1) * RET_DV)
        q = _rope(q_ref[s, :, ks], c_ref[...], s1_ref[...], s2_ref[...], half)
        k = _rope(k_ref[s, :, ks], c_ref[...], s1_ref[...], s2_ref[...], half) * (RET_DK ** -0.5)
        v = v_ref[s, :, vs]
        state = st0_ref[s, h].astype(F32)
        inner = jnp.sum(q * k, axis=-1, keepdims=True)
        q16 = jnp.broadcast_to(q, (16, RET_DK)).astype(BF16)
        cross = jnp.dot(q16, state.astype(BF16), preferred_element_type=F32)[0:1]
        o = inner * v + cross * gamma
        k16 = jnp.where(row0, jnp.broadcast_to(k, (16, RET_DK)), 0.0).astype(BF16)
        v16 = jnp.broadcast_to(v, (16, RET_DV)).astype(BF16)
        st_ref[s, h] = gamma * state + _tn_dot(k16, v16)
        o_ref[s, :, vs] = _ret_finish(o, gn_ref[h], gt_ref[s, :, vs]).astype(o_ref.dtype)


def _retention_decode(z3, tables, state0, gn):
    b = z3.shape[0]
    wq, wv = RET_HEADS * RET_DK, RET_HEADS * RET_DV
    tspec = pl.BlockSpec((1, RET_DK), lambda bi: (0, 0))
    sb = 4 if b % 4 == 0 else 1
    st_spec = pl.BlockSpec((sb, RET_HEADS, RET_DK, RET_DV), lambda bi: (bi, 0, 0, 0))
    o, st = pl.pallas_call(
        _ret_decode_kernel,
        grid=(b // sb,),
        in_specs=[pl.BlockSpec((sb, 1, wq), lambda bi: (bi, 0, C_RQ // wq)),
                  pl.BlockSpec((sb, 1, wq), lambda bi: (bi, 0, C_RK // wq)),
                  pl.BlockSpec((sb, 1, wv), lambda bi: (bi, 0, C_RV // wv)),
                  pl.BlockSpec((sb, 1, wv), lambda bi: (bi, 0, C_RG // wv)),
                  tspec, tspec, tspec, st_spec,
                  pl.BlockSpec((RET_HEADS, 1, RET_DV), lambda bi: (0, 0, 0))],
        out_specs=[pl.BlockSpec((sb, 1, wv), lambda bi: (bi, 0, 0)), st_spec],
        out_shape=[jax.ShapeDtypeStruct((b, 1, wv), BF16),
                   jax.ShapeDtypeStruct((b, RET_HEADS, RET_DK, RET_DV), F32)],
        compiler_params=_cp("arbitrary"), name="retention_decode",
    )(z3, z3, z3, z3, *tables, state0, gn.reshape(RET_HEADS, 1, RET_DV))
    return o, st


def _compress_weights(pos_emb, w):
    g, dh = NSA_KV_HEADS, HEAD_DIM
    eye = jnp.eye(g, dtype=w.dtype)
    wbig = jnp.einsum("lde,gh->lgdhe", w, eye).reshape(CMP_BLOCK * g * dh, g * dh)
    pos_row = jnp.broadcast_to(pos_emb[:, None, :], (CMP_BLOCK, g, dh)).reshape(1, CMP_BLOCK * g * dh)
    return pos_row, wbig.astype(BF16)


def _kv_rows(a):
    return a.reshape(-1, a.shape[-1])


def _compress_pool_kernel(x_ref, pos_ref, w_ref, o_ref, *, page, groups):
    dh = x_ref.shape[1]
    p = x_ref.shape[0] // (page * groups)
    per_page = page // CMP_BLOCK
    for g in range(groups):
        acc = jnp.zeros((per_page * p, dh), F32)
        for l in range(CMP_BLOCK):
            xl = jnp.concatenate([x_ref[pl.ds((n * CMP_BLOCK + l) * groups + g, p, stride=page * groups), :]
                                  for n in range(per_page)], axis=0)
            xl = (xl + pos_ref[l:l + 1, :]).astype(BF16)
            acc = acc + jnp.dot(xl, w_ref[l], preferred_element_type=F32)
        o_ref[:, :, g * dh:(g + 1) * dh] = acc.reshape(per_page, p, dh)


def _compress_pool(pool, pos_emb, w, name):
    _, n_pool, page, groups, dh = pool.shape
    per_page = page // CMP_BLOCK
    p = max(c for c in range(8, 65, 8) if n_pool % c == 0)
    return pl.pallas_call(
        functools.partial(_compress_pool_kernel, page=page, groups=groups), grid=(n_pool // p,),
        in_specs=[pl.BlockSpec((p * page * groups, dh), lambda i: (i, 0)),
                  pl.BlockSpec((CMP_BLOCK, dh), lambda i: (0, 0)),
                  pl.BlockSpec((CMP_BLOCK, dh, dh), lambda i: (0, 0, 0))],
        out_specs=pl.BlockSpec((per_page, p, groups * dh), lambda i: (0, i, 0)),
        out_shape=jax.ShapeDtypeStruct((per_page, n_pool, groups * dh), F32),
        compiler_params=_cp("arbitrary"), name=name,
    )(_kv_rows(pool), pos_emb, w.astype(BF16))


def _masked_softmax_parts(parts, masks, axis=-1):
    sm = [jnp.where(m, s, NEG_INF) for s, m in zip(parts, masks)]
    mx = functools.reduce(jnp.maximum, [jnp.max(s, axis=axis, keepdims=True) for s in sm])
    ex = [jnp.exp(s - mx) for s in sm]
    den = functools.reduce(lambda a, b2: a + b2, [jnp.sum(e, axis=axis, keepdims=True) for e in ex])
    inv = 1.0 / den
    return [jnp.where(m, e * inv, 0.0) for e, m in zip(ex, masks)]


def _select_blocks(score, nsel, axis=-1):
    axis = axis % score.ndim
    nbs = score.shape[axis]
    pos = lax.broadcasted_iota(I32, score.shape, axis)
    sel = jnp.zeros(score.shape, jnp.bool_)
    picks = []
    for _ in range(nsel):
        m = jnp.max(score, axis=axis, keepdims=True)
        idx = jnp.min(jnp.where(score == m, pos, nbs), axis=axis, keepdims=True)
        hit = pos == idx
        sel = jnp.logical_or(sel, hit)
        score = jnp.where(hit, -jnp.inf, score)
        picks.append(idx)
    return sel, picks


def _nsa_prompt_kernel(q_ref, ck_ref, cv_ref, ks_ref, vs_ref, kw_ref, vw_ref, gt_ref, o_ref, osel_ref):
    i = pl.program_id(1)
    qb = q_ref.shape[0]
    t = ks_ref.shape[0]
    nbs = t // SEL_BLOCK
    hg, dh = HEADS_PER_GROUP, HEAD_DIM
    kvw = NSA_KV_HEADS * dh
    scale = dh ** -0.5
    pos1 = i * qb + lax.broadcasted_iota(I32, (qb, 1), 0)
    pos = jnp.concatenate([pos1] * hg, axis=0)
    blk = lax.broadcasted_iota(I32, (nbs, 1), 0)
    pos_row = i * qb + lax.broadcasted_iota(I32, (1, hg * qb), 1) % qb
    pos1_row = pos_row[:, :qb]
    vis_e = blk * SEL_BLOCK + (CMP_BLOCK - 1) <= pos_row
    vis_o = blk * SEL_BLOCK + (SEL_BLOCK - 1) <= pos_row
    valid = blk * SEL_BLOCK <= pos1_row
    forced = jnp.logical_or(blk == 0, blk == pos1_row // SEL_BLOCK)
    n_case = 4 if t % (4 * qb) == 0 else 1
    span = t // n_case
    case = (i * qb) // span

    def selected_attend(limit, q, sel_f, cs):
        key = lax.broadcasted_iota(I32, (1, limit), 1)
        expand = (lax.broadcasted_iota(I32, (nbs, limit), 1) // SEL_BLOCK
                  == lax.broadcasted_iota(I32, (nbs, limit), 0)).astype(BF16)
        selk = _tn_dot(sel_f, expand)
        smask = jnp.logical_and(jnp.concatenate([selk] * hg, axis=0) > 0.5, key <= pos)
        (p_s,) = _masked_softmax_parts([_nt_dot(q, ks_ref[0:limit, cs].astype(BF16)) * scale], [smask])
        osel_ref[...] = jnp.dot(p_s.astype(BF16), vs_ref[0:limit, cs].astype(BF16), preferred_element_type=F32)

    wlen = WINDOW + qb
    wstart = pl.multiple_of(jnp.maximum(i * qb - WINDOW, 0), qb)
    wkey = wstart + lax.broadcasted_iota(I32, (1, wlen), 1)
    wdist = pos - wkey
    wmask = jnp.logical_and(wdist >= 0, wdist <= WINDOW)
    gates = gt_ref[...]

    for g in range(NSA_KV_HEADS):
        cs = slice(g * dh, (g + 1) * dh)
        q = jnp.concatenate([q_ref[:, (g * hg + hh) * dh:(g * hg + hh + 1) * dh] for hh in range(hg)],
                            axis=0).astype(BF16)
        co = slice(kvw + g * dh, kvw + (g + 1) * dh)
        ck_e, ck_o = ck_ref[:, cs].astype(BF16), ck_ref[:, co].astype(BF16)
        cv_e, cv_o = cv_ref[:, cs].astype(BF16), cv_ref[:, co].astype(BF16)
        p_e, p_o = _masked_softmax_parts([_nt_dot(ck_e, q) * scale, _nt_dot(ck_o, q) * scale], [vis_e, vis_o],
                                         axis=0)
        o_cmp = _tn_dot(p_e.astype(BF16), cv_e) + _tn_dot(p_o.astype(BF16), cv_o)
        psum = p_e + p_o
        imp = functools.reduce(lambda a, b2: a + b2, [psum[:, hh * qb:(hh + 1) * qb] for hh in range(hg)])
        score = jnp.where(forced, FORCE, jnp.where(valid, imp, -FORCE))
        sel, _ = _select_blocks(score, min(SEL_TOPK, nbs), axis=0)
        sel_f = jnp.where(sel, 1.0, 0.0).astype(BF16)
        for c in range(n_case):
            pl.when(case == c)(functools.partial(selected_attend, (c + 1) * span, q, sel_f, cs))
        o_sel = osel_ref[...]
        kwin = kw_ref[pl.ds(wstart, wlen), cs].astype(BF16)
        vwin = vw_ref[pl.ds(wstart, wlen), cs].astype(BF16)
        (p_w,) = _masked_softmax_parts([_nt_dot(q, kwin) * scale], [wmask])
        o_win = jnp.dot(p_w.astype(BF16), vwin, preferred_element_type=F32)
        for hh in range(hg):
            hd = g * hg + hh
            rs = slice(hh * qb, (hh + 1) * qb)
            o_ref[:, hd * dh:(hd + 1) * dh] = (gates[:, 3 * hd:3 * hd + 1] * o_cmp[rs]
                                                 + gates[:, 3 * hd + 1:3 * hd + 2] * o_sel[rs]
                                                 + gates[:, 3 * hd + 2:3 * hd + 3] * o_win[rs]).astype(o_ref.dtype)


def _nsa_prompt(qn3, ck3, cv3, ksn3, z3, kwn3, gates3):
    b, t, _ = qn3.shape
    qb = Q_BLOCK
    kvw = NSA_KV_HEADS * HEAD_DIM
    assert t % qb == 0 and t >= WINDOW + qb and t % (2 * SEL_BLOCK) == 0
    nbs = ck3.shape[1]
    full = lambda cb: pl.BlockSpec((None, t, kvw), lambda bi, i: (bi, 0, cb))
    return pl.pallas_call(
        _nsa_prompt_kernel,
        grid=(b, t // qb),
        in_specs=[pl.BlockSpec((None, qb, NSA_HEADS * HEAD_DIM), lambda bi, i: (bi, i, 0)),
                  pl.BlockSpec((None, nbs, CMP_PER_SEL * kvw), lambda bi, i: (bi, 0, 0)),
                  pl.BlockSpec((None, nbs, CMP_PER_SEL * kvw), lambda bi, i: (bi, 0, 0)),
                  full(0), full(C_VS // kvw), full(0), full(C_VW // kvw),
                  pl.BlockSpec((None, qb, LANES), lambda bi, i: (bi, i, 0))],
        out_specs=pl.BlockSpec((None, qb, NSA_HEADS * HEAD_DIM), lambda bi, i: (bi, i, 0)),
        out_shape=jax.ShapeDtypeStruct((b, t, NSA_HEADS * HEAD_DIM), BF16),
        scratch_shapes=[pltpu.VMEM((HEADS_PER_GROUP * qb, HEAD_DIM), F32)],
        compiler_params=_cp("arbitrary", "arbitrary"), name="nsa_prompt",
    )(qn3, ck3, cv3, ksn3, z3, kwn3, z3, gates3)


def _decode_q16(q_ref):
    dh = HEAD_DIM
    rows = [q_ref[:, h * dh:(h + 1) * dh] for h in range(NSA_HEADS)]
    return jnp.concatenate(rows + [jnp.zeros((16 - NSA_HEADS, dh), F32)], axis=0).astype(BF16)


def _nsa_decode_select_kernel(q_ref, ck_ref, cv_ref, ocmp_ref, idx_ref, *, past):
    for s in range(q_ref.shape[0]):
        _nsa_decode_select_one(q_ref.at[s], ck_ref.at[s], cv_ref.at[s], ocmp_ref.at[s], idx_ref.at[s], past)


def _nsa_decode_select_one(q_ref, ck_ref, cv_ref, ocmp_ref, idx_ref, past):
    dh, hg = HEAD_DIM, HEADS_PER_GROUP
    kvw = NSA_KV_HEADS * dh
    nhalf = ck_ref.shape[0]
    nbs = past // SEL_BLOCK + 1
    scale = dh ** -0.5
    q16 = _decode_q16(q_ref)
    row = lax.broadcasted_iota(I32, (16, 1), 0)
    blk = lax.broadcasted_iota(I32, (1, LANES), 1)
    blk_h = lax.broadcasted_iota(I32, (1, nhalf), 1)
    vis_e = blk_h * SEL_BLOCK + (CMP_BLOCK - 1) <= past
    vis_o = blk_h * SEL_BLOCK + (SEL_BLOCK - 1) <= past
    o_all = jnp.zeros((16, dh), F32)
    idx_row = jnp.zeros((1, LANES), I32)
    for g in range(NSA_KV_HEADS):
        cs = slice(g * dh, (g + 1) * dh)
        co = slice(kvw + g * dh, kvw + (g + 1) * dh)
        ck_e, ck_o = ck_ref[:, cs].astype(BF16), ck_ref[:, co].astype(BF16)
        cv_e, cv_o = cv_ref[:, cs].astype(BF16), cv_ref[:, co].astype(BF16)
        p_e, p_o = _masked_softmax_parts([_nt_dot(q16, ck_e) * scale, _nt_dot(q16, ck_o) * scale], [vis_e, vis_o])
        o_g = (jnp.dot(p_e.astype(BF16), cv_e, preferred_element_type=F32)
               + jnp.dot(p_o.astype(BF16), cv_o, preferred_element_type=F32))
        in_g = jnp.logical_and(row >= g * hg, row < (g + 1) * hg)
        o_all = jnp.where(in_g, o_g, o_all)
        imp = jnp.sum(jnp.where(in_g, p_e + p_o, 0.0), axis=0, keepdims=True)
        score = jnp.where(blk_h == 0, FORCE, imp)
        nsel = min(SEL_TOPK, nbs)
        _, past_picks = _select_blocks(score, min(nsel - 1, nhalf))
        picks = past_picks[:1] + [jnp.full((1, 1), nhalf, I32)] + past_picks[1:]
        for kk, p in enumerate(picks):
            idx_row = jnp.where(blk == g * SEL_TOPK + kk, p, idx_row)
        for kk in range(len(picks), SEL_TOPK):
            idx_row = jnp.where(blk == g * SEL_TOPK + kk, -1, idx_row)
    ocmp_ref[...] = o_all
    idx_ref[...] = idx_row


def _nsa_decode_select(qn3, ck3, cv3, past):
    b = qn3.shape[0]
    nhalf = ck3.shape[1]
    kvw = NSA_KV_HEADS * HEAD_DIM
    assert nhalf % 8 == 0 and nhalf == past // SEL_BLOCK
    sb = 8 if b % 8 == 0 else 1
    return pl.pallas_call(
        functools.partial(_nsa_decode_select_kernel, past=past),
        grid=(b // sb,),
        in_specs=[pl.BlockSpec((sb, 1, NSA_HEADS * HEAD_DIM), lambda bi: (bi, 0, 0)),
                  pl.BlockSpec((sb, nhalf, CMP_PER_SEL * kvw), lambda bi: (bi, 0, 0)),
                  pl.BlockSpec((sb, nhalf, CMP_PER_SEL * kvw), lambda bi: (bi, 0, 0))],
        out_specs=[pl.BlockSpec((sb, 16, HEAD_DIM), lambda bi: (bi, 0, 0)),
                   pl.BlockSpec((sb, 1, LANES), lambda bi: (bi, 0, 0))],
        out_shape=[jax.ShapeDtypeStruct((b, 16, HEAD_DIM), F32),
                   jax.ShapeDtypeStruct((b, 1, LANES), I32)],
        compiler_params=_cp("arbitrary"), name="nsa_decode_select",
    )(qn3, ck3, cv3)


def _nsa_decode_attend_kernel(*refs, past, nsel):
    n_blk = NSA_KV_HEADS * nsel
    rows_ref = refs[0]
    kb_refs = refs[1:1 + n_blk]
    vb_refs = refs[1 + n_blk:1 + 2 * n_blk]
    (q_ref, idx_ref, ksn_ref, vs_ref, wk_ref, wv_ref, kwn_ref, vw_ref, ocmp_ref, gt_ref, o_ref) = refs[1 + 2 * n_blk:]
    del rows_ref
    dh, hg = HEAD_DIM, HEADS_PER_GROUP
    scale = dh ** -0.5
    self_blk = past // SEL_BLOCK
    q16 = _decode_q16(q_ref)
    q16f = q16.astype(F32)
    row = lax.broadcasted_iota(I32, (16, 1), 0)
    lane = lax.broadcasted_iota(I32, (1, LANES), 1)
    idx_row = idx_ref[...]
    o_sel = jnp.zeros((16, dh), F32)
    o_win = jnp.zeros((16, dh), F32)
    for g in range(NSA_KV_HEADS):
        cs = slice(g * dh, (g + 1) * dh)
        in_g = jnp.logical_and(row >= g * hg, row < (g + 1) * hg)
        k_self = ksn_ref[:, cs].astype(BF16).astype(F32)
        v_self = vs_ref[:, cs].astype(BF16).astype(F32)
        s_self = jnp.sum(q16f * k_self, axis=-1, keepdims=True) * scale
        parts, masks, vals = [], [], []
        self_sel = jnp.zeros((1, 1), jnp.bool_)
        for kk in range(nsel):
            pick = jnp.max(jnp.where(lane == g * SEL_TOPK + kk, idx_row, -1), axis=-1, keepdims=True)
            in_pool = jnp.logical_and(pick >= 0, pick < self_blk)
            self_sel = jnp.logical_or(self_sel, pick == self_blk)
            grp = pl.ds(g, SEL_BLOCK, stride=NSA_KV_HEADS)
            kb = kb_refs[g * nsel + kk][grp, :].astype(BF16)
            parts.append(_nt_dot(q16, kb) * scale)
            masks.append(in_pool)
            vals.append(vb_refs[g * nsel + kk][grp, :].astype(BF16))
        parts.append(s_self)
        masks.append(self_sel)
        probs = _masked_softmax_parts(parts, masks)
        acc = probs[-1].astype(BF16).astype(F32) * v_self
        for p, vv in zip(probs[:-1], vals):
            acc = acc + jnp.dot(p.astype(BF16), vv, preferred_element_type=F32)
        o_sel = jnp.where(in_g, acc, o_sel)
        kw_self = kwn_ref[:, cs].astype(BF16).astype(F32)
        vw_self = vw_ref[:, cs].astype(BF16).astype(F32)
        sw_self = jnp.sum(q16f * kw_self, axis=-1, keepdims=True) * scale
        wgrp = pl.ds(g, wk_ref.shape[0] // NSA_KV_HEADS, stride=NSA_KV_HEADS)
        wparts = [_nt_dot(q16, wk_ref[wgrp, :].astype(BF16)) * scale, sw_self]
        wmasks = [lane[:, :1] >= 0, lane[:, :1] >= 0]
        pw, pws = _masked_softmax_parts(wparts, wmasks)
        accw = (jnp.dot(pw.astype(BF16), wv_ref[wgrp, :].astype(BF16), preferred_element_type=F32)
                + pws.astype(BF16).astype(F32) * vw_self)
        o_win = jnp.where(in_g, accw, o_win)
    gates = gt_ref[...]
    o_cmp = ocmp_ref[...]
    for hd in range(NSA_HEADS):
        o_ref[:, hd * dh:(hd + 1) * dh] = (gates[:, 3 * hd:3 * hd + 1] * o_cmp[hd:hd + 1]
                                             + gates[:, 3 * hd + 1:3 * hd + 2] * o_sel[hd:hd + 1]
                                             + gates[:, 3 * hd + 2:3 * hd + 3] * o_win[hd:hd + 1]).astype(o_ref.dtype)


def _nsa_decode_attend(pool_rows, sel_k_pool, sel_v_pool, qn3, idx3, ksn3, z3, win_k, win_v, kwn3, ocmp, gates3,
                       past, nsel):
    b = qn3.shape[0]
    kvw = NSA_KV_HEADS * HEAD_DIM
    n_blk = NSA_KV_HEADS * nsel
    nbuf = win_k.shape[2]

    def blk_spec(tt):
        return pl.BlockSpec((SEL_BLOCK * NSA_KV_HEADS, HEAD_DIM), lambda bi, r: (r[bi * n_blk + tt], 0))

    def row_spec(cb, w=kvw):
        return pl.BlockSpec((None, 1, w), lambda bi, r: (bi, 0, cb))

    wspec = pl.BlockSpec((nbuf * NSA_KV_HEADS, HEAD_DIM), lambda bi, r: (bi, 0))
    sel_k_pool, sel_v_pool, win_k, win_v = (_kv_rows(a) for a in (sel_k_pool, sel_v_pool, win_k, win_v))
    in_specs = ([blk_spec(tt) for tt in range(n_blk)] + [blk_spec(tt) for tt in range(n_blk)]
                + [pl.BlockSpec((None, 1, NSA_HEADS * HEAD_DIM), lambda bi, r: (bi, 0, 0)),
                   pl.BlockSpec((None, 1, LANES), lambda bi, r: (bi, 0, 0)),
                   row_spec(0), row_spec(C_VS // kvw),
                   wspec, wspec,
                   row_spec(0), row_spec(C_VW // kvw),
                   pl.BlockSpec((None, 16, HEAD_DIM), lambda bi, r: (bi, 0, 0)),
                   pl.BlockSpec((None, 1, LANES), lambda bi, r: (bi, 0, 0))])
    return pl.pallas_call(
        functools.partial(_nsa_decode_attend_kernel, past=past, nsel=nsel),
        grid_spec=pltpu.PrefetchScalarGridSpec(
            num_scalar_prefetch=1, grid=(b,), in_specs=in_specs,
            out_specs=pl.BlockSpec((None, 1, NSA_HEADS * HEAD_DIM), lambda bi, r: (bi, 0, 0))),
        out_shape=jax.ShapeDtypeStruct((b, 1, NSA_HEADS * HEAD_DIM), BF16),
        compiler_params=_cp("arbitrary"), name="nsa_decode_attend",
    )(pool_rows, *([sel_k_pool] * n_blk), *([sel_v_pool] * n_blk),
      qn3, idx3, ksn3, z3, win_k, win_v, kwn3, z3, ocmp, gates3)


def _mem_attn_kernel(*refs, cache_layout):
    dh = MEM_HEAD_DIM
    if cache_layout:
        q_ref, g_ref, mk0_ref, mk1_ref, mv0_ref, mv1_ref, o_ref = refs
    else:
        q_ref, g_ref, mk_ref, mv_ref, o_ref = refs
    tq = q_ref.shape[0]
    rows = max(tq, 16)
    for h in range(MEM_HEADS):
        cs = slice(h * dh, (h + 1) * dh)
        x = q_ref[:, cs]
        q = x * lax.rsqrt(jnp.mean(x * x, axis=-1, keepdims=True) + EPS) * g_ref[...]
        if rows != tq:
            q = jnp.broadcast_to(q, (rows, dh))
        if cache_layout:
            mk = jnp.concatenate([mk0_ref[:, h, :], mk1_ref[:, h, :]], axis=1)
            mv = jnp.concatenate([mv0_ref[:, h, :], mv1_ref[:, h, :]], axis=1)
        else:
            mk, mv = mk_ref[:, cs], mv_ref[:, cs]
        s = _nt_dot(q.astype(BF16), mk.astype(BF16)) * (dh ** -0.5)
        m = jnp.max(s, axis=-1, keepdims=True)
        e = jnp.exp(s - m)
        p = e / jnp.sum(e, axis=-1, keepdims=True)
        o = jnp.dot(p.astype(BF16), mv.astype(BF16), preferred_element_type=F32)
        o_ref[:, cs] = o[0:tq].astype(o_ref.dtype)


def _mem_attention(z3, gq, mk, mv, cache_layout=False):
    b, t, _ = z3.shape
    w = MEM_HEADS * MEM_HEAD_DIM
    tq = min(t, 512)
    assert t % tq == 0
    if cache_layout:
        m = mk.shape[2]
        assert MEM_HEAD_DIM == 2 * LANES
        half = lambda c: pl.BlockSpec((None, None, m, MEM_HEADS, LANES), lambda bi, i: (0, bi, 0, 0, c))
        kv_specs, kv_args = [half(0), half(1), half(0), half(1)], [mk, mk, mv, mv]
    else:
        m = mk.shape[1]
        kv_specs, kv_args = [pl.BlockSpec((None, m, w), lambda bi, i: (bi, 0, 0))] * 2, [mk, mv]
    return pl.pallas_call(
        functools.partial(_mem_attn_kernel, cache_layout=cache_layout),
        grid=(b, t // tq),
        in_specs=[pl.BlockSpec((None, tq, w), lambda bi, i: (bi, i, C_MQ // w)),
                  pl.BlockSpec((1, MEM_HEAD_DIM), lambda bi, i: (0, 0))] + kv_specs,
        out_specs=pl.BlockSpec((None, tq, w), lambda bi, i: (bi, i, 0)),
        out_shape=jax.ShapeDtypeStruct((b, t, w), BF16),
        compiler_params=_cp("arbitrary", "arbitrary"), name="mem_attention",
    )(z3, gq.reshape(1, MEM_HEAD_DIM), *kv_args)


def _sigmoid_cols_kernel(x_ref, o_ref):
    o_ref[...] = jax.nn.sigmoid(x_ref[...])


def _nsa_gates(z, col):
    n = z.shape[0]
    tm = min(n, 1024)
    return pl.pallas_call(
        _sigmoid_cols_kernel, grid=(n // tm,),
        in_specs=[pl.BlockSpec((tm, LANES), lambda i: (i, col // LANES))],
        out_specs=pl.BlockSpec((tm, LANES), lambda i: (i, 0)),
        out_shape=jax.ShapeDtypeStruct((n, LANES), F32),
        compiler_params=_cp("arbitrary"), name="nsa_gates",
    )(z)


def _merge_kernel(r_ref, n_ref, m_ref, g0_ref, g1_ref, g2_ref, w_ref, o_ref):
    acc = jnp.zeros(o_ref.shape, F32)
    for c, (b_ref, g_ref) in enumerate(((r_ref, g0_ref), (n_ref, g1_ref), (m_ref, g2_ref))):
        up = jnp.dot(b_ref[...].astype(BF16), w_ref[c], preferred_element_type=F32)
        acc = acc + jax.nn.sigmoid(g_ref[...]) * up
    o_ref[...] = acc.astype(BF16)


def _merge(o_ret, o_nsa, o_mem, z, wb, d_model):
    n, bw = o_ret.shape
    tm = min(n, 1024)
    tn = 512
    gb = C_MG // tn
    nj = d_model // tn
    bspec = pl.BlockSpec((tm, bw), lambda i, j: (i, 0))
    gspec = lambda c: pl.BlockSpec((tm, tn), lambda i, j: (i, gb + c * nj + j))
    return pl.pallas_call(
        _merge_kernel, grid=(n // tm, nj),
        in_specs=[bspec, bspec, bspec, gspec(0), gspec(1), gspec(2),
                  pl.BlockSpec((N_BRANCH, bw, tn), lambda i, j: (0, 0, j))],
        out_specs=pl.BlockSpec((tm, tn), lambda i, j: (i, j)),
        out_shape=jax.ShapeDtypeStruct((n, d_model), BF16),
        compiler_params=_cp("arbitrary", "arbitrary"), name="branch_merge",
    )(o_ret, o_nsa, o_mem, z, z, z, wb)


def _topk_rows(s, kk):
    e, tl = s.shape
    nt = e // 8
    assert e % 8 == 0 and nt & (nt - 1) == 0
    sub = lax.broadcasted_iota(I32, (8, tl), 0).astype(F32)
    vals = [s[8 * k:8 * k + 8] for k in range(nt)]
    idxs = [sub + float(8 * k) for k in range(nt)]
    for a, b in _sorting_network(nt):
        va, vb, ia, ib = vals[a], vals[b], idxs[a], idxs[b]
        swap = jnp.logical_or(vb > va, jnp.logical_and(vb == va, ib < ia))
        vals[a], vals[b] = jnp.maximum(va, vb), jnp.minimum(va, vb)
        idxs[a], idxs[b] = jnp.where(swap, ib, ia), jnp.where(swap, ia, ib)
    out_v, out_i = [], []
    for step in range(kk):
        m = jnp.max(vals[0], axis=0, keepdims=True)
        i = jnp.min(jnp.where(vals[0] == m, idxs[0], float(e)), axis=0, keepdims=True)
        out_v.append(m)
        out_i.append(i)
        win = idxs[0] == i
        depth = min(nt, kk - step)
        for k in range(depth - 1):
            vals[k] = jnp.where(win, vals[k + 1], vals[k])
            idxs[k] = jnp.where(win, idxs[k + 1], idxs[k])
        vals[depth - 1] = jnp.where(win, -jnp.inf, vals[depth - 1])
    return jnp.concatenate(out_v, axis=0), jnp.concatenate(out_i, axis=0)


def _sorting_network(n):
    def merge(lo, hi, r):
        step = r * 2
        if step < hi - lo:
            yield from merge(lo, hi, step)
            yield from merge(lo + r, hi, step)
            yield from [(i, i + r) for i in range(lo + r, hi - r, step)]
        else:
            yield (lo, lo + r)

    def sort(lo, hi):
        if hi - lo >= 1:
            mid = lo + (hi - lo) // 2
            yield from sort(lo, mid)
            yield from sort(mid + 1, hi)
            yield from merge(lo, hi, 1)

    return list(sort(0, n - 1))


def _pair_pieces(kk):
    pieces, cur = [], []

    def flush(rows):
        pieces.append(rows + [None] * (8 - len(rows)))

    for a in range(kk):
        grp = [(a, b) for b in range(kk // (a + 1))]
        if cur and len(cur) + len(grp) > 8:
            flush(cur)
            cur = []
        cur = cur + grp
        while len(cur) >= 8:
            flush(cur[:8])
            cur = cur[8:]
    if cur:
        flush(cur)
    return pieces


def _rows_by_runs(x, ids):
    if ids[0] % 8 == 0 and ids == list(range(ids[0], ids[0] + 8)):
        return x[ids[0]:ids[0] + 8]
    r = lax.broadcasted_iota(I32, (8, 1), 0)
    runs = []
    for p, i in enumerate(ids):
        if not runs or runs[-1][1] != i:
            runs.append((p, i))
    out = jnp.broadcast_to(x[runs[-1][1]:runs[-1][1] + 1], (8, x.shape[1]))
    for (_, i), (nxt, _) in zip(reversed(runs[:-1]), reversed(runs[1:])):
        out = jnp.where(r < nxt, x[i:i + 1], out)
    return out


ROUTE_HEADS_PER_STEP = 4


def _peer_route_kernel(q_ref, sk_ref, i_ref, j_ref, g_ref):
    kk = PEER_TOPK
    for hh in range(ROUTE_HEADS_PER_STEP):
        rows = slice(hh * kk, (hh + 1) * kk)
        i_ref[rows, :], j_ref[rows, :], g_ref[rows, :] = _peer_route_head(
            q_ref[:, hh * PEER_DKEY:(hh + 1) * PEER_DKEY], sk_ref[hh])


def _peer_route_head(q, sk):
    kk = PEER_TOPK
    st = _nt_dot(sk, q.astype(BF16))
    v0, i0 = _topk_rows(st[0:PEER_KEYS], kk)
    v1, i1 = _topk_rows(st[PEER_KEYS:2 * PEER_KEYS], kk)
    cand, ci, cj = [], [], []
    for piece in _pair_pieces(kk):
        ra = [p[0] if p else 0 for p in piece]
        rb = [p[1] if p else 0 for p in piece]
        live = lax.broadcasted_iota(I32, (8, 1), 0) < sum(p is not None for p in piece)
        cand.append(jnp.where(live, _rows_by_runs(v0, ra) + _rows_by_runs(v1, rb), -jnp.inf))
        ci.append(_rows_by_runs(i0, ra))
        cj.append(_rows_by_runs(i1, rb))
    cand, ci, cj = (jnp.concatenate(c, axis=0) for c in (cand, ci, cj))
    ridx = lax.broadcasted_iota(I32, cand.shape, 0).astype(F32)
    sc, si, sj = [], [], []
    for _ in range(kk):
        m = jnp.max(cand, axis=0, keepdims=True)
        r = jnp.min(jnp.where(cand == m, ridx, float(cand.shape[0])), axis=0, keepdims=True)
        hit = ridx == r
        sc.append(m)
        si.append(jnp.max(jnp.where(hit, ci, -1.0), axis=0, keepdims=True))
        sj.append(jnp.max(jnp.where(hit, cj, -1.0), axis=0, keepdims=True))
        cand = jnp.where(hit, -jnp.inf, cand)
    sc = jnp.concatenate(sc, axis=0)
    e = jnp.exp(sc - jnp.max(sc, axis=0, keepdims=True))
    return (jnp.concatenate(si, axis=0).astype(I32), jnp.concatenate(sj, axis=0).astype(I32),
            e / jnp.sum(e, axis=0, keepdims=True))


def _peer_route(q, skbd):
    n = q.shape[0]
    tl = min(n, LANES)
    kk = PEER_TOPK
    slots = PEER_HEADS * kk
    hps = ROUTE_HEADS_PER_STEP
    ospec = pl.BlockSpec((hps * kk, tl), lambda i, h: (h, i))
    return pl.pallas_call(
        _peer_route_kernel, grid=(n // tl, PEER_HEADS // hps),
        in_specs=[pl.BlockSpec((tl, hps * PEER_DKEY), lambda i, h: (i, h)),
                  pl.BlockSpec((hps, 2 * PEER_KEYS, PEER_DKEY), lambda i, h: (h, 0, 0))],
        out_specs=[ospec, ospec, ospec],
        out_shape=[jax.ShapeDtypeStruct((slots, n), I32), jax.ShapeDtypeStruct((slots, n), I32),
                   jax.ShapeDtypeStruct((slots, n), F32)],
        compiler_params=_cp("arbitrary", "arbitrary"), name="peer_route",
    )(q, skbd)


TOKEN_GROUP = 8
PACK_GROUP = 16
U32 = jnp.uint32


def _peer_weights_kernel(i_ref, j_ref, g_ref, o_ref, it_scr, jt_scr, gt_scr):
    it_scr[...] = i_ref[...].T
    jt_scr[...] = j_ref[...].T
    gt_scr[...] = g_ref[...].T
    key = lax.broadcasted_iota(I32, (PEER_KEYS, i_ref.shape[0]), 0)

    def one_hots(n):
        irow = it_scr[pl.ds(n, 1), :]
        jrow = jt_scr[pl.ds(n, 1), :]
        grow = gt_scr[pl.ds(n, 1), :]
        a = jnp.where(key == irow, grow, 0.0).astype(BF16)
        b = jnp.where(key == jrow, 1.0, 0.0).astype(BF16)
        return a, b

    groups_per_trip = 2 if o_ref.shape[0] % 2 == 0 else 1

    def bf16_bits(r):
        return lax.bitcast_convert_type(r.astype(BF16).astype(F32), U32)

    def body(trip, carry):
        for gi in range(groups_per_trip):
            grp = trip * groups_per_trip + gi
            res = [_nt_dot(*one_hots(grp * PACK_GROUP + t)) for t in range(PACK_GROUP)]
            for t in range(TOKEN_GROUP):
                word = bf16_bits(res[t + TOKEN_GROUP]) | (bf16_bits(res[t]) >> 16)
                o_ref[grp, pl.ds(t, PEER_KEYS, stride=TOKEN_GROUP), :] = word
        return carry

    lax.fori_loop(0, o_ref.shape[0] // groups_per_trip, body, 0)


def _peer_weights(it, jt, gt):
    slots, n = it.shape
    tb = min(n, LANES)
    assert tb % PACK_GROUP == 0
    ispec = pl.BlockSpec((slots, tb), lambda i: (0, i))
    out = pl.pallas_call(
        _peer_weights_kernel, grid=(n // tb,),
        in_specs=[ispec, ispec, ispec],
        out_specs=pl.BlockSpec((tb // PACK_GROUP, PEER_KEYS * TOKEN_GROUP, PEER_KEYS), lambda i: (i, 0, 0)),
        out_shape=jax.ShapeDtypeStruct((n // PACK_GROUP, PEER_KEYS * TOKEN_GROUP, PEER_KEYS), U32),
        scratch_shapes=[pltpu.VMEM((tb, slots), I32), pltpu.VMEM((tb, slots), I32), pltpu.VMEM((tb, slots), F32)],
        compiler_params=_cp("arbitrary"), name="peer_weights",
    )(it, jt, gt)
    return out.reshape(n // PACK_GROUP, PEER_KEYS, TOKEN_GROUP, PEER_KEYS)


def _peer_dense_kernel(h_ref, g_ref, u_ref, v_ref, x_ref, o_ref):
    @pl.when(pl.program_id(1) == 0)
    def _():
        o_ref[...] = x_ref[...]

    a = jnp.dot(h_ref[...], u_ref[...], preferred_element_type=F32)
    act = 0.5 * a * (1.0 + lax.erf(a * (2.0 ** -0.5)))
    tm = h_ref.shape[0]

    def slab(ii):
        words = g_ref[:, ii]
        lo = lax.bitcast_convert_type(words << 16, F32)
        hi = lax.bitcast_convert_type(words & jnp.uint32(0xFFFF0000), F32)
        return jnp.concatenate([lo, hi], axis=1).reshape(tm, PEER_KEYS)

    w = jnp.concatenate([(slab(ii) * act[:, ii * PEER_KEYS:(ii + 1) * PEER_KEYS]).astype(BF16)
                         for ii in range(g_ref.shape[1])], axis=1)
    o_ref[...] += jnp.dot(w, v_ref[...], preferred_element_type=F32)


PEER_EXPERT_TILE = 1024


def _peer_dense(hn, gw, ut, vv, x):
    n, d = hn.shape
    n_tiles, _, te = ut.shape
    tm = min(n, 512)
    return pl.pallas_call(
        _peer_dense_kernel, grid=(n // tm, n_tiles),
        in_specs=[pl.BlockSpec((tm, d), lambda i, j: (i, 0)),
                  pl.BlockSpec((tm // PACK_GROUP, te // PEER_KEYS, TOKEN_GROUP, PEER_KEYS),
                               lambda i, j: (i, j, 0, 0)),
                  pl.BlockSpec((None, d, te), lambda i, j: (j, 0, 0)),
                  pl.BlockSpec((te, d), lambda i, j: (j, 0)),
                  pl.BlockSpec((tm, d), lambda i, j: (i, 0))],
        out_specs=pl.BlockSpec((tm, d), lambda i, j: (i, 0)),
        out_shape=jax.ShapeDtypeStruct((n, d), F32),
        compiler_params=_cp("arbitrary", "arbitrary"), name="peer_dense",
    )(hn, gw, ut, vv, x)


def _prepare_weights(norm_attn, w_in, cmp_pos_k, cmp_pos_v, cmp_w_k, cmp_w_v, w_mem_kv, w_branch, w_out,
                     peer_wq, peer_subkeys, peer_u, peer_v):
    d = w_in.shape[0]
    ng_w = NSA_HEADS * 3
    o_ng = 5632
    o_mq = o_ng + ng_w
    o_mg = o_mq + MEM_HEADS * MEM_HEAD_DIM
    width = C_MG + N_BRANCH * d + LANES
    width = -(-width // Z_TILE) * Z_TILE
    w_r = jnp.concatenate([w_in[:, 0:4096], w_in[:, o_mq:o_mg], w_in[:, 4096:o_ng], w_in[:, o_mg:],
                           w_in[:, o_ng:o_mq],
                           jnp.zeros((d, width - (C_MG + N_BRANCH * d + ng_w)), w_in.dtype)], axis=1).astype(BF16)
    pos_k, wk_big = _compress_weights(cmp_pos_k, cmp_w_k)
    pos_v, wv_big = _compress_weights(cmp_pos_v, cmp_w_v)
    half = PEER_DKEY // 2
    sk = peer_subkeys.astype(BF16)
    zk = jnp.zeros((PEER_HEADS, PEER_KEYS, half), BF16)
    skbd = jnp.concatenate([jnp.concatenate([sk[:, 0], zk], axis=2),
                            jnp.concatenate([zk, sk[:, 1]], axis=2)], axis=1)
    return dict(w_in=w_r, c_ng=C_MG + N_BRANCH * d, pos_k=pos_k, wk_big=wk_big, pos_v=pos_v, wv_big=wv_big,
                w_mem_kv=w_mem_kv.astype(BF16), w_branch=w_branch.astype(BF16), w_out=w_out.astype(BF16),
                peer_wq=peer_wq.astype(BF16), skbd=skbd, vv=peer_v.astype(BF16),
                ut=peer_u.astype(BF16).reshape(-1, PEER_EXPERT_TILE, peer_u.shape[1]).transpose(0, 2, 1))


def _layer_common(x, seq, pos, norm_attn, nsa_q_norm, nsa_k_norm, pw):
    z = _prologue_matmul(x, norm_attn, pw["w_in"], mode="norm", name="in_proj", tm=1024)
    nsa_tabs = _rope_tables(pos, ROPE_DIMS, ROPE_THETA, HEAD_DIM)
    half = ROPE_DIMS // 2
    kvw = NSA_KV_HEADS * HEAD_DIM
    qn = _headnorm(z, C_NQ, NSA_HEADS * HEAD_DIM, nsa_q_norm, HEAD_DIM, seq, nsa_tabs, half)
    kcn = _headnorm(z, C_KC, kvw, nsa_k_norm[0], HEAD_DIM, seq, nsa_tabs, half)
    ksn = _headnorm(z, C_KS, kvw, nsa_k_norm[1], HEAD_DIM, seq, nsa_tabs, half)
    kwn = _headnorm(z, C_KW, kvw, nsa_k_norm[2], HEAD_DIM, seq, nsa_tabs, half)
    gates = _nsa_gates(z, pw["c_ng"])
    ret_tabs = _rope_tables(pos, RET_DK, RET_THETA, RET_DK)
    return z, qn, kcn, ksn, kwn, gates, ret_tabs


def _layer_tail(x, z, o_ret, o_nsa, o_mem, norm_ffn, pw):
    d = x.shape[1]
    merged = _merge(o_ret, o_nsa, o_mem, z, pw["w_branch"], d)
    x1 = _prologue_matmul(merged, jnp.ones((d,), F32), pw["w_out"], mode="none", residual=x, name="out_proj")
    q, hn = _prologue_matmul(x1, norm_ffn, pw["peer_wq"], mode="norm", emit_h=True, name="peer_query")
    it, jt, gt = _peer_route(q, pw["skbd"])
    return _peer_dense(hn, _peer_weights(it, jt, gt), pw["ut"], pw["vv"], x1)


def kernel(x_prompt, x_sample, cache_cmp_k, cache_cmp_v, cache_sel_k, cache_sel_v, cache_win_k, cache_win_v,
           state_ret, cache_mem_k, cache_mem_v, page_table, mem_prompt, norm_attn, w_in, ret_gn, nsa_q_norm,
           nsa_k_norm, cmp_pos_k, cmp_pos_v, cmp_w_k, cmp_w_v, norm_mem, w_mem_kv, mem_q_norm, mem_k_norm,
           w_branch, w_out, norm_ffn, peer_wq, peer_subkeys, peer_u, peer_v):
    depth = w_in.shape[0]
    assert depth == 1
    l = 0
    bp, t, d = x_prompt.shape
    bs, ts, _ = x_sample.shape
    assert ts == 1
    n_pool, page, g_kv, dh = cache_cmp_k.shape[1:]
    n_pages = page_table.shape[1]
    past = n_pages * page
    kvw = g_kv * dh
    memw = MEM_HEADS * MEM_HEAD_DIM
    pw = _prepare_weights(norm_attn[l], w_in[l], cmp_pos_k[l], cmp_pos_v[l], cmp_w_k[l], cmp_w_v[l], w_mem_kv[l],
                          w_branch[l], w_out[l], peer_wq[l], peer_subkeys[l], peer_u[l], peer_v[l])
    cmp_cols = CMP_BLOCK * kvw

    n_p = bp * t
    xp = x_prompt.reshape(n_p, d)
    m_tok = mem_prompt.shape[1]
    mkv = _prologue_matmul(mem_prompt.reshape(bp * m_tok, d), norm_mem[l], pw["w_mem_kv"], mode="norm",
                           name="mem_kv_proj")
    mk_p = _headnorm(mkv, 0, memw, mem_k_norm[l], MEM_HEAD_DIM, m_tok)
    mv_p = mkv[:, memw:]
    z, qn, kcn, ksn, kwn, gates, ret_tabs = _layer_common(xp, t, jnp.arange(t), norm_attn[l], nsa_q_norm[l],
                                                         nsa_k_norm[l], pw)
    z3 = z.reshape(bp, t, z.shape[1])
    o_ret, st_p = _retention_prompt(z3, ret_tabs, jnp.zeros((bp, RET_HEADS, RET_DK, RET_DV), F32), ret_gn[l])
    vc_p = z[:, C_VC:C_VC + kvw]
    nbc = t // CMP_BLOCK
    ck_p = _prologue_matmul(kcn.reshape(bp * nbc, cmp_cols), pw["pos_k"], pw["wk_big"], mode="bias", tm=256,
                            name="compress_k")
    cv_p = _prologue_matmul(vc_p.reshape(bp * nbc, cmp_cols), pw["pos_v"], pw["wv_big"], mode="bias", tm=256,
                            name="compress_v")
    pair = CMP_PER_SEL * kvw
    o_nsa = _nsa_prompt(qn.reshape(bp, t, -1), ck_p.reshape(bp, nbc // CMP_PER_SEL, pair),
                        cv_p.reshape(bp, nbc // CMP_PER_SEL, pair),
                        ksn.reshape(bp, t, kvw), z3, kwn.reshape(bp, t, kvw), gates.reshape(bp, t, LANES))
    o_mem = _mem_attention(z3, mem_q_norm[l], mk_p.reshape(bp, m_tok, memw), mv_p.reshape(bp, m_tok, memw))
    y_p = _layer_tail(xp, z, o_ret.reshape(n_p, -1), o_nsa.reshape(n_p, -1), o_mem.reshape(n_p, -1), norm_ffn[l], pw)

    nbuf_p = min(WINDOW, t)
    kv5 = lambda a, bb, tt: a.reshape(1, bb, tt, g_kv, dh)
    outs_p = (kv5(kcn, bp, t), kv5(vc_p, bp, t), kv5(ksn, bp, t), kv5(z[:, C_VS:C_VS + kvw], bp, t),
              kv5(kwn, bp, t)[:, :, t - nbuf_p:], kv5(z[:, C_VW:C_VW + kvw], bp, t)[:, :, t - nbuf_p:],
              st_p[None], mk_p.reshape(1, bp, m_tok, MEM_HEADS, MEM_HEAD_DIM),
              mv_p.reshape(1, bp, m_tok, MEM_HEADS, MEM_HEAD_DIM))

    xs = x_sample.reshape(bs, d)
    zs, qn_s, kcn_s, ksn_s, kwn_s, gates_s, ret_tabs_s = _layer_common(
        xs, 1, jnp.full((1,), past, I32), norm_attn[l], nsa_q_norm[l], nsa_k_norm[l], pw)
    zs3 = zs.reshape(bs, 1, zs.shape[1])
    o_ret_s, st_s = _retention_decode(zs3, ret_tabs_s, state_ret[l], ret_gn[l])
    per_page = page // CMP_BLOCK
    ckp = _compress_pool(cache_cmp_k, cmp_pos_k[l], cmp_w_k[l], "compress_pool_k")
    cvp = _compress_pool(cache_cmp_v, cmp_pos_v[l], cmp_w_v[l], "compress_pool_v")
    pages_of = lambda a: a[:, page_table].transpose(1, 2, 0, 3).reshape(bs, n_pages * per_page // CMP_PER_SEL, pair)
    ck_s, cv_s = pages_of(ckp), pages_of(cvp)
    qn_s3 = qn_s.reshape(bs, 1, -1)
    ocmp_s, idx_s = _nsa_decode_select(qn_s3, ck_s, cv_s, past)
    nbs = past // SEL_BLOCK + 1
    nsel = min(SEL_TOPK, nbs)
    per_page_sel = page // SEL_BLOCK
    picks = jnp.clip(idx_s[:, 0, :NSA_KV_HEADS * SEL_TOPK].reshape(bs, NSA_KV_HEADS, SEL_TOPK)[:, :, :nsel],
                     0, nbs - 2)
    pool_rows = (jnp.take_along_axis(page_table, (picks // per_page_sel).reshape(bs, -1), axis=1) * per_page_sel
                 + (picks % per_page_sel).reshape(bs, -1)).reshape(-1).astype(I32)
    o_nsa_s = _nsa_decode_attend(
        pool_rows, cache_sel_k, cache_sel_v, qn_s3, idx_s, ksn_s.reshape(bs, 1, kvw), zs3, cache_win_k, cache_win_v,
        kwn_s.reshape(bs, 1, kvw), ocmp_s, gates_s.reshape(bs, 1, LANES), past, nsel)
    o_mem_s = _mem_attention(zs3, mem_q_norm[l], cache_mem_k, cache_mem_v, cache_layout=True)
    y_s = _layer_tail(xs, zs, o_ret_s.reshape(bs, -1), o_nsa_s.reshape(bs, -1), o_mem_s.reshape(bs, -1),
                      norm_ffn[l], pw)

    nbuf_s = cache_win_k.shape[2]
    win_k_s = jnp.concatenate([cache_win_k[l], kv5(kwn_s, bs, 1)[0]], axis=1)[None, :, -nbuf_s:]
    win_v_s = jnp.concatenate([cache_win_v[l], kv5(zs[:, C_VW:C_VW + kvw], bs, 1)[0]], axis=1)[None, :, -nbuf_s:]
    outs_s = (kv5(kcn_s, bs, 1), kv5(zs[:, C_VC:C_VC + kvw], bs, 1), kv5(ksn_s, bs, 1),
              kv5(zs[:, C_VS:C_VS + kvw], bs, 1), win_k_s, win_v_s, st_s[None].astype(x_sample.dtype))
    return (y_p.reshape(bp, t, d), y_s.reshape(bs, ts, d)) + outs_p + outs_s
```

```python
import functools

import numpy as np
import jax
import jax.numpy as jnp
from jax import lax
from jax.experimental import pallas as pl
from jax.experimental.pallas import tpu as pltpu

F32 = jnp.float32
BF16 = jnp.bfloat16
I32 = jnp.int32

EPS = 1e-6
NEG_INF = -1e30
FORCE = 1e4

RET_HEADS, RET_DK, RET_DV, RET_CHUNK, RET_THETA = 4, 128, 256, 128, 10000.0
NSA_HEADS, NSA_KV_HEADS, HEAD_DIM = 8, 2, 128
HEADS_PER_GROUP = NSA_HEADS // NSA_KV_HEADS
CMP_BLOCK, SEL_BLOCK, SEL_TOPK, WINDOW, Q_BLOCK = 32, 64, 4, 256, 128
CMP_PER_SEL = SEL_BLOCK // CMP_BLOCK
ROPE_THETA, ROPE_DIMS = 500000.0, HEAD_DIM // 4
MEM_HEADS, MEM_HEAD_DIM = 4, 256
PEER_HEADS, PEER_KEYS, PEER_DKEY, PEER_TOPK = 8, 128, 128, 16
N_BRANCH = 3
LANES = 128

C_RQ, C_RK, C_RV, C_RG, C_NQ, C_MQ = 0, 512, 1024, 2048, 3072, 4096
C_KC, C_VC, C_KS, C_VS, C_KW, C_VW = 5120, 5376, 5632, 5888, 6144, 6400
C_MG = 6656
Z_TILE = 512


def _cp(*sem):
    return pltpu.CompilerParams(dimension_semantics=sem, vmem_limit_bytes=56 * 1024 * 1024)


def _nt_dot(a, b):
    return lax.dot_general(a, b, (((1,), (1,)), ((), ())), preferred_element_type=F32)


def _tn_dot(a, b):
    return lax.dot_general(a, b, (((0,), (0,)), ((), ())), preferred_element_type=F32)


def _pm_kernel(*refs, mode, residual, emit_h):
    x_ref, g_ref, w_ref = refs[:3]
    k = 3
    r_ref = None
    if residual:
        r_ref = refs[k]
        k += 1
    o_ref = refs[k]
    k += 1
    h_ref = None
    if emit_h:
        h_ref = refs[k]
        k += 1
    h_scr = refs[k]

    @pl.when(pl.program_id(1) == 0)
    def _():
        x = x_ref[...].astype(F32)
        if mode == "norm":
            h = x * lax.rsqrt(jnp.mean(x * x, axis=-1, keepdims=True) + EPS) * g_ref[...]
        elif mode == "bias":
            h = x + g_ref[...]
        else:
            h = x
        h_scr[...] = h.astype(BF16)
        if emit_h:
            h_ref[...] = h_scr[...]

    acc = jnp.dot(h_scr[...], w_ref[...], preferred_element_type=F32)
    if residual:
        acc = acc + r_ref[...]
    o_ref[...] = acc


def _prologue_matmul(x, g, w, *, mode, name, residual=None, emit_h=False, tm=512):
    n, kdim = x.shape
    wout = w.shape[1]
    tm = min(tm, n)
    tn = next(c for c in (1024, 512, 256, 128) if wout % c == 0)
    assert n % tm == 0 and wout % tn == 0
    in_specs = [pl.BlockSpec((tm, kdim), lambda i, j: (i, 0)),
                pl.BlockSpec((1, kdim), lambda i, j: (0, 0)),
                pl.BlockSpec((kdim, tn), lambda i, j: (0, j))]
    args = [x, g.reshape(1, kdim).astype(F32), w]
    if residual is not None:
        in_specs.append(pl.BlockSpec((tm, tn), lambda i, j: (i, j)))
        args.append(residual)
    out_shape = [jax.ShapeDtypeStruct((n, wout), F32)]
    out_specs = [pl.BlockSpec((tm, tn), lambda i, j: (i, j))]
    if emit_h:
        out_shape.append(jax.ShapeDtypeStruct((n, kdim), BF16))
        out_specs.append(pl.BlockSpec((tm, kdim), lambda i, j: (i, 0)))
    res = pl.pallas_call(
        functools.partial(_pm_kernel, mode=mode, residual=residual is not None, emit_h=emit_h),
        grid=(n // tm, wout // tn),
        in_specs=in_specs, out_specs=out_specs, out_shape=out_shape,
        scratch_shapes=[pltpu.VMEM((tm, kdim), BF16)],
        compiler_params=_cp("arbitrary", "arbitrary"), name=name,
    )(*args)
    return res if emit_h else res[0]


def _rope_tables(pos, rot_dims, theta, dh):
    half = rot_dims // 2
    inv = theta ** (-jnp.arange(half, dtype=F32) * 2.0 / rot_dims)
    ang = pos.astype(F32)[:, None] * inv[None, :]
    cos, sin = jnp.cos(ang), jnp.sin(ang)
    t = pos.shape[0]
    one = jnp.ones((t, dh - rot_dims), F32)
    zh = jnp.zeros((t, half), F32)
    zr = jnp.zeros((t, dh - rot_dims), F32)
    c = jnp.concatenate([cos, cos, one], axis=1)
    s1 = jnp.concatenate([-sin, zh, zr], axis=1)
    s2 = jnp.concatenate([zh, sin, zr], axis=1)
    return c, s1, s2


def _rope(y, c, s1, s2, half):
    dh = y.shape[-1]
    return y * c + pltpu.roll(y, dh - half, 1) * s1 + pltpu.roll(y, half, 1) * s2


def _headnorm_kernel(*refs, nh, dh, rope_half):
    if rope_half:
        x_ref, g_ref, c_ref, s1_ref, s2_ref, o_ref = refs
    else:
        x_ref, g_ref, o_ref = refs
    for h in range(nh):
        x = x_ref[:, h * dh:(h + 1) * dh]
        y = x * lax.rsqrt(jnp.mean(x * x, axis=-1, keepdims=True) + EPS) * g_ref[...]
        if rope_half:
            y = _rope(y, c_ref[...], s1_ref[...], s2_ref[...], rope_half)
        o_ref[:, h * dh:(h + 1) * dh] = y


def _headnorm(x, col, width, gain, dh, seq, tables=None, rope_half=0):
    n = x.shape[0]
    tm = min(512, seq) if seq > 1 else n
    assert n % tm == 0 and col % width == 0
    cb = col // width
    in_specs = [pl.BlockSpec((tm, width), lambda i: (i, cb)),
                pl.BlockSpec((1, dh), lambda i: (0, 0))]
    args = [x, gain.reshape(1, dh).astype(F32)]
    if rope_half:
        if seq > 1:
            nb = seq // tm
            tspec = pl.BlockSpec((tm, dh), lambda i: (i % nb, 0))
        else:
            tspec = pl.BlockSpec((1, dh), lambda i: (0, 0))
        in_specs += [tspec] * 3
        args += list(tables)
    return pl.pallas_call(
        functools.partial(_headnorm_kernel, nh=width // dh, dh=dh, rope_half=rope_half),
        grid=(n // tm,),
        in_specs=in_specs,
        out_specs=pl.BlockSpec((tm, width), lambda i: (i, 0)),
        out_shape=jax.ShapeDtypeStruct((n, width), F32),
        compiler_params=_cp("arbitrary"), name="head_norm",
    )(*args)


def _nsa_prep_kernel(q_ref, kvc_ref, kvs_ref, kvw_ref, ng_ref, gq_ref, gk_ref, c_ref, s1_ref, s2_ref,
                     qn_ref, kc_ref, vc_ref, ks_ref, vs_ref, kw_ref, vw_ref, gate_ref):
    dh, half = HEAD_DIM, ROPE_DIMS // 2
    tm = q_ref.shape[0]
    tabs = (c_ref[...], s1_ref[...], s2_ref[...])

    def norm_rope(x, gain):
        y = x * lax.rsqrt(jnp.mean(x * x, axis=-1, keepdims=True) + EPS) * gain
        return _rope(y, *tabs, half)

    for h in range(NSA_HEADS):
        qn_ref[:, h * dh:(h + 1) * dh] = norm_rope(q_ref[:, h * dh:(h + 1) * dh], gq_ref[...])
    kvw = NSA_KV_HEADS * dh
    for b, (src, k_out, v_out) in enumerate(((kvc_ref, kc_ref, vc_ref), (kvs_ref, ks_ref, vs_ref),
                                             (kvw_ref, kw_ref, vw_ref))):
        for g in range(NSA_KV_HEADS):
            rows = pl.ds(g, tm, stride=NSA_KV_HEADS)
            k_out[rows, :] = norm_rope(src[:, g * dh:(g + 1) * dh], gk_ref[b:b + 1, :])
            v_out[rows, :] = src[:, kvw + g * dh:kvw + (g + 1) * dh]
    gate_ref[...] = jax.nn.sigmoid(ng_ref[...])


def _nsa_prep(z, seq, tables, gq, gk, c_ng):
    n = z.shape[0]
    dh = HEAD_DIM
    kvw = NSA_KV_HEADS * dh
    qw = NSA_HEADS * dh
    tm = min(512, seq) if seq > 1 else n
    assert n % tm == 0 and C_NQ % qw == 0 and C_VC == C_KC + kvw and C_VS == C_KS + kvw and C_VW == C_KW + kvw
    pair = lambda col: pl.BlockSpec((tm, 2 * kvw), lambda i: (i, col // (2 * kvw)))
    if seq > 1:
        nb = seq // tm
        tspec = pl.BlockSpec((tm, dh), lambda i: (i % nb, 0))
    else:
        tspec = pl.BlockSpec((1, dh), lambda i: (0, 0))
    rows_spec = pl.BlockSpec((tm * NSA_KV_HEADS, dh), lambda i: (i, 0))
    rows_shape = jax.ShapeDtypeStruct((n * NSA_KV_HEADS, dh), F32)
    return pl.pallas_call(
        _nsa_prep_kernel, grid=(n // tm,),
        in_specs=[pl.BlockSpec((tm, qw), lambda i: (i, C_NQ // qw)), pair(C_KC), pair(C_KS), pair(C_KW),
                  pl.BlockSpec((tm, LANES), lambda i: (i, c_ng // LANES)),
                  pl.BlockSpec((1, dh), lambda i: (0, 0)), pl.BlockSpec((3, dh), lambda i: (0, 0)),
                  tspec, tspec, tspec],
        out_specs=[pl.BlockSpec((tm, qw), lambda i: (i, 0))] + [rows_spec] * 6
                  + [pl.BlockSpec((tm, LANES), lambda i: (i, 0))],
        out_shape=[jax.ShapeDtypeStruct((n, qw), F32)] + [rows_shape] * 6 + [jax.ShapeDtypeStruct((n, LANES), F32)],
        compiler_params=_cp("arbitrary"), name="nsa_prep",
    )(z, z, z, z, z, gq.reshape(1, dh).astype(F32), gk.astype(F32), *tables)


def _log_gamma():
    return np.log1p(-(np.float32(2.0) ** (-5.0 - np.arange(RET_HEADS, dtype=np.float32)))).astype(np.float32)


def _ret_finish(o, gn, gate):
    y = o * lax.rsqrt(jnp.mean(o * o, axis=-1, keepdims=True) + EPS) * gn
    return y * (gate * jax.nn.sigmoid(gate))


def _ret_prompt_kernel(q_ref, k_ref, v_ref, gt_ref, c_ref, s1_ref, s2_ref, st0_ref, gn_ref, o_ref, st_ref):
    cl = q_ref.shape[0]

    @pl.when(pl.program_id(1) == 0)
    def _():
        st_ref[...] = st0_ref[...]

    half = RET_DK // 2
    ri = lax.broadcasted_iota(I32, (cl, cl), 0).astype(F32)
    ci = lax.broadcasted_iota(I32, (cl, cl), 1).astype(F32)
    rel = ri - ci
    rowi = lax.broadcasted_iota(I32, (cl, 1), 0).astype(F32)
    for h, lg in enumerate(_log_gamma().tolist()):
        ks, vs = slice(h * RET_DK, (h + 1) * RET_DK), slice(h * RET_DV, (h + 1) * RET_DV)
        q = _rope(q_ref[:, ks], c_ref[...], s1_ref[...], s2_ref[...], half)
        k = _rope(k_ref[:, ks], c_ref[...], s1_ref[...], s2_ref[...], half) * (RET_DK ** -0.5)
        decay = jnp.where(rel >= 0, jnp.exp(jnp.maximum(rel, 0.0) * lg), 0.0)
        qb, kb, vb = q.astype(BF16), k.astype(BF16), v_ref[:, vs].astype(BF16)
        inner = _nt_dot(qb, kb) * decay
        state = st_ref[h]
        cross = jnp.exp((rowi + 1.0) * lg)
        o = jnp.dot(inner.astype(BF16), vb, preferred_element_type=F32)
        o = o + jnp.dot(qb, state.astype(BF16), preferred_element_type=F32) * cross
        k_dec = (k * jnp.exp((cl - 1.0 - rowi) * lg)).astype(BF16)
        st_ref[h] = float(np.exp(np.float32(cl) * np.float32(lg))) * state + _tn_dot(k_dec, vb)
        o_ref[:, vs] = _ret_finish(o, gn_ref[h], gt_ref[:, vs]).astype(o_ref.dtype)


def _retention_prompt(z3, tables, state0, gn):
    b, t, _ = z3.shape
    cl = RET_CHUNK
    assert t % cl == 0
    wq, wv = RET_HEADS * RET_DK, RET_HEADS * RET_DV
    tspec = pl.BlockSpec((cl, RET_DK), lambda bi, c: (c, 0))
    st_spec = pl.BlockSpec((None, RET_HEADS, RET_DK, RET_DV), lambda bi, c: (bi, 0, 0, 0))
    o, st = pl.pallas_call(
        _ret_prompt_kernel,
        grid=(b, t // cl),
        in_specs=[pl.BlockSpec((None, cl, wq), lambda bi, c: (bi, c, C_RQ // wq)),
                  pl.BlockSpec((None, cl, wq), lambda bi, c: (bi, c, C_RK // wq)),
                  pl.BlockSpec((None, cl, wv), lambda bi, c: (bi, c, C_RV // wv)),
                  pl.BlockSpec((None, cl, wv), lambda bi, c: (bi, c, C_RG // wv)),
                  tspec, tspec, tspec, st_spec,
                  pl.BlockSpec((RET_HEADS, 1, RET_DV), lambda bi, c: (0, 0, 0))],
        out_specs=[pl.BlockSpec((None, cl, wv), lambda bi, c: (bi, c, 0)), st_spec],
        out_shape=[jax.ShapeDtypeStruct((b, t, wv), BF16),
                   jax.ShapeDtypeStruct((b, RET_HEADS, RET_DK, RET_DV), F32)],
        compiler_params=_cp("arbitrary", "arbitrary"), name="retention_prompt",
    )(z3, z3, z3, z3, *tables, state0, gn.reshape(RET_HEADS, 1, RET_DV))
    return o, st


def _ret_decode_kernel(q_ref, k_ref, v_ref, gt_ref, c_ref, s1_ref, s2_ref, st0_ref, gn_ref, o_ref, st_ref):
    lgs = _log_gamma()
    half = RET_DK // 2
    row0 = lax.broadcasted_iota(I32, (16, 1), 0) == 0
    for s, h in [(s, h) for s in range(q_ref.shape[0]) for h in range(RET_HEADS)]:
        gamma = float(np.exp(lgs[h]))
        ks, vs = slice(h * RET_DK, (h + 1) * RET_DK), slice(h * RET_DV, (h + 1) * RET_DV)
        q = _rope(q_ref[s, :, ks], c_ref[...], s1_ref[...], s2_ref[...], half)
        k = _rope(k_ref[s, :, ks], c_ref[...], s1_ref[...], s2_ref[...], half) * (RET_DK ** -0.5)
        v = v_ref[s, :, vs]
        state = st0_ref[s, h].astype(F32)
        inner = jnp.sum(q * k, axis=-1, keepdims=True)
        q16 = jnp.broadcast_to(q, (16, RET_DK)).astype(BF16)
        cross = jnp.dot(q16, state.astype(BF16), preferred_element_type=F32)[0:1]
        o = inner * v + cross * gamma
        k16 = jnp.where(row0, jnp.broadcast_to(k, (16, RET_DK)), 0.0).astype(BF16)
        v16 = jnp.broadcast_to(v, (16, RET_DV)).astype(BF16)
        st_ref[s, h] = gamma * state + _tn_dot(k16, v16)
        o_ref[s, :, vs] = _ret_finish(o, gn_ref[h], gt_ref[s, :, vs]).astype(o_ref.dtype)


def _retention_decode(z3, tables, state0, gn):
    b = z3.shape[0]
    wq, wv = RET_HEADS * RET_DK, RET_HEADS * RET_DV
    tspec = pl.BlockSpec((1, RET_DK), lambda bi: (0, 0))
    sb = 4 if b % 4 == 0 else 1
    st_spec = pl.BlockSpec((sb, RET_HEADS, RET_DK, RET_DV), lambda bi: (bi, 0, 0, 0))
    o, st = pl.pallas_call(
        _ret_decode_kernel,
        grid=(b // sb,),
        in_specs=[pl.BlockSpec((sb, 1, wq), lambda bi: (bi, 0, C_RQ // wq)),
                  pl.BlockSpec((sb, 1, wq), lambda bi: (bi, 0, C_RK // wq)),
                  pl.BlockSpec((sb, 1, wv), lambda bi: (bi, 0, C_RV // wv)),
                  pl.BlockSpec((sb, 1, wv), lambda bi: (bi, 0, C_RG // wv)),
                  tspec, tspec, tspec, st_spec,
                  pl.BlockSpec((RET_HEADS, 1, RET_DV), lambda bi: (0, 0, 0))],
        out_specs=[pl.BlockSpec((sb, 1, wv), lambda bi: (bi, 0, 0)), st_spec],
        out_shape=[jax.ShapeDtypeStruct((b, 1, wv), BF16),
                   jax.ShapeDtypeStruct((b, RET_HEADS, RET_DK, RET_DV), F32)],
        compiler_params=_cp("arbitrary"), name="retention_decode",
    )(z3, z3, z3, z3, *tables, state0, gn.reshape(RET_HEADS, 1, RET_DV))
    return o, st


def _compress_weights(pos_emb, w):
    g, dh = NSA_KV_HEADS, HEAD_DIM
    eye = jnp.eye(g, dtype=w.dtype)
    wbig = jnp.einsum("lde,gh->lgdhe", w, eye).reshape(CMP_BLOCK * g * dh, g * dh)
    pos_row = jnp.broadcast_to(pos_emb[:, None, :], (CMP_BLOCK, g, dh)).reshape(1, CMP_BLOCK * g * dh)
    return pos_row, wbig.astype(BF16)


def _kv_rows(a):
    return a.reshape(-1, a.shape[-1])


def _compress_pool_kernel(x_ref, pos_ref, w_ref, o_ref, *, page, groups):
    dh = x_ref.shape[1]
    p = x_ref.shape[0] // (page * groups)
    per_page = page // CMP_BLOCK
    for g in range(groups):
        acc = jnp.zeros((per_page * p, dh), F32)
        for l in range(CMP_BLOCK):
            xl = jnp.concatenate([x_ref[pl.ds((n * CMP_BLOCK + l) * groups + g, p, stride=page * groups), :]
                                  for n in range(per_page)], axis=0)
            xl = (xl + pos_ref[l:l + 1, :]).astype(BF16)
            acc = acc + jnp.dot(xl, w_ref[l], preferred_element_type=F32)
        o_ref[:, :, g * dh:(g + 1) * dh] = acc.reshape(per_page, p, dh)


def _compress_pool(pool, pos_emb, w, name):
    _, n_pool, page, groups, dh = pool.shape
    per_page = page // CMP_BLOCK
    p = max(c for c in range(8, 65, 8) if n_pool % c == 0)
    return pl.pallas_call(
        functools.partial(_compress_pool_kernel, page=page, groups=groups), grid=(n_pool // p,),
        in_specs=[pl.BlockSpec((p * page * groups, dh), lambda i: (i, 0)),
                  pl.BlockSpec((CMP_BLOCK, dh), lambda i: (0, 0)),
                  pl.BlockSpec((CMP_BLOCK, dh, dh), lambda i: (0, 0, 0))],
        out_specs=pl.BlockSpec((per_page, p, groups * dh), lambda i: (0, i, 0)),
        out_shape=jax.ShapeDtypeStruct((per_page, n_pool, groups * dh), F32),
        compiler_params=_cp("arbitrary"), name=name,
    )(_kv_rows(pool), pos_emb, w.astype(BF16))


def _masked_softmax_parts(parts, masks, axis=-1):
    sm = [jnp.where(m, s, NEG_INF) for s, m in zip(parts, masks)]
    mx = functools.reduce(jnp.maximum, [jnp.max(s, axis=axis, keepdims=True) for s in sm])
    ex = [jnp.exp(s - mx) for s in sm]
    den = functools.reduce(lambda a, b2: a + b2, [jnp.sum(e, axis=axis, keepdims=True) for e in ex])
    inv = 1.0 / den
    return [jnp.where(m, e * inv, 0.0) for e, m in zip(ex, masks)]


def _select_blocks(score, nsel, axis=-1):
    axis = axis % score.ndim
    nbs = score.shape[axis]
    pos = lax.broadcasted_iota(I32, score.shape, axis)
    sel = jnp.zeros(score.shape, jnp.bool_)
    picks = []
    for _ in range(nsel):
        m = jnp.max(score, axis=axis, keepdims=True)
        idx = jnp.min(jnp.where(score == m, pos, nbs), axis=axis, keepdims=True)
        hit = pos == idx
        sel = jnp.logical_or(sel, hit)
        score = jnp.where(hit, -jnp.inf, score)
        picks.append(idx)
    return sel, picks


def _nsa_prompt_kernel(q_ref, ck_ref, cv_ref, ks_ref, vs_ref, kw_ref, vw_ref, gt_ref, o_ref, osel_ref):
    i = pl.program_id(1)
    qb = q_ref.shape[0]
    t = ks_ref.shape[0] // NSA_KV_HEADS
    nbs = t // SEL_BLOCK
    hg, dh = HEADS_PER_GROUP, HEAD_DIM
    kvw = NSA_KV_HEADS * dh
    scale = dh ** -0.5
    pos1 = i * qb + lax.broadcasted_iota(I32, (qb, 1), 0)
    pos = jnp.concatenate([pos1] * hg, axis=0)
    blk = lax.broadcasted_iota(I32, (nbs, 1), 0)
    pos_row = i * qb + lax.broadcasted_iota(I32, (1, hg * qb), 1) % qb
    pos1_row = pos_row[:, :qb]
    vis_e = blk * SEL_BLOCK + (CMP_BLOCK - 1) <= pos_row
    vis_o = blk * SEL_BLOCK + (SEL_BLOCK - 1) <= pos_row
    valid = blk * SEL_BLOCK <= pos1_row
    forced = jnp.logical_or(blk == 0, blk == pos1_row // SEL_BLOCK)
    n_case = 4 if t % (4 * qb) == 0 else 1
    span = t // n_case
    case = (i * qb) // span

    def selected_attend(limit, q, sel_f, g):
        key = lax.broadcasted_iota(I32, (1, limit), 1)
        expand = (lax.broadcasted_iota(I32, (nbs, limit), 1) // SEL_BLOCK
                  == lax.broadcasted_iota(I32, (nbs, limit), 0)).astype(BF16)
        selk = _tn_dot(sel_f, expand)
        smask = jnp.logical_and(jnp.concatenate([selk] * hg, axis=0) > 0.5, key <= pos)
        rows = pl.ds(g, limit, stride=NSA_KV_HEADS)
        (p_s,) = _masked_softmax_parts([_nt_dot(q, ks_ref[rows, :].astype(BF16)) * scale], [smask])
        osel_ref[...] = jnp.dot(p_s.astype(BF16), vs_ref[rows, :].astype(BF16), preferred_element_type=F32)

    wlen = WINDOW + qb
    wstart = pl.multiple_of(jnp.maximum(i * qb - WINDOW, 0), qb)
    wkey = wstart + lax.broadcasted_iota(I32, (1, wlen), 1)
    wdist = pos - wkey
    wmask = jnp.logical_and(wdist >= 0, wdist <= WINDOW)
    gates = gt_ref[...]

    for g in range(NSA_KV_HEADS):
        cs = slice(g * dh, (g + 1) * dh)
        q = jnp.concatenate([q_ref[:, (g * hg + hh) * dh:(g * hg + hh + 1) * dh] for hh in range(hg)],
                            axis=0).astype(BF16)
        co = slice(kvw + g * dh, kvw + (g + 1) * dh)
        ck_e, ck_o = ck_ref[:, cs].astype(BF16), ck_ref[:, co].astype(BF16)
        cv_e, cv_o = cv_ref[:, cs].astype(BF16), cv_ref[:, co].astype(BF16)
        p_e, p_o = _masked_softmax_parts([_nt_dot(ck_e, q) * scale, _nt_dot(ck_o, q) * scale], [vis_e, vis_o],
                                         axis=0)
        o_cmp = _tn_dot(p_e.astype(BF16), cv_e) + _tn_dot(p_o.astype(BF16), cv_o)
        psum = p_e + p_o
        imp = functools.reduce(lambda a, b2: a + b2, [psum[:, hh * qb:(hh + 1) * qb] for hh in range(hg)])
        score = jnp.where(forced, FORCE, jnp.where(valid, imp, -FORCE))
        sel, _ = _select_blocks(score, min(SEL_TOPK, nbs), axis=0)
        sel_f = jnp.where(sel, 1.0, 0.0).astype(BF16)
        for c in range(n_case):
            pl.when(case == c)(functools.partial(selected_attend, (c + 1) * span, q, sel_f, g))
        o_sel = osel_ref[...]
        wrows = pl.ds(wstart * NSA_KV_HEADS + g, wlen, stride=NSA_KV_HEADS)
        kwin = kw_ref[wrows, :].astype(BF16)
        vwin = vw_ref[wrows, :].astype(BF16)
        (p_w,) = _masked_softmax_parts([_nt_dot(q, kwin) * scale], [wmask])
        o_win = jnp.dot(p_w.astype(BF16), vwin, preferred_element_type=F32)
        for hh in range(hg):
            hd = g * hg + hh
            rs = slice(hh * qb, (hh + 1) * qb)
            o_ref[:, hd * dh:(hd + 1) * dh] = (gates[:, 3 * hd:3 * hd + 1] * o_cmp[rs]
                                                 + gates[:, 3 * hd + 1:3 * hd + 2] * o_sel[rs]
                                                 + gates[:, 3 * hd + 2:3 * hd + 3] * o_win[rs]).astype(o_ref.dtype)


def _nsa_prompt(qn3, ck3, cv3, ks3, vs3, kw3, vw3, gates3):
    b, t, _ = qn3.shape
    qb = Q_BLOCK
    kvw = NSA_KV_HEADS * HEAD_DIM
    assert t % qb == 0 and t >= WINDOW + qb and t % (2 * SEL_BLOCK) == 0
    nbs = ck3.shape[1]
    full = pl.BlockSpec((None, t * NSA_KV_HEADS, HEAD_DIM), lambda bi, i: (bi, 0, 0))
    return pl.pallas_call(
        _nsa_prompt_kernel,
        grid=(b, t // qb),
        in_specs=[pl.BlockSpec((None, qb, NSA_HEADS * HEAD_DIM), lambda bi, i: (bi, i, 0)),
                  pl.BlockSpec((None, nbs, CMP_PER_SEL * kvw), lambda bi, i: (bi, 0, 0)),
                  pl.BlockSpec((None, nbs, CMP_PER_SEL * kvw), lambda bi, i: (bi, 0, 0)),
                  full, full, full, full,
                  pl.BlockSpec((None, qb, LANES), lambda bi, i: (bi, i, 0))],
        out_specs=pl.BlockSpec((None, qb, NSA_HEADS * HEAD_DIM), lambda bi, i: (bi, i, 0)),
        out_shape=jax.ShapeDtypeStruct((b, t, NSA_HEADS * HEAD_DIM), BF16),
        scratch_shapes=[pltpu.VMEM((HEADS_PER_GROUP * qb, HEAD_DIM), F32)],
        compiler_params=_cp("arbitrary", "arbitrary"), name="nsa_prompt",
    )(qn3, ck3, cv3, ks3, vs3, kw3, vw3, gates3)


def _decode_q16(q_ref):
    dh = HEAD_DIM
    rows = [q_ref[:, h * dh:(h + 1) * dh] for h in range(NSA_HEADS)]
    return jnp.concatenate(rows + [jnp.zeros((16 - NSA_HEADS, dh), F32)], axis=0).astype(BF16)


def _nsa_decode_select_kernel(q_ref, ck_ref, cv_ref, ocmp_ref, idx_ref, *, past):
    for s in range(q_ref.shape[0]):
        _nsa_decode_select_one(q_ref.at[s], ck_ref.at[s], cv_ref.at[s], ocmp_ref.at[s], idx_ref.at[s], past)


def _nsa_decode_select_one(q_ref, ck_ref, cv_ref, ocmp_ref, idx_ref, past):
    dh, hg = HEAD_DIM, HEADS_PER_GROUP
    kvw = NSA_KV_HEADS * dh
    nhalf = ck_ref.shape[0]
    nbs = past // SEL_BLOCK + 1
    scale = dh ** -0.5
    q16 = _decode_q16(q_ref)
    row = lax.broadcasted_iota(I32, (16, 1), 0)
    blk = lax.broadcasted_iota(I32, (1, LANES), 1)
    blk_h = lax.broadcasted_iota(I32, (1, nhalf), 1)
    vis_e = blk_h * SEL_BLOCK + (CMP_BLOCK - 1) <= past
    vis_o = blk_h * SEL_BLOCK + (SEL_BLOCK - 1) <= past
    o_all = jnp.zeros((16, dh), F32)
    idx_row = jnp.zeros((1, LANES), I32)
    for g in range(NSA_KV_HEADS):
        cs = slice(g * dh, (g + 1) * dh)
        co = slice(kvw + g * dh, kvw + (g + 1) * dh)
        ck_e, ck_o = ck_ref[:, cs].astype(BF16), ck_ref[:, co].astype(BF16)
        cv_e, cv_o = cv_ref[:, cs].astype(BF16), cv_ref[:, co].astype(BF16)
        p_e, p_o = _masked_softmax_parts([_nt_dot(q16, ck_e) * scale, _nt_dot(q16, ck_o) * scale], [vis_e, vis_o])
        o_g = (jnp.dot(p_e.astype(BF16), cv_e, preferred_element_type=F32)
               + jnp.dot(p_o.astype(BF16), cv_o, preferred_element_type=F32))
        in_g = jnp.logical_and(row >= g * hg, row < (g + 1) * hg)
        o_all = jnp.where(in_g, o_g, o_all)
        imp = jnp.sum(jnp.where(in_g, p_e + p_o, 0.0), axis=0, keepdims=True)
        score = jnp.where(blk_h == 0, FORCE, imp)
        nsel = min(SEL_TOPK, nbs)
        _, past_picks = _select_blocks(score, min(nsel - 1, nhalf))
        picks = past_picks[:1] + [jnp.full((1, 1), nhalf, I32)] + past_picks[1:]
        for kk, p in enumerate(picks):
            idx_row = jnp.where(blk == g * SEL_TOPK + kk, p, idx_row)
        for kk in range(len(picks), SEL_TOPK):
            idx_row = jnp.where(blk == g * SEL_TOPK + kk, -1, idx_row)
    ocmp_ref[...] = o_all
    idx_ref[...] = idx_row


def _nsa_decode_select(qn3, ck3, cv3, past):
    b = qn3.shape[0]
    nhalf = ck3.shape[1]
    kvw = NSA_KV_HEADS * HEAD_DIM
    assert nhalf % 8 == 0 and nhalf == past // SEL_BLOCK
    sb = 8 if b % 8 == 0 else 1
    return pl.pallas_call(
        functools.partial(_nsa_decode_select_kernel, past=past),
        grid=(b // sb,),
        in_specs=[pl.BlockSpec((sb, 1, NSA_HEADS * HEAD_DIM), lambda bi: (bi, 0, 0)),
                  pl.BlockSpec((sb, nhalf, CMP_PER_SEL * kvw), lambda bi: (bi, 0, 0)),
                  pl.BlockSpec((sb, nhalf, CMP_PER_SEL * kvw), lambda bi: (bi, 0, 0))],
        out_specs=[pl.BlockSpec((sb, 16, HEAD_DIM), lambda bi: (bi, 0, 0)),
                   pl.BlockSpec((sb, 1, LANES), lambda bi: (bi, 0, 0))],
        out_shape=[jax.ShapeDtypeStruct((b, 16, HEAD_DIM), F32),
                   jax.ShapeDtypeStruct((b, 1, LANES), I32)],
        compiler_params=_cp("arbitrary"), name="nsa_decode_select",
    )(qn3, ck3, cv3)


def _nsa_decode_attend_kernel(*refs, past, nsel):
    n_blk = NSA_KV_HEADS * nsel
    rows_ref = refs[0]
    kb_refs = refs[1:1 + n_blk]
    vb_refs = refs[1 + n_blk:1 + 2 * n_blk]
    (q_ref, idx_ref, ksn_ref, vs_ref, wk_ref, wv_ref, kwn_ref, vw_ref, ocmp_ref, gt_ref, o_ref) = refs[1 + 2 * n_blk:]
    del rows_ref
    dh, hg = HEAD_DIM, HEADS_PER_GROUP
    scale = dh ** -0.5
    self_blk = past // SEL_BLOCK
    q16 = _decode_q16(q_ref)
    q16f = q16.astype(F32)
    row = lax.broadcasted_iota(I32, (16, 1), 0)
    lane = lax.broadcasted_iota(I32, (1, LANES), 1)
    idx_row = idx_ref[...]
    o_sel = jnp.zeros((16, dh), F32)
    o_win = jnp.zeros((16, dh), F32)
    for g in range(NSA_KV_HEADS):
        cs = slice(g * dh, (g + 1) * dh)
        in_g = jnp.logical_and(row >= g * hg, row < (g + 1) * hg)
        k_self = ksn_ref[:, cs].astype(BF16).astype(F32)
        v_self = vs_ref[:, cs].astype(BF16).astype(F32)
        s_self = jnp.sum(q16f * k_self, axis=-1, keepdims=True) * scale
        parts, masks, vals = [], [], []
        self_sel = jnp.zeros((1, 1), jnp.bool_)
        for kk in range(nsel):
            pick = jnp.max(jnp.where(lane == g * SEL_TOPK + kk, idx_row, -1), axis=-1, keepdims=True)
            in_pool = jnp.logical_and(pick >= 0, pick < self_blk)
            self_sel = jnp.logical_or(self_sel, pick == self_blk)
            grp = pl.ds(g, SEL_BLOCK, stride=NSA_KV_HEADS)
            kb = kb_refs[g * nsel + kk][grp, :].astype(BF16)
            parts.append(_nt_dot(q16, kb) * scale)
            masks.append(in_pool)
            vals.append(vb_refs[g * nsel + kk][grp, :].astype(BF16))
        parts.append(s_self)
        masks.append(self_sel)
        probs = _masked_softmax_parts(parts, masks)
        acc = probs[-1].astype(BF16).astype(F32) * v_self
        for p, vv in zip(probs[:-1], vals):
            acc = acc + jnp.dot(p.astype(BF16), vv, preferred_element_type=F32)
        o_sel = jnp.where(in_g, acc, o_sel)
        kw_self = kwn_ref[:, cs].astype(BF16).astype(F32)
        vw_self = vw_ref[:, cs].astype(BF16).astype(F32)
        sw_self = jnp.sum(q16f * kw_self, axis=-1, keepdims=True) * scale
        wgrp = pl.ds(g, wk_ref.shape[0] // NSA_KV_HEADS, stride=NSA_KV_HEADS)
        wparts = [_nt_dot(q16, wk_ref[wgrp, :].astype(BF16)) * scale, sw_self]
        wmasks = [lane[:, :1] >= 0, lane[:, :1] >= 0]
        pw, pws = _masked_softmax_parts(wparts, wmasks)
        accw = (jnp.dot(pw.astype(BF16), wv_ref[wgrp, :].astype(BF16), preferred_element_type=F32)
                + pws.astype(BF16).astype(F32) * vw_self)
        o_win = jnp.where(in_g, accw, o_win)
    gates = gt_ref[...]
    o_cmp = ocmp_ref[...]
    for hd in range(NSA_HEADS):
        o_ref[:, hd * dh:(hd + 1) * dh] = (gates[:, 3 * hd:3 * hd + 1] * o_cmp[hd:hd + 1]
                                             + gates[:, 3 * hd + 1:3 * hd + 2] * o_sel[hd:hd + 1]
                                             + gates[:, 3 * hd + 2:3 * hd + 3] * o_win[hd:hd + 1]).astype(o_ref.dtype)


def _nsa_decode_attend(pool_rows, sel_k_pool, sel_v_pool, qn3, idx3, ksn3, vs3, win_k, win_v, kwn3, vw3, ocmp,
                       gates3, past, nsel):
    b = qn3.shape[0]
    kvw = NSA_KV_HEADS * HEAD_DIM
    n_blk = NSA_KV_HEADS * nsel
    nbuf = win_k.shape[2]

    def blk_spec(tt):
        return pl.BlockSpec((SEL_BLOCK * NSA_KV_HEADS, HEAD_DIM), lambda bi, r: (r[bi * n_blk + tt], 0))

    row_spec = pl.BlockSpec((None, 1, kvw), lambda bi, r: (bi, 0, 0))
    wspec = pl.BlockSpec((nbuf * NSA_KV_HEADS, HEAD_DIM), lambda bi, r: (bi, 0))
    sel_k_pool, sel_v_pool, win_k, win_v = (_kv_rows(a) for a in (sel_k_pool, sel_v_pool, win_k, win_v))
    in_specs = ([blk_spec(tt) for tt in range(n_blk)] + [blk_spec(tt) for tt in range(n_blk)]
                + [pl.BlockSpec((None, 1, NSA_HEADS * HEAD_DIM), lambda bi, r: (bi, 0, 0)),
                   pl.BlockSpec((None, 1, LANES), lambda bi, r: (bi, 0, 0)),
                   row_spec, row_spec, wspec, wspec, row_spec, row_spec,
                   pl.BlockSpec((None, 16, HEAD_DIM), lambda bi, r: (bi, 0, 0)),
                   pl.BlockSpec((None, 1, LANES), lambda bi, r: (bi, 0, 0))])
    return pl.pallas_call(
        functools.partial(_nsa_decode_attend_kernel, past=past, nsel=nsel),
        grid_spec=pltpu.PrefetchScalarGridSpec(
            num_scalar_prefetch=1, grid=(b,), in_specs=in_specs,
            out_specs=pl.BlockSpec((None, 1, NSA_HEADS * HEAD_DIM), lambda bi, r: (bi, 0, 0))),
        out_shape=jax.ShapeDtypeStruct((b, 1, NSA_HEADS * HEAD_DIM), BF16),
        compiler_params=_cp("arbitrary"), name="nsa_decode_attend",
    )(pool_rows, *([sel_k_pool] * n_blk), *([sel_v_pool] * n_blk),
      qn3, idx3, ksn3, vs3, win_k, win_v, kwn3, vw3, ocmp, gates3)


def _mem_attn_kernel(*refs, cache_layout):
    dh = MEM_HEAD_DIM
    if cache_layout:
        q_ref, g_ref, mk0_ref, mk1_ref, mv0_ref, mv1_ref, o_ref = refs
    else:
        q_ref, g_ref, mk_ref, mv_ref, o_ref = refs
    tq = q_ref.shape[0]
    rows = max(tq, 16)
    for h in range(MEM_HEADS):
        cs = slice(h * dh, (h + 1) * dh)
        x = q_ref[:, cs]
        q = x * lax.rsqrt(jnp.mean(x * x, axis=-1, keepdims=True) + EPS) * g_ref[...]
        if rows != tq:
            q = jnp.broadcast_to(q, (rows, dh))
        if cache_layout:
            mk = jnp.concatenate([mk0_ref[:, h, :], mk1_ref[:, h, :]], axis=1)
            mv = jnp.concatenate([mv0_ref[:, h, :], mv1_ref[:, h, :]], axis=1)
        else:
            mk, mv = mk_ref[:, cs], mv_ref[:, cs]
        s = _nt_dot(q.astype(BF16), mk.astype(BF16)) * (dh ** -0.5)
        m = jnp.max(s, axis=-1, keepdims=True)
        e = jnp.exp(s - m)
        p = e / jnp.sum(e, axis=-1, keepdims=True)
        o = jnp.dot(p.astype(BF16), mv.astype(BF16), preferred_element_type=F32)
        o_ref[:, cs] = o[0:tq].astype(o_ref.dtype)


def _mem_attention(z3, gq, mk, mv, cache_layout=False):
    b, t, _ = z3.shape
    w = MEM_HEADS * MEM_HEAD_DIM
    tq = min(t, 512)
    assert t % tq == 0
    if cache_layout:
        m = mk.shape[2]
        assert MEM_HEAD_DIM == 2 * LANES
        half = lambda c: pl.BlockSpec((None, None, m, MEM_HEADS, LANES), lambda bi, i: (0, bi, 0, 0, c))
        kv_specs, kv_args = [half(0), half(1), half(0), half(1)], [mk, mk, mv, mv]
    else:
        m = mk.shape[1]
        kv_specs, kv_args = [pl.BlockSpec((None, m, w), lambda bi, i: (bi, 0, 0))] * 2, [mk, mv]
    return pl.pallas_call(
        functools.partial(_mem_attn_kernel, cache_layout=cache_layout),
        grid=(b, t // tq),
        in_specs=[pl.BlockSpec((None, tq, w), lambda bi, i: (bi, i, C_MQ // w)),
                  pl.BlockSpec((1, MEM_HEAD_DIM), lambda bi, i: (0, 0))] + kv_specs,
        out_specs=pl.BlockSpec((None, tq, w), lambda bi, i: (bi, i, 0)),
        out_shape=jax.ShapeDtypeStruct((b, t, w), BF16),
        compiler_params=_cp("arbitrary", "arbitrary"), name="mem_attention",
    )(z3, gq.reshape(1, MEM_HEAD_DIM), *kv_args)


def _merge_kernel(r_ref, n_ref, m_ref, g0_ref, g1_ref, g2_ref, w_ref, o_ref):
    acc = jnp.zeros(o_ref.shape, F32)
    for c, (b_ref, g_ref) in enumerate(((r_ref, g0_ref), (n_ref, g1_ref), (m_ref, g2_ref))):
        up = jnp.dot(b_ref[...].astype(BF16), w_ref[c], preferred_element_type=F32)
        acc = acc + jax.nn.sigmoid(g_ref[...]) * up
    o_ref[...] = acc.astype(BF16)


def _merge(o_ret, o_nsa, o_mem, z, wb, d_model):
    n, bw = o_ret.shape
    tm = min(n, 1024)
    tn = 512
    gb = C_MG // tn
    nj = d_model // tn
    bspec = pl.BlockSpec((tm, bw), lambda i, j: (i, 0))
    gspec = lambda c: pl.BlockSpec((tm, tn), lambda i, j: (i, gb + c * nj + j))
    return pl.pallas_call(
        _merge_kernel, grid=(n // tm, nj),
        in_specs=[bspec, bspec, bspec, gspec(0), gspec(1), gspec(2),
                  pl.BlockSpec((N_BRANCH, bw, tn), lambda i, j: (0, 0, j))],
        out_specs=pl.BlockSpec((tm, tn), lambda i, j: (i, j)),
        out_shape=jax.ShapeDtypeStruct((n, d_model), BF16),
        compiler_params=_cp("arbitrary", "arbitrary"), name="branch_merge",
    )(o_ret, o_nsa, o_mem, z, z, z, wb)


def _topk_rows(s, kk):
    e, tl = s.shape
    nt = e // 8
    assert e % 8 == 0 and nt & (nt - 1) == 0
    sub = lax.broadcasted_iota(I32, (8, tl), 0).astype(F32)
    vals = [s[8 * k:8 * k + 8] for k in range(nt)]
    idxs = [sub + float(8 * k) for k in range(nt)]
    for a, b in _sorting_network(nt):
        va, vb, ia, ib = vals[a], vals[b], idxs[a], idxs[b]
        swap = jnp.logical_or(vb > va, jnp.logical_and(vb == va, ib < ia))
        vals[a], vals[b] = jnp.maximum(va, vb), jnp.minimum(va, vb)
        idxs[a], idxs[b] = jnp.where(swap, ib, ia), jnp.where(swap, ia, ib)
    out_v, out_i = [], []
    for step in range(kk):
        m = jnp.max(vals[0], axis=0, keepdims=True)
        i = jnp.min(jnp.where(vals[0] == m, idxs[0], float(e)), axis=0, keepdims=True)
        out_v.append(m)
        out_i.append(i)
        win = idxs[0] == i
        depth = min(nt, kk - step)
        for k in range(depth - 1):
            vals[k] = jnp.where(win, vals[k + 1], vals[k])
            idxs[k] = jnp.where(win, idxs[k + 1], idxs[k])
        vals[depth - 1] = jnp.where(win, -jnp.inf, vals[depth - 1])
    return jnp.concatenate(out_v, axis=0), jnp.concatenate(out_i, axis=0)


def _sorting_network(n):
    def merge(lo, hi, r):
        step = r * 2
        if step < hi - lo:
            yield from merge(lo, hi, step)
            yield from merge(lo + r, hi, step)
            yield from [(i, i + r) for i in range(lo + r, hi - r, step)]
        else:
            yield (lo, lo + r)

    def sort(lo, hi):
        if hi - lo >= 1:
            mid = lo + (hi - lo) // 2
            yield from sort(lo, mid)
            yield from sort(mid + 1, hi)
            yield from merge(lo, hi, 1)

    return list(sort(0, n - 1))


def _pair_pieces(kk):
    pieces, cur = [], []

    def flush(rows):
        pieces.append(rows + [None] * (8 - len(rows)))

    for a in range(kk):
        grp = [(a, b) for b in range(kk // (a + 1))]
        if cur and len(cur) + len(grp) > 8:
            flush(cur)
            cur = []
        cur = cur + grp
        while len(cur) >= 8:
            flush(cur[:8])
            cur = cur[8:]
    if cur:
        flush(cur)
    return pieces


def _rows_by_runs(x, ids):
    if ids[0] % 8 == 0 and ids == list(range(ids[0], ids[0] + 8)):
        return x[ids[0]:ids[0] + 8]
    r = lax.broadcasted_iota(I32, (8, 1), 0)
    runs = []
    for p, i in enumerate(ids):
        if not runs or runs[-1][1] != i:
            runs.append((p, i))
    out = jnp.broadcast_to(x[runs[-1][1]:runs[-1][1] + 1], (8, x.shape[1]))
    for (_, i), (nxt, _) in zip(reversed(runs[:-1]), reversed(runs[1:])):
        out = jnp.where(r < nxt, x[i:i + 1], out)
    return out


ROUTE_HEADS_PER_STEP = 4


def _peer_route_kernel(q_ref, sk_ref, i_ref, j_ref, g_ref):
    kk = PEER_TOPK
    for hh in range(ROUTE_HEADS_PER_STEP):
        rows = slice(hh * kk, (hh + 1) * kk)
        i_ref[rows, :], j_ref[rows, :], g_ref[rows, :] = _peer_route_head(
            q_ref[:, hh * PEER_DKEY:(hh + 1) * PEER_DKEY], sk_ref[hh])


def _peer_route_head(q, sk):
    kk = PEER_TOPK
    st = _nt_dot(sk, q.astype(BF16))
    v0, i0 = _topk_rows(st[0:PEER_KEYS], kk)
    v1, i1 = _topk_rows(st[PEER_KEYS:2 * PEER_KEYS], kk)
    cand, ci, cj = [], [], []
    for piece in _pair_pieces(kk):
        ra = [p[0] if p else 0 for p in piece]
        rb = [p[1] if p else 0 for p in piece]
        live = lax.broadcasted_iota(I32, (8, 1), 0) < sum(p is not None for p in piece)
        cand.append(jnp.where(live, _rows_by_runs(v0, ra) + _rows_by_runs(v1, rb), -jnp.inf))
        ci.append(_rows_by_runs(i0, ra))
        cj.append(_rows_by_runs(i1, rb))
    cand, ci, cj = (jnp.concatenate(c, axis=0) for c in (cand, ci, cj))
    ridx = lax.broadcasted_iota(I32, cand.shape, 0).astype(F32)
    sc, si, sj = [], [], []
    for _ in range(kk):
        m = jnp.max(cand, axis=0, keepdims=True)
        r = jnp.min(jnp.where(cand == m, ridx, float(cand.shape[0])), axis=0, keepdims=True)
        hit = ridx == r
        sc.append(m)
        si.append(jnp.max(jnp.where(hit, ci, -1.0), axis=0, keepdims=True))
        sj.append(jnp.max(jnp.where(hit, cj, -1.0), axis=0, keepdims=True))
        cand = jnp.where(hit, -jnp.inf, cand)
    sc = jnp.concatenate(sc, axis=0)
    e = jnp.exp(sc - jnp.max(sc, axis=0, keepdims=True))
    return (jnp.concatenate(si, axis=0).astype(I32), jnp.concatenate(sj, axis=0).astype(I32),
            e / jnp.sum(e, axis=0, keepdims=True))


def _peer_route(q, skbd):
    n = q.shape[0]
    tl = min(n, LANES)
    kk = PEER_TOPK
    slots = PEER_HEADS * kk
    hps = ROUTE_HEADS_PER_STEP
    ospec = pl.BlockSpec((hps * kk, tl), lambda i, h: (h, i))
    return pl.pallas_call(
        _peer_route_kernel, grid=(n // tl, PEER_HEADS // hps),
        in_specs=[pl.BlockSpec((tl, hps * PEER_DKEY), lambda i, h: (i, h)),
                  pl.BlockSpec((hps, 2 * PEER_KEYS, PEER_DKEY), lambda i, h: (h, 0, 0))],
        out_specs=[ospec, ospec, ospec],
        out_shape=[jax.ShapeDtypeStruct((slots, n), I32), jax.ShapeDtypeStruct((slots, n), I32),
                   jax.ShapeDtypeStruct((slots, n), F32)],
        compiler_params=_cp("arbitrary", "arbitrary"), name="peer_route",
    )(q, skbd)


TOKEN_GROUP = 8


def _peer_weights_kernel(i_ref, j_ref, g_ref, o_ref, it_scr, jt_scr, gt_scr):
    it_scr[...] = i_ref[...].T
    jt_scr[...] = j_ref[...].T
    gt_scr[...] = g_ref[...].T
    key = lax.broadcasted_iota(I32, (PEER_KEYS, i_ref.shape[0]), 0)

    def one_hots(n):
        irow = it_scr[pl.ds(n, 1), :]
        jrow = jt_scr[pl.ds(n, 1), :]
        grow = gt_scr[pl.ds(n, 1), :]
        a = jnp.where(key == irow, grow, 0.0).astype(BF16)
        b = jnp.where(key == jrow, 1.0, 0.0).astype(BF16)
        return a, b

    groups_per_trip = 4 if o_ref.shape[0] % 4 == 0 else 1

    def body(nb2, carry):
        results = []
        for gi in range(groups_per_trip):
            nb = nb2 * groups_per_trip + gi
            for t in range(TOKEN_GROUP):
                a, b = one_hots(nb * TOKEN_GROUP + t)
                results.append((nb, t, _nt_dot(a, b)))
        for nb, t, r in results:
            o_ref[nb, pl.ds(t, PEER_KEYS, stride=TOKEN_GROUP), :] = r
        return carry

    lax.fori_loop(0, o_ref.shape[0] // groups_per_trip, body, 0)


def _peer_weights(it, jt, gt):
    slots, n = it.shape
    tb = min(n, LANES)
    assert tb % TOKEN_GROUP == 0
    ispec = pl.BlockSpec((slots, tb), lambda i: (0, i))
    out = pl.pallas_call(
        _peer_weights_kernel, grid=(n // tb,),
        in_specs=[ispec, ispec, ispec],
        out_specs=pl.BlockSpec((tb // TOKEN_GROUP, PEER_KEYS * TOKEN_GROUP, PEER_KEYS), lambda i: (i, 0, 0)),
        out_shape=jax.ShapeDtypeStruct((n // TOKEN_GROUP, PEER_KEYS * TOKEN_GROUP, PEER_KEYS), F32),
        scratch_shapes=[pltpu.VMEM((tb, slots), I32), pltpu.VMEM((tb, slots), I32), pltpu.VMEM((tb, slots), F32)],
        compiler_params=_cp("arbitrary"), name="peer_weights",
    )(it, jt, gt)
    return out.reshape(n // TOKEN_GROUP, PEER_KEYS, TOKEN_GROUP, PEER_KEYS)


def _peer_dense_kernel(h_ref, g_ref, u_ref, v_ref, x_ref, o_ref):
    @pl.when(pl.program_id(1) == 0)
    def _():
        o_ref[...] = x_ref[...]

    a = jnp.dot(h_ref[...], u_ref[...], preferred_element_type=F32)
    act = 0.5 * a * (1.0 + lax.erf(a * (2.0 ** -0.5)))
    tm = h_ref.shape[0]
    w = jnp.concatenate([(g_ref[:, ii].reshape(tm, PEER_KEYS)
                          * act[:, ii * PEER_KEYS:(ii + 1) * PEER_KEYS]).astype(BF16)
                         for ii in range(g_ref.shape[1])], axis=1)
    o_ref[...] += jnp.dot(w, v_ref[...], preferred_element_type=F32)


def _peer_dense(hn, gw, ut, vv, x):
    n, d = hn.shape
    ne = ut.shape[1]
    tm = min(n, 512)
    te = 1024
    return pl.pallas_call(
        _peer_dense_kernel, grid=(n // tm, ne // te),
        in_specs=[pl.BlockSpec((tm, d), lambda i, j: (i, 0)),
                  pl.BlockSpec((tm // TOKEN_GROUP, te // PEER_KEYS, TOKEN_GROUP, PEER_KEYS),
                               lambda i, j: (i, j, 0, 0)),
                  pl.BlockSpec((d, te), lambda i, j: (0, j)),
                  pl.BlockSpec((te, d), lambda i, j: (j, 0)),
                  pl.BlockSpec((tm, d), lambda i, j: (i, 0))],
        out_specs=pl.BlockSpec((tm, d), lambda i, j: (i, 0)),
        out_shape=jax.ShapeDtypeStruct((n, d), F32),
        compiler_params=_cp("arbitrary", "arbitrary"), name="peer_dense",
    )(hn, gw, ut, vv, x)


def _prepare_weights(norm_attn, w_in, cmp_pos_k, cmp_pos_v, cmp_w_k, cmp_w_v, w_mem_kv, w_branch, w_out,
                     peer_wq, peer_subkeys, peer_u, peer_v):
    d = w_in.shape[0]
    ng_w = NSA_HEADS * 3
    o_ng = 5632
    o_mq = o_ng + ng_w
    o_mg = o_mq + MEM_HEADS * MEM_HEAD_DIM
    width = C_MG + N_BRANCH * d + LANES
    width = -(-width // Z_TILE) * Z_TILE
    w_r = jnp.concatenate([w_in[:, 0:4096], w_in[:, o_mq:o_mg], w_in[:, 4096:o_ng], w_in[:, o_mg:],
                           w_in[:, o_ng:o_mq],
                           jnp.zeros((d, width - (C_MG + N_BRANCH * d + ng_w)), w_in.dtype)], axis=1).astype(BF16)
    pos_k, wk_big = _compress_weights(cmp_pos_k, cmp_w_k)
    pos_v, wv_big = _compress_weights(cmp_pos_v, cmp_w_v)
    half = PEER_DKEY // 2
    sk = peer_subkeys.astype(BF16)
    zk = jnp.zeros((PEER_HEADS, PEER_KEYS, half), BF16)
    skbd = jnp.concatenate([jnp.concatenate([sk[:, 0], zk], axis=2),
                            jnp.concatenate([zk, sk[:, 1]], axis=2)], axis=1)
    return dict(w_in=w_r, c_ng=C_MG + N_BRANCH * d, pos_k=pos_k, wk_big=wk_big, pos_v=pos_v, wv_big=wv_big,
                w_mem_kv=w_mem_kv.astype(BF16), w_branch=w_branch.astype(BF16), w_out=w_out.astype(BF16),
                peer_wq=peer_wq.astype(BF16), skbd=skbd, ut=peer_u.astype(BF16).T, vv=peer_v.astype(BF16))


def _layer_common(x, seq, pos, norm_attn, nsa_q_norm, nsa_k_norm, pw):
    z = _prologue_matmul(x, norm_attn, pw["w_in"], mode="norm", name="in_proj", tm=1024)
    nsa_tabs = _rope_tables(pos, ROPE_DIMS, ROPE_THETA, HEAD_DIM)
    qn, kc, vc, ks, vs, kw, vw, gates = _nsa_prep(z, seq, nsa_tabs, nsa_q_norm, nsa_k_norm, pw["c_ng"])
    ret_tabs = _rope_tables(pos, RET_DK, RET_THETA, RET_DK)
    return z, qn, (kc, vc, ks, vs, kw, vw), gates, ret_tabs


def _layer_tail(x, z, o_ret, o_nsa, o_mem, norm_ffn, pw):
    d = x.shape[1]
    merged = _merge(o_ret, o_nsa, o_mem, z, pw["w_branch"], d)
    x1 = _prologue_matmul(merged, jnp.ones((d,), F32), pw["w_out"], mode="none", residual=x, name="out_proj")
    q, hn = _prologue_matmul(x1, norm_ffn, pw["peer_wq"], mode="norm", emit_h=True, name="peer_query")
    it, jt, gt = _peer_route(q, pw["skbd"])
    return _peer_dense(hn, _peer_weights(it, jt, gt), pw["ut"], pw["vv"], x1)


def kernel(x_prompt, x_sample, cache_cmp_k, cache_cmp_v, cache_sel_k, cache_sel_v, cache_win_k, cache_win_v,
           state_ret, cache_mem_k, cache_mem_v, page_table, mem_prompt, norm_attn, w_in, ret_gn, nsa_q_norm,
           nsa_k_norm, cmp_pos_k, cmp_pos_v, cmp_w_k, cmp_w_v, norm_mem, w_mem_kv, mem_q_norm, mem_k_norm,
           w_branch, w_out, norm_ffn, peer_wq, peer_subkeys, peer_u, peer_v):
    depth = w_in.shape[0]
    assert depth == 1
    l = 0
    bp, t, d = x_prompt.shape
    bs, ts, _ = x_sample.shape
    assert ts == 1
    n_pool, page, g_kv, dh = cache_cmp_k.shape[1:]
    n_pages = page_table.shape[1]
    past = n_pages * page
    kvw = g_kv * dh
    memw = MEM_HEADS * MEM_HEAD_DIM
    pw = _prepare_weights(norm_attn[l], w_in[l], cmp_pos_k[l], cmp_pos_v[l], cmp_w_k[l], cmp_w_v[l], w_mem_kv[l],
                          w_branch[l], w_out[l], peer_wq[l], peer_subkeys[l], peer_u[l], peer_v[l])
    cmp_cols = CMP_BLOCK * kvw

    n_p = bp * t
    xp = x_prompt.reshape(n_p, d)
    m_tok = mem_prompt.shape[1]
    mkv = _prologue_matmul(mem_prompt.reshape(bp * m_tok, d), norm_mem[l], pw["w_mem_kv"], mode="norm",
                           name="mem_kv_proj")
    mk_p = _headnorm(mkv, 0, memw, mem_k_norm[l], MEM_HEAD_DIM, m_tok)
    mv_p = mkv[:, memw:]
    z, qn, kv_p, gates, ret_tabs = _layer_common(xp, t, jnp.arange(t), norm_attn[l], nsa_q_norm[l],
                                                 nsa_k_norm[l], pw)
    z3 = z.reshape(bp, t, z.shape[1])
    o_ret, st_p = _retention_prompt(z3, ret_tabs, jnp.zeros((bp, RET_HEADS, RET_DK, RET_DV), F32), ret_gn[l])
    nbc = t // CMP_BLOCK
    ck_p = _prologue_matmul(kv_p[0].reshape(bp * nbc, cmp_cols), pw["pos_k"], pw["wk_big"], mode="bias", tm=256,
                            name="compress_k")
    cv_p = _prologue_matmul(kv_p[1].reshape(bp * nbc, cmp_cols), pw["pos_v"], pw["wv_big"], mode="bias", tm=256,
                            name="compress_v")
    pair = CMP_PER_SEL * kvw
    seq_rows = lambda a: a.reshape(bp, t * g_kv, dh)
    o_nsa = _nsa_prompt(qn.reshape(bp, t, -1), ck_p.reshape(bp, nbc // CMP_PER_SEL, pair),
                        cv_p.reshape(bp, nbc // CMP_PER_SEL, pair),
                        *(seq_rows(a) for a in kv_p[2:]), gates.reshape(bp, t, LANES))
    o_mem = _mem_attention(z3, mem_q_norm[l], mk_p.reshape(bp, m_tok, memw), mv_p.reshape(bp, m_tok, memw))
    y_p = _layer_tail(xp, z, o_ret.reshape(n_p, -1), o_nsa.reshape(n_p, -1), o_mem.reshape(n_p, -1), norm_ffn[l], pw)

    nbuf_p = min(WINDOW, t)
    kv5 = lambda a, bb, tt: a.reshape(1, bb, tt, g_kv, dh)
    outs_p = tuple(kv5(a, bp, t) for a in kv_p[:4]) + tuple(kv5(a, bp, t)[:, :, t - nbuf_p:] for a in kv_p[4:]) + (
        st_p[None], mk_p.reshape(1, bp, m_tok, MEM_HEADS, MEM_HEAD_DIM),
        mv_p.reshape(1, bp, m_tok, MEM_HEADS, MEM_HEAD_DIM))

    xs = x_sample.reshape(bs, d)
    zs, qn_s, kv_s, gates_s, ret_tabs_s = _layer_common(
        xs, 1, jnp.full((1,), past, I32), norm_attn[l], nsa_q_norm[l], nsa_k_norm[l], pw)
    zs3 = zs.reshape(bs, 1, zs.shape[1])
    o_ret_s, st_s = _retention_decode(zs3, ret_tabs_s, state_ret[l], ret_gn[l])
    per_page = page // CMP_BLOCK
    ckp = _compress_pool(cache_cmp_k, cmp_pos_k[l], cmp_w_k[l], "compress_pool_k")
    cvp = _compress_pool(cache_cmp_v, cmp_pos_v[l], cmp_w_v[l], "compress_pool_v")
    pages_of = lambda a: a[:, page_table].transpose(1, 2, 0, 3).reshape(bs, n_pages * per_page // CMP_PER_SEL, pair)
    ck_s, cv_s = pages_of(ckp), pages_of(cvp)
    qn_s3 = qn_s.reshape(bs, 1, -1)
    ocmp_s, idx_s = _nsa_decode_select(qn_s3, ck_s, cv_s, past)
    nbs = past // SEL_BLOCK + 1
    nsel = min(SEL_TOPK, nbs)
    per_page_sel = page // SEL_BLOCK
    picks = jnp.clip(idx_s[:, 0, :NSA_KV_HEADS * SEL_TOPK].reshape(bs, NSA_KV_HEADS, SEL_TOPK)[:, :, :nsel],
                     0, nbs - 2)
    pool_rows = (jnp.take_along_axis(page_table, (picks // per_page_sel).reshape(bs, -1), axis=1) * per_page_sel
                 + (picks % per_page_sel).reshape(bs, -1)).reshape(-1).astype(I32)
    tok_row = lambda a: a.reshape(bs, 1, kvw)
    o_nsa_s = _nsa_decode_attend(
        pool_rows, cache_sel_k, cache_sel_v, qn_s3, idx_s, tok_row(kv_s[2]), tok_row(kv_s[3]), cache_win_k,
        cache_win_v, tok_row(kv_s[4]), tok_row(kv_s[5]), ocmp_s, gates_s.reshape(bs, 1, LANES), past, nsel)
    o_mem_s = _mem_attention(zs3, mem_q_norm[l], cache_mem_k, cache_mem_v, cache_layout=True)
    y_s = _layer_tail(xs, zs, o_ret_s.reshape(bs, -1), o_nsa_s.reshape(bs, -1), o_mem_s.reshape(bs, -1),
                      norm_ffn[l], pw)

    nbuf_s = cache_win_k.shape[2]
    win_k_s = jnp.concatenate([cache_win_k[l], kv5(kv_s[4], bs, 1)[0]], axis=1)[None, :, -nbuf_s:]
    win_v_s = jnp.concatenate([cache_win_v[l], kv5(kv_s[5], bs, 1)[0]], axis=1)[None, :, -nbuf_s:]
    outs_s = tuple(kv5(a, bs, 1) for a in kv_s[:4]) + (win_k_s, win_v_s, st_s[None].astype(x_sample.dtype))
    return (y_p.reshape(bp, t, d), y_s.reshape(bs, ts, d)) + outs_p + outs_s
```
